```python
import math
import jax, jax.numpy as jnp
from jax import lax
import numpy as np

D_MODEL = 1024
BATCH = 2
SEQ = 8192
DEPTH = 4
DEC_BATCH = 128
DEC_SEQ = 4
PAST_LEN = 8192
PAGE_SIZE = 128

N_EVEN = (DEPTH + 1) // 2
N_ODD = DEPTH // 2
SSD_HEADS = 16
SSD_HEAD_DIM = 64
SSD_INNER = SSD_HEADS * SSD_HEAD_DIM
SSD_GROUPS = 2
SSD_STATE = 128
SSD_CONV = 4
SSD_CHUNK = 128
SSD_CONV_DIM = SSD_INNER + 2 * SSD_GROUPS * SSD_STATE
GMLP_GROUPS = 8
GMLP_WIDTH = 1024
GMLP_GROUP_DIM = GMLP_WIDTH // GMLP_GROUPS
GMLP_CHUNK = 128
EVEN_IN = SSD_INNER + SSD_CONV_DIM + SSD_HEADS + 2 * GMLP_WIDTH
EVEN_MIX = SSD_INNER + GMLP_WIDTH
MLA_HEADS = 8
MLA_NOPE = 128
MLA_ROPE = 64
MLA_V = 128
MLA_Q_RANK = 256
MLA_KV_RANK = 256
MLA_DOWN = MLA_Q_RANK + MLA_KV_RANK + MLA_ROPE
MLA_SCALE = (MLA_NOPE + MLA_ROPE) ** -0.5
ROPE_BASE = 10000.0
ATTN_BLOCK = 128
MEM_TOKENS = 256
MEM_HEADS = 4
MEM_HEAD_DIM = 128
MEM_INNER = MEM_HEADS * MEM_HEAD_DIM
D_FF = 2816
N_EXPERTS = 8
TOP_K = 2
D_FF_EXPERT = 3584
EPS = 1e-6

kernel_name = 'hybrid_ssd_gmlp_mla_decoder_step'


def rmsnorm(x, g):
    xf = x.astype(jnp.float32)
    y = xf * lax.rsqrt(jnp.mean(xf * xf, axis=-1, keepdims=True) + EPS)
    return (y * g.astype(jnp.float32)).astype(x.dtype)


def layernorm(x, g, b):
    xf = x.astype(jnp.float32)
    mu = jnp.mean(xf, axis=-1, keepdims=True)
    xc = xf - mu
    y = xc * lax.rsqrt(jnp.mean(xc * xc, axis=-1, keepdims=True) + EPS)
    return (y * g.astype(jnp.float32) + b.astype(jnp.float32)).astype(x.dtype)


def causal_conv(u, buf, w, b):
    L = u.shape[1]
    cat = jnp.concatenate([buf.astype(u.dtype), u], axis=1)
    out = b + w[0] * cat[:, :L]
    for k in range(1, SSD_CONV):
        out = out + w[k] * cat[:, k:k + L]
    return out, cat[:, L:]


def ssd_scan(xdt, a, Bm, Cm, h0):
    bsz, L = xdt.shape[:2]
    Q = min(SSD_CHUNK, L)
    pad = (-L) % Q
    if pad:
        padf = lambda t: jnp.pad(t, [(0, 0), (0, pad)] + [(0, 0)] * (t.ndim - 2))
        xdt, a, Bm, Cm = padf(xdt), padf(a), padf(Bm), padf(Cm)
    nc = (L + pad) // Q
    G, R, P = xdt.shape[2:]
    N = Bm.shape[-1]
    xdt = xdt.reshape(bsz, nc, Q, G, R, P)
    a = a.reshape(bsz, nc, Q, G, R)
    Bm = Bm.reshape(bsz, nc, Q, G, N)
    Cm = Cm.reshape(bsz, nc, Q, G, N)
    a_cum = jnp.cumsum(a, axis=2)
    causal = jnp.tril(jnp.ones((Q, Q), bool))
    seg = a_cum[:, :, :, None] - a_cum[:, :, None, :]
    decay = jnp.exp(jnp.where(causal[:, :, None, None], seg, -jnp.inf))
    cb = jnp.einsum('bctgn,bcsgn->bctsg', Cm, Bm)
    y_diag = jnp.einsum('bctsgr,bcsgrp->bctgrp', cb[..., None] * decay, xdt)
    decay_end = jnp.exp(a_cum[:, :, -1:] - a_cum)
    states = jnp.einsum('bcsgn,bcsgrp->bcgrpn', Bm, decay_end[..., None] * xdt)
    chunk_decay = jnp.exp(a_cum[:, :, -1])

    def step(h, inp):
        st, dc = inp
        return h * dc[..., None, None] + st, h

    h_last, h_in = lax.scan(step, h0, (jnp.moveaxis(states, 1, 0), jnp.moveaxis(chunk_decay, 1, 0)))
    h_in = jnp.moveaxis(h_in, 0, 1)
    y_off = jnp.einsum('bctgn,bcgrpn->bctgrp', Cm, h_in) * jnp.exp(a_cum)[..., None]
    y = (y_diag + y_off).reshape(bsz, nc * Q, G, R, P)[:, :L]
    return y, h_last


def chunk_spatial(v, w_s, b_s):
    bsz, L, _ = v.shape
    Q = min(GMLP_CHUNK, L)
    pad = (-L) % Q
    vp = jnp.pad(v, ((0, 0), (0, pad), (0, 0)))
    nc = (L + pad) // Q
    vp = vp.reshape(bsz, nc, Q, GMLP_GROUPS, GMLP_GROUP_DIM)
    w = jnp.tril(w_s[:, :Q, :Q])
    out = jnp.einsum('gts,bcsgd->bctgd', w, vp) + b_s[:, :Q].T[:, :, None]
    return out.reshape(bsz, nc * Q, GMLP_WIDTH)[:, :L]


def even_mixer(h, conv_buf, ssd_h0, w_in, conv_w, conv_b, dt_bias, a_log, d_skip, ssd_gain,
               ln_g, ln_b, w_s, b_s, w_out):
    f32 = jnp.float32
    bsz, L, _ = h.shape
    R = SSD_HEADS // SSD_GROUPS
    z, xbc, dt, uv = jnp.split(h @ w_in, [SSD_INNER, SSD_INNER + SSD_CONV_DIM,
                                           SSD_INNER + SSD_CONV_DIM + SSD_HEADS], axis=-1)
    xbc, new_buf = causal_conv(xbc, conv_buf, conv_w, conv_b)
    xbc = jax.nn.silu(xbc).astype(f32)
    xs, Bm, Cm = jnp.split(xbc, [SSD_INNER, SSD_INNER + SSD_GROUPS * SSD_STATE], axis=-1)
    xs = xs.reshape(bsz, L, SSD_GROUPS, R, SSD_HEAD_DIM)
    Bm = Bm.reshape(bsz, L, SSD_GROUPS, SSD_STATE)
    Cm = Cm.reshape(bsz, L, SSD_GROUPS, SSD_STATE)
    dt = jax.nn.softplus(dt.astype(f32) + dt_bias.astype(f32)).reshape(bsz, L, SSD_GROUPS, R)
    a = dt * (-jnp.exp(a_log.astype(f32))).reshape(SSD_GROUPS, R)
    h0 = ssd_h0.astype(f32).reshape(bsz, SSD_GROUPS, R, SSD_HEAD_DIM, SSD_STATE)
    y, h_last = ssd_scan(xs * dt[..., None], a, Bm, Cm, h0)
    y = y + d_skip.astype(f32).reshape(SSD_GROUPS, R, 1) * xs
    y = y.reshape(bsz, L, SSD_GROUPS, SSD_INNER // SSD_GROUPS) * jax.nn.silu(z.astype(f32)).reshape(bsz, L, SSD_GROUPS, -1)
    y_a = rmsnorm(y, ssd_gain.reshape(SSD_GROUPS, -1)).reshape(bsz, L, SSD_INNER).astype(h.dtype)
    u, v = jnp.split(jax.nn.gelu(uv), 2, axis=-1)
    v = layernorm(v, ln_g, ln_b)
    y_b = u * chunk_spatial(v, w_s, b_s)
    out = jnp.concatenate([y_a, y_b], axis=-1) @ w_out
    new_h = h_last.reshape(bsz, SSD_HEADS, SSD_HEAD_DIM, SSD_STATE).astype(h.dtype)
    return out, new_buf, new_h, v


def rope(x, pos):
    half = x.shape[-1] // 2
    inv_freq = ROPE_BASE ** (-jnp.arange(half, dtype=jnp.float32) / half)
    ang = pos.astype(jnp.float32)[:, None] * inv_freq
    cos = jnp.cos(ang)[:, None, :]
    sin = jnp.sin(ang)[:, None, :]
    xf = x.astype(jnp.float32)
    x1, x2 = xf[..., :half], xf[..., half:]
    return jnp.concatenate([x1 * cos - x2 * sin, x1 * sin + x2 * cos], axis=-1).astype(x.dtype)


def mla_project(h, pos, w_down, q_gain, kv_gain, w_uq, w_uk):
    cq, ckv, kpe = jnp.split(h @ w_down, [MLA_Q_RANK, MLA_Q_RANK + MLA_KV_RANK], axis=-1)
    cq = rmsnorm(cq, q_gain)
    ckv = rmsnorm(ckv, kv_gain)
    q = jnp.einsum('blr,rhd->blhd', cq, w_uq)
    q_lat = jnp.einsum('blhd,rhd->blhr', q[..., :MLA_NOPE], w_uk)
    q_pe = rope(q[..., MLA_NOPE:], pos)
    kpe = rope(kpe[:, :, None, :], pos)[:, :, 0, :]
    return q_lat, q_pe, ckv, kpe


def mla_attend(q_lat, q_pe, q_pos, ckv, kpe, k_pos, w_uv, w_o):
    bsz, Lq = q_lat.shape[:2]
    Qb = min(ATTN_BLOCK, Lq)
    pad = (-Lq) % Qb
    nb = (Lq + pad) // Qb

    def to_blocks(t):
        t = jnp.pad(t, [(0, 0), (0, pad)] + [(0, 0)] * (t.ndim - 2))
        return jnp.moveaxis(t.reshape((bsz, nb, Qb) + t.shape[2:]), 1, 0)

    qpos_b = jnp.pad(q_pos, (0, pad)).reshape(nb, Qb)

    def block(args):
        ql, qp, qpos = args
        s = jnp.einsum('bqhr,bkr->bhqk', ql, ckv) + jnp.einsum('bqhd,bkd->bhqk', qp, kpe)
        s = s.astype(jnp.float32) * MLA_SCALE
        mask = k_pos[None, :] <= qpos[:, None]
        s = jnp.where(mask[None, None], s, -jnp.inf)
        p = jax.nn.softmax(s, axis=-1).astype(ckv.dtype)
        return jnp.einsum('bhqk,bkr->bqhr', p, ckv)

    o_lat = lax.map(block, (to_blocks(q_lat), to_blocks(q_pe), qpos_b))
    o_lat = jnp.moveaxis(o_lat, 0, 1).reshape(bsz, nb * Qb, MLA_HEADS, MLA_KV_RANK)[:, :Lq]
    o = jnp.einsum('blhr,rhd->blhd', o_lat, w_uv).reshape(bsz, Lq, MLA_HEADS * MLA_V)
    return o @ w_o


def mem_kv(mem, gain, w_k, w_v):
    bsz, M, _ = mem.shape
    m = rmsnorm(mem, gain)
    k = (m @ w_k).reshape(bsz, M, MEM_HEADS, MEM_HEAD_DIM)
    v = (m @ w_v).reshape(bsz, M, MEM_HEADS, MEM_HEAD_DIM)
    return k, v


def mem_attend(h, k, v, w_q, w_o):
    bsz, L, _ = h.shape
    q = (h @ w_q).reshape(bsz, L, MEM_HEADS, MEM_HEAD_DIM)
    s = jnp.einsum('blhd,bmhd->bhlm', q, k).astype(jnp.float32) * MEM_HEAD_DIM ** -0.5
    p = jax.nn.softmax(s, axis=-1).astype(v.dtype)
    o = jnp.einsum('bhlm,bmhd->blhd', p, v).reshape(bsz, L, MEM_INNER)
    return o @ w_o


def swiglu(h, w_gu, w_down):
    g, u = jnp.split(h @ w_gu, 2, axis=-1)
    return (jax.nn.silu(g) * u) @ w_down


def moe(h, w_router, e_gu, e_down):
    logits = (h @ w_router).astype(jnp.float32)
    top_v, top_i = lax.top_k(logits, TOP_K)
    gates = jax.nn.softmax(top_v, axis=-1)
    comb = jnp.sum(jax.nn.one_hot(top_i, N_EXPERTS, dtype=jnp.float32) * gates[..., None], axis=-2).astype(h.dtype)
    out = jnp.zeros_like(h)
    for e in range(N_EXPERTS):
        out = out + comb[..., e:e + 1] * swiglu(h, e_gu[e], e_down[e])
    return out


def setup_inputs(seed: int = 0) -> dict:
    key = jax.random.key(seed)
    keys = iter(jax.random.split(key, 64))
    f32 = jnp.float32
    D = D_MODEL

    def nrm(shape, scale):
        return jax.random.normal(next(keys), shape, f32) * scale

    def gain(shape):
        return 1.0 + nrm(shape, 0.02)

    n_pages = PAST_LEN // PAGE_SIZE
    used = DEC_BATCH * n_pages
    n_pool = used + used // 4
    page_table = jax.random.permutation(next(keys), n_pool)[:used].reshape(DEC_BATCH, n_pages).astype(jnp.int32)
    dt0 = jnp.exp(jax.random.uniform(next(keys), (N_EVEN, SSD_HEADS), f32, math.log(1e-3), math.log(1e-1)))
    dt_bias = dt0 + jnp.log(-jnp.expm1(-dt0))
    a_log = jnp.log(jax.random.uniform(next(keys), (N_EVEN, SSD_HEADS), f32, 1.0, 16.0))
    return {
        'x_prompt': nrm((BATCH, SEQ, D), 1.0),
        'x_sample': nrm((DEC_BATCH, DEC_SEQ, D), 1.0),
        'state_ssd': nrm((N_EVEN, DEC_BATCH, SSD_HEADS, SSD_HEAD_DIM, SSD_STATE), 0.1),
        'state_conv': nrm((N_EVEN, DEC_BATCH, SSD_CONV - 1, SSD_CONV_DIM), 1.0),
        'cache_mla_ckv': nrm((N_ODD, n_pool, PAGE_SIZE, MLA_KV_RANK), 1.0),
        'cache_mla_kpe': nrm((N_ODD, n_pool, PAGE_SIZE, MLA_ROPE), 1.0),
        'cache_mem_k': nrm((DEPTH, DEC_BATCH, MEM_TOKENS, MEM_HEADS, MEM_HEAD_DIM), 1.0),
        'cache_mem_v': nrm((DEPTH, DEC_BATCH, MEM_TOKENS, MEM_HEADS, MEM_HEAD_DIM), 1.0),
        'page_table': page_table,
        'mem_prompt': nrm((BATCH, MEM_TOKENS, D), 1.0),
        'mix_norm': gain((DEPTH, D)),
        'w_in': nrm((N_EVEN, D, EVEN_IN), D ** -0.5),
        'conv_w': nrm((N_EVEN, SSD_CONV, SSD_CONV_DIM), SSD_CONV ** -0.5),
        'conv_b': nrm((N_EVEN, SSD_CONV_DIM), 0.01),
        'dt_bias': dt_bias,
        'a_log': a_log,
        'd_skip': gain((N_EVEN, SSD_HEADS)),
        'ssd_gain': gain((N_EVEN, SSD_INNER)),
        'gmlp_ln_g': gain((N_EVEN, GMLP_WIDTH)),
        'gmlp_ln_b': nrm((N_EVEN, GMLP_WIDTH), 0.01),
        'gmlp_ws': nrm((N_EVEN, GMLP_GROUPS, GMLP_CHUNK, GMLP_CHUNK), GMLP_CHUNK ** -0.5),
        'gmlp_bs': gain((N_EVEN, GMLP_GROUPS, GMLP_CHUNK)),
        'w_out_even': nrm((N_EVEN, EVEN_MIX, D), EVEN_MIX ** -0.5),
        'w_mla_down': nrm((N_ODD, D, MLA_DOWN), D ** -0.5),
        'mla_q_gain': gain((N_ODD, MLA_Q_RANK)),
        'mla_kv_gain': gain((N_ODD, MLA_KV_RANK)),
        'w_mla_uq': nrm((N_ODD, MLA_Q_RANK, MLA_HEADS, MLA_NOPE + MLA_ROPE), MLA_Q_RANK ** -0.5),
        'w_mla_uk': nrm((N_ODD, MLA_KV_RANK, MLA_HEADS, MLA_NOPE), MLA_KV_RANK ** -0.5),
        'w_mla_uv': nrm((N_ODD, MLA_KV_RANK, MLA_HEADS, MLA_V), MLA_KV_RANK ** -0.5),
        'w_mla_o': nrm((N_ODD, MLA_HEADS * MLA_V, D), (MLA_HEADS * MLA_V) ** -0.5),
        'xattn_norm': gain((DEPTH, D)),
        'mem_norm': gain((DEPTH, D)),
        'w_mem_q': nrm((DEPTH, D, MEM_INNER), D ** -0.5),
        'w_mem_k': nrm((DEPTH, D, MEM_INNER), D ** -0.5),
        'w_mem_v': nrm((DEPTH, D, MEM_INNER), D ** -0.5),
        'w_mem_o': nrm((DEPTH, MEM_INNER, D), MEM_INNER ** -0.5),
        'ffn_norm': gain((DEPTH, D)),
        'w_ffn_gu': nrm((N_EVEN, D, 2 * D_FF), D ** -0.5),
        'w_ffn_down': nrm((N_EVEN, D_FF, D), D_FF ** -0.5),
        'w_router': nrm((N_ODD, D, N_EXPERTS), D ** -0.5),
        'w_exp_gu': nrm((N_ODD, N_EXPERTS, D, 2 * D_FF_EXPERT), D ** -0.5),
        'w_exp_down': nrm((N_ODD, N_EXPERTS, D_FF_EXPERT, D), D_FF_EXPERT ** -0.5),
        'final_norm': gain((D,)),
    }


def reference(x_prompt, x_sample, state_ssd, state_conv, cache_mla_ckv, cache_mla_kpe,
              cache_mem_k, cache_mem_v, page_table, mem_prompt,
              mix_norm, w_in, conv_w, conv_b, dt_bias, a_log, d_skip, ssd_gain,
              gmlp_ln_g, gmlp_ln_b, gmlp_ws, gmlp_bs, w_out_even,
              w_mla_down, mla_q_gain, mla_kv_gain, w_mla_uq, w_mla_uk, w_mla_uv, w_mla_o,
              xattn_norm, mem_norm, w_mem_q, w_mem_k, w_mem_v, w_mem_o,
              ffn_norm, w_ffn_gu, w_ffn_down, w_router, w_exp_gu, w_exp_down, final_norm):
    pos_p = jnp.arange(SEQ, dtype=jnp.int32)
    pos_s = PAST_LEN + jnp.arange(DEC_SEQ, dtype=jnp.int32)
    kpos_s = jnp.arange(PAST_LEN + DEC_SEQ, dtype=jnp.int32)
    hp, hs = x_prompt, x_sample
    p_ssd, p_conv, p_ckv, p_kpe, p_mk, p_mv = [], [], [], [], [], []
    s_ssd, s_conv, s_v, s_ckv, s_kpe = [], [], [], [], []
    for l in range(DEPTH):
        i = l // 2
        if l % 2 == 0:
            wts = (w_in[i], conv_w[i], conv_b[i], dt_bias[i], a_log[i], d_skip[i], ssd_gain[i],
                   gmlp_ln_g[i], gmlp_ln_b[i], gmlp_ws[i], gmlp_bs[i], w_out_even[i])
            buf0 = jnp.zeros((BATCH, SSD_CONV - 1, SSD_CONV_DIM), hp.dtype)
            h00 = jnp.zeros((BATCH, SSD_HEADS, SSD_HEAD_DIM, SSD_STATE), jnp.float32)
            dp, buf_p, ssd_p, _ = even_mixer(rmsnorm(hp, mix_norm[l]), buf0, h00, *wts)
            ds, buf_s, ssd_s, v_s = even_mixer(rmsnorm(hs, mix_norm[l]), state_conv[i], state_ssd[i], *wts)
            hp = hp + dp
            hs = hs + ds
            p_ssd.append(ssd_p)
            p_conv.append(buf_p)
            s_ssd.append(ssd_s)
            s_conv.append(buf_s)
            s_v.append(v_s)
        else:
            wp = (w_mla_down[i], mla_q_gain[i], mla_kv_gain[i], w_mla_uq[i], w_mla_uk[i])
            ql, qr, ckv, kpe = mla_project(rmsnorm(hp, mix_norm[l]), pos_p, *wp)
            hp = hp + mla_attend(ql, qr, pos_p, ckv, kpe, pos_p, w_mla_uv[i], w_mla_o[i])
            ql_s, qr_s, ckv_s, kpe_s = mla_project(rmsnorm(hs, mix_norm[l]), pos_s, *wp)
            past_ckv = cache_mla_ckv[i, page_table].reshape(DEC_BATCH, PAST_LEN, MLA_KV_RANK)
            past_kpe = cache_mla_kpe[i, page_table].reshape(DEC_BATCH, PAST_LEN, MLA_ROPE)
            keys_ckv = jnp.concatenate([past_ckv.astype(ckv_s.dtype), ckv_s], axis=1)
            keys_kpe = jnp.concatenate([past_kpe.astype(kpe_s.dtype), kpe_s], axis=1)
            hs = hs + mla_attend(ql_s, qr_s, pos_s, keys_ckv, keys_kpe, kpos_s, w_mla_uv[i], w_mla_o[i])
            p_ckv.append(ckv)
            p_kpe.append(kpe)
            s_ckv.append(ckv_s)
            s_kpe.append(kpe_s)
        mk_p, mv_p = mem_kv(mem_prompt, mem_norm[l], w_mem_k[l], w_mem_v[l])
        hp = hp + mem_attend(rmsnorm(hp, xattn_norm[l]), mk_p, mv_p, w_mem_q[l], w_mem_o[l])
        hs = hs + mem_attend(rmsnorm(hs, xattn_norm[l]), cache_mem_k[l].astype(hs.dtype),
                             cache_mem_v[l].astype(hs.dtype), w_mem_q[l], w_mem_o[l])
        p_mk.append(mk_p)
        p_mv.append(mv_p)
        if l % 2 == 0:
            hp = hp + swiglu(rmsnorm(hp, ffn_norm[l]), w_ffn_gu[i], w_ffn_down[i])
            hs = hs + swiglu(rmsnorm(hs, ffn_norm[l]), w_ffn_gu[i], w_ffn_down[i])
        else:
            hp = hp + moe(rmsnorm(hp, ffn_norm[l]), w_router[i], w_exp_gu[i], w_exp_down[i])
            hs = hs + moe(rmsnorm(hs, ffn_norm[l]), w_router[i], w_exp_gu[i], w_exp_down[i])
    y_prompt = rmsnorm(hp, final_norm)
    y_sample = rmsnorm(hs, final_norm)
    return (y_prompt, y_sample,
            jnp.stack(p_ssd), jnp.stack(p_conv), jnp.stack(p_ckv), jnp.stack(p_kpe),
            jnp.stack(p_mk), jnp.stack(p_mv),
            jnp.stack(s_ssd), jnp.stack(s_conv), jnp.stack(s_v), jnp.stack(s_ckv), jnp.stack(s_kpe))
```

```python
import functools

import jax
import jax.numpy as jnp
from jax import lax
from jax.experimental import pallas as pl
from jax.experimental.pallas import tpu as pltpu

F32 = jnp.float32
BF = jnp.bfloat16
EPS = 1e-6
NEG = -1e30

D = 1024
SSD_HEADS = 16
SSD_HEAD_DIM = 64
SSD_INNER = SSD_HEADS * SSD_HEAD_DIM
SSD_GROUPS = 2
SSD_STATE = 128
SSD_CONV = 4
CONV_DIM = SSD_INNER + 2 * SSD_GROUPS * SSD_STATE
GMLP_GROUPS = 8
GMLP_WIDTH = 1024
CHUNK = 128
ZXU = SSD_INNER + CONV_DIM + 2 * GMLP_WIDTH
MLA_HEADS = 8
NOPE = 128
ROPE = 64
MLA_V = 128
Q_RANK = 256
KV_RANK = 256
MLA_SCALE = (NOPE + ROPE) ** -0.5
QK = KV_RANK + ROPE
ROPE_BASE = 10000.0
MEM_HEADS = 4
MEM_HD = 128
MEM_INNER = MEM_HEADS * MEM_HD
D_FF = 2816
N_EXP = 8
D_FFE = 3584
PAGE = 128
LANES = 128
SUBLANES = 8
ROW_TILE = 512
PAGES_PER_STEP = 8
NEW_KEY_ROWS = 16
SAMPLES_PER_STEP = 8


def _cp(sem, vmem_mb=None):
    kw = dict(dimension_semantics=sem)
    if vmem_mb is not None:
        kw["vmem_limit_bytes"] = vmem_mb * 1024 * 1024
    return pltpu.CompilerParams(**kw)


def _rms(x, g):
    return x * lax.rsqrt(jnp.mean(x * x, axis=-1, keepdims=True) + EPS) * g


def _dot(a, b):
    return jnp.dot(a, b, preferred_element_type=F32)


def _dot_nt(a, b):
    return lax.dot_general(a, b, (((1,), (1,)), ((), ())), preferred_element_type=F32)


def _dot_f32(a, b):
    return jnp.dot(a, b, preferred_element_type=F32, precision=lax.Precision.HIGHEST)


def _silu(x):
    return x * jax.nn.sigmoid(x)


def _full(shape):
    n = len(shape)
    return pl.BlockSpec(shape, lambda *_: (0,) * n)


def _largest_tile(n, cap):
    best = LANES
    for t in range(LANES, cap + 1, LANES):
        if n % t == 0:
            best = t
    return best


def _rms_matmul_kernel(x_ref, g_ref, w_ref, o_ref, xn_ref, *, precise):
    @pl.when(pl.program_id(1) == 0)
    def _():
        xn_ref[...] = _rms(x_ref[...], g_ref[...]).astype(xn_ref.dtype)

    if precise:
        o_ref[...] = _dot_f32(xn_ref[...], w_ref[...]).astype(o_ref.dtype)
    else:
        o_ref[...] = _dot(xn_ref[...], w_ref[...]).astype(o_ref.dtype)


def _rms_matmul(x, g, w, *, tn, out_dtype=F32, precise=False):
    t, k = x.shape
    n = w.shape[1]
    tm = min(ROW_TILE, t)
    return pl.pallas_call(
        functools.partial(_rms_matmul_kernel, precise=precise),
        grid=(t // tm, n // tn),
        in_specs=[
            pl.BlockSpec((tm, k), lambda i, j: (i, 0)),
            pl.BlockSpec((1, k), lambda i, j: (0, 0)),
            pl.BlockSpec((k, tn), lambda i, j: (0, j)),
        ],
        out_specs=pl.BlockSpec((tm, tn), lambda i, j: (i, j)),
        out_shape=jax.ShapeDtypeStruct((t, n), out_dtype),
        scratch_shapes=[pltpu.VMEM((tm, k), F32 if precise else BF)],
        compiler_params=_cp(("parallel", "arbitrary")),
        name="rms_matmul",
    )(x, g, w)


def _matmul_res_kernel(a_ref, w_ref, r_ref, o_ref):
    o_ref[...] = r_ref[...] + _dot(a_ref[...].astype(BF), w_ref[...])


def _matmul_res(a, w, res, *, tn=512):
    t, k = a.shape
    n = w.shape[1]
    tm = min(ROW_TILE, t)
    return pl.pallas_call(
        _matmul_res_kernel,
        grid=(t // tm, n // tn),
        in_specs=[
            pl.BlockSpec((tm, k), lambda i, j: (i, 0)),
            pl.BlockSpec((k, tn), lambda i, j: (0, j)),
            pl.BlockSpec((tm, tn), lambda i, j: (i, j)),
        ],
        out_specs=pl.BlockSpec((tm, tn), lambda i, j: (i, j)),
        out_shape=jax.ShapeDtypeStruct((t, n), F32),
        compiler_params=_cp(("parallel", "arbitrary")),
        name="matmul_res",
    )(a, w, res)


FF_CHUNK = 256


def _ffn_kernel(x_ref, g_ref, wgu_ref, wd_ref, o_ref, hid_ref, *, ff):
    x = x_ref[...]
    xn = _rms(x, g_ref[...]).astype(BF)
    for c in range(ff // FF_CHUNK):
        lo = c * FF_CHUNK
        gate = _dot(xn, wgu_ref[:, lo:lo + FF_CHUNK])
        up = _dot(xn, wgu_ref[:, ff + lo:ff + lo + FF_CHUNK])
        hid_ref[:, lo:lo + FF_CHUNK] = (_silu(gate) * up).astype(BF)
    o_ref[...] = x + _dot(hid_ref[...], wd_ref[...])


def _ffn(x, g, wgu, wd):
    t = x.shape[0]
    ff = wd.shape[0]
    tm = min(ROW_TILE, t)
    return pl.pallas_call(
        functools.partial(_ffn_kernel, ff=ff),
        grid=(t // tm,),
        in_specs=[
            pl.BlockSpec((tm, D), lambda i: (i, 0)),
            _full((1, D)),
            pl.BlockSpec((D, 2 * ff), lambda i: (0, 0), pipeline_mode=pl.Buffered(1)),
            pl.BlockSpec((ff, D), lambda i: (0, 0), pipeline_mode=pl.Buffered(1)),
        ],
        out_specs=pl.BlockSpec((tm, D), lambda i: (i, 0)),
        out_shape=jax.ShapeDtypeStruct((t, D), F32),
        scratch_shapes=[pltpu.VMEM((tm, ff), BF)],
        compiler_params=_cp(("parallel",), vmem_mb=48),
        name="ffn",
    )(x, g, wgu, wd)


def _softplus(x):
    return jnp.maximum(x, 0.0) + jnp.log1p(jnp.exp(-jnp.abs(x)))


def _gelu_tanh(x):
    return 0.5 * x * (1.0 + jnp.tanh(0.7978845608028654 * (x + 0.044715 * (x * x * x))))


def _even_kernel(proj_ref, dt_ref, cinit_ref, sinit_ref, cw_ref, cb_ref, dtb_ref, alog_ref,
                 dsk_ref, sg_ref, lng_ref, lnb_ref, ws_ref, bst_ref,
                 ymix_ref, cout_ref, sout_ref, v_ref, ext_ref, ht_ref, *, lb, lc):
    q = CHUNK
    c = pl.program_id(1)
    last = pl.num_programs(1) - 1

    @pl.when(c == 0)
    def _():
        ext_ref[0:SUBLANES, :] = cinit_ref[0]
        ht_ref[...] = sinit_ref[0].T

    if lb == q:
        p = proj_ref[0]
        dtr = dt_ref[0]
    else:
        p = jnp.concatenate([proj_ref[0], jnp.zeros((q - lb, ZXU), F32)], axis=0)
        dtr = jnp.concatenate([dt_ref[0], jnp.zeros((q - lb, LANES), F32)], axis=0)
    z = p[:, :SSD_INNER]
    xbc_raw = p[:, SSD_INNER:SSD_INNER + CONV_DIM]
    uv = p[:, SSD_INNER + CONV_DIM:]

    ext_ref[SUBLANES:SUBLANES + q, :] = xbc_raw
    conv = (cb_ref[...] + cw_ref[0:1, :] * ext_ref[5:5 + q, :] + cw_ref[1:2, :] * ext_ref[6:6 + q, :]
            + cw_ref[2:3, :] * ext_ref[7:7 + q, :] + cw_ref[3:4, :] * xbc_raw)

    @pl.when(c == last)
    def _():
        cout_ref[0] = ext_ref[SUBLANES + lc - 3:SUBLANES + lc, :]

    ext_ref[0:SUBLANES, :] = ext_ref[q:q + SUBLANES, :]

    xbc = _silu(conv)
    xs = xbc[:, :SSD_INNER]
    gw = SSD_STATE
    bm = [xbc[:, SSD_INNER + g * gw:SSD_INNER + (g + 1) * gw] for g in range(SSD_GROUPS)]
    cm = [xbc[:, SSD_INNER + (SSD_GROUPS + g) * gw:SSD_INNER + (SSD_GROUPS + g + 1) * gw]
          for g in range(SSD_GROUPS)]

    row = lax.broadcasted_iota(jnp.int32, (q, q), 0)
    col = lax.broadcasted_iota(jnp.int32, (q, q), 1)
    causal = row >= col

    dt = _softplus(dtr + dtb_ref[...])
    if lc < q:
        dt = jnp.where(lax.broadcasted_iota(jnp.int32, (q, LANES), 0) < lc, dt, 0.0)
    a = dt * (-jnp.exp(alog_ref[...]))
    a_cum = _dot_f32(causal.astype(F32), a)
    a_cum_t = a_cum.T
    a_last = a_cum[q - 1:q, :]
    decay_end = jnp.exp(a_last - a_cum)
    ea = jnp.exp(a_cum)
    chunk_decay = jnp.exp(a_last)

    cmb = [m.astype(BF) for m in cm]
    cb = [_dot_nt(cmb[g], bm[g].astype(BF)) for g in range(SSD_GROUPS)]
    bt = [bm[g].T.astype(BF) for g in range(SSD_GROUPS)]
    heads_per_group = SSD_HEADS // SSD_GROUPS
    ys = []
    for r in range(SSD_HEADS):
        g = r // heads_per_group
        lo = r * SSD_HEAD_DIM
        hi = lo + SSD_HEAD_DIM
        seg = a_cum[:, r:r + 1] - a_cum_t[r:r + 1, :]
        lmat = jnp.where(causal, jnp.exp(jnp.minimum(seg, 0.0)), 0.0)
        m = (cb[g] * lmat).astype(BF)
        xh = xs[:, lo:hi]
        xdt = xh * dt[:, r:r + 1]
        y_diag = _dot(m, xdt.astype(BF))
        h_old = ht_ref[:, lo:hi]
        y_off = _dot(cmb[g], h_old.astype(BF)) * ea[:, r:r + 1]
        ys.append(y_diag + y_off + dsk_ref[:, lo:hi] * xh)
        xd = (xdt * decay_end[:, r:r + 1]).astype(BF)
        ht_ref[:, lo:hi] = h_old * chunk_decay[:, r:r + 1] + _dot(bt[g], xd)
    y = jnp.concatenate(ys, axis=1) * _silu(z)
    gi = SSD_INNER // SSD_GROUPS
    yn = [_rms(y[:, g * gi:(g + 1) * gi], sg_ref[:, g * gi:(g + 1) * gi]) for g in range(SSD_GROUPS)]
    ymix_ref[0, :, 0:SSD_INNER] = jnp.concatenate(yn, axis=1)[:lb].astype(BF)

    @pl.when(c == last)
    def _():
        sout_ref[0] = ht_ref[...].T

    uvg = _gelu_tanh(uv)
    u = uvg[:, :GMLP_WIDTH]
    v = uvg[:, GMLP_WIDTH:]
    mu = jnp.mean(v, axis=-1, keepdims=True)
    vc = v - mu
    vn = vc * lax.rsqrt(jnp.mean(vc * vc, axis=-1, keepdims=True) + EPS) * lng_ref[...] + lnb_ref[...]
    if v_ref is not None:
        v_ref[0] = vn[:lb]
    gd = GMLP_WIDTH // GMLP_GROUPS
    yb = []
    for g in range(GMLP_GROUPS):
        wt = jnp.where(causal, ws_ref[g], 0.0).astype(BF)
        sp = _dot(wt, vn[:, g * gd:(g + 1) * gd].astype(BF)) + bst_ref[:, g:g + 1]
        yb.append(u[:, g * gd:(g + 1) * gd] * sp)
    ymix_ref[0, :, SSD_INNER:SSD_INNER + GMLP_WIDTH] = jnp.concatenate(yb, axis=1)[:lb].astype(BF)


def _even_kernel_no_v(*refs, lb, lc):
    n_in = 14
    ins, outs, scr = refs[:n_in], refs[n_in:n_in + 3], refs[n_in + 3:]
    _even_kernel(*ins, *outs, None, *scr, lb=lb, lc=lc)


def _even_mixer(proj, dtp, cinit8, sinit, prm, *, lb, lc, want_v):
    b, lp, _ = proj.shape
    nchunks = lp // lb
    par = [prm["conv_w"], prm["conv_b"], prm["dt_bias"], prm["a_log"], prm["d_skip"], prm["ssd_gain"],
           prm["ln_g"], prm["ln_b"], prm["ws"], prm["bst"]]
    in_specs = [
        pl.BlockSpec((1, lb, ZXU), lambda i, c: (i, c, 0)),
        pl.BlockSpec((1, lb, LANES), lambda i, c: (i, c, 0)),
        pl.BlockSpec((1, SUBLANES, CONV_DIM), lambda i, c: (i, 0, 0)),
        pl.BlockSpec((1, SSD_INNER, SSD_STATE), lambda i, c: (i, 0, 0)),
    ] + [_full(w.shape) for w in par]
    out_specs = [
        pl.BlockSpec((1, lb, SSD_INNER + GMLP_WIDTH), lambda i, c: (i, c, 0)),
        pl.BlockSpec((1, SSD_CONV - 1, CONV_DIM), lambda i, c: (i, 0, 0)),
        pl.BlockSpec((1, SSD_INNER, SSD_STATE), lambda i, c: (i, 0, 0)),
    ]
    out_shape = [
        jax.ShapeDtypeStruct((b, lp, SSD_INNER + GMLP_WIDTH), BF),
        jax.ShapeDtypeStruct((b, SSD_CONV - 1, CONV_DIM), F32),
        jax.ShapeDtypeStruct((b, SSD_INNER, SSD_STATE), F32),
    ]
    if want_v:
        out_specs.append(pl.BlockSpec((1, lb, GMLP_WIDTH), lambda i, c: (i, c, 0)))
        out_shape.append(jax.ShapeDtypeStruct((b, lp, GMLP_WIDTH), F32))
        body = functools.partial(_even_kernel, lb=lb, lc=lc)
    else:
        body = functools.partial(_even_kernel_no_v, lb=lb, lc=lc)
    return pl.pallas_call(
        body,
        grid=(b, nchunks),
        in_specs=in_specs,
        out_specs=out_specs,
        out_shape=out_shape,
        scratch_shapes=[pltpu.VMEM((CHUNK + 2 * SUBLANES, CONV_DIM), F32),
                        pltpu.VMEM((SSD_STATE, SSD_INNER), F32)],
        compiler_params=_cp(("parallel", "arbitrary"), vmem_mb=48),
        name="even_mixer",
    )(proj, dtp, cinit8, sinit, *par)


def _mla_proj_kernel(x_ref, g_ref, wdq_ref, wdkv_ref, wdk2_ref, qg_ref, kvg_ref, wqn_ref, wqp_ref,
                     wuk_ref, cos_ref, sin_ref, ckv_ref, kpe_ref, kcat_ref, q_ref):
    xn = _rms(x_ref[...], g_ref[...]).astype(BF)
    cqn = _rms(_dot(xn, wdq_ref[...]), qg_ref[...]).astype(BF)
    ckv = _rms(_dot(xn, wdkv_ref[...]), kvg_ref[...])
    kk = _dot(xn, wdk2_ref[...])
    cos = cos_ref[...]
    sin = sin_ref[...]
    kpe = kk[:, :ROPE] * cos[:, :ROPE] + kk[:, ROPE:] * sin[:, :ROPE]
    ckv_ref[...] = ckv
    kpe_ref[...] = kpe
    kcat_ref[:, :KV_RANK] = ckv.astype(BF)
    kcat_ref[:, KV_RANK:] = kpe.astype(BF)
    qn = _dot(cqn, wqn_ref[...])
    qp = _dot(cqn, wqp_ref[...])
    hr = MLA_HEADS * ROPE
    qpe = qp[:, :hr] * cos + qp[:, hr:] * sin
    for h in range(MLA_HEADS):
        ql = _dot(qn[:, h * NOPE:(h + 1) * NOPE].astype(BF), wuk_ref[h])
        q_ref[0, h, :, :KV_RANK] = (ql * MLA_SCALE).astype(BF)
        q_ref[0, h, :, KV_RANK:] = (qpe[:, h * ROPE:(h + 1) * ROPE] * MLA_SCALE).astype(BF)


def _mla_proj(x, g, prm, cos8, sin8, *, nb):
    t = x.shape[0]
    seq = t // nb
    tm = min(ROW_TILE, seq)
    tpb = seq // tm
    w = [prm["wdq"], prm["wdkv"], prm["wdk2"], prm["q_gain"], prm["kv_gain"], prm["wqn"], prm["wqp"],
         prm["wuk"]]
    return pl.pallas_call(
        _mla_proj_kernel,
        grid=(t // tm,),
        in_specs=[pl.BlockSpec((tm, D), lambda i: (i, 0)), _full((1, D))] + [_full(a.shape) for a in w] + [
            pl.BlockSpec((tm, MLA_HEADS * ROPE), lambda i: (i % tpb, 0)),
            pl.BlockSpec((tm, MLA_HEADS * ROPE), lambda i: (i % tpb, 0)),
        ],
        out_specs=[
            pl.BlockSpec((tm, KV_RANK), lambda i: (i, 0)),
            pl.BlockSpec((tm, ROPE), lambda i: (i, 0)),
            pl.BlockSpec((tm, QK), lambda i: (i, 0)),
            pl.BlockSpec((1, MLA_HEADS, tm, QK), lambda i: (i // tpb, 0, i % tpb, 0)),
        ],
        out_shape=[
            jax.ShapeDtypeStruct((t, KV_RANK), F32),
            jax.ShapeDtypeStruct((t, ROPE), F32),
            jax.ShapeDtypeStruct((t, QK), BF),
            jax.ShapeDtypeStruct((nb, MLA_HEADS, seq, QK), BF),
        ],
        compiler_params=_cp(("parallel",)),
        name="mla_proj",
    )(x, g, *w, cos8, sin8)


FLASH_TQ = 128
FLASH_TK = 512


def _flash_kernel(q_ref, k_ref, o_ref, m_ref, l_ref, acc_ref, *, tq, tk):
    qi = pl.program_id(1)
    rows = MLA_HEADS * tq
    qm = q_ref[0].reshape(rows, QK)
    m_ref[...] = jnp.full((rows, 1), NEG, F32)
    l_ref[...] = jnp.zeros((rows, 1), F32)
    acc_ref[...] = jnp.zeros((rows, KV_RANK), F32)
    n_full = (qi * tq) // tk

    def step(ki, masked):
        kb = k_ref[0, pl.ds(pl.multiple_of(ki * tk, tk), tk), :]
        s = _dot_nt(qm, kb)
        if masked:
            kpos = ki * tk + lax.broadcasted_iota(jnp.int32, (rows, tk), 1)
            qpos = qi * tq + lax.broadcasted_iota(jnp.int32, (rows, tk), 0) % tq
            s = jnp.where(kpos <= qpos, s, NEG)
        m_old = m_ref[...]
        m_new = jnp.maximum(m_old, jnp.max(s, axis=-1, keepdims=True))
        alpha = jnp.exp(m_old - m_new)
        p = jnp.exp(s - m_new)
        l_ref[...] = alpha * l_ref[...] + jnp.sum(p, axis=-1, keepdims=True)
        acc_ref[...] = alpha * acc_ref[...] + _dot(p.astype(BF), kb[:, :KV_RANK])
        m_ref[...] = m_new

    def body(ki, carry):
        step(ki, False)
        return carry

    lax.fori_loop(0, n_full, body, 0)
    step(n_full, True)
    o = acc_ref[...] / l_ref[...]
    o_ref[0] = o.reshape(MLA_HEADS, tq, KV_RANK).astype(BF)


def _flash(q, kcat):
    nb, _, seq, _ = q.shape
    tq = min(FLASH_TQ, seq)
    tk = min(FLASH_TK, seq)
    rows = MLA_HEADS * tq
    return pl.pallas_call(
        functools.partial(_flash_kernel, tq=tq, tk=tk),
        grid=(nb, seq // tq),
        in_specs=[
            pl.BlockSpec((1, MLA_HEADS, tq, QK), lambda b, i: (b, 0, i, 0)),
            pl.BlockSpec((1, seq, QK), lambda b, i: (b, 0, 0)),
        ],
        out_specs=pl.BlockSpec((1, MLA_HEADS, tq, KV_RANK), lambda b, i: (b, 0, i, 0)),
        out_shape=jax.ShapeDtypeStruct((nb, MLA_HEADS, seq, KV_RANK), BF),
        scratch_shapes=[pltpu.VMEM((rows, 1), F32), pltpu.VMEM((rows, 1), F32),
                        pltpu.VMEM((rows, KV_RANK), F32)],
        compiler_params=_cp(("parallel", "arbitrary"), vmem_mb=48),
        name="mla_flash",
    )(q, kcat)


def _decode_kernel(pt_ref, q_ref, nk_ref, *refs, dec_seq):
    del pt_ref
    npg = PAGES_PER_STEP
    ck_refs = refs[:npg]
    kp_refs = refs[npg:2 * npg]
    o_ref, m_ref, l_ref, acc_ref = refs[2 * npg:]
    j = pl.program_id(1)
    rows = MLA_HEADS * dec_seq

    @pl.when(j == 0)
    def _():
        m_ref[...] = jnp.full((rows, 1), NEG, F32)
        l_ref[...] = jnp.zeros((rows, 1), F32)
        acc_ref[...] = jnp.zeros((rows, KV_RANK), F32)

    qm = q_ref[0]
    ql = qm[:, :KV_RANK]
    qp = qm[:, KV_RANK:]

    def update(s, vals):
        m_old = m_ref[...]
        m_new = jnp.maximum(m_old, jnp.max(s, axis=-1, keepdims=True))
        alpha = jnp.exp(m_old - m_new)
        p = jnp.exp(s - m_new)
        l_ref[...] = alpha * l_ref[...] + jnp.sum(p, axis=-1, keepdims=True)
        acc_ref[...] = alpha * acc_ref[...] + _dot(p.astype(BF), vals)
        m_ref[...] = m_new

    ck = jnp.concatenate([r[0, 0].astype(BF) for r in ck_refs], axis=0)
    kp = jnp.concatenate([r[0, 0].astype(BF) for r in kp_refs], axis=0)
    update(_dot_nt(ql, ck) + _dot_nt(qp, kp), ck)

    @pl.when(j == pl.num_programs(1) - 1)
    def _():
        nk = nk_ref[0]
        s = _dot_nt(qm, nk)
        kt = lax.broadcasted_iota(jnp.int32, (rows, NEW_KEY_ROWS), 1)
        qt = lax.broadcasted_iota(jnp.int32, (rows, NEW_KEY_ROWS), 0) % dec_seq
        update(jnp.where(kt <= qt, s, NEG), nk[:, :KV_RANK])
        o_ref[0] = acc_ref[...] / l_ref[...]


def _decode(page_table, q, newk, cache_ckv, cache_kpe, layer, dec_seq):
    nb, n_pages = page_table.shape
    npg = PAGES_PER_STEP
    rows = MLA_HEADS * dec_seq

    def ck_spec(k):
        return pl.BlockSpec((1, 1, PAGE, KV_RANK), lambda b, j, pt: (layer, pt[b, j * npg + k], 0, 0))

    def kp_spec(k):
        return pl.BlockSpec((1, 1, PAGE, ROPE), lambda b, j, pt: (layer, pt[b, j * npg + k], 0, 0))

    grid_spec = pltpu.PrefetchScalarGridSpec(
        num_scalar_prefetch=1,
        grid=(nb, n_pages // npg),
        in_specs=[
            pl.BlockSpec((1, rows, QK), lambda b, j, pt: (b, 0, 0)),
            pl.BlockSpec((1, NEW_KEY_ROWS, QK), lambda b, j, pt: (b, 0, 0)),
        ] + [ck_spec(k) for k in range(npg)] + [kp_spec(k) for k in range(npg)],
        out_specs=pl.BlockSpec((1, rows, KV_RANK), lambda b, j, pt: (b, 0, 0)),
        scratch_shapes=[pltpu.VMEM((rows, 1), F32), pltpu.VMEM((rows, 1), F32),
                        pltpu.VMEM((rows, KV_RANK), F32)],
    )
    return pl.pallas_call(
        functools.partial(_decode_kernel, dec_seq=dec_seq),
        grid_spec=grid_spec,
        out_shape=jax.ShapeDtypeStruct((nb, rows, KV_RANK), F32),
        compiler_params=_cp(("parallel", "arbitrary")),
        name="mla_decode",
    )(page_table, q, newk, *([cache_ckv] * npg), *([cache_kpe] * npg))


def _mla_out_kernel(o_ref, wuv_ref, wo_ref, r_ref, out_ref):
    parts = [_dot(o_ref[0, h], wuv_ref[h]).astype(BF) for h in range(MLA_HEADS)]
    out_ref[...] = r_ref[...] + _dot(jnp.concatenate(parts, axis=1), wo_ref[...])


def _mla_out(o_lat, wuv, wo, res):
    nb, _, seq, _ = o_lat.shape
    tm = min(ROW_TILE, seq)
    tpb = seq // tm
    t = nb * seq
    return pl.pallas_call(
        _mla_out_kernel,
        grid=(t // tm,),
        in_specs=[
            pl.BlockSpec((1, MLA_HEADS, tm, KV_RANK), lambda i: (i // tpb, 0, i % tpb, 0)),
            _full(wuv.shape), _full(wo.shape),
            pl.BlockSpec((tm, D), lambda i: (i, 0)),
        ],
        out_specs=pl.BlockSpec((tm, D), lambda i: (i, 0)),
        out_shape=jax.ShapeDtypeStruct((t, D), F32),
        compiler_params=_cp(("parallel",)),
        name="mla_out",
    )(o_lat, wuv, wo, res)


def _softmax_rows(s):
    m = jnp.max(s, axis=-1, keepdims=True)
    p = jnp.exp(s - m)
    return p / jnp.sum(p, axis=-1, keepdims=True)


def _mem_prompt_kernel(x_ref, g_ref, wq_ref, k_ref, v_ref, wo_ref, o_ref):
    x = x_ref[...]
    xn = _rms(x, g_ref[...]).astype(BF)
    qm = (_dot(xn, wq_ref[...]) * MEM_HD ** -0.5).astype(BF)
    km = k_ref[0].astype(BF)
    vm = v_ref[0].astype(BF)
    parts = []
    for h in range(MEM_HEADS):
        sl = slice(h * MEM_HD, (h + 1) * MEM_HD)
        p = _softmax_rows(_dot_nt(qm[:, sl], km[:, sl]))
        parts.append(_dot(p.astype(BF), vm[:, sl]).astype(BF))
    o_ref[...] = x + _dot(jnp.concatenate(parts, axis=1), wo_ref[...])


def _mem_prompt(x, g, wq, km, vm, wo):
    t = x.shape[0]
    nb, mt, _ = km.shape
    seq = t // nb
    tm = min(ROW_TILE, seq)
    tpb = seq // tm
    return pl.pallas_call(
        _mem_prompt_kernel,
        grid=(t // tm,),
        in_specs=[
            pl.BlockSpec((tm, D), lambda i: (i, 0)), _full((1, D)), _full(wq.shape),
            pl.BlockSpec((1, mt, MEM_INNER), lambda i: (i // tpb, 0, 0)),
            pl.BlockSpec((1, mt, MEM_INNER), lambda i: (i // tpb, 0, 0)),
            _full(wo.shape),
        ],
        out_specs=pl.BlockSpec((tm, D), lambda i: (i, 0)),
        out_shape=jax.ShapeDtypeStruct((t, D), F32),
        compiler_params=_cp(("parallel",)),
        name="mem_attn_prompt",
    )(x, g, wq, km, vm, wo)


MEM_ROWS = MEM_HEADS * SUBLANES


def _mem_sample_kernel(x_ref, g_ref, wq_ref, k_ref, v_ref, wo_ref, o_ref):
    ns = SAMPLES_PER_STEP
    x = x_ref[...].reshape(ns * MEM_ROWS, D)
    xn = _rms(x, g_ref[...]).astype(BF)
    qall = _dot(xn, wq_ref[...]) * MEM_HD ** -0.5
    head_of_row = lax.broadcasted_iota(jnp.int32, (MEM_ROWS, MEM_INNER), 0) // SUBLANES
    head_of_lane = lax.broadcasted_iota(jnp.int32, (MEM_ROWS, MEM_INNER), 1) // MEM_HD
    own = head_of_row == head_of_lane
    outs = []
    for s in range(ns):
        qs = jnp.where(own, qall[s * MEM_ROWS:(s + 1) * MEM_ROWS], 0.0).astype(BF)
        p = _softmax_rows(_dot_nt(qs, k_ref[0, s].astype(BF)))
        of = _dot(p.astype(BF), v_ref[0, s].astype(BF))
        outs.append(jnp.where(own, of, 0.0).astype(BF))
    part = _dot(jnp.concatenate(outs, axis=0), wo_ref[...])
    for s in range(ns):
        base = s * MEM_ROWS
        acc = x[base:base + SUBLANES]
        for h in range(MEM_HEADS):
            acc = acc + part[base + h * SUBLANES:base + (h + 1) * SUBLANES]
        o_ref[s] = acc


def _mem_sample(x32, g, wq, cache_k, cache_v, wo, layer):
    nb = x32.shape[0]
    mt = cache_k.shape[2]
    ns = SAMPLES_PER_STEP
    return pl.pallas_call(
        _mem_sample_kernel,
        grid=(nb // ns,),
        in_specs=[
            pl.BlockSpec((ns, MEM_ROWS, D), lambda i: (i, 0, 0)), _full((1, D)), _full(wq.shape),
            pl.BlockSpec((1, ns, mt, MEM_INNER), lambda i: (layer, i, 0, 0)),
            pl.BlockSpec((1, ns, mt, MEM_INNER), lambda i: (layer, i, 0, 0)),
            _full(wo.shape),
        ],
        out_specs=pl.BlockSpec((ns, SUBLANES, D), lambda i: (i, 0, 0)),
        out_shape=jax.ShapeDtypeStruct((nb, SUBLANES, D), F32),
        compiler_params=_cp(("parallel",), vmem_mb=48),
        name="mem_attn_sample",
    )(x32, g, wq, cache_k, cache_v, wo)


def _router_kernel(x_ref, g_ref, wr_ref, xn_ref, route_ref):
    xn = _rms(x_ref[...], g_ref[...])
    xn_ref[...] = xn
    lane = lax.broadcasted_iota(jnp.int32, (xn.shape[0], LANES), 1).astype(F32)
    lg = jnp.where(lane < N_EXP, _dot_f32(xn, wr_ref[...]), NEG)
    m1 = jnp.max(lg, axis=-1, keepdims=True)
    i1 = jnp.min(jnp.where(lg == m1, lane, float(LANES)), axis=-1, keepdims=True)
    lg2 = jnp.where(lane == i1, NEG, lg)
    m2 = jnp.max(lg2, axis=-1, keepdims=True)
    i2 = jnp.min(jnp.where(lg2 == m2, lane, float(LANES)), axis=-1, keepdims=True)
    e = jnp.exp(m2 - m1)
    g1 = 1.0 / (1.0 + e)
    g2 = e * g1
    route_ref[...] = jnp.where(lane == 0, i1, jnp.where(lane == 1, i2, jnp.where(lane == 2, g1,
                               jnp.where(lane == 3, g2, 0.0))))


def _router(x, g, wr_pad):
    t = x.shape[0]
    tm = min(ROW_TILE, t)
    return pl.pallas_call(
        _router_kernel,
        grid=(t // tm,),
        in_specs=[pl.BlockSpec((tm, D), lambda i: (i, 0)), _full((1, D)), _full(wr_pad.shape)],
        out_specs=[pl.BlockSpec((tm, D), lambda i: (i, 0)), pl.BlockSpec((tm, LANES), lambda i: (i, 0))],
        out_shape=[jax.ShapeDtypeStruct((t, D), F32), jax.ShapeDtypeStruct((t, LANES), F32)],
        compiler_params=_cp(("parallel",)),
        name="router",
    )(x, g, wr_pad)


GATHER_TILE = 256


def _row_copy(x_hbm, o_ref, sem, src_row, dst_row):
    return pltpu.make_async_copy(x_hbm.at[pl.ds(src_row, 1)], o_ref.at[pl.ds(dst_row, 1)], sem)


def _gather_kernel(idx_ref, x_hbm, o_ref, sem):
    tg = o_ref.shape[0]
    base = pl.program_id(0) * tg

    def start(r, c):
        _row_copy(x_hbm, o_ref, sem, idx_ref[base + r], r).start()
        return c

    lax.fori_loop(0, tg, start, 0)

    def wait(r, c):
        _row_copy(x_hbm, o_ref, sem, 0, r).wait()
        return c

    lax.fori_loop(0, tg, wait, 0)


def _gather_rows(x, idx):
    n = idx.shape[0]
    width = x.shape[1]
    tg = min(GATHER_TILE, n)
    grid_spec = pltpu.PrefetchScalarGridSpec(
        num_scalar_prefetch=1,
        grid=(n // tg,),
        in_specs=[pl.BlockSpec(memory_space=pl.ANY)],
        out_specs=pl.BlockSpec((tg, width), lambda i, idx_ref: (i, 0)),
        scratch_shapes=[pltpu.SemaphoreType.DMA],
    )
    return pl.pallas_call(
        _gather_kernel,
        grid_spec=grid_spec,
        out_shape=jax.ShapeDtypeStruct((n, width), x.dtype),
        compiler_params=_cp(("arbitrary",)),
        name="gather_rows",
    )(idx, x)


MOE_HALF = D_FFE // 2
MOE_CHUNK = 256


def _moe_ffn_kernel(te_ref, nused_ref, x_ref, gate_ref, wg_ref, wu_ref, wd_ref, o_ref, hid_ref):
    del te_ref
    i = pl.program_id(0)
    j = pl.program_id(1)

    @pl.when(i < nused_ref[0])
    def _():
        xb = x_ref[...].astype(BF)
        for c in range(MOE_HALF // MOE_CHUNK):
            lo = c * MOE_CHUNK
            gate = _dot(xb, wg_ref[0, :, lo:lo + MOE_CHUNK])
            up = _dot(xb, wu_ref[0, :, lo:lo + MOE_CHUNK])
            hid_ref[:, lo:lo + MOE_CHUNK] = (_silu(gate) * up).astype(BF)
        part = _dot(hid_ref[...], wd_ref[0]) * gate_ref[...]

        @pl.when(j == 0)
        def _():
            o_ref[...] = part

        @pl.when(j != 0)
        def _():
            o_ref[...] += part

    @pl.when(i >= nused_ref[0])
    def _():
        o_ref[...] = jnp.zeros_like(o_ref)


def _moe_ffn(tile_expert, nused, xs, gates, wgu, wd, *, tm):
    npad = xs.shape[0]
    nh = D_FFE // MOE_HALF
    grid_spec = pltpu.PrefetchScalarGridSpec(
        num_scalar_prefetch=2,
        grid=(npad // tm, nh),
        in_specs=[
            pl.BlockSpec((tm, D), lambda i, j, te, nu: (i, 0)),
            pl.BlockSpec((tm, 1), lambda i, j, te, nu: (i, 0)),
            pl.BlockSpec((1, D, MOE_HALF), lambda i, j, te, nu: (te[i], 0, j)),
            pl.BlockSpec((1, D, MOE_HALF), lambda i, j, te, nu: (te[i], 0, nh + j)),
            pl.BlockSpec((1, MOE_HALF, D), lambda i, j, te, nu: (te[i], j, 0)),
        ],
        out_specs=pl.BlockSpec((tm, D), lambda i, j, te, nu: (i, 0)),
        scratch_shapes=[pltpu.VMEM((tm, MOE_HALF), BF)],
    )
    return pl.pallas_call(
        _moe_ffn_kernel,
        grid_spec=grid_spec,
        out_shape=jax.ShapeDtypeStruct((npad, D), F32),
        compiler_params=_cp(("parallel", "arbitrary"), vmem_mb=56),
        name="moe_ffn",
    )(tile_expert, nused, xs, gates, wgu, wgu, wd)


def _combine_kernel(h_ref, y0_ref, y1_ref, g_ref, o_ref, *, final):
    out = h_ref[...] + y0_ref[...] + y1_ref[...]
    if final:
        out = _rms(out, g_ref[...])
    o_ref[...] = out


def _combine(h, y2, g, *, final):
    t = h.shape[0]
    tm = min(ROW_TILE, t)
    nt = t // tm
    return pl.pallas_call(
        functools.partial(_combine_kernel, final=final),
        grid=(nt,),
        in_specs=[
            pl.BlockSpec((tm, D), lambda i: (i, 0)),
            pl.BlockSpec((tm, D), lambda i: (i, 0)),
            pl.BlockSpec((tm, D), lambda i: (nt + i, 0)),
            _full((1, D)),
        ],
        out_specs=pl.BlockSpec((tm, D), lambda i: (i, 0)),
        out_shape=jax.ShapeDtypeStruct((t, D), F32),
        compiler_params=_cp(("parallel",)),
        name="moe_combine",
    )(h, y2, y2, g)


def _moe(h, g, wr_pad, wgu, wd, final_g, *, final):
    t = h.shape[0]
    tm = min(ROW_TILE, t)
    xn, route = _router(h, g, wr_pad)
    eidx = route[:, :2].astype(jnp.int32)
    gate = route[:, 2:4]
    e_flat = eidx.T.reshape(-1)
    g_flat = gate.T.reshape(-1)
    tok = jnp.tile(jnp.arange(t, dtype=jnp.int32), 2)
    onehot = (e_flat[:, None] == jnp.arange(N_EXP, dtype=jnp.int32)[None, :]).astype(jnp.int32)
    csum = jnp.cumsum(onehot, axis=0)
    counts = csum[-1]
    rank = jnp.sum(onehot * csum, axis=1) - 1
    padded = ((counts + tm - 1) // tm) * tm
    ends = jnp.cumsum(padded)
    starts = ends - padded
    dest = starts[e_flat] + rank
    n_tiles = (2 * t) // tm + N_EXP
    npad = n_tiles * tm
    row_tok = jnp.zeros((npad,), jnp.int32).at[dest].set(tok)
    row_gate = jnp.zeros((npad,), F32).at[dest].set(g_flat)
    tile_start = jnp.arange(n_tiles, dtype=jnp.int32) * tm
    tile_expert = jnp.minimum(jnp.sum((tile_start[:, None] >= ends[None, :]).astype(jnp.int32), axis=1),
                              N_EXP - 1).astype(jnp.int32)
    nused = (ends[-1] // tm).astype(jnp.int32).reshape(1)
    xs = _gather_rows(xn, row_tok)
    ys = _moe_ffn(tile_expert, nused, xs, row_gate[:, None], wgu, wd, tm=tm)
    y2 = _gather_rows(ys, dest)
    return _combine(h, y2, final_g, final=final)


def _rope_tables(pos):
    half = ROPE // 2
    inv_freq = ROPE_BASE ** (-jnp.arange(half, dtype=F32) / half)
    ang = pos.astype(F32)[:, None] * inv_freq
    cos = jnp.cos(ang)
    sin = jnp.sin(ang)
    cc = jnp.concatenate([cos, cos], axis=-1)
    ss = jnp.concatenate([-sin, sin], axis=-1)
    return jnp.tile(cc, (1, MLA_HEADS)), jnp.tile(ss, (1, MLA_HEADS))


def _even_layer(h, g, prm, cinit, sinit, *, nb, sample):
    t = h.shape[0]
    seq = t // nb
    proj = _rms_matmul(h, g, prm["w_zxu"], tn=_largest_tile(ZXU, 1536))
    dtp = _rms_matmul(h, g, prm["w_dt"], tn=LANES, precise=True)
    proj = proj.reshape(nb, seq, ZXU)
    dtp = dtp.reshape(nb, seq, LANES)
    if sample:
        pad = ((0, 0), (0, SUBLANES - seq), (0, 0))
        proj = jnp.pad(proj, pad)
        dtp = jnp.pad(dtp, pad)
        lb, lc = SUBLANES, seq
    else:
        lb, lc = CHUNK, CHUNK
    cinit8 = jnp.pad(cinit, ((0, 0), (SUBLANES - (SSD_CONV - 1), 0), (0, 0)))
    outs = _even_mixer(proj, dtp, cinit8, sinit.reshape(nb, SSD_INNER, SSD_STATE), prm,
                       lb=lb, lc=lc, want_v=sample)
    ymix, cout, sout = outs[:3]
    v = None
    if sample:
        ymix = ymix[:, :seq]
        v = outs[3][:, :seq]
    h = _matmul_res(ymix.reshape(t, SSD_INNER + GMLP_WIDTH), prm["w_out"], h)
    return h, cout, sout.reshape(nb, SSD_HEADS, SSD_HEAD_DIM, SSD_STATE), v


def _prep_even(i, w_in, conv_w, conv_b, dt_bias, a_log, d_skip, ssd_gain, ln_g, ln_b, ws, bs, w_out):
    w = w_in[i]
    o1 = SSD_INNER + CONV_DIM
    w_zxu = jnp.concatenate([w[:, :o1], w[:, o1 + SSD_HEADS:]], axis=1).astype(BF)
    w_dt = jnp.pad(w[:, o1:o1 + SSD_HEADS], ((0, 0), (0, LANES - SSD_HEADS)))
    padl = (0, LANES - SSD_HEADS)
    return dict(
        w_zxu=w_zxu, w_dt=w_dt,
        conv_w=jnp.pad(conv_w[i], ((0, SUBLANES - SSD_CONV), (0, 0))),
        conv_b=conv_b[i][None, :],
        dt_bias=jnp.pad(dt_bias[i], padl)[None, :],
        a_log=jnp.pad(a_log[i], padl)[None, :],
        d_skip=jnp.repeat(d_skip[i], SSD_HEAD_DIM)[None, :],
        ssd_gain=ssd_gain[i][None, :],
        ln_g=ln_g[i][None, :], ln_b=ln_b[i][None, :],
        ws=ws[i], bst=bs[i].T,
        w_out=w_out[i].astype(BF),
    )


def _prep_mla(i, w_down, q_gain, kv_gain, w_uq, w_uk, w_uv, w_o):
    wd = w_down[i]
    wk = wd[:, Q_RANK + KV_RANK:]
    half = ROPE // 2
    rot = lambda a: jnp.concatenate([a[..., half:], a[..., :half]], axis=-1)
    uq = w_uq[i]
    uq_pe = uq[:, :, NOPE:]
    return dict(
        wdq=wd[:, :Q_RANK].astype(BF),
        wdkv=wd[:, Q_RANK:Q_RANK + KV_RANK].astype(BF),
        wdk2=jnp.concatenate([wk, rot(wk)], axis=1).astype(BF),
        q_gain=q_gain[i][None, :], kv_gain=kv_gain[i][None, :],
        wqn=uq[:, :, :NOPE].reshape(Q_RANK, MLA_HEADS * NOPE).astype(BF),
        wqp=jnp.concatenate([uq_pe.reshape(Q_RANK, -1), rot(uq_pe).reshape(Q_RANK, -1)], axis=1).astype(BF),
        wuk=jnp.transpose(w_uk[i], (1, 2, 0)).astype(BF),
        wuv=jnp.transpose(w_uv[i], (1, 0, 2)).astype(BF),
        wo=w_o[i].astype(BF),
    )


def kernel(x_prompt, x_sample, state_ssd, state_conv, cache_mla_ckv, cache_mla_kpe, cache_mem_k, cache_mem_v, page_table, mem_prompt, mix_norm, w_in, conv_w, conv_b, dt_bias, a_log, d_skip, ssd_gain, gmlp_ln_g, gmlp_ln_b, gmlp_ws, gmlp_bs, w_out_even, w_mla_down, mla_q_gain, mla_kv_gain, w_mla_uq, w_mla_uk, w_mla_uv, w_mla_o, xattn_norm, mem_norm, w_mem_q, w_mem_k, w_mem_v, w_mem_o, ffn_norm, w_ffn_gu, w_ffn_down, w_router, w_exp_gu, w_exp_down, final_norm):
    nbp, seq, _ = x_prompt.shape
    nbs, dseq, _ = x_sample.shape
    depth = mix_norm.shape[0]
    past = page_table.shape[1] * PAGE
    mt = mem_prompt.shape[1]
    hp = x_prompt.reshape(nbp * seq, D)
    hs = x_sample.reshape(nbs * dseq, D)
    cos_p, sin_p = _rope_tables(jnp.arange(seq, dtype=jnp.int32))
    cos_s, sin_s = _rope_tables(past + jnp.arange(dseq, dtype=jnp.int32))
    cos_s = jnp.tile(cos_s, (nbs, 1))
    sin_s = jnp.tile(sin_s, (nbs, 1))
    cache_k4 = cache_mem_k.reshape(depth, nbs, mt, MEM_INNER)
    cache_v4 = cache_mem_v.reshape(depth, nbs, mt, MEM_INNER)
    final_g = final_norm[None, :]

    p_ssd, p_conv, p_ckv, p_kpe, p_mk, p_mv = [], [], [], [], [], []
    s_ssd, s_conv, s_v, s_ckv, s_kpe = [], [], [], [], []
    for l in range(depth):
        i = l // 2
        g_mix = mix_norm[l][None, :]
        if l % 2 == 0:
            prm = _prep_even(i, w_in, conv_w, conv_b, dt_bias, a_log, d_skip, ssd_gain, gmlp_ln_g,
                             gmlp_ln_b, gmlp_ws, gmlp_bs, w_out_even)
            buf0 = jnp.zeros((nbp, SSD_CONV - 1, CONV_DIM), F32)
            h00 = jnp.zeros((nbp, SSD_HEADS, SSD_HEAD_DIM, SSD_STATE), F32)
            hp, buf_p, ssd_p, _ = _even_layer(hp, g_mix, prm, buf0, h00, nb=nbp, sample=False)
            hs, buf_s, ssd_s, v_s = _even_layer(hs, g_mix, prm, state_conv[i], state_ssd[i], nb=nbs,
                                                sample=True)
            p_ssd.append(ssd_p)
            p_conv.append(buf_p)
            s_ssd.append(ssd_s)
            s_conv.append(buf_s)
            s_v.append(v_s)
        else:
            prm = _prep_mla(i, w_mla_down, mla_q_gain, mla_kv_gain, w_mla_uq, w_mla_uk, w_mla_uv, w_mla_o)
            ckv, kpe, kcat, q = _mla_proj(hp, g_mix, prm, cos_p, sin_p, nb=nbp)
            o_lat = _flash(q, kcat.reshape(nbp, seq, QK))
            hp = _mla_out(o_lat, prm["wuv"], prm["wo"], hp)
            p_ckv.append(ckv.reshape(nbp, seq, KV_RANK))
            p_kpe.append(kpe.reshape(nbp, seq, ROPE))

            ckv_s, kpe_s, kcat_s, q_s = _mla_proj(hs, g_mix, prm, cos_s, sin_s, nb=1)
            q_s = q_s[0].reshape(MLA_HEADS, nbs, dseq, QK).transpose(1, 0, 2, 3).reshape(nbs, MLA_HEADS * dseq, QK)
            newk = jnp.pad(kcat_s.reshape(nbs, dseq, QK), ((0, 0), (0, NEW_KEY_ROWS - dseq), (0, 0)))
            o_s = _decode(page_table, q_s, newk, cache_mla_ckv, cache_mla_kpe, i, dseq)
            o_s = o_s.reshape(nbs, MLA_HEADS, dseq, KV_RANK).transpose(1, 0, 2, 3)
            o_s = o_s.reshape(1, MLA_HEADS, nbs * dseq, KV_RANK).astype(BF)
            hs = _mla_out(o_s, prm["wuv"], prm["wo"], hs)
            s_ckv.append(ckv_s.reshape(nbs, dseq, KV_RANK))
            s_kpe.append(kpe_s.reshape(nbs, dseq, ROPE))

        wkv = jnp.concatenate([w_mem_k[l], w_mem_v[l]], axis=1).astype(BF)
        kv = _rms_matmul(mem_prompt.reshape(nbp * mt, D), mem_norm[l][None, :], wkv, tn=MEM_INNER)
        mk_p = kv[:, :MEM_INNER].reshape(nbp, mt, MEM_INNER)
        mv_p = kv[:, MEM_INNER:].reshape(nbp, mt, MEM_INNER)
        g_x = xattn_norm[l][None, :]
        wq = w_mem_q[l].astype(BF)
        wo = w_mem_o[l].astype(BF)
        hp = _mem_prompt(hp, g_x, wq, mk_p, mv_p, wo)
        x32 = jnp.pad(hs.reshape(nbs, dseq, D), ((0, 0), (0, SUBLANES - dseq), (0, 0)))
        x32 = jnp.tile(x32, (1, MEM_HEADS, 1))
        hs = _mem_sample(x32, g_x, wq, cache_k4, cache_v4, wo, l)[:, :dseq].reshape(nbs * dseq, D)
        p_mk.append(mk_p.reshape(nbp, mt, MEM_HEADS, MEM_HD))
        p_mv.append(mv_p.reshape(nbp, mt, MEM_HEADS, MEM_HD))

        g_f = ffn_norm[l][None, :]
        if l % 2 == 0:
            wgu = w_ffn_gu[i].astype(BF)
            wd = w_ffn_down[i].astype(BF)
            hp = _ffn(hp, g_f, wgu, wd)
            hs = _ffn(hs, g_f, wgu, wd)
        else:
            wr = jnp.pad(w_router[i], ((0, 0), (0, LANES - N_EXP)))
            wgu = w_exp_gu[i].astype(BF)
            wd = w_exp_down[i].astype(BF)
            final = l == depth - 1
            hp = _moe(hp, g_f, wr, wgu, wd, final_g, final=final)
            hs = _moe(hs, g_f, wr, wgu, wd, final_g, final=final)
    if depth % 2 == 1:
        raise NotImplementedError("the final norm is fused into the last routed-expert layer")
    y_prompt = hp.reshape(nbp, seq, D)
    y_sample = hs.reshape(nbs, dseq, D)
    return (y_prompt, y_sample,
            jnp.stack(p_ssd), jnp.stack(p_conv), jnp.stack(p_ckv), jnp.stack(p_kpe),
            jnp.stack(p_mk), jnp.stack(p_mv),
            jnp.stack(s_ssd), jnp.stack(s_conv), jnp.stack(s_v), jnp.stack(s_ckv), jnp.stack(s_kpe))
```

```python
import functools

import jax
import jax.numpy as jnp
from jax import lax
from jax.experimental import pallas as pl
from jax.experimental.pallas import tpu as pltpu

F32 = jnp.float32
BF = jnp.bfloat16
EPS = 1e-6
NEG = -1e30

D = 1024
SSD_HEADS = 16
SSD_HEAD_DIM = 64
SSD_INNER = SSD_HEADS * SSD_HEAD_DIM
SSD_GROUPS = 2
SSD_STATE = 128
SSD_CONV = 4
CONV_DIM = SSD_INNER + 2 * SSD_GROUPS * SSD_STATE
GMLP_GROUPS = 8
GMLP_WIDTH = 1024
CHUNK = 128
ZXU = SSD_INNER + CONV_DIM + 2 * GMLP_WIDTH
MLA_HEADS = 8
NOPE = 128
ROPE = 64
MLA_V = 128
Q_RANK = 256
KV_RANK = 256
MLA_SCALE = (NOPE + ROPE) ** -0.5
QK = KV_RANK + ROPE
ROPE_BASE = 10000.0
MEM_HEADS = 4
MEM_HD = 128
MEM_INNER = MEM_HEADS * MEM_HD
D_FF = 2816
N_EXP = 8
D_FFE = 3584
PAGE = 128
LANES = 128
SUBLANES = 8
ROW_TILE = 512
NEW_KEY_ROWS = 16
SAMPLES_PER_STEP = 8


def _cp(sem, vmem_mb=None):
    kw = dict(dimension_semantics=sem)
    if vmem_mb is not None:
        kw["vmem_limit_bytes"] = vmem_mb * 1024 * 1024
    return pltpu.CompilerParams(**kw)


def _rms(x, g):
    return x * lax.rsqrt(jnp.mean(x * x, axis=-1, keepdims=True) + EPS) * g


def _dot(a, b):
    return jnp.dot(a, b, preferred_element_type=F32)


def _dot_nt(a, b):
    return lax.dot_general(a, b, (((1,), (1,)), ((), ())), preferred_element_type=F32)


def _dot_f32(a, b):
    return jnp.dot(a, b, preferred_element_type=F32, precision=lax.Precision.HIGHEST)


def _silu(x):
    return x * jax.nn.sigmoid(x)


def _full(shape):
    n = len(shape)
    return pl.BlockSpec(shape, lambda *_: (0,) * n)


def _largest_tile(n, cap):
    best = LANES
    for t in range(LANES, cap + 1, LANES):
        if n % t == 0:
            best = t
    return best


def _rms_matmul_kernel(x_ref, g_ref, w_ref, o_ref, xn_ref, *, precise):
    @pl.when(pl.program_id(1) == 0)
    def _():
        xn_ref[...] = _rms(x_ref[...], g_ref[...]).astype(xn_ref.dtype)

    if precise:
        o_ref[...] = _dot_f32(xn_ref[...], w_ref[...]).astype(o_ref.dtype)
    else:
        o_ref[...] = _dot(xn_ref[...], w_ref[...]).astype(o_ref.dtype)


def _rms_matmul(x, g, w, *, tn, out_dtype=F32, precise=False):
    t, k = x.shape
    n = w.shape[1]
    tm = min(ROW_TILE, t)
    return pl.pallas_call(
        functools.partial(_rms_matmul_kernel, precise=precise),
        grid=(t // tm, n // tn),
        in_specs=[
            pl.BlockSpec((tm, k), lambda i, j: (i, 0)),
            pl.BlockSpec((1, k), lambda i, j: (0, 0)),
            pl.BlockSpec((k, tn), lambda i, j: (0, j)),
        ],
        out_specs=pl.BlockSpec((tm, tn), lambda i, j: (i, j)),
        out_shape=jax.ShapeDtypeStruct((t, n), out_dtype),
        scratch_shapes=[pltpu.VMEM((tm, k), F32 if precise else BF)],
        compiler_params=_cp(("parallel", "arbitrary")),
        name="rms_matmul",
    )(x, g, w)


def _matmul_res_kernel(a_ref, w_ref, r_ref, o_ref):
    o_ref[...] = r_ref[...] + _dot(a_ref[...].astype(BF), w_ref[...])


def _matmul_res(a, w, res, *, tn=512):
    t, k = a.shape
    n = w.shape[1]
    tm = min(ROW_TILE, t)
    return pl.pallas_call(
        _matmul_res_kernel,
        grid=(t // tm, n // tn),
        in_specs=[
            pl.BlockSpec((tm, k), lambda i, j: (i, 0)),
            pl.BlockSpec((k, tn), lambda i, j: (0, j)),
            pl.BlockSpec((tm, tn), lambda i, j: (i, j)),
        ],
        out_specs=pl.BlockSpec((tm, tn), lambda i, j: (i, j)),
        out_shape=jax.ShapeDtypeStruct((t, n), F32),
        compiler_params=_cp(("parallel", "arbitrary")),
        name="matmul_res",
    )(a, w, res)


FF_CHUNK = 256


def _ffn_kernel(x_ref, g_ref, wgu_ref, wd_ref, o_ref, hid_ref, *, ff):
    x = x_ref[...]
    xn = _rms(x, g_ref[...]).astype(BF)
    for c in range(ff // FF_CHUNK):
        lo = c * FF_CHUNK
        gate = _dot(xn, wgu_ref[:, lo:lo + FF_CHUNK])
        up = _dot(xn, wgu_ref[:, ff + lo:ff + lo + FF_CHUNK])
        hid_ref[:, lo:lo + FF_CHUNK] = (_silu(gate) * up).astype(BF)
    o_ref[...] = x + _dot(hid_ref[...], wd_ref[...])


def _ffn(x, g, wgu, wd):
    t = x.shape[0]
    ff = wd.shape[0]
    tm = min(ROW_TILE, t)
    return pl.pallas_call(
        functools.partial(_ffn_kernel, ff=ff),
        grid=(t // tm,),
        in_specs=[
            pl.BlockSpec((tm, D), lambda i: (i, 0)),
            _full((1, D)),
            pl.BlockSpec((D, 2 * ff), lambda i: (0, 0), pipeline_mode=pl.Buffered(1)),
            pl.BlockSpec((ff, D), lambda i: (0, 0), pipeline_mode=pl.Buffered(1)),
        ],
        out_specs=pl.BlockSpec((tm, D), lambda i: (i, 0)),
        out_shape=jax.ShapeDtypeStruct((t, D), F32),
        scratch_shapes=[pltpu.VMEM((tm, ff), BF)],
        compiler_params=_cp(("parallel",), vmem_mb=48),
        name="ffn",
    )(x, g, wgu, wd)


def _softplus(x):
    return jnp.maximum(x, 0.0) + jnp.log1p(jnp.exp(-jnp.abs(x)))


def _gelu_tanh(x):
    return 0.5 * x * (1.0 + jnp.tanh(0.7978845608028654 * (x + 0.044715 * (x * x * x))))


def _even_kernel(proj_ref, dt_ref, cinit_ref, sinit_ref, cw_ref, cb_ref, dtb_ref, alog_ref,
                 dsk_ref, sg_ref, lng_ref, lnb_ref, ws_ref, bst_ref,
                 ymix_ref, cout_ref, sout_ref, v_ref, ext_ref, ht_ref, *, lb, lc):
    q = CHUNK
    c = pl.program_id(1)
    last = pl.num_programs(1) - 1

    @pl.when(c == 0)
    def _():
        ext_ref[0:SUBLANES, :] = cinit_ref[0]
        ht_ref[...] = sinit_ref[0, 0].T

    if lb == q:
        p = proj_ref[0]
        dtr = dt_ref[0]
    else:
        p = jnp.concatenate([proj_ref[0], jnp.zeros((q - lb, ZXU), F32)], axis=0)
        dtr = jnp.concatenate([dt_ref[0], jnp.zeros((q - lb, LANES), F32)], axis=0)
    z = p[:, :SSD_INNER]
    xbc_raw = p[:, SSD_INNER:SSD_INNER + CONV_DIM]
    uv = p[:, SSD_INNER + CONV_DIM:]

    ext_ref[SUBLANES:SUBLANES + q, :] = xbc_raw
    conv = (cb_ref[...] + cw_ref[0:1, :] * ext_ref[5:5 + q, :] + cw_ref[1:2, :] * ext_ref[6:6 + q, :]
            + cw_ref[2:3, :] * ext_ref[7:7 + q, :] + cw_ref[3:4, :] * xbc_raw)

    @pl.when(c == last)
    def _():
        cout_ref[0] = ext_ref[SUBLANES + lc - 3:SUBLANES + lc, :]

    ext_ref[0:SUBLANES, :] = ext_ref[q:q + SUBLANES, :]

    xbc = _silu(conv)
    xs = xbc[:, :SSD_INNER]
    gw = SSD_STATE
    bm = [xbc[:, SSD_INNER + g * gw:SSD_INNER + (g + 1) * gw] for g in range(SSD_GROUPS)]
    cm = [xbc[:, SSD_INNER + (SSD_GROUPS + g) * gw:SSD_INNER + (SSD_GROUPS + g + 1) * gw]
          for g in range(SSD_GROUPS)]

    row = lax.broadcasted_iota(jnp.int32, (q, q), 0)
    col = lax.broadcasted_iota(jnp.int32, (q, q), 1)
    causal = row >= col

    dt = _softplus(dtr + dtb_ref[...])
    if lc < q:
        dt = jnp.where(lax.broadcasted_iota(jnp.int32, (q, LANES), 0) < lc, dt, 0.0)
    a = dt * (-jnp.exp(alog_ref[...]))
    a_cum = _dot_f32(causal.astype(F32), a)
    a_cum_t = a_cum.T
    a_last = a_cum[q - 1:q, :]
    decay_end = jnp.exp(a_last - a_cum)
    ea = jnp.exp(a_cum)
    chunk_decay = jnp.exp(a_last)

    cmb = [m.astype(BF) for m in cm]
    cb = [_dot_nt(cmb[g], bm[g].astype(BF)) for g in range(SSD_GROUPS)]
    bt = [bm[g].T.astype(BF) for g in range(SSD_GROUPS)]
    heads_per_group = SSD_HEADS // SSD_GROUPS
    ys = []
    for r in range(SSD_HEADS):
        g = r // heads_per_group
        lo = r * SSD_HEAD_DIM
        hi = lo + SSD_HEAD_DIM
        seg = a_cum[:, r:r + 1] - a_cum_t[r:r + 1, :]
        lmat = jnp.where(causal, jnp.exp(jnp.minimum(seg, 0.0)), 0.0)
        m = (cb[g] * lmat).astype(BF)
        xh = xs[:, lo:hi]
        xdt = xh * dt[:, r:r + 1]
        y_diag = _dot(m, xdt.astype(BF))
        h_old = ht_ref[:, lo:hi]
        y_off = _dot(cmb[g], h_old.astype(BF)) * ea[:, r:r + 1]
        ys.append(y_diag + y_off + dsk_ref[:, lo:hi] * xh)
        xd = (xdt * decay_end[:, r:r + 1]).astype(BF)
        ht_ref[:, lo:hi] = h_old * chunk_decay[:, r:r + 1] + _dot(bt[g], xd)
    y = jnp.concatenate(ys, axis=1) * _silu(z)
    gi = SSD_INNER // SSD_GROUPS
    yn = [_rms(y[:, g * gi:(g + 1) * gi], sg_ref[:, g * gi:(g + 1) * gi]) for g in range(SSD_GROUPS)]
    ymix_ref[0, :, 0:SSD_INNER] = jnp.concatenate(yn, axis=1)[:lb].astype(BF)

    @pl.when(c == last)
    def _():
        sout_ref[0] = ht_ref[...].T

    uvg = _gelu_tanh(uv)
    u = uvg[:, :GMLP_WIDTH]
    v = uvg[:, GMLP_WIDTH:]
    mu = jnp.mean(v, axis=-1, keepdims=True)
    vc = v - mu
    vn = vc * lax.rsqrt(jnp.mean(vc * vc, axis=-1, keepdims=True) + EPS) * lng_ref[...] + lnb_ref[...]
    if v_ref is not None:
        v_ref[0] = vn[:lb]
    gd = GMLP_WIDTH // GMLP_GROUPS
    yb = []
    for g in range(GMLP_GROUPS):
        wt = jnp.where(causal, ws_ref[g], 0.0).astype(BF)
        sp = _dot(wt, vn[:, g * gd:(g + 1) * gd].astype(BF)) + bst_ref[:, g:g + 1]
        yb.append(u[:, g * gd:(g + 1) * gd] * sp)
    ymix_ref[0, :, SSD_INNER:SSD_INNER + GMLP_WIDTH] = jnp.concatenate(yb, axis=1)[:lb].astype(BF)


def _even_kernel_no_v(*refs, lb, lc):
    n_in = 14
    ins, outs, scr = refs[:n_in], refs[n_in:n_in + 3], refs[n_in + 3:]
    _even_kernel(*ins, *outs, None, *scr, lb=lb, lc=lc)


def _even_mixer(proj, dtp, cinit8, sinit, prm, *, lb, lc, want_v, layer):
    b, lp, _ = proj.shape
    nchunks = lp // lb
    par = [prm["conv_w"], prm["conv_b"], prm["dt_bias"], prm["a_log"], prm["d_skip"], prm["ssd_gain"],
           prm["ln_g"], prm["ln_b"], prm["ws"], prm["bst"]]
    in_specs = [
        pl.BlockSpec((1, lb, ZXU), lambda i, c: (i, c, 0)),
        pl.BlockSpec((1, lb, LANES), lambda i, c: (i, c, 0)),
        pl.BlockSpec((1, SUBLANES, CONV_DIM), lambda i, c: (i, 0, 0)),
        pl.BlockSpec((1, 1, SSD_INNER, SSD_STATE), lambda i, c: (layer, i, 0, 0)),
    ] + [_full(w.shape) for w in par]
    out_specs = [
        pl.BlockSpec((1, lb, SSD_INNER + GMLP_WIDTH), lambda i, c: (i, c, 0)),
        pl.BlockSpec((1, SSD_CONV - 1, CONV_DIM), lambda i, c: (i, 0, 0)),
        pl.BlockSpec((1, SSD_INNER, SSD_STATE), lambda i, c: (i, 0, 0)),
    ]
    out_shape = [
        jax.ShapeDtypeStruct((b, lp, SSD_INNER + GMLP_WIDTH), BF),
        jax.ShapeDtypeStruct((b, SSD_CONV - 1, CONV_DIM), F32),
        jax.ShapeDtypeStruct((b, SSD_INNER, SSD_STATE), F32),
    ]
    if want_v:
        out_specs.append(pl.BlockSpec((1, lb, GMLP_WIDTH), lambda i, c: (i, c, 0)))
        out_shape.append(jax.ShapeDtypeStruct((b, lp, GMLP_WIDTH), F32))
        body = functools.partial(_even_kernel, lb=lb, lc=lc)
    else:
        body = functools.partial(_even_kernel_no_v, lb=lb, lc=lc)
    return pl.pallas_call(
        body,
        grid=(b, nchunks),
        in_specs=in_specs,
        out_specs=out_specs,
        out_shape=out_shape,
        scratch_shapes=[pltpu.VMEM((CHUNK + 2 * SUBLANES, CONV_DIM), F32),
                        pltpu.VMEM((SSD_STATE, SSD_INNER), F32)],
        compiler_params=_cp(("parallel", "arbitrary"), vmem_mb=48),
        name="even_mixer",
    )(proj, dtp, cinit8, sinit, *par)


def _mla_proj_kernel(x_ref, g_ref, wdq_ref, wdkv_ref, wdk2_ref, qg_ref, kvg_ref, wqn_ref, wqp_ref,
                     wuk_ref, cos_ref, sin_ref, ckv_ref, kpe_ref, kcat_ref, q_ref):
    xn = _rms(x_ref[...], g_ref[...]).astype(BF)
    cqn = _rms(_dot(xn, wdq_ref[...]), qg_ref[...]).astype(BF)
    ckv = _rms(_dot(xn, wdkv_ref[...]), kvg_ref[...])
    kk = _dot(xn, wdk2_ref[...])
    cos = cos_ref[...]
    sin = sin_ref[...]
    kpe = kk[:, :ROPE] * cos[:, :ROPE] + kk[:, ROPE:] * sin[:, :ROPE]
    ckv_ref[...] = ckv
    kpe_ref[...] = kpe
    kcat_ref[:, :KV_RANK] = ckv.astype(BF)
    kcat_ref[:, KV_RANK:] = kpe.astype(BF)
    qn = _dot(cqn, wqn_ref[...])
    qp = _dot(cqn, wqp_ref[...])
    hr = MLA_HEADS * ROPE
    qpe = qp[:, :hr] * cos + qp[:, hr:] * sin
    for h in range(MLA_HEADS):
        ql = _dot(qn[:, h * NOPE:(h + 1) * NOPE].astype(BF), wuk_ref[h])
        q_ref[0, h, :, :KV_RANK] = (ql * MLA_SCALE).astype(BF)
        q_ref[0, h, :, KV_RANK:] = (qpe[:, h * ROPE:(h + 1) * ROPE] * MLA_SCALE).astype(BF)


def _mla_proj(x, g, prm, cos8, sin8, *, nb):
    t = x.shape[0]
    seq = t // nb
    tm = min(ROW_TILE, seq)
    tpb = seq // tm
    w = [prm["wdq"], prm["wdkv"], prm["wdk2"], prm["q_gain"], prm["kv_gain"], prm["wqn"], prm["wqp"],
         prm["wuk"]]
    return pl.pallas_call(
        _mla_proj_kernel,
        grid=(t // tm,),
        in_specs=[pl.BlockSpec((tm, D), lambda i: (i, 0)), _full((1, D))] + [_full(a.shape) for a in w] + [
            pl.BlockSpec((tm, MLA_HEADS * ROPE), lambda i: (i % tpb, 0)),
            pl.BlockSpec((tm, MLA_HEADS * ROPE), lambda i: (i % tpb, 0)),
        ],
        out_specs=[
            pl.BlockSpec((tm, KV_RANK), lambda i: (i, 0)),
            pl.BlockSpec((tm, ROPE), lambda i: (i, 0)),
            pl.BlockSpec((tm, QK), lambda i: (i, 0)),
            pl.BlockSpec((1, MLA_HEADS, tm, QK), lambda i: (i // tpb, 0, i % tpb, 0)),
        ],
        out_shape=[
            jax.ShapeDtypeStruct((t, KV_RANK), F32),
            jax.ShapeDtypeStruct((t, ROPE), F32),
            jax.ShapeDtypeStruct((t, QK), BF),
            jax.ShapeDtypeStruct((nb, MLA_HEADS, seq, QK), BF),
        ],
        compiler_params=_cp(("parallel",)),
        name="mla_proj",
    )(x, g, *w, cos8, sin8)


HEAD_QK = NOPE + ROPE
LOG2E = 1.4426950408889634


def _mla_proj_prompt_kernel(x_ref, g_ref, wdq_ref, wdkv_ref, wdk2_ref, qg_ref, kvg_ref, wqn_ref, wqp_ref,
                            wkn_ref, wv_ref, cos_ref, sin_ref, ckv_ref, kpe_ref, q_ref, k_ref, v_ref):
    xn = _rms(x_ref[...], g_ref[...]).astype(BF)
    cqn = _rms(_dot(xn, wdq_ref[...]), qg_ref[...]).astype(BF)
    ckv = _rms(_dot(xn, wdkv_ref[...]), kvg_ref[...])
    kk = _dot(xn, wdk2_ref[...])
    cos = cos_ref[...]
    sin = sin_ref[...]
    kpe = kk[:, :ROPE] * cos[:, :ROPE] + kk[:, ROPE:] * sin[:, :ROPE]
    ckv_ref[...] = ckv
    kpe_ref[...] = kpe
    ckv_b = ckv.astype(BF)
    kpe_b = kpe.astype(BF)
    kn = _dot(ckv_b, wkn_ref[...])
    vv = _dot(ckv_b, wv_ref[...])
    qn = _dot(cqn, wqn_ref[...])
    qp = _dot(cqn, wqp_ref[...])
    hr = MLA_HEADS * ROPE
    qpe = qp[:, :hr] * cos + qp[:, hr:] * sin
    qscale = MLA_SCALE * LOG2E
    for h in range(MLA_HEADS):
        q_ref[0, h, :, :NOPE] = (qn[:, h * NOPE:(h + 1) * NOPE] * qscale).astype(BF)
        q_ref[0, h, :, NOPE:] = (qpe[:, h * ROPE:(h + 1) * ROPE] * qscale).astype(BF)
        k_ref[0, h, :, :NOPE] = kn[:, h * NOPE:(h + 1) * NOPE].astype(BF)
        k_ref[0, h, :, NOPE:] = kpe_b
        v_ref[0, h] = vv[:, h * MLA_V:(h + 1) * MLA_V].astype(BF)


def _mla_proj_prompt(x, g, prm, cos8, sin8, *, nb):
    t = x.shape[0]
    seq = t // nb
    tm = min(ROW_TILE, seq)
    tpb = seq // tm
    w = [prm["wdq"], prm["wdkv"], prm["wdk2"], prm["q_gain"], prm["kv_gain"], prm["wqn"], prm["wqp"],
         prm["wkn"], prm["wv"]]
    head_spec = lambda width: pl.BlockSpec((1, MLA_HEADS, tm, width), lambda i: (i // tpb, 0, i % tpb, 0))
    head_shape = lambda width: jax.ShapeDtypeStruct((nb, MLA_HEADS, seq, width), BF)
    return pl.pallas_call(
        _mla_proj_prompt_kernel,
        grid=(t // tm,),
        in_specs=[pl.BlockSpec((tm, D), lambda i: (i, 0)), _full((1, D))] + [_full(a.shape) for a in w] + [
            pl.BlockSpec((tm, MLA_HEADS * ROPE), lambda i: (i % tpb, 0)),
            pl.BlockSpec((tm, MLA_HEADS * ROPE), lambda i: (i % tpb, 0)),
        ],
        out_specs=[
            pl.BlockSpec((tm, KV_RANK), lambda i: (i, 0)),
            pl.BlockSpec((tm, ROPE), lambda i: (i, 0)),
            head_spec(HEAD_QK), head_spec(HEAD_QK), head_spec(MLA_V),
        ],
        out_shape=[
            jax.ShapeDtypeStruct((t, KV_RANK), F32),
            jax.ShapeDtypeStruct((t, ROPE), F32),
            head_shape(HEAD_QK), head_shape(HEAD_QK), head_shape(MLA_V),
        ],
        compiler_params=_cp(("parallel",)),
        name="mla_proj_prompt",
    )(x, g, *w, cos8, sin8)


FLASH_T = 512
FLASH_HEADS = 2


def _flash_kernel(q_ref, k_ref, v_ref, o_ref, m_ref, acc_ref, *, t):
    qi = pl.program_id(2)
    for hh in range(FLASH_HEADS):
        m_ref[hh] = jnp.full((t, 1), NEG, F32)
        acc_ref[hh] = jnp.zeros((t, 2 * MLA_V), F32)
    ones = jnp.ones((t, MLA_V), BF)

    def block(ki, masked):
        start = pl.multiple_of(ki * t, t)
        for hh in range(FLASH_HEADS):
            kb = k_ref[0, hh, pl.ds(start, t), :]
            vb = v_ref[0, hh, pl.ds(start, t), :]
            s = _dot_nt(q_ref[0, hh], kb)
            if masked:
                row = lax.broadcasted_iota(jnp.int32, (t, t), 0)
                col = lax.broadcasted_iota(jnp.int32, (t, t), 1)
                s = jnp.where(col <= row, s, NEG)
            m_old = m_ref[hh]
            m_new = jnp.maximum(m_old, jnp.max(s, axis=-1, keepdims=True))
            alpha = jnp.exp2(m_old - m_new)
            p = jnp.exp2(s - m_new).astype(BF)
            acc_ref[hh] = alpha * acc_ref[hh] + _dot(p, jnp.concatenate([vb, ones], axis=1))
            m_ref[hh] = m_new

    def body(ki, carry):
        block(ki, False)
        return carry

    lax.fori_loop(0, qi, body, 0)
    block(qi, True)
    for hh in range(FLASH_HEADS):
        acc = acc_ref[hh]
        o_ref[0, :, hh * MLA_V:(hh + 1) * MLA_V] = (acc[:, :MLA_V] / acc[:, MLA_V:MLA_V + 1]).astype(BF)


def _flash(q, k, v):
    nb, _, seq, _ = q.shape
    t = min(FLASH_T, seq)
    nh = FLASH_HEADS
    return pl.pallas_call(
        functools.partial(_flash_kernel, t=t),
        grid=(nb, MLA_HEADS // nh, seq // t),
        in_specs=[
            pl.BlockSpec((1, nh, t, HEAD_QK), lambda b, h, i: (b, h, i, 0)),
            pl.BlockSpec((1, nh, seq, HEAD_QK), lambda b, h, i: (b, h, 0, 0)),
            pl.BlockSpec((1, nh, seq, MLA_V), lambda b, h, i: (b, h, 0, 0)),
        ],
        out_specs=pl.BlockSpec((1, t, nh * MLA_V), lambda b, h, i: (b, i, h)),
        out_shape=jax.ShapeDtypeStruct((nb, seq, MLA_HEADS * MLA_V), BF),
        scratch_shapes=[pltpu.VMEM((nh, t, 1), F32), pltpu.VMEM((nh, t, 2 * MLA_V), F32)],
        compiler_params=_cp(("parallel", "parallel", "arbitrary"), vmem_mb=48),
        name="mla_flash",
    )(q, k, v)


def _page_copies(pt_ref, ckv_hbm, kpt_hbm, ckbuf, kpbuf, sem, sample, slot, *, layer, n_pages):
    copies = []
    for p in range(n_pages):
        pg = pt_ref[sample, p]
        copies.append(pltpu.make_async_copy(
            ckv_hbm.at[layer, pg], ckbuf.at[slot, pl.ds(p * PAGE, PAGE), :], sem.at[0, slot]))
        copies.append(pltpu.make_async_copy(
            kpt_hbm.at[layer, pg], kpbuf.at[slot, :, pl.ds(p * PAGE, PAGE)], sem.at[1, slot]))
    return copies


def _decode_kernel(pt_ref, q_ref, nk_ref, ckv_hbm, kpt_hbm, o_ref, ckbuf, kpbuf, sem, *,
                   dec_seq, layer, n_pages):
    b = pl.program_id(0)
    last = pl.num_programs(0) - 1
    slot = b % 2
    copies = functools.partial(_page_copies, pt_ref, ckv_hbm, kpt_hbm, ckbuf, kpbuf, sem,
                               layer=layer, n_pages=n_pages)

    @pl.when(b == 0)
    def _():
        for c in copies(0, 0):
            c.start()

    for c in copies(b, slot):
        c.wait()
    nxt = jnp.minimum(b + 1, last)
    for c in copies(nxt, 1 - slot):
        c.start()

    rows = MLA_HEADS * dec_seq
    qm = q_ref[0]
    ql = qm[:, :KV_RANK]
    qp = qm[:, KV_RANK:]
    ck = ckbuf[slot].astype(BF)
    kp = kpbuf[slot].astype(BF)
    s = _dot_nt(ql, ck) + _dot(qp, kp)
    nk = nk_ref[0]
    kt = lax.broadcasted_iota(jnp.int32, (rows, NEW_KEY_ROWS), 1)
    qt = lax.broadcasted_iota(jnp.int32, (rows, NEW_KEY_ROWS), 0) % dec_seq
    s_new = jnp.where(kt <= qt, _dot_nt(qm, nk), NEG)
    m = jnp.maximum(jnp.max(s, axis=-1, keepdims=True), jnp.max(s_new, axis=-1, keepdims=True))
    p = jnp.exp(s - m)
    p_new = jnp.exp(s_new - m)
    denom = jnp.sum(p, axis=-1, keepdims=True) + jnp.sum(p_new, axis=-1, keepdims=True)
    o_ref[0] = (_dot(p.astype(BF), ck) + _dot(p_new.astype(BF), nk[:, :KV_RANK])) / denom

    @pl.when(b == last)
    def _():
        for c in copies(nxt, 1 - slot):
            c.wait()


def _decode(page_table, q, newk, cache_ckv, cache_kpe_t, layer, dec_seq):
    nb, n_pages = page_table.shape
    rows = MLA_HEADS * dec_seq
    keys = n_pages * PAGE
    grid_spec = pltpu.PrefetchScalarGridSpec(
        num_scalar_prefetch=1,
        grid=(nb,),
        in_specs=[
            pl.BlockSpec((1, rows, QK), lambda b, pt: (b, 0, 0)),
            pl.BlockSpec((1, NEW_KEY_ROWS, QK), lambda b, pt: (b, 0, 0)),
            pl.BlockSpec(memory_space=pl.ANY),
            pl.BlockSpec(memory_space=pl.ANY),
        ],
        out_specs=pl.BlockSpec((1, rows, KV_RANK), lambda b, pt: (b, 0, 0)),
        scratch_shapes=[pltpu.VMEM((2, keys, KV_RANK), F32), pltpu.VMEM((2, ROPE, keys), F32),
                        pltpu.SemaphoreType.DMA((2, 2))],
    )
    return pl.pallas_call(
        functools.partial(_decode_kernel, dec_seq=dec_seq, layer=layer, n_pages=n_pages),
        grid_spec=grid_spec,
        out_shape=jax.ShapeDtypeStruct((nb, rows, KV_RANK), F32),
        compiler_params=_cp(("arbitrary",), vmem_mb=48),
        name="mla_decode",
    )(page_table, q, newk, cache_ckv, cache_kpe_t)


def _mla_out_kernel(o_ref, wuv_ref, wo_ref, r_ref, out_ref):
    parts = [_dot(o_ref[0, h], wuv_ref[h]).astype(BF) for h in range(MLA_HEADS)]
    out_ref[...] = r_ref[...] + _dot(jnp.concatenate(parts, axis=1), wo_ref[...])


def _mla_out(o_lat, wuv, wo, res):
    nb, _, seq, _ = o_lat.shape
    tm = min(ROW_TILE, seq)
    tpb = seq // tm
    t = nb * seq
    return pl.pallas_call(
        _mla_out_kernel,
        grid=(t // tm,),
        in_specs=[
            pl.BlockSpec((1, MLA_HEADS, tm, KV_RANK), lambda i: (i // tpb, 0, i % tpb, 0)),
            _full(wuv.shape), _full(wo.shape),
            pl.BlockSpec((tm, D), lambda i: (i, 0)),
        ],
        out_specs=pl.BlockSpec((tm, D), lambda i: (i, 0)),
        out_shape=jax.ShapeDtypeStruct((t, D), F32),
        compiler_params=_cp(("parallel",)),
        name="mla_out",
    )(o_lat, wuv, wo, res)


def _softmax_rows(s):
    m = jnp.max(s, axis=-1, keepdims=True)
    p = jnp.exp(s - m)
    return p / jnp.sum(p, axis=-1, keepdims=True)


def _mem_prompt_kernel(x_ref, g_ref, wq_ref, k_ref, v_ref, wo_ref, o_ref):
    x = x_ref[...]
    xn = _rms(x, g_ref[...]).astype(BF)
    qm = (_dot(xn, wq_ref[...]) * MEM_HD ** -0.5).astype(BF)
    km = k_ref[0].astype(BF)
    vm = v_ref[0].astype(BF)
    parts = []
    for h in range(MEM_HEADS):
        sl = slice(h * MEM_HD, (h + 1) * MEM_HD)
        p = _softmax_rows(_dot_nt(qm[:, sl], km[:, sl]))
        parts.append(_dot(p.astype(BF), vm[:, sl]).astype(BF))
    o_ref[...] = x + _dot(jnp.concatenate(parts, axis=1), wo_ref[...])


def _mem_prompt(x, g, wq, km, vm, wo):
    t = x.shape[0]
    nb, mt, _ = km.shape
    seq = t // nb
    tm = min(ROW_TILE, seq)
    tpb = seq // tm
    return pl.pallas_call(
        _mem_prompt_kernel,
        grid=(t // tm,),
        in_specs=[
            pl.BlockSpec((tm, D), lambda i: (i, 0)), _full((1, D)), _full(wq.shape),
            pl.BlockSpec((1, mt, MEM_INNER), lambda i: (i // tpb, 0, 0)),
            pl.BlockSpec((1, mt, MEM_INNER), lambda i: (i // tpb, 0, 0)),
            _full(wo.shape),
        ],
        out_specs=pl.BlockSpec((tm, D), lambda i: (i, 0)),
        out_shape=jax.ShapeDtypeStruct((t, D), F32),
        compiler_params=_cp(("parallel",)),
        name="mem_attn_prompt",
    )(x, g, wq, km, vm, wo)


MEM_ROWS = MEM_HEADS * SUBLANES


def _mem_sample_kernel(x_ref, g_ref, wq_ref, k_ref, v_ref, wo_ref, o_ref):
    ns = SAMPLES_PER_STEP
    x = x_ref[...].reshape(ns * SUBLANES, D)
    xn = _rms(x, g_ref[...]).astype(BF)
    qall = _dot(xn, wq_ref[...]) * MEM_HD ** -0.5
    cols = k_ref.shape[2]
    head_of_row = lax.broadcasted_iota(jnp.int32, (MEM_ROWS, cols), 0) // SUBLANES
    head_of_col = lax.broadcasted_iota(jnp.int32, (MEM_ROWS, cols), 1) % MEM_HEADS
    own = head_of_row == head_of_col
    outs = []
    for s in range(ns):
        qs = qall[s * SUBLANES:(s + 1) * SUBLANES]
        qst = jnp.concatenate([qs[:, h * MEM_HD:(h + 1) * MEM_HD] for h in range(MEM_HEADS)], axis=0)
        sc = jnp.where(own, _dot_nt(qst.astype(BF), k_ref[0, s].astype(BF)), NEG)
        o = _dot(_softmax_rows(sc).astype(BF), v_ref[0, s].astype(BF))
        outs.append(jnp.concatenate([o[h * SUBLANES:(h + 1) * SUBLANES] for h in range(MEM_HEADS)], axis=1))
    out = x + _dot(jnp.concatenate(outs, axis=0).astype(BF), wo_ref[...])
    o_ref[...] = out.reshape(ns, SUBLANES, D)


def _mem_sample(x8, g, wq, cache_k, cache_v, wo, layer):
    nb = x8.shape[0]
    rows = cache_k.shape[2]
    ns = SAMPLES_PER_STEP
    return pl.pallas_call(
        _mem_sample_kernel,
        grid=(nb // ns,),
        in_specs=[
            pl.BlockSpec((ns, SUBLANES, D), lambda i: (i, 0, 0)), _full((1, D)), _full(wq.shape),
            pl.BlockSpec((1, ns, rows, MEM_HD), lambda i: (layer, i, 0, 0)),
            pl.BlockSpec((1, ns, rows, MEM_HD), lambda i: (layer, i, 0, 0)),
            _full(wo.shape),
        ],
        out_specs=pl.BlockSpec((ns, SUBLANES, D), lambda i: (i, 0, 0)),
        out_shape=jax.ShapeDtypeStruct((nb, SUBLANES, D), F32),
        compiler_params=_cp(("parallel",), vmem_mb=48),
        name="mem_attn_sample",
    )(x8, g, wq, cache_k, cache_v, wo)


def _router_kernel(x_ref, g_ref, wr_ref, xn_ref, route_ref):
    xn = _rms(x_ref[...], g_ref[...])
    xn_ref[...] = xn
    lane = lax.broadcasted_iota(jnp.int32, (xn.shape[0], LANES), 1).astype(F32)
    lg = jnp.where(lane < N_EXP, _dot_f32(xn, wr_ref[...]), NEG)
    m1 = jnp.max(lg, axis=-1, keepdims=True)
    i1 = jnp.min(jnp.where(lg == m1, lane, float(LANES)), axis=-1, keepdims=True)
    lg2 = jnp.where(lane == i1, NEG, lg)
    m2 = jnp.max(lg2, axis=-1, keepdims=True)
    i2 = jnp.min(jnp.where(lg2 == m2, lane, float(LANES)), axis=-1, keepdims=True)
    e = jnp.exp(m2 - m1)
    g1 = 1.0 / (1.0 + e)
    g2 = e * g1
    route_ref[...] = jnp.where(lane == 0, i1, jnp.where(lane == 1, i2, jnp.where(lane == 2, g1,
                               jnp.where(lane == 3, g2, 0.0))))


def _router(x, g, wr_pad):
    t = x.shape[0]
    tm = min(ROW_TILE, t)
    return pl.pallas_call(
        _router_kernel,
        grid=(t // tm,),
        in_specs=[pl.BlockSpec((tm, D), lambda i: (i, 0)), _full((1, D)), _full(wr_pad.shape)],
        out_specs=[pl.BlockSpec((tm, D), lambda i: (i, 0)), pl.BlockSpec((tm, LANES), lambda i: (i, 0))],
        out_shape=[jax.ShapeDtypeStruct((t, D), F32), jax.ShapeDtypeStruct((t, LANES), F32)],
        compiler_params=_cp(("parallel",)),
        name="router",
    )(x, g, wr_pad)


GATHER_TILE = 256


def _row_copy(x_hbm, o_ref, sem, src_row, dst_row):
    return pltpu.make_async_copy(x_hbm.at[pl.ds(src_row, 1)], o_ref.at[pl.ds(dst_row, 1)], sem)


def _gather_kernel(idx_ref, x_hbm, o_ref, sem):
    tg = o_ref.shape[0]
    base = pl.program_id(0) * tg

    def start(r, c):
        _row_copy(x_hbm, o_ref, sem, idx_ref[base + r], r).start()
        return c

    lax.fori_loop(0, tg, start, 0)

    def wait(r, c):
        _row_copy(x_hbm, o_ref, sem, 0, r).wait()
        return c

    lax.fori_loop(0, tg, wait, 0)


def _gather_rows(x, idx):
    n = idx.shape[0]
    width = x.shape[1]
    tg = min(GATHER_TILE, n)
    grid_spec = pltpu.PrefetchScalarGridSpec(
        num_scalar_prefetch=1,
        grid=(n // tg,),
        in_specs=[pl.BlockSpec(memory_space=pl.ANY)],
        out_specs=pl.BlockSpec((tg, width), lambda i, idx_ref: (i, 0)),
        scratch_shapes=[pltpu.SemaphoreType.DMA],
    )
    return pl.pallas_call(
        _gather_kernel,
        grid_spec=grid_spec,
        out_shape=jax.ShapeDtypeStruct((n, width), x.dtype),
        compiler_params=_cp(("arbitrary",)),
        name="gather_rows",
    )(idx, x)


MOE_HALF = D_FFE // 2
MOE_CHUNK = 256


def _moe_ffn_kernel(te_ref, nused_ref, x_ref, gate_ref, wg_ref, wu_ref, wd_ref, o_ref, hid_ref):
    del te_ref
    i = pl.program_id(0)
    j = pl.program_id(1)

    @pl.when(i < nused_ref[0])
    def _():
        xb = x_ref[...].astype(BF)
        for c in range(MOE_HALF // MOE_CHUNK):
            lo = c * MOE_CHUNK
            gate = _dot(xb, wg_ref[0, :, lo:lo + MOE_CHUNK])
            up = _dot(xb, wu_ref[0, :, lo:lo + MOE_CHUNK])
            hid_ref[:, lo:lo + MOE_CHUNK] = (_silu(gate) * up).astype(BF)
        part = _dot(hid_ref[...], wd_ref[0]) * gate_ref[...]

        @pl.when(j == 0)
        def _():
            o_ref[...] = part

        @pl.when(j != 0)
        def _():
            o_ref[...] += part

    @pl.when(i >= nused_ref[0])
    def _():
        o_ref[...] = jnp.zeros_like(o_ref)


def _moe_ffn(tile_expert, nused, xs, gates, wgu, wd, *, tm):
    npad = xs.shape[0]
    nh = D_FFE // MOE_HALF
    grid_spec = pltpu.PrefetchScalarGridSpec(
        num_scalar_prefetch=2,
        grid=(npad // tm, nh),
        in_specs=[
            pl.BlockSpec((tm, D), lambda i, j, te, nu: (i, 0)),
            pl.BlockSpec((tm, 1), lambda i, j, te, nu: (i, 0)),
            pl.BlockSpec((1, D, MOE_HALF), lambda i, j, te, nu: (te[i], 0, j)),
            pl.BlockSpec((1, D, MOE_HALF), lambda i, j, te, nu: (te[i], 0, nh + j)),
            pl.BlockSpec((1, MOE_HALF, D), lambda i, j, te, nu: (te[i], j, 0)),
        ],
        out_specs=pl.BlockSpec((tm, D), lambda i, j, te, nu: (i, 0)),
        scratch_shapes=[pltpu.VMEM((tm, MOE_HALF), BF)],
    )
    return pl.pallas_call(
        _moe_ffn_kernel,
        grid_spec=grid_spec,
        out_shape=jax.ShapeDtypeStruct((npad, D), F32),
        compiler_params=_cp(("parallel", "arbitrary"), vmem_mb=56),
        name="moe_ffn",
    )(tile_expert, nused, xs, gates, wgu, wgu, wd)


def _combine_kernel(h_ref, y0_ref, y1_ref, g_ref, o_ref, *, final):
    out = h_ref[...] + y0_ref[...] + y1_ref[...]
    if final:
        out = _rms(out, g_ref[...])
    o_ref[...] = out


def _combine(h, y2, g, *, final):
    t = h.shape[0]
    tm = min(ROW_TILE, t)
    nt = t // tm
    return pl.pallas_call(
        functools.partial(_combine_kernel, final=final),
        grid=(nt,),
        in_specs=[
            pl.BlockSpec((tm, D), lambda i: (i, 0)),
            pl.BlockSpec((tm, D), lambda i: (i, 0)),
            pl.BlockSpec((tm, D), lambda i: (nt + i, 0)),
            _full((1, D)),
        ],
        out_specs=pl.BlockSpec((tm, D), lambda i: (i, 0)),
        out_shape=jax.ShapeDtypeStruct((t, D), F32),
        compiler_params=_cp(("parallel",)),
        name="moe_combine",
    )(h, y2, y2, g)


def _moe(h, g, wr_pad, wgu, wd, final_g, *, final):
    t = h.shape[0]
    tm = min(ROW_TILE, t)
    xn, route = _router(h, g, wr_pad)
    eidx = route[:, :2].astype(jnp.int32)
    gate = route[:, 2:4]
    e_flat = eidx.T.reshape(-1)
    g_flat = gate.T.reshape(-1)
    tok = jnp.tile(jnp.arange(t, dtype=jnp.int32), 2)
    onehot = (e_flat[:, None] == jnp.arange(N_EXP, dtype=jnp.int32)[None, :]).astype(jnp.int32)
    csum = jnp.cumsum(onehot, axis=0)
    counts = csum[-1]
    rank = jnp.sum(onehot * csum, axis=1) - 1
    padded = ((counts + tm - 1) // tm) * tm
    ends = jnp.cumsum(padded)
    starts = ends - padded
    dest = starts[e_flat] + rank
    n_tiles = (2 * t) // tm + N_EXP
    npad = n_tiles * tm
    row_tok = jnp.zeros((npad,), jnp.int32).at[dest].set(tok)
    row_gate = jnp.zeros((npad,), F32).at[dest].set(g_flat)
    tile_start = jnp.arange(n_tiles, dtype=jnp.int32) * tm
    tile_expert = jnp.minimum(jnp.sum((tile_start[:, None] >= ends[None, :]).astype(jnp.int32), axis=1),
                              N_EXP - 1).astype(jnp.int32)
    nused = (ends[-1] // tm).astype(jnp.int32).reshape(1)
    xs = _gather_rows(xn, row_tok)
    ys = _moe_ffn(tile_expert, nused, xs, row_gate[:, None], wgu, wd, tm=tm)
    y2 = _gather_rows(ys, dest)
    return _combine(h, y2, final_g, final=final)


def _rope_tables(pos):
    half = ROPE // 2
    inv_freq = ROPE_BASE ** (-jnp.arange(half, dtype=F32) / half)
    ang = pos.astype(F32)[:, None] * inv_freq
    cos = jnp.cos(ang)
    sin = jnp.sin(ang)
    cc = jnp.concatenate([cos, cos], axis=-1)
    ss = jnp.concatenate([-sin, sin], axis=-1)
    return jnp.tile(cc, (1, MLA_HEADS)), jnp.tile(ss, (1, MLA_HEADS))


def _even_layer(h, g, prm, cinit, sinit, layer, *, nb, sample):
    t = h.shape[0]
    seq = t // nb
    proj = _rms_matmul(h, g, prm["w_zxu"], tn=_largest_tile(ZXU, 1536))
    dtp = _rms_matmul(h, g, prm["w_dt"], tn=LANES, precise=True)
    proj = proj.reshape(nb, seq, ZXU)
    dtp = dtp.reshape(nb, seq, LANES)
    if sample:
        pad = ((0, 0), (0, SUBLANES - seq), (0, 0))
        proj = jnp.pad(proj, pad)
        dtp = jnp.pad(dtp, pad)
        lb, lc = SUBLANES, seq
    else:
        lb, lc = CHUNK, CHUNK
    cinit8 = jnp.pad(cinit, ((0, 0), (SUBLANES - (SSD_CONV - 1), 0), (0, 0)))
    outs = _even_mixer(proj, dtp, cinit8, sinit.reshape(-1, nb, SSD_INNER, SSD_STATE), prm,
                       lb=lb, lc=lc, want_v=sample, layer=layer)
    ymix, cout, sout = outs[:3]
    v = None
    if sample:
        ymix = ymix[:, :seq]
        v = outs[3][:, :seq]
    h = _matmul_res(ymix.reshape(t, SSD_INNER + GMLP_WIDTH), prm["w_out"], h)
    return h, cout, sout.reshape(nb, SSD_HEADS, SSD_HEAD_DIM, SSD_STATE), v


def _prep_even(i, w_in, conv_w, conv_b, dt_bias, a_log, d_skip, ssd_gain, ln_g, ln_b, ws, bs, w_out):
    w = w_in[i]
    o1 = SSD_INNER + CONV_DIM
    w_zxu = jnp.concatenate([w[:, :o1], w[:, o1 + SSD_HEADS:]], axis=1).astype(BF)
    w_dt = jnp.pad(w[:, o1:o1 + SSD_HEADS], ((0, 0), (0, LANES - SSD_HEADS)))
    padl = (0, LANES - SSD_HEADS)
    return dict(
        w_zxu=w_zxu, w_dt=w_dt,
        conv_w=jnp.pad(conv_w[i], ((0, SUBLANES - SSD_CONV), (0, 0))),
        conv_b=conv_b[i][None, :],
        dt_bias=jnp.pad(dt_bias[i], padl)[None, :],
        a_log=jnp.pad(a_log[i], padl)[None, :],
        d_skip=jnp.repeat(d_skip[i], SSD_HEAD_DIM)[None, :],
        ssd_gain=ssd_gain[i][None, :],
        ln_g=ln_g[i][None, :], ln_b=ln_b[i][None, :],
        ws=ws[i], bst=bs[i].T,
        w_out=w_out[i].astype(BF),
    )


def _prep_mla(i, w_down, q_gain, kv_gain, w_uq, w_uk, w_uv, w_o):
    wd = w_down[i]
    wk = wd[:, Q_RANK + KV_RANK:]
    half = ROPE // 2
    rot = lambda a: jnp.concatenate([a[..., half:], a[..., :half]], axis=-1)
    uq = w_uq[i]
    uq_pe = uq[:, :, NOPE:]
    return dict(
        wdq=wd[:, :Q_RANK].astype(BF),
        wdkv=wd[:, Q_RANK:Q_RANK + KV_RANK].astype(BF),
        wdk2=jnp.concatenate([wk, rot(wk)], axis=1).astype(BF),
        q_gain=q_gain[i][None, :], kv_gain=kv_gain[i][None, :],
        wqn=uq[:, :, :NOPE].reshape(Q_RANK, MLA_HEADS * NOPE).astype(BF),
        wqp=jnp.concatenate([uq_pe.reshape(Q_RANK, -1), rot(uq_pe).reshape(Q_RANK, -1)], axis=1).astype(BF),
        wuk=jnp.transpose(w_uk[i], (1, 2, 0)).astype(BF),
        wuv=jnp.transpose(w_uv[i], (1, 0, 2)).astype(BF),
        wkn=w_uk[i].reshape(KV_RANK, MLA_HEADS * NOPE).astype(BF),
        wv=w_uv[i].reshape(KV_RANK, MLA_HEADS * MLA_V).astype(BF),
        wo=w_o[i].astype(BF),
    )


def kernel(x_prompt, x_sample, state_ssd, state_conv, cache_mla_ckv, cache_mla_kpe, cache_mem_k, cache_mem_v, page_table, mem_prompt, mix_norm, w_in, conv_w, conv_b, dt_bias, a_log, d_skip, ssd_gain, gmlp_ln_g, gmlp_ln_b, gmlp_ws, gmlp_bs, w_out_even, w_mla_down, mla_q_gain, mla_kv_gain, w_mla_uq, w_mla_uk, w_mla_uv, w_mla_o, xattn_norm, mem_norm, w_mem_q, w_mem_k, w_mem_v, w_mem_o, ffn_norm, w_ffn_gu, w_ffn_down, w_router, w_exp_gu, w_exp_down, final_norm):
    nbp, seq, _ = x_prompt.shape
    nbs, dseq, _ = x_sample.shape
    depth = mix_norm.shape[0]
    past = page_table.shape[1] * PAGE
    mt = mem_prompt.shape[1]
    hp = x_prompt.reshape(nbp * seq, D)
    hs = x_sample.reshape(nbs * dseq, D)
    cos_p, sin_p = _rope_tables(jnp.arange(seq, dtype=jnp.int32))
    cos_s, sin_s = _rope_tables(past + jnp.arange(dseq, dtype=jnp.int32))
    cos_s = jnp.tile(cos_s, (nbs, 1))
    sin_s = jnp.tile(sin_s, (nbs, 1))
    cache_k4 = cache_mem_k.reshape(depth, nbs, mt * MEM_HEADS, MEM_HD)
    cache_v4 = cache_mem_v.reshape(depth, nbs, mt * MEM_HEADS, MEM_HD)
    cache_kpe_t = jnp.swapaxes(cache_mla_kpe, 2, 3)
    final_g = final_norm[None, :]

    p_ssd, p_conv, p_ckv, p_kpe, p_mk, p_mv = [], [], [], [], [], []
    s_ssd, s_conv, s_v, s_ckv, s_kpe = [], [], [], [], []
    for l in range(depth):
        i = l // 2
        g_mix = mix_norm[l][None, :]
        if l % 2 == 0:
            prm = _prep_even(i, w_in, conv_w, conv_b, dt_bias, a_log, d_skip, ssd_gain, gmlp_ln_g,
                             gmlp_ln_b, gmlp_ws, gmlp_bs, w_out_even)
            buf0 = jnp.zeros((nbp, SSD_CONV - 1, CONV_DIM), F32)
            h00 = jnp.zeros((1, nbp, SSD_HEADS, SSD_HEAD_DIM, SSD_STATE), F32)
            hp, buf_p, ssd_p, _ = _even_layer(hp, g_mix, prm, buf0, h00, 0, nb=nbp, sample=False)
            hs, buf_s, ssd_s, v_s = _even_layer(hs, g_mix, prm, state_conv[i], state_ssd, i, nb=nbs,
                                                sample=True)
            p_ssd.append(ssd_p)
            p_conv.append(buf_p)
            s_ssd.append(ssd_s)
            s_conv.append(buf_s)
            s_v.append(v_s)
        else:
            prm = _prep_mla(i, w_mla_down, mla_q_gain, mla_kv_gain, w_mla_uq, w_mla_uk, w_mla_uv, w_mla_o)
            ckv, kpe, qh, kh, vh = _mla_proj_prompt(hp, g_mix, prm, cos_p, sin_p, nb=nbp)
            o_p = _flash(qh, kh, vh)
            hp = _matmul_res(o_p.reshape(nbp * seq, MLA_HEADS * MLA_V), prm["wo"], hp)
            p_ckv.append(ckv.reshape(nbp, seq, KV_RANK))
            p_kpe.append(kpe.reshape(nbp, seq, ROPE))

            ckv_s, kpe_s, kcat_s, q_s = _mla_proj(hs, g_mix, prm, cos_s, sin_s, nb=1)
            q_s = q_s[0].reshape(MLA_HEADS, nbs, dseq, QK).transpose(1, 0, 2, 3).reshape(nbs, MLA_HEADS * dseq, QK)
            newk = jnp.pad(kcat_s.reshape(nbs, dseq, QK), ((0, 0), (0, NEW_KEY_ROWS - dseq), (0, 0)))
            o_s = _decode(page_table, q_s, newk, cache_mla_ckv, cache_kpe_t, i, dseq)
            o_s = o_s.reshape(nbs, MLA_HEADS, dseq, KV_RANK).transpose(1, 0, 2, 3)
            o_s = o_s.reshape(1, MLA_HEADS, nbs * dseq, KV_RANK).astype(BF)
            hs = _mla_out(o_s, prm["wuv"], prm["wo"], hs)
            s_ckv.append(ckv_s.reshape(nbs, dseq, KV_RANK))
            s_kpe.append(kpe_s.reshape(nbs, dseq, ROPE))

        wkv = jnp.concatenate([w_mem_k[l], w_mem_v[l]], axis=1).astype(BF)
        kv = _rms_matmul(mem_prompt.reshape(nbp * mt, D), mem_norm[l][None, :], wkv, tn=MEM_INNER)
        mk_p = kv[:, :MEM_INNER].reshape(nbp, mt, MEM_INNER)
        mv_p = kv[:, MEM_INNER:].reshape(nbp, mt, MEM_INNER)
        g_x = xattn_norm[l][None, :]
        wq = w_mem_q[l].astype(BF)
        wo = w_mem_o[l].astype(BF)
        hp = _mem_prompt(hp, g_x, wq, mk_p, mv_p, wo)
        x8 = jnp.pad(hs.reshape(nbs, dseq, D), ((0, 0), (0, SUBLANES - dseq), (0, 0)))
        hs = _mem_sample(x8, g_x, wq, cache_k4, cache_v4, wo, l)[:, :dseq].reshape(nbs * dseq, D)
        p_mk.append(mk_p.reshape(nbp, mt, MEM_HEADS, MEM_HD))
        p_mv.append(mv_p.reshape(nbp, mt, MEM_HEADS, MEM_HD))

        g_f = ffn_norm[l][None, :]
        if l % 2 == 0:
            wgu = w_ffn_gu[i].astype(BF)
            wd = w_ffn_down[i].astype(BF)
            hp = _ffn(hp, g_f, wgu, wd)
            hs = _ffn(hs, g_f, wgu, wd)
        else:
            wr = jnp.pad(w_router[i], ((0, 0), (0, LANES - N_EXP)))
            wgu = w_exp_gu[i].astype(BF)
            wd = w_exp_down[i].astype(BF)
            final = l == depth - 1
            hp = _moe(hp, g_f, wr, wgu, wd, final_g, final=final)
            hs = _moe(hs, g_f, wr, wgu, wd, final_g, final=final)
    if depth % 2 == 1:
        raise NotImplementedError("the final norm is fused into the last routed-expert layer")
    y_prompt = hp.reshape(nbp, seq, D)
    y_sample = hs.reshape(nbs, dseq, D)
    return (y_prompt, y_sample,
            jnp.stack(p_ssd), jnp.stack(p_conv), jnp.stack(p_ckv), jnp.stack(p_kpe),
            jnp.stack(p_mk), jnp.stack(p_mv),
            jnp.stack(s_ssd), jnp.stack(s_conv), jnp.stack(s_v), jnp.stack(s_ckv), jnp.stack(s_kpe))
```

```python
import functools

import jax
import jax.numpy as jnp
from jax import lax
from jax.experimental import pallas as pl
from jax.experimental.pallas import tpu as pltpu
from jax.experimental.pallas import tpu_sc as plsc

F32 = jnp.float32
BF = jnp.bfloat16
EPS = 1e-6
NEG = -1e30

D = 1024
SSD_HEADS = 16
SSD_HEAD_DIM = 64
SSD_INNER = SSD_HEADS * SSD_HEAD_DIM
SSD_GROUPS = 2
SSD_STATE = 128
SSD_CONV = 4
CONV_DIM = SSD_INNER + 2 * SSD_GROUPS * SSD_STATE
GMLP_GROUPS = 8
GMLP_WIDTH = 1024
CHUNK = 128
SAMPLE_Q = 16
ZXU = SSD_INNER + CONV_DIM + 2 * GMLP_WIDTH
MLA_HEADS = 8
NOPE = 128
ROPE = 64
MLA_V = 128
Q_RANK = 256
KV_RANK = 256
MLA_SCALE = (NOPE + ROPE) ** -0.5
QK = KV_RANK + ROPE
ROPE_BASE = 10000.0
MEM_HEADS = 4
MEM_HD = 128
MEM_INNER = MEM_HEADS * MEM_HD
D_FF = 2816
N_EXP = 8
D_FFE = 3584
PAGE = 128
LANES = 128
SUBLANES = 8
ROW_TILE = 512
NEW_KEY_ROWS = 16
SAMPLES_PER_STEP = 8


def _cp(sem, vmem_mb=None):
    kw = dict(dimension_semantics=sem)
    if vmem_mb is not None:
        kw["vmem_limit_bytes"] = vmem_mb * 1024 * 1024
    return pltpu.CompilerParams(**kw)


def _rms(x, g):
    return x * lax.rsqrt(jnp.mean(x * x, axis=-1, keepdims=True) + EPS) * g


def _dot(a, b):
    return jnp.dot(a, b, preferred_element_type=F32)


def _dot_nt(a, b):
    return lax.dot_general(a, b, (((1,), (1,)), ((), ())), preferred_element_type=F32)


def _dot_f32(a, b):
    return jnp.dot(a, b, preferred_element_type=F32, precision=lax.Precision.HIGHEST)


def _silu(x):
    return x * jax.nn.sigmoid(x)


def _full(shape):
    n = len(shape)
    return pl.BlockSpec(shape, lambda *_: (0,) * n)


def _largest_tile(n, cap):
    best = LANES
    for t in range(LANES, cap + 1, LANES):
        if n % t == 0:
            best = t
    return best


def _rms_matmul_kernel(x_ref, g_ref, w_ref, o_ref, xn_ref, *, precise):
    @pl.when(pl.program_id(1) == 0)
    def _():
        xn_ref[...] = _rms(x_ref[...], g_ref[...]).astype(xn_ref.dtype)

    if precise:
        o_ref[...] = _dot_f32(xn_ref[...], w_ref[...]).astype(o_ref.dtype)
    else:
        o_ref[...] = _dot(xn_ref[...], w_ref[...]).astype(o_ref.dtype)


def _rms_matmul(x, g, w, *, tn, out_dtype=F32, precise=False):
    t, k = x.shape
    n = w.shape[1]
    tm = min(ROW_TILE, t)
    return pl.pallas_call(
        functools.partial(_rms_matmul_kernel, precise=precise),
        grid=(t // tm, n // tn),
        in_specs=[
            pl.BlockSpec((tm, k), lambda i, j: (i, 0)),
            pl.BlockSpec((1, k), lambda i, j: (0, 0)),
            pl.BlockSpec((k, tn), lambda i, j: (0, j)),
        ],
        out_specs=pl.BlockSpec((tm, tn), lambda i, j: (i, j)),
        out_shape=jax.ShapeDtypeStruct((t, n), out_dtype),
        scratch_shapes=[pltpu.VMEM((tm, k), F32 if precise else BF)],
        compiler_params=_cp(("parallel", "arbitrary")),
        name="rms_matmul",
    )(x, g, w)


def _matmul_res_kernel(a_ref, w_ref, r_ref, o_ref):
    o_ref[...] = r_ref[...] + _dot(a_ref[...].astype(BF), w_ref[...])


def _matmul_res(a, w, res, *, tn=512):
    t, k = a.shape
    n = w.shape[1]
    tm = min(ROW_TILE, t)
    return pl.pallas_call(
        _matmul_res_kernel,
        grid=(t // tm, n // tn),
        in_specs=[
            pl.BlockSpec((tm, k), lambda i, j: (i, 0)),
            pl.BlockSpec((k, tn), lambda i, j: (0, j)),
            pl.BlockSpec((tm, tn), lambda i, j: (i, j)),
        ],
        out_specs=pl.BlockSpec((tm, tn), lambda i, j: (i, j)),
        out_shape=jax.ShapeDtypeStruct((t, n), F32),
        compiler_params=_cp(("parallel", "arbitrary")),
        name="matmul_res",
    )(a, w, res)


FF_CHUNK = 256


def _ffn_kernel(x_ref, g_ref, wgu_ref, wd_ref, o_ref, hid_ref, *, ff):
    x = x_ref[...]
    xn = _rms(x, g_ref[...]).astype(BF)
    for c in range(ff // FF_CHUNK):
        lo = c * FF_CHUNK
        gate = _dot(xn, wgu_ref[:, lo:lo + FF_CHUNK])
        up = _dot(xn, wgu_ref[:, ff + lo:ff + lo + FF_CHUNK])
        hid_ref[:, lo:lo + FF_CHUNK] = (_silu(gate) * up).astype(BF)
    o_ref[...] = x + _dot(hid_ref[...], wd_ref[...])


def _ffn(x, g, wgu, wd):
    t = x.shape[0]
    ff = wd.shape[0]
    tm = min(ROW_TILE, t)
    return pl.pallas_call(
        functools.partial(_ffn_kernel, ff=ff),
        grid=(t // tm,),
        in_specs=[
            pl.BlockSpec((tm, D), lambda i: (i, 0)),
            _full((1, D)),
            pl.BlockSpec((D, 2 * ff), lambda i: (0, 0), pipeline_mode=pl.Buffered(1)),
            pl.BlockSpec((ff, D), lambda i: (0, 0), pipeline_mode=pl.Buffered(1)),
        ],
        out_specs=pl.BlockSpec((tm, D), lambda i: (i, 0)),
        out_shape=jax.ShapeDtypeStruct((t, D), F32),
        scratch_shapes=[pltpu.VMEM((tm, ff), BF)],
        compiler_params=_cp(("parallel",), vmem_mb=48),
        name="ffn",
    )(x, g, wgu, wd)


def _softplus(x):
    return jnp.maximum(x, 0.0) + jnp.log1p(jnp.exp(-jnp.abs(x)))


def _gelu_tanh(x):
    return 0.5 * x * (1.0 + jnp.tanh(0.7978845608028654 * (x + 0.044715 * (x * x * x))))


def _even_kernel(proj_ref, dt_ref, cinit_ref, sinit_ref, cw_ref, cb_ref, dtb_ref, alog_ref,
                 dsk_ref, sg_ref, lng_ref, lnb_ref, ws_ref, bst_ref,
                 ymix_ref, cout_ref, sout_ref, v_ref, ext_ref, ht_ref, *, q, lb, lc):
    c = pl.program_id(1)
    last = pl.num_programs(1) - 1

    @pl.when(c == 0)
    def _():
        ext_ref[0:SUBLANES, :] = cinit_ref[0]
        ht_ref[...] = sinit_ref[0, 0].T

    if lb == q:
        p = proj_ref[0]
        dtr = dt_ref[0]
    else:
        p = jnp.concatenate([proj_ref[0], jnp.zeros((q - lb, ZXU), F32)], axis=0)
        dtr = jnp.concatenate([dt_ref[0], jnp.zeros((q - lb, LANES), F32)], axis=0)
    z = p[:, :SSD_INNER]
    xbc_raw = p[:, SSD_INNER:SSD_INNER + CONV_DIM]
    uv = p[:, SSD_INNER + CONV_DIM:]

    ext_ref[SUBLANES:SUBLANES + q, :] = xbc_raw
    conv = (cb_ref[...] + cw_ref[0:1, :] * ext_ref[5:5 + q, :] + cw_ref[1:2, :] * ext_ref[6:6 + q, :]
            + cw_ref[2:3, :] * ext_ref[7:7 + q, :] + cw_ref[3:4, :] * xbc_raw)

    @pl.when(c == last)
    def _():
        cout_ref[0] = ext_ref[SUBLANES + lc - 3:SUBLANES + lc, :]

    ext_ref[0:SUBLANES, :] = ext_ref[q:q + SUBLANES, :]

    xbc = _silu(conv)
    xs = xbc[:, :SSD_INNER]
    gw = SSD_STATE
    bm = [xbc[:, SSD_INNER + g * gw:SSD_INNER + (g + 1) * gw] for g in range(SSD_GROUPS)]
    cm = [xbc[:, SSD_INNER + (SSD_GROUPS + g) * gw:SSD_INNER + (SSD_GROUPS + g + 1) * gw]
          for g in range(SSD_GROUPS)]

    row = lax.broadcasted_iota(jnp.int32, (q, q), 0)
    col = lax.broadcasted_iota(jnp.int32, (q, q), 1)
    causal = row >= col

    dt = _softplus(dtr + dtb_ref[...])
    if lc < q:
        dt = jnp.where(lax.broadcasted_iota(jnp.int32, (q, LANES), 0) < lc, dt, 0.0)
    a = dt * (-jnp.exp(alog_ref[...]))
    a_cum = _dot_f32(causal.astype(F32), a)
    a_cum_t = a_cum.T
    a_last = a_cum[q - 1:q, :]
    decay_end = jnp.exp(a_last - a_cum)
    ea = jnp.exp(a_cum)
    chunk_decay = jnp.exp(a_last)

    cmb = [m.astype(BF) for m in cm]
    cb = [_dot_nt(cmb[g], bm[g].astype(BF)) for g in range(SSD_GROUPS)]
    bt = [bm[g].T.astype(BF) for g in range(SSD_GROUPS)]
    heads_per_group = SSD_HEADS // SSD_GROUPS
    ys = []
    for r in range(SSD_HEADS):
        g = r // heads_per_group
        lo = r * SSD_HEAD_DIM
        hi = lo + SSD_HEAD_DIM
        seg = a_cum[:, r:r + 1] - a_cum_t[r:r + 1, :]
        lmat = jnp.where(causal, jnp.exp(jnp.minimum(seg, 0.0)), 0.0)
        m = (cb[g] * lmat).astype(BF)
        xh = xs[:, lo:hi]
        xdt = xh * dt[:, r:r + 1]
        y_diag = _dot(m, xdt.astype(BF))
        h_old = ht_ref[:, lo:hi]
        y_off = _dot(cmb[g], h_old.astype(BF)) * ea[:, r:r + 1]
        ys.append(y_diag + y_off + dsk_ref[:, lo:hi] * xh)
        xd = (xdt * decay_end[:, r:r + 1]).astype(BF)
        ht_ref[:, lo:hi] = h_old * chunk_decay[:, r:r + 1] + _dot(bt[g], xd)
    y = jnp.concatenate(ys, axis=1) * _silu(z)
    gi = SSD_INNER // SSD_GROUPS
    yn = [_rms(y[:, g * gi:(g + 1) * gi], sg_ref[:, g * gi:(g + 1) * gi]) for g in range(SSD_GROUPS)]
    ymix_ref[0, :, 0:SSD_INNER] = jnp.concatenate(yn, axis=1)[:lb].astype(BF)

    @pl.when(c == last)
    def _():
        sout_ref[0] = ht_ref[...].T

    uvg = _gelu_tanh(uv)
    u = uvg[:, :GMLP_WIDTH]
    v = uvg[:, GMLP_WIDTH:]
    mu = jnp.mean(v, axis=-1, keepdims=True)
    vc = v - mu
    vn = vc * lax.rsqrt(jnp.mean(vc * vc, axis=-1, keepdims=True) + EPS) * lng_ref[...] + lnb_ref[...]
    if v_ref is not None:
        v_ref[0] = vn[:lb]
    gd = GMLP_WIDTH // GMLP_GROUPS
    yb = []
    for g in range(GMLP_GROUPS):
        wt = jnp.where(causal, ws_ref[g, :q, :q], 0.0).astype(BF)
        sp = _dot(wt, vn[:, g * gd:(g + 1) * gd].astype(BF)) + bst_ref[:q, g:g + 1]
        yb.append(u[:, g * gd:(g + 1) * gd] * sp)
    ymix_ref[0, :, SSD_INNER:SSD_INNER + GMLP_WIDTH] = jnp.concatenate(yb, axis=1)[:lb].astype(BF)


def _even_kernel_no_v(*refs, q, lb, lc):
    n_in = 14
    ins, outs, scr = refs[:n_in], refs[n_in:n_in + 3], refs[n_in + 3:]
    _even_kernel(*ins, *outs, None, *scr, q=q, lb=lb, lc=lc)


def _even_mixer(proj, dtp, cinit8, sinit, prm, *, q, lb, lc, want_v, layer):
    b, lp, _ = proj.shape
    nchunks = lp // lb
    par = [prm["conv_w"], prm["conv_b"], prm["dt_bias"], prm["a_log"], prm["d_skip"], prm["ssd_gain"],
           prm["ln_g"], prm["ln_b"], prm["ws"], prm["bst"]]
    in_specs = [
        pl.BlockSpec((1, lb, ZXU), lambda i, c: (i, c, 0)),
        pl.BlockSpec((1, lb, LANES), lambda i, c: (i, c, 0)),
        pl.BlockSpec((1, SUBLANES, CONV_DIM), lambda i, c: (i, 0, 0)),
        pl.BlockSpec((1, 1, SSD_INNER, SSD_STATE), lambda i, c: (layer, i, 0, 0)),
    ] + [_full(w.shape) for w in par]
    out_specs = [
        pl.BlockSpec((1, lb, SSD_INNER + GMLP_WIDTH), lambda i, c: (i, c, 0)),
        pl.BlockSpec((1, SSD_CONV - 1, CONV_DIM), lambda i, c: (i, 0, 0)),
        pl.BlockSpec((1, SSD_INNER, SSD_STATE), lambda i, c: (i, 0, 0)),
    ]
    out_shape = [
        jax.ShapeDtypeStruct((b, lp, SSD_INNER + GMLP_WIDTH), BF),
        jax.ShapeDtypeStruct((b, SSD_CONV - 1, CONV_DIM), F32),
        jax.ShapeDtypeStruct((b, SSD_INNER, SSD_STATE), F32),
    ]
    if want_v:
        out_specs.append(pl.BlockSpec((1, lb, GMLP_WIDTH), lambda i, c: (i, c, 0)))
        out_shape.append(jax.ShapeDtypeStruct((b, lp, GMLP_WIDTH), F32))
        body = functools.partial(_even_kernel, q=q, lb=lb, lc=lc)
    else:
        body = functools.partial(_even_kernel_no_v, q=q, lb=lb, lc=lc)
    return pl.pallas_call(
        body,
        grid=(b, nchunks),
        in_specs=in_specs,
        out_specs=out_specs,
        out_shape=out_shape,
        scratch_shapes=[pltpu.VMEM((q + 2 * SUBLANES, CONV_DIM), F32),
                        pltpu.VMEM((SSD_STATE, SSD_INNER), F32)],
        compiler_params=_cp(("parallel", "arbitrary"), vmem_mb=48),
        name="even_mixer",
    )(proj, dtp, cinit8, sinit, *par)


def _mla_proj_kernel(x_ref, g_ref, wdq_ref, wdkv_ref, wdk2_ref, qg_ref, kvg_ref, wqn_ref, wqp_ref,
                     wuk_ref, cos_ref, sin_ref, ckv_ref, kpe_ref, kcat_ref, q_ref):
    xn = _rms(x_ref[...], g_ref[...]).astype(BF)
    cqn = _rms(_dot(xn, wdq_ref[...]), qg_ref[...]).astype(BF)
    ckv = _rms(_dot(xn, wdkv_ref[...]), kvg_ref[...])
    kk = _dot(xn, wdk2_ref[...])
    cos = cos_ref[...]
    sin = sin_ref[...]
    kpe = kk[:, :ROPE] * cos[:, :ROPE] + kk[:, ROPE:] * sin[:, :ROPE]
    ckv_ref[...] = ckv
    kpe_ref[...] = kpe
    kcat_ref[:, :KV_RANK] = ckv.astype(BF)
    kcat_ref[:, KV_RANK:] = kpe.astype(BF)
    qn = _dot(cqn, wqn_ref[...])
    qp = _dot(cqn, wqp_ref[...])
    hr = MLA_HEADS * ROPE
    qpe = qp[:, :hr] * cos + qp[:, hr:] * sin
    for h in range(MLA_HEADS):
        ql = _dot(qn[:, h * NOPE:(h + 1) * NOPE].astype(BF), wuk_ref[h])
        q_ref[0, h, :, :KV_RANK] = (ql * MLA_SCALE).astype(BF)
        q_ref[0, h, :, KV_RANK:] = (qpe[:, h * ROPE:(h + 1) * ROPE] * MLA_SCALE).astype(BF)


def _mla_proj(x, g, prm, cos8, sin8, *, nb):
    t = x.shape[0]
    seq = t // nb
    tm = min(ROW_TILE, seq)
    tpb = seq // tm
    w = [prm["wdq"], prm["wdkv"], prm["wdk2"], prm["q_gain"], prm["kv_gain"], prm["wqn"], prm["wqp"],
         prm["wuk"]]
    return pl.pallas_call(
        _mla_proj_kernel,
        grid=(t // tm,),
        in_specs=[pl.BlockSpec((tm, D), lambda i: (i, 0)), _full((1, D))] + [_full(a.shape) for a in w] + [
            pl.BlockSpec((tm, MLA_HEADS * ROPE), lambda i: (i % tpb, 0)),
            pl.BlockSpec((tm, MLA_HEADS * ROPE), lambda i: (i % tpb, 0)),
        ],
        out_specs=[
            pl.BlockSpec((tm, KV_RANK), lambda i: (i, 0)),
            pl.BlockSpec((tm, ROPE), lambda i: (i, 0)),
            pl.BlockSpec((tm, QK), lambda i: (i, 0)),
            pl.BlockSpec((1, MLA_HEADS, tm, QK), lambda i: (i // tpb, 0, i % tpb, 0)),
        ],
        out_shape=[
            jax.ShapeDtypeStruct((t, KV_RANK), F32),
            jax.ShapeDtypeStruct((t, ROPE), F32),
            jax.ShapeDtypeStruct((t, QK), BF),
            jax.ShapeDtypeStruct((nb, MLA_HEADS, seq, QK), BF),
        ],
        compiler_params=_cp(("parallel",)),
        name="mla_proj",
    )(x, g, *w, cos8, sin8)


HEAD_QK = NOPE + ROPE
LOG2E = 1.4426950408889634


def _mla_proj_prompt_kernel(x_ref, g_ref, wdq_ref, wdkv_ref, wdk2_ref, qg_ref, kvg_ref, wqn_ref, wqp_ref,
                            wkn_ref, wv_ref, cos_ref, sin_ref, ckv_ref, kpe_ref, q_ref, k_ref, v_ref):
    xn = _rms(x_ref[...], g_ref[...]).astype(BF)
    cqn = _rms(_dot(xn, wdq_ref[...]), qg_ref[...]).astype(BF)
    ckv = _rms(_dot(xn, wdkv_ref[...]), kvg_ref[...])
    kk = _dot(xn, wdk2_ref[...])
    cos = cos_ref[...]
    sin = sin_ref[...]
    kpe = kk[:, :ROPE] * cos[:, :ROPE] + kk[:, ROPE:] * sin[:, :ROPE]
    ckv_ref[...] = ckv
    kpe_ref[...] = kpe
    ckv_b = ckv.astype(BF)
    kpe_b = kpe.astype(BF)
    kn = _dot(ckv_b, wkn_ref[...])
    vv = _dot(ckv_b, wv_ref[...])
    qn = _dot(cqn, wqn_ref[...])
    qp = _dot(cqn, wqp_ref[...])
    hr = MLA_HEADS * ROPE
    qpe = qp[:, :hr] * cos + qp[:, hr:] * sin
    qscale = MLA_SCALE * LOG2E
    for h in range(MLA_HEADS):
        q_ref[0, h, :, :NOPE] = (qn[:, h * NOPE:(h + 1) * NOPE] * qscale).astype(BF)
        q_ref[0, h, :, NOPE:] = (qpe[:, h * ROPE:(h + 1) * ROPE] * qscale).astype(BF)
        k_ref[0, h, :, :NOPE] = kn[:, h * NOPE:(h + 1) * NOPE].astype(BF)
        k_ref[0, h, :, NOPE:] = kpe_b
        v_ref[0, h] = vv[:, h * MLA_V:(h + 1) * MLA_V].astype(BF)


def _mla_proj_prompt(x, g, prm, cos8, sin8, *, nb):
    t = x.shape[0]
    seq = t // nb
    tm = min(ROW_TILE, seq)
    tpb = seq // tm
    w = [prm["wdq"], prm["wdkv"], prm["wdk2"], prm["q_gain"], prm["kv_gain"], prm["wqn"], prm["wqp"],
         prm["wkn"], prm["wv"]]
    head_spec = lambda width: pl.BlockSpec((1, MLA_HEADS, tm, width), lambda i: (i // tpb, 0, i % tpb, 0))
    head_shape = lambda width: jax.ShapeDtypeStruct((nb, MLA_HEADS, seq, width), BF)
    return pl.pallas_call(
        _mla_proj_prompt_kernel,
        grid=(t // tm,),
        in_specs=[pl.BlockSpec((tm, D), lambda i: (i, 0)), _full((1, D))] + [_full(a.shape) for a in w] + [
            pl.BlockSpec((tm, MLA_HEADS * ROPE), lambda i: (i % tpb, 0)),
            pl.BlockSpec((tm, MLA_HEADS * ROPE), lambda i: (i % tpb, 0)),
        ],
        out_specs=[
            pl.BlockSpec((tm, KV_RANK), lambda i: (i, 0)),
            pl.BlockSpec((tm, ROPE), lambda i: (i, 0)),
            head_spec(HEAD_QK), head_spec(HEAD_QK), head_spec(MLA_V),
        ],
        out_shape=[
            jax.ShapeDtypeStruct((t, KV_RANK), F32),
            jax.ShapeDtypeStruct((t, ROPE), F32),
            head_shape(HEAD_QK), head_shape(HEAD_QK), head_shape(MLA_V),
        ],
        compiler_params=_cp(("parallel",)),
        name="mla_proj_prompt",
    )(x, g, *w, cos8, sin8)


FLASH_T = 512
FLASH_HEADS = 2


def _flash_kernel(q_ref, k_ref, v_ref, o_ref, m_ref, acc_ref, *, t):
    qi = pl.program_id(2)
    for hh in range(FLASH_HEADS):
        m_ref[hh] = jnp.full((t, 1), NEG, F32)
        acc_ref[hh] = jnp.zeros((t, 2 * MLA_V), F32)
    ones = jnp.ones((t, MLA_V), BF)

    def block(ki, masked):
        start = pl.multiple_of(ki * t, t)
        for hh in range(FLASH_HEADS):
            kb = k_ref[0, hh, pl.ds(start, t), :]
            vb = v_ref[0, hh, pl.ds(start, t), :]
            s = _dot_nt(q_ref[0, hh], kb)
            if masked:
                row = lax.broadcasted_iota(jnp.int32, (t, t), 0)
                col = lax.broadcasted_iota(jnp.int32, (t, t), 1)
                s = jnp.where(col <= row, s, NEG)
            m_old = m_ref[hh]
            m_new = jnp.maximum(m_old, jnp.max(s, axis=-1, keepdims=True))
            alpha = jnp.exp2(m_old - m_new)
            p = jnp.exp2(s - m_new).astype(BF)
            acc_ref[hh] = alpha * acc_ref[hh] + _dot(p, jnp.concatenate([vb, ones], axis=1))
            m_ref[hh] = m_new

    def body(ki, carry):
        block(ki, False)
        return carry

    lax.fori_loop(0, qi, body, 0)
    block(qi, True)
    for hh in range(FLASH_HEADS):
        acc = acc_ref[hh]
        o_ref[0, :, hh * MLA_V:(hh + 1) * MLA_V] = (acc[:, :MLA_V] / acc[:, MLA_V:MLA_V + 1]).astype(BF)


def _flash(q, k, v):
    nb, _, seq, _ = q.shape
    t = min(FLASH_T, seq)
    nh = FLASH_HEADS
    return pl.pallas_call(
        functools.partial(_flash_kernel, t=t),
        grid=(nb, MLA_HEADS // nh, seq // t),
        in_specs=[
            pl.BlockSpec((1, nh, t, HEAD_QK), lambda b, h, i: (b, h, i, 0)),
            pl.BlockSpec((1, nh, seq, HEAD_QK), lambda b, h, i: (b, h, 0, 0)),
            pl.BlockSpec((1, nh, seq, MLA_V), lambda b, h, i: (b, h, 0, 0)),
        ],
        out_specs=pl.BlockSpec((1, t, nh * MLA_V), lambda b, h, i: (b, i, h)),
        out_shape=jax.ShapeDtypeStruct((nb, seq, MLA_HEADS * MLA_V), BF),
        scratch_shapes=[pltpu.VMEM((nh, t, 1), F32), pltpu.VMEM((nh, t, 2 * MLA_V), F32)],
        compiler_params=_cp(("parallel", "parallel", "arbitrary"), vmem_mb=48),
        name="mla_flash",
    )(q, k, v)


def _page_copies(pt_ref, ckv_hbm, kpt_hbm, ckbuf, kpbuf, sem, sample, slot, *, layer, n_pages):
    copies = []
    for p in range(n_pages):
        pg = pt_ref[sample, p]
        copies.append(pltpu.make_async_copy(
            ckv_hbm.at[layer, pg], ckbuf.at[slot, pl.ds(p * PAGE, PAGE), :], sem.at[0, slot]))
        copies.append(pltpu.make_async_copy(
            kpt_hbm.at[layer, pg], kpbuf.at[slot, :, pl.ds(p * PAGE, PAGE)], sem.at[1, slot]))
    return copies


def _decode_kernel(pt_ref, q_ref, nk_ref, ckv_hbm, kpt_hbm, o_ref, ckbuf, kpbuf, sem, *,
                   dec_seq, layer, n_pages):
    b = pl.program_id(0)
    last = pl.num_programs(0) - 1
    slot = b % 2
    copies = functools.partial(_page_copies, pt_ref, ckv_hbm, kpt_hbm, ckbuf, kpbuf, sem,
                               layer=layer, n_pages=n_pages)

    @pl.when(b == 0)
    def _():
        for c in copies(0, 0):
            c.start()

    for c in copies(b, slot):
        c.wait()
    nxt = jnp.minimum(b + 1, last)
    for c in copies(nxt, 1 - slot):
        c.start()

    rows = MLA_HEADS * dec_seq
    qm = q_ref[0]
    ql = qm[:, :KV_RANK]
    qp = qm[:, KV_RANK:]
    ck = ckbuf[slot].astype(BF)
    kp = kpbuf[slot].astype(BF)
    s = _dot_nt(ql, ck) + _dot(qp, kp)
    nk = nk_ref[0]
    kt = lax.broadcasted_iota(jnp.int32, (rows, NEW_KEY_ROWS), 1)
    qt = lax.broadcasted_iota(jnp.int32, (rows, NEW_KEY_ROWS), 0) % dec_seq
    s_new = jnp.where(kt <= qt, _dot_nt(qm, nk), NEG)
    m = jnp.maximum(jnp.max(s, axis=-1, keepdims=True), jnp.max(s_new, axis=-1, keepdims=True))
    p = jnp.exp(s - m)
    p_new = jnp.exp(s_new - m)
    denom = jnp.sum(p, axis=-1, keepdims=True) + jnp.sum(p_new, axis=-1, keepdims=True)
    o_ref[0] = (_dot(p.astype(BF), ck) + _dot(p_new.astype(BF), nk[:, :KV_RANK])) / denom

    @pl.when(b == last)
    def _():
        for c in copies(nxt, 1 - slot):
            c.wait()


def _decode(page_table, q, newk, cache_ckv, cache_kpe_t, layer, dec_seq):
    nb, n_pages = page_table.shape
    rows = MLA_HEADS * dec_seq
    keys = n_pages * PAGE
    grid_spec = pltpu.PrefetchScalarGridSpec(
        num_scalar_prefetch=1,
        grid=(nb,),
        in_specs=[
            pl.BlockSpec((1, rows, QK), lambda b, pt: (b, 0, 0)),
            pl.BlockSpec((1, NEW_KEY_ROWS, QK), lambda b, pt: (b, 0, 0)),
            pl.BlockSpec(memory_space=pl.ANY),
            pl.BlockSpec(memory_space=pl.ANY),
        ],
        out_specs=pl.BlockSpec((1, rows, KV_RANK), lambda b, pt: (b, 0, 0)),
        scratch_shapes=[pltpu.VMEM((2, keys, KV_RANK), F32), pltpu.VMEM((2, ROPE, keys), F32),
                        pltpu.SemaphoreType.DMA((2, 2))],
    )
    return pl.pallas_call(
        functools.partial(_decode_kernel, dec_seq=dec_seq, layer=layer, n_pages=n_pages),
        grid_spec=grid_spec,
        out_shape=jax.ShapeDtypeStruct((nb, rows, KV_RANK), F32),
        compiler_params=_cp(("arbitrary",), vmem_mb=48),
        name="mla_decode",
    )(page_table, q, newk, cache_ckv, cache_kpe_t)


def _mla_out_kernel(o_ref, wuv_ref, wo_ref, r_ref, out_ref):
    parts = [_dot(o_ref[0, h], wuv_ref[h]).astype(BF) for h in range(MLA_HEADS)]
    out_ref[...] = r_ref[...] + _dot(jnp.concatenate(parts, axis=1), wo_ref[...])


def _mla_out(o_lat, wuv, wo, res):
    nb, _, seq, _ = o_lat.shape
    tm = min(ROW_TILE, seq)
    tpb = seq // tm
    t = nb * seq
    return pl.pallas_call(
        _mla_out_kernel,
        grid=(t // tm,),
        in_specs=[
            pl.BlockSpec((1, MLA_HEADS, tm, KV_RANK), lambda i: (i // tpb, 0, i % tpb, 0)),
            _full(wuv.shape), _full(wo.shape),
            pl.BlockSpec((tm, D), lambda i: (i, 0)),
        ],
        out_specs=pl.BlockSpec((tm, D), lambda i: (i, 0)),
        out_shape=jax.ShapeDtypeStruct((t, D), F32),
        compiler_params=_cp(("parallel",)),
        name="mla_out",
    )(o_lat, wuv, wo, res)


def _softmax_rows(s):
    m = jnp.max(s, axis=-1, keepdims=True)
    p = jnp.exp(s - m)
    return p / jnp.sum(p, axis=-1, keepdims=True)


def _mem_prompt_kernel(x_ref, g_ref, wq_ref, k_ref, v_ref, wo_ref, o_ref):
    x = x_ref[...]
    xn = _rms(x, g_ref[...]).astype(BF)
    qm = (_dot(xn, wq_ref[...]) * MEM_HD ** -0.5).astype(BF)
    km = k_ref[0].astype(BF)
    vm = v_ref[0].astype(BF)
    parts = []
    for h in range(MEM_HEADS):
        sl = slice(h * MEM_HD, (h + 1) * MEM_HD)
        p = _softmax_rows(_dot_nt(qm[:, sl], km[:, sl]))
        parts.append(_dot(p.astype(BF), vm[:, sl]).astype(BF))
    o_ref[...] = x + _dot(jnp.concatenate(parts, axis=1), wo_ref[...])


def _mem_prompt(x, g, wq, km, vm, wo):
    t = x.shape[0]
    nb, mt, _ = km.shape
    seq = t // nb
    tm = min(ROW_TILE, seq)
    tpb = seq // tm
    return pl.pallas_call(
        _mem_prompt_kernel,
        grid=(t // tm,),
        in_specs=[
            pl.BlockSpec((tm, D), lambda i: (i, 0)), _full((1, D)), _full(wq.shape),
            pl.BlockSpec((1, mt, MEM_INNER), lambda i: (i // tpb, 0, 0)),
            pl.BlockSpec((1, mt, MEM_INNER), lambda i: (i // tpb, 0, 0)),
            _full(wo.shape),
        ],
        out_specs=pl.BlockSpec((tm, D), lambda i: (i, 0)),
        out_shape=jax.ShapeDtypeStruct((t, D), F32),
        compiler_params=_cp(("parallel",)),
        name="mem_attn_prompt",
    )(x, g, wq, km, vm, wo)


MEM_ROWS = MEM_HEADS * SUBLANES


def _mem_sample_kernel(x_ref, g_ref, wq_ref, k_ref, v_ref, wo_ref, o_ref):
    ns = SAMPLES_PER_STEP
    x = x_ref[...].reshape(ns * SUBLANES, D)
    xn = _rms(x, g_ref[...]).astype(BF)
    qall = _dot(xn, wq_ref[...]) * MEM_HD ** -0.5
    cols = k_ref.shape[2]
    head_of_row = lax.broadcasted_iota(jnp.int32, (MEM_ROWS, cols), 0) // SUBLANES
    head_of_col = lax.broadcasted_iota(jnp.int32, (MEM_ROWS, cols), 1) % MEM_HEADS
    own = head_of_row == head_of_col
    outs = []
    for s in range(ns):
        qs = qall[s * SUBLANES:(s + 1) * SUBLANES]
        qst = jnp.concatenate([qs[:, h * MEM_HD:(h + 1) * MEM_HD] for h in range(MEM_HEADS)], axis=0)
        sc = jnp.where(own, _dot_nt(qst.astype(BF), k_ref[0, s].astype(BF)), NEG)
        o = _dot(_softmax_rows(sc).astype(BF), v_ref[0, s].astype(BF))
        outs.append(jnp.concatenate([o[h * SUBLANES:(h + 1) * SUBLANES] for h in range(MEM_HEADS)], axis=1))
    out = x + _dot(jnp.concatenate(outs, axis=0).astype(BF), wo_ref[...])
    o_ref[...] = out.reshape(ns, SUBLANES, D)


def _mem_sample(x8, g, wq, cache_k, cache_v, wo, layer):
    nb = x8.shape[0]
    rows = cache_k.shape[2]
    ns = SAMPLES_PER_STEP
    return pl.pallas_call(
        _mem_sample_kernel,
        grid=(nb // ns,),
        in_specs=[
            pl.BlockSpec((ns, SUBLANES, D), lambda i: (i, 0, 0)), _full((1, D)), _full(wq.shape),
            pl.BlockSpec((1, ns, rows, MEM_HD), lambda i: (layer, i, 0, 0)),
            pl.BlockSpec((1, ns, rows, MEM_HD), lambda i: (layer, i, 0, 0)),
            _full(wo.shape),
        ],
        out_specs=pl.BlockSpec((ns, SUBLANES, D), lambda i: (i, 0, 0)),
        out_shape=jax.ShapeDtypeStruct((nb, SUBLANES, D), F32),
        compiler_params=_cp(("parallel",), vmem_mb=48),
        name="mem_attn_sample",
    )(x8, g, wq, cache_k, cache_v, wo)


def _router_kernel(x_ref, g_ref, wr_ref, xn_ref, route_ref):
    xn = _rms(x_ref[...], g_ref[...])
    xn_ref[...] = xn
    lane = lax.broadcasted_iota(jnp.int32, (xn.shape[0], LANES), 1).astype(F32)
    lg = jnp.where(lane < N_EXP, _dot_f32(xn, wr_ref[...]), NEG)
    m1 = jnp.max(lg, axis=-1, keepdims=True)
    i1 = jnp.min(jnp.where(lg == m1, lane, float(LANES)), axis=-1, keepdims=True)
    lg2 = jnp.where(lane == i1, NEG, lg)
    m2 = jnp.max(lg2, axis=-1, keepdims=True)
    i2 = jnp.min(jnp.where(lg2 == m2, lane, float(LANES)), axis=-1, keepdims=True)
    e = jnp.exp(m2 - m1)
    g1 = 1.0 / (1.0 + e)
    g2 = e * g1
    route_ref[...] = jnp.where(lane == 0, i1, jnp.where(lane == 1, i2, jnp.where(lane == 2, g1,
                               jnp.where(lane == 3, g2, 0.0))))


def _router(x, g, wr_pad):
    t = x.shape[0]
    tm = min(ROW_TILE, t)
    return pl.pallas_call(
        _router_kernel,
        grid=(t // tm,),
        in_specs=[pl.BlockSpec((tm, D), lambda i: (i, 0)), _full((1, D)), _full(wr_pad.shape)],
        out_specs=[pl.BlockSpec((tm, D), lambda i: (i, 0)), pl.BlockSpec((tm, LANES), lambda i: (i, 0))],
        out_shape=[jax.ShapeDtypeStruct((t, D), F32), jax.ShapeDtypeStruct((t, LANES), F32)],
        compiler_params=_cp(("parallel",)),
        name="router",
    )(x, g, wr_pad)


SC_ROWS = 32


def _sc_mesh():
    return plsc.VectorSubcoreMesh(core_axis_name="c", subcore_axis_name="s")


def _sc_workers():
    info = plsc.get_sparse_core_info()
    return info.num_cores, info.num_cores * info.num_subcores


def _sc_gather_rows(table, idx):
    n = idx.shape[0]
    width = table.shape[1]
    ncores, nw = _sc_workers()
    per_w = n // nw
    n_chunks = per_w // SC_ROWS
    assert per_w * nw == n and n_chunks * SC_ROWS == per_w

    @functools.partial(
        pl.kernel, mesh=_sc_mesh(),
        out_type=jax.ShapeDtypeStruct((n, width), table.dtype),
        scratch_types=[pltpu.VMEM((per_w,), jnp.int32), pltpu.VMEM((SC_ROWS, width), table.dtype),
                       pltpu.SemaphoreType.DMA],
        name="sc_gather_rows",
    )
    def body(table_hbm, idx_hbm, out_hbm, idx_v, rows_v, sem):
        wid = lax.axis_index("s") * ncores + lax.axis_index("c")
        base = wid * per_w
        pltpu.sync_copy(idx_hbm.at[pl.ds(base, per_w)], idx_v)

        @pl.loop(0, n_chunks)
        def _(j):
            off = pl.multiple_of(j * SC_ROWS, SC_ROWS)
            pltpu.async_copy(table_hbm.at[idx_v.at[pl.ds(off, SC_ROWS)]], rows_v, sem).wait()
            pltpu.sync_copy(rows_v, out_hbm.at[pl.ds(base + off, SC_ROWS)])

    return body(table, idx)


def _sc_scatter_rows(src, idx, n_out):
    n = idx.shape[0]
    t, width = src.shape
    ncores, nw = _sc_workers()
    per_w = n // nw
    n_chunks = per_w // SC_ROWS
    assert per_w * nw == n and n_chunks * SC_ROWS == per_w and t % per_w == 0
    idx3 = idx.reshape(nw, n_chunks, SC_ROWS)

    @functools.partial(
        pl.kernel, mesh=_sc_mesh(),
        out_type=jax.ShapeDtypeStruct((n_out, width), src.dtype),
        scratch_types=[pltpu.VMEM((n_chunks, SC_ROWS), jnp.int32), pltpu.VMEM((SC_ROWS, width), src.dtype),
                       pltpu.SemaphoreType.DMA],
        name="sc_scatter_rows",
    )
    def body(src_hbm, idx_hbm, out_hbm, idx_v, rows_v, sem):
        wid = lax.axis_index("s") * ncores + lax.axis_index("c")
        base = lax.rem(wid * per_w, t)
        pltpu.sync_copy(idx_hbm.at[wid], idx_v)

        @pl.loop(0, n_chunks)
        def _(j):
            off = pl.multiple_of(j * SC_ROWS, SC_ROWS)
            pltpu.sync_copy(src_hbm.at[pl.ds(base + off, SC_ROWS)], rows_v)
            pltpu.async_copy(rows_v, out_hbm.at[idx_v.at[j]], sem).wait()

    return body(src, idx3)


MOE_HALF = D_FFE // 2
MOE_CHUNK = 256


def _moe_ffn_kernel(te_ref, nused_ref, x_ref, wg_ref, wu_ref, wd_ref, o_ref, hid_ref):
    del te_ref
    i = pl.program_id(0)
    j = pl.program_id(1)

    @pl.when(i < nused_ref[0])
    def _():
        xb = x_ref[...].astype(BF)
        for c in range(MOE_HALF // MOE_CHUNK):
            lo = c * MOE_CHUNK
            gate = _dot(xb, wg_ref[0, :, lo:lo + MOE_CHUNK])
            up = _dot(xb, wu_ref[0, :, lo:lo + MOE_CHUNK])
            hid_ref[:, lo:lo + MOE_CHUNK] = (_silu(gate) * up).astype(BF)
        part = _dot(hid_ref[...], wd_ref[0])

        @pl.when(j == 0)
        def _():
            o_ref[...] = part

        @pl.when(j != 0)
        def _():
            o_ref[...] += part

    @pl.when(i >= nused_ref[0])
    def _():
        o_ref[...] = jnp.zeros_like(o_ref)


def _moe_ffn(tile_expert, nused, xs, wgu, wd, *, tm):
    npad = xs.shape[0]
    nh = D_FFE // MOE_HALF
    grid_spec = pltpu.PrefetchScalarGridSpec(
        num_scalar_prefetch=2,
        grid=(npad // tm, nh),
        in_specs=[
            pl.BlockSpec((tm, D), lambda i, j, te, nu: (i, 0)),
            pl.BlockSpec((1, D, MOE_HALF), lambda i, j, te, nu: (te[i], 0, j)),
            pl.BlockSpec((1, D, MOE_HALF), lambda i, j, te, nu: (te[i], 0, nh + j)),
            pl.BlockSpec((1, MOE_HALF, D), lambda i, j, te, nu: (te[i], j, 0)),
        ],
        out_specs=pl.BlockSpec((tm, D), lambda i, j, te, nu: (i, 0)),
        scratch_shapes=[pltpu.VMEM((tm, MOE_HALF), BF)],
    )
    return pl.pallas_call(
        _moe_ffn_kernel,
        grid_spec=grid_spec,
        out_shape=jax.ShapeDtypeStruct((npad, D), F32),
        compiler_params=_cp(("parallel", "arbitrary"), vmem_mb=56),
        name="moe_ffn",
    )(tile_expert, nused, xs, wgu, wgu, wd)


ROUTE_GATE_LANE = 2


def _combine_kernel(h_ref, y0_ref, y1_ref, route_ref, g_ref, o_ref, *, final):
    gl = ROUTE_GATE_LANE
    route = route_ref[...]
    out = h_ref[...] + route[:, gl:gl + 1] * y0_ref[...] + route[:, gl + 1:gl + 2] * y1_ref[...]
    if final:
        out = _rms(out, g_ref[...])
    o_ref[...] = out


def _combine(h, y2, route, g, *, final):
    t = h.shape[0]
    tm = min(ROW_TILE, t)
    nt = t // tm
    return pl.pallas_call(
        functools.partial(_combine_kernel, final=final),
        grid=(nt,),
        in_specs=[
            pl.BlockSpec((tm, D), lambda i: (i, 0)),
            pl.BlockSpec((tm, D), lambda i: (i, 0)),
            pl.BlockSpec((tm, D), lambda i: (nt + i, 0)),
            pl.BlockSpec((tm, LANES), lambda i: (i, 0)),
            _full((1, D)),
        ],
        out_specs=pl.BlockSpec((tm, D), lambda i: (i, 0)),
        out_shape=jax.ShapeDtypeStruct((t, D), F32),
        compiler_params=_cp(("parallel",)),
        name="moe_combine",
    )(h, y2, y2, route, g)


def _moe(h, g, wr_pad, wgu, wd, final_g, *, final):
    t = h.shape[0]
    tm = min(ROW_TILE, t)
    xn, route = _router(h, g, wr_pad)
    eidx = route[:, :ROUTE_GATE_LANE].astype(jnp.int32)
    e_flat = eidx.T.reshape(-1)
    onehot = (e_flat[:, None] == jnp.arange(N_EXP, dtype=jnp.int32)[None, :]).astype(jnp.int32)
    csum = jnp.cumsum(onehot, axis=0)
    counts = csum[-1]
    rank = jnp.sum(onehot * csum, axis=1) - 1
    padded = ((counts + tm - 1) // tm) * tm
    ends = jnp.cumsum(padded)
    starts = ends - padded
    dest = (jnp.sum(onehot * starts[None, :], axis=1) + rank).astype(jnp.int32)
    n_tiles = (2 * t) // tm + N_EXP
    tile_start = jnp.arange(n_tiles, dtype=jnp.int32) * tm
    tile_expert = jnp.minimum(jnp.sum((tile_start[:, None] >= ends[None, :]).astype(jnp.int32), axis=1),
                              N_EXP - 1).astype(jnp.int32)
    nused = (ends[-1] // tm).astype(jnp.int32).reshape(1)
    xs = _sc_scatter_rows(xn, dest, n_tiles * tm)
    ys = _moe_ffn(tile_expert, nused, xs, wgu, wd, tm=tm)
    y2 = _sc_gather_rows(ys, dest)
    return _combine(h, y2, route, final_g, final=final)


def _rope_tables(pos):
    half = ROPE // 2
    inv_freq = ROPE_BASE ** (-jnp.arange(half, dtype=F32) / half)
    ang = pos.astype(F32)[:, None] * inv_freq
    cos = jnp.cos(ang)
    sin = jnp.sin(ang)
    cc = jnp.concatenate([cos, cos], axis=-1)
    ss = jnp.concatenate([-sin, sin], axis=-1)
    return jnp.tile(cc, (1, MLA_HEADS)), jnp.tile(ss, (1, MLA_HEADS))


def _even_layer(h, g, prm, cinit, sinit, layer, *, nb, sample):
    t = h.shape[0]
    seq = t // nb
    proj = _rms_matmul(h, g, prm["w_zxu"], tn=_largest_tile(ZXU, 1536))
    dtp = _rms_matmul(h, g, prm["w_dt"], tn=LANES, precise=True)
    proj = proj.reshape(nb, seq, ZXU)
    dtp = dtp.reshape(nb, seq, LANES)
    if sample:
        pad = ((0, 0), (0, SUBLANES - seq), (0, 0))
        proj = jnp.pad(proj, pad)
        dtp = jnp.pad(dtp, pad)
        q, lb, lc = SAMPLE_Q, SUBLANES, seq
    else:
        q, lb, lc = CHUNK, CHUNK, CHUNK
    cinit8 = jnp.pad(cinit, ((0, 0), (SUBLANES - (SSD_CONV - 1), 0), (0, 0)))
    outs = _even_mixer(proj, dtp, cinit8, sinit.reshape(-1, nb, SSD_INNER, SSD_STATE), prm,
                       q=q, lb=lb, lc=lc, want_v=sample, layer=layer)
    ymix, cout, sout = outs[:3]
    v = None
    if sample:
        ymix = ymix[:, :seq]
        v = outs[3][:, :seq]
    h = _matmul_res(ymix.reshape(t, SSD_INNER + GMLP_WIDTH), prm["w_out"], h)
    return h, cout, sout.reshape(nb, SSD_HEADS, SSD_HEAD_DIM, SSD_STATE), v


def _prep_even(i, w_in, conv_w, conv_b, dt_bias, a_log, d_skip, ssd_gain, ln_g, ln_b, ws, bs, w_out):
    w = w_in[i]
    o1 = SSD_INNER + CONV_DIM
    w_zxu = jnp.concatenate([w[:, :o1], w[:, o1 + SSD_HEADS:]], axis=1).astype(BF)
    w_dt = jnp.pad(w[:, o1:o1 + SSD_HEADS], ((0, 0), (0, LANES - SSD_HEADS)))
    padl = (0, LANES - SSD_HEADS)
    return dict(
        w_zxu=w_zxu, w_dt=w_dt,
        conv_w=jnp.pad(conv_w[i], ((0, SUBLANES - SSD_CONV), (0, 0))),
        conv_b=conv_b[i][None, :],
        dt_bias=jnp.pad(dt_bias[i], padl)[None, :],
        a_log=jnp.pad(a_log[i], padl)[None, :],
        d_skip=jnp.repeat(d_skip[i], SSD_HEAD_DIM)[None, :],
        ssd_gain=ssd_gain[i][None, :],
        ln_g=ln_g[i][None, :], ln_b=ln_b[i][None, :],
        ws=ws[i], bst=bs[i].T,
        w_out=w_out[i].astype(BF),
    )


def _prep_mla(i, w_down, q_gain, kv_gain, w_uq, w_uk, w_uv, w_o):
    wd = w_down[i]
    wk = wd[:, Q_RANK + KV_RANK:]
    half = ROPE // 2
    rot = lambda a: jnp.concatenate([a[..., half:], a[..., :half]], axis=-1)
    uq = w_uq[i]
    uq_pe = uq[:, :, NOPE:]
    return dict(
        wdq=wd[:, :Q_RANK].astype(BF),
        wdkv=wd[:, Q_RANK:Q_RANK + KV_RANK].astype(BF),
        wdk2=jnp.concatenate([wk, rot(wk)], axis=1).astype(BF),
        q_gain=q_gain[i][None, :], kv_gain=kv_gain[i][None, :],
        wqn=uq[:, :, :NOPE].reshape(Q_RANK, MLA_HEADS * NOPE).astype(BF),
        wqp=jnp.concatenate([uq_pe.reshape(Q_RANK, -1), rot(uq_pe).reshape(Q_RANK, -1)], axis=1).astype(BF),
        wuk=jnp.transpose(w_uk[i], (1, 2, 0)).astype(BF),
        wuv=jnp.transpose(w_uv[i], (1, 0, 2)).astype(BF),
        wkn=w_uk[i].reshape(KV_RANK, MLA_HEADS * NOPE).astype(BF),
        wv=w_uv[i].reshape(KV_RANK, MLA_HEADS * MLA_V).astype(BF),
        wo=w_o[i].astype(BF),
    )


def kernel(x_prompt, x_sample, state_ssd, state_conv, cache_mla_ckv, cache_mla_kpe, cache_mem_k, cache_mem_v, page_table, mem_prompt, mix_norm, w_in, conv_w, conv_b, dt_bias, a_log, d_skip, ssd_gain, gmlp_ln_g, gmlp_ln_b, gmlp_ws, gmlp_bs, w_out_even, w_mla_down, mla_q_gain, mla_kv_gain, w_mla_uq, w_mla_uk, w_mla_uv, w_mla_o, xattn_norm, mem_norm, w_mem_q, w_mem_k, w_mem_v, w_mem_o, ffn_norm, w_ffn_gu, w_ffn_down, w_router, w_exp_gu, w_exp_down, final_norm):
    nbp, seq, _ = x_prompt.shape
    nbs, dseq, _ = x_sample.shape
    depth = mix_norm.shape[0]
    past = page_table.shape[1] * PAGE
    mt = mem_prompt.shape[1]
    hp = x_prompt.reshape(nbp * seq, D)
    hs = x_sample.reshape(nbs * dseq, D)
    cos_p, sin_p = _rope_tables(jnp.arange(seq, dtype=jnp.int32))
    cos_s, sin_s = _rope_tables(past + jnp.arange(dseq, dtype=jnp.int32))
    cos_s = jnp.tile(cos_s, (nbs, 1))
    sin_s = jnp.tile(sin_s, (nbs, 1))
    cache_k4 = cache_mem_k.reshape(depth, nbs, mt * MEM_HEADS, MEM_HD)
    cache_v4 = cache_mem_v.reshape(depth, nbs, mt * MEM_HEADS, MEM_HD)
    cache_kpe_t = jnp.swapaxes(cache_mla_kpe, 2, 3)
    final_g = final_norm[None, :]

    p_ssd, p_conv, p_ckv, p_kpe, p_mk, p_mv = [], [], [], [], [], []
    s_ssd, s_conv, s_v, s_ckv, s_kpe = [], [], [], [], []
    for l in range(depth):
        i = l // 2
        g_mix = mix_norm[l][None, :]
        if l % 2 == 0:
            prm = _prep_even(i, w_in, conv_w, conv_b, dt_bias, a_log, d_skip, ssd_gain, gmlp_ln_g,
                             gmlp_ln_b, gmlp_ws, gmlp_bs, w_out_even)
            buf0 = jnp.zeros((nbp, SSD_CONV - 1, CONV_DIM), F32)
            h00 = jnp.zeros((1, nbp, SSD_HEADS, SSD_HEAD_DIM, SSD_STATE), F32)
            hp, buf_p, ssd_p, _ = _even_layer(hp, g_mix, prm, buf0, h00, 0, nb=nbp, sample=False)
            hs, buf_s, ssd_s, v_s = _even_layer(hs, g_mix, prm, state_conv[i], state_ssd, i, nb=nbs,
                                                sample=True)
            p_ssd.append(ssd_p)
            p_conv.append(buf_p)
            s_ssd.append(ssd_s)
            s_conv.append(buf_s)
            s_v.append(v_s)
        else:
            prm = _prep_mla(i, w_mla_down, mla_q_gain, mla_kv_gain, w_mla_uq, w_mla_uk, w_mla_uv, w_mla_o)
            ckv, kpe, qh, kh, vh = _mla_proj_prompt(hp, g_mix, prm, cos_p, sin_p, nb=nbp)
            o_p = _flash(qh, kh, vh)
            hp = _matmul_res(o_p.reshape(nbp * seq, MLA_HEADS * MLA_V), prm["wo"], hp)
            p_ckv.append(ckv.reshape(nbp, seq, KV_RANK))
            p_kpe.append(kpe.reshape(nbp, seq, ROPE))

            ckv_s, kpe_s, kcat_s, q_s = _mla_proj(hs, g_mix, prm, cos_s, sin_s, nb=1)
            q_s = q_s[0].reshape(MLA_HEADS, nbs, dseq, QK).transpose(1, 0, 2, 3).reshape(nbs, MLA_HEADS * dseq, QK)
            newk = jnp.pad(kcat_s.reshape(nbs, dseq, QK), ((0, 0), (0, NEW_KEY_ROWS - dseq), (0, 0)))
            o_s = _decode(page_table, q_s, newk, cache_mla_ckv, cache_kpe_t, i, dseq)
            o_s = o_s.reshape(nbs, MLA_HEADS, dseq, KV_RANK).transpose(1, 0, 2, 3)
            o_s = o_s.reshape(1, MLA_HEADS, nbs * dseq, KV_RANK).astype(BF)
            hs = _mla_out(o_s, prm["wuv"], prm["wo"], hs)
            s_ckv.append(ckv_s.reshape(nbs, dseq, KV_RANK))
            s_kpe.append(kpe_s.reshape(nbs, dseq, ROPE))

        wkv = jnp.concatenate([w_mem_k[l], w_mem_v[l]], axis=1).astype(BF)
        kv = _rms_matmul(mem_prompt.reshape(nbp * mt, D), mem_norm[l][None, :], wkv, tn=MEM_INNER)
        mk_p = kv[:, :MEM_INNER].reshape(nbp, mt, MEM_INNER)
        mv_p = kv[:, MEM_INNER:].reshape(nbp, mt, MEM_INNER)
        g_x = xattn_norm[l][None, :]
        wq = w_mem_q[l].astype(BF)
        wo = w_mem_o[l].astype(BF)
        hp = _mem_prompt(hp, g_x, wq, mk_p, mv_p, wo)
        x8 = jnp.pad(hs.reshape(nbs, dseq, D), ((0, 0), (0, SUBLANES - dseq), (0, 0)))
        hs = _mem_sample(x8, g_x, wq, cache_k4, cache_v4, wo, l)[:, :dseq].reshape(nbs * dseq, D)
        p_mk.append(mk_p.reshape(nbp, mt, MEM_HEADS, MEM_HD))
        p_mv.append(mv_p.reshape(nbp, mt, MEM_HEADS, MEM_HD))

        g_f = ffn_norm[l][None, :]
        if l % 2 == 0:
            wgu = w_ffn_gu[i].astype(BF)
            wd = w_ffn_down[i].astype(BF)
            hp = _ffn(hp, g_f, wgu, wd)
            hs = _ffn(hs, g_f, wgu, wd)
        else:
            wr = jnp.pad(w_router[i], ((0, 0), (0, LANES - N_EXP)))
            wgu = w_exp_gu[i].astype(BF)
            wd = w_exp_down[i].astype(BF)
            final = l == depth - 1
            hp = _moe(hp, g_f, wr, wgu, wd, final_g, final=final)
            hs = _moe(hs, g_f, wr, wgu, wd, final_g, final=final)
    if depth % 2 == 1:
        raise NotImplementedError("the final norm is fused into the last routed-expert layer")
    y_prompt = hp.reshape(nbp, seq, D)
    y_sample = hs.reshape(nbs, dseq, D)
    return (y_prompt, y_sample,
            jnp.stack(p_ssd), jnp.stack(p_conv), jnp.stack(p_ckv), jnp.stack(p_kpe),
            jnp.stack(p_mk), jnp.stack(p_mv),
            jnp.stack(s_ssd), jnp.stack(s_conv), jnp.stack(s_v), jnp.stack(s_ckv), jnp.stack(s_kpe))
```

```python
import functools

import jax
import jax.numpy as jnp
from jax import lax
from jax.experimental import pallas as pl
from jax.experimental.pallas import tpu as pltpu
from jax.experimental.pallas import tpu_sc as plsc

F32 = jnp.float32
BF = jnp.bfloat16
EPS = 1e-6
NEG = -1e30

D = 1024
SSD_HEADS = 16
SSD_HEAD_DIM = 64
SSD_INNER = SSD_HEADS * SSD_HEAD_DIM
SSD_GROUPS = 2
SSD_STATE = 128
SSD_CONV = 4
CONV_DIM = SSD_INNER + 2 * SSD_GROUPS * SSD_STATE
GMLP_GROUPS = 8
GMLP_WIDTH = 1024
CHUNK = 128
SAMPLE_Q = 16
ZXU = SSD_INNER + CONV_DIM + 2 * GMLP_WIDTH
MLA_HEADS = 8
NOPE = 128
ROPE = 64
MLA_V = 128
Q_RANK = 256
KV_RANK = 256
MLA_SCALE = (NOPE + ROPE) ** -0.5
QK = KV_RANK + ROPE
ROPE_BASE = 10000.0
MEM_HEADS = 4
MEM_HD = 128
MEM_INNER = MEM_HEADS * MEM_HD
D_FF = 2816
N_EXP = 8
D_FFE = 3584
PAGE = 128
LANES = 128
SUBLANES = 8
ROW_TILE = 512
NEW_KEY_ROWS = 16
SAMPLES_PER_STEP = 8


def _cp(sem, vmem_mb=None):
    kw = dict(dimension_semantics=sem)
    if vmem_mb is not None:
        kw["vmem_limit_bytes"] = vmem_mb * 1024 * 1024
    return pltpu.CompilerParams(**kw)


def _rms(x, g):
    return x * lax.rsqrt(jnp.mean(x * x, axis=-1, keepdims=True) + EPS) * g


def _dot(a, b):
    return jnp.dot(a, b, preferred_element_type=F32)


def _dot_nt(a, b):
    return lax.dot_general(a, b, (((1,), (1,)), ((), ())), preferred_element_type=F32)


def _dot_f32(a, b):
    return jnp.dot(a, b, preferred_element_type=F32, precision=lax.Precision.HIGHEST)


def _silu(x):
    return x * jax.nn.sigmoid(x)


def _full(shape):
    n = len(shape)
    return pl.BlockSpec(shape, lambda *_: (0,) * n)


def _largest_tile(n, cap):
    best = LANES
    for t in range(LANES, cap + 1, LANES):
        if n % t == 0:
            best = t
    return best


def _rms_matmul_kernel(x_ref, g_ref, w_ref, o_ref, xn_ref, *, precise):
    @pl.when(pl.program_id(1) == 0)
    def _():
        xn_ref[...] = _rms(x_ref[...], g_ref[...]).astype(xn_ref.dtype)

    if precise:
        o_ref[...] = _dot_f32(xn_ref[...], w_ref[...]).astype(o_ref.dtype)
    else:
        o_ref[...] = _dot(xn_ref[...], w_ref[...]).astype(o_ref.dtype)


def _rms_matmul(x, g, w, *, tn, out_dtype=F32, precise=False):
    t, k = x.shape
    n = w.shape[1]
    tm = min(ROW_TILE, t)
    return pl.pallas_call(
        functools.partial(_rms_matmul_kernel, precise=precise),
        grid=(t // tm, n // tn),
        in_specs=[
            pl.BlockSpec((tm, k), lambda i, j: (i, 0)),
            pl.BlockSpec((1, k), lambda i, j: (0, 0)),
            pl.BlockSpec((k, tn), lambda i, j: (0, j)),
        ],
        out_specs=pl.BlockSpec((tm, tn), lambda i, j: (i, j)),
        out_shape=jax.ShapeDtypeStruct((t, n), out_dtype),
        scratch_shapes=[pltpu.VMEM((tm, k), F32 if precise else BF)],
        compiler_params=_cp(("parallel", "arbitrary")),
        name="rms_matmul",
    )(x, g, w)


def _matmul_res_kernel(a_ref, w_ref, r_ref, o_ref):
    o_ref[...] = r_ref[...] + _dot(a_ref[...].astype(BF), w_ref[...])


def _matmul_res(a, w, res, *, tn=512):
    t, k = a.shape
    n = w.shape[1]
    tm = min(ROW_TILE, t)
    return pl.pallas_call(
        _matmul_res_kernel,
        grid=(t // tm, n // tn),
        in_specs=[
            pl.BlockSpec((tm, k), lambda i, j: (i, 0)),
            pl.BlockSpec((k, tn), lambda i, j: (0, j)),
            pl.BlockSpec((tm, tn), lambda i, j: (i, j)),
        ],
        out_specs=pl.BlockSpec((tm, tn), lambda i, j: (i, j)),
        out_shape=jax.ShapeDtypeStruct((t, n), F32),
        compiler_params=_cp(("parallel", "arbitrary")),
        name="matmul_res",
    )(a, w, res)


FF_CHUNK = 256


def _ffn_kernel(x_ref, g_ref, wgu_ref, wd_ref, o_ref, hid_ref, *, ff):
    x = x_ref[...]
    xn = _rms(x, g_ref[...]).astype(BF)
    for c in range(ff // FF_CHUNK):
        lo = c * FF_CHUNK
        gate = _dot(xn, wgu_ref[:, lo:lo + FF_CHUNK])
        up = _dot(xn, wgu_ref[:, ff + lo:ff + lo + FF_CHUNK])
        hid_ref[:, lo:lo + FF_CHUNK] = (_silu(gate) * up).astype(BF)
    o_ref[...] = x + _dot(hid_ref[...], wd_ref[...])


def _ffn(x, g, wgu, wd):
    t = x.shape[0]
    ff = wd.shape[0]
    tm = min(ROW_TILE, t)
    return pl.pallas_call(
        functools.partial(_ffn_kernel, ff=ff),
        grid=(t // tm,),
        in_specs=[
            pl.BlockSpec((tm, D), lambda i: (i, 0)),
            _full((1, D)),
            pl.BlockSpec((D, 2 * ff), lambda i: (0, 0), pipeline_mode=pl.Buffered(1)),
            pl.BlockSpec((ff, D), lambda i: (0, 0), pipeline_mode=pl.Buffered(1)),
        ],
        out_specs=pl.BlockSpec((tm, D), lambda i: (i, 0)),
        out_shape=jax.ShapeDtypeStruct((t, D), F32),
        scratch_shapes=[pltpu.VMEM((tm, ff), BF)],
        compiler_params=_cp(("parallel",), vmem_mb=48),
        name="ffn",
    )(x, g, wgu, wd)


def _softplus(x):
    return jnp.maximum(x, 0.0) + jnp.log1p(jnp.exp(-jnp.abs(x)))


def _gelu_tanh(x):
    return 0.5 * x * (1.0 + jnp.tanh(0.7978845608028654 * (x + 0.044715 * (x * x * x))))


def _even_kernel(proj_ref, dt_ref, cinit_ref, sinit_ref, cw_ref, cb_ref, dtb_ref, alog_ref,
                 dsk_ref, sg_ref, lng_ref, lnb_ref, ws_ref, bst_ref,
                 ymix_ref, cout_ref, sout_ref, v_ref, ext_ref, ht_ref, *, q, lb, lc):
    c = pl.program_id(1)
    last = pl.num_programs(1) - 1

    @pl.when(c == 0)
    def _():
        ext_ref[0:SUBLANES, :] = cinit_ref[0]
        ht_ref[...] = sinit_ref[0, 0].T

    if lb == q:
        p = proj_ref[0]
        dtr = dt_ref[0]
    else:
        p = jnp.concatenate([proj_ref[0], jnp.zeros((q - lb, ZXU), F32)], axis=0)
        dtr = jnp.concatenate([dt_ref[0], jnp.zeros((q - lb, LANES), F32)], axis=0)
    z = p[:, :SSD_INNER]
    xbc_raw = p[:, SSD_INNER:SSD_INNER + CONV_DIM]
    uv = p[:, SSD_INNER + CONV_DIM:]

    ext_ref[SUBLANES:SUBLANES + q, :] = xbc_raw
    conv = (cb_ref[...] + cw_ref[0:1, :] * ext_ref[5:5 + q, :] + cw_ref[1:2, :] * ext_ref[6:6 + q, :]
            + cw_ref[2:3, :] * ext_ref[7:7 + q, :] + cw_ref[3:4, :] * xbc_raw)

    @pl.when(c == last)
    def _():
        cout_ref[0] = ext_ref[SUBLANES + lc - 3:SUBLANES + lc, :]

    ext_ref[0:SUBLANES, :] = ext_ref[q:q + SUBLANES, :]

    xbc = _silu(conv)
    xs = xbc[:, :SSD_INNER]
    gw = SSD_STATE
    bm = [xbc[:, SSD_INNER + g * gw:SSD_INNER + (g + 1) * gw] for g in range(SSD_GROUPS)]
    cm = [xbc[:, SSD_INNER + (SSD_GROUPS + g) * gw:SSD_INNER + (SSD_GROUPS + g + 1) * gw]
          for g in range(SSD_GROUPS)]

    row = lax.broadcasted_iota(jnp.int32, (q, q), 0)
    col = lax.broadcasted_iota(jnp.int32, (q, q), 1)
    causal = row >= col

    dt = _softplus(dtr + dtb_ref[...])
    if lc < q:
        dt = jnp.where(lax.broadcasted_iota(jnp.int32, (q, LANES), 0) < lc, dt, 0.0)
    a = dt * (-jnp.exp(alog_ref[...]))
    a_cum = _dot_f32(causal.astype(F32), a)
    a_cum_t = a_cum.T
    a_last = a_cum[q - 1:q, :]
    decay_end = jnp.exp(a_last - a_cum)
    ea = jnp.exp(a_cum)
    chunk_decay = jnp.exp(a_last)

    cmb = [m.astype(BF) for m in cm]
    cb = [_dot_nt(cmb[g], bm[g].astype(BF)) for g in range(SSD_GROUPS)]
    bt = [bm[g].T.astype(BF) for g in range(SSD_GROUPS)]
    heads_per_group = SSD_HEADS // SSD_GROUPS
    ys = []
    for r in range(SSD_HEADS):
        g = r // heads_per_group
        lo = r * SSD_HEAD_DIM
        hi = lo + SSD_HEAD_DIM
        seg = a_cum[:, r:r + 1] - a_cum_t[r:r + 1, :]
        lmat = jnp.where(causal, jnp.exp(jnp.minimum(seg, 0.0)), 0.0)
        m = (cb[g] * lmat).astype(BF)
        xh = xs[:, lo:hi]
        xdt = xh * dt[:, r:r + 1]
        y_diag = _dot(m, xdt.astype(BF))
        h_old = ht_ref[:, lo:hi]
        y_off = _dot(cmb[g], h_old.astype(BF)) * ea[:, r:r + 1]
        ys.append(y_diag + y_off + dsk_ref[:, lo:hi] * xh)
        xd = (xdt * decay_end[:, r:r + 1]).astype(BF)
        ht_ref[:, lo:hi] = h_old * chunk_decay[:, r:r + 1] + _dot(bt[g], xd)
    y = jnp.concatenate(ys, axis=1) * _silu(z)
    gi = SSD_INNER // SSD_GROUPS
    yn = [_rms(y[:, g * gi:(g + 1) * gi], sg_ref[:, g * gi:(g + 1) * gi]) for g in range(SSD_GROUPS)]
    ymix_ref[0, :, 0:SSD_INNER] = jnp.concatenate(yn, axis=1)[:lb].astype(BF)

    @pl.when(c == last)
    def _():
        sout_ref[0] = ht_ref[...].T

    uvg = _gelu_tanh(uv)
    u = uvg[:, :GMLP_WIDTH]
    v = uvg[:, GMLP_WIDTH:]
    mu = jnp.mean(v, axis=-1, keepdims=True)
    vc = v - mu
    vn = vc * lax.rsqrt(jnp.mean(vc * vc, axis=-1, keepdims=True) + EPS) * lng_ref[...] + lnb_ref[...]
    if v_ref is not None:
        v_ref[0] = vn[:lb]
    gd = GMLP_WIDTH // GMLP_GROUPS
    yb = []
    for g in range(GMLP_GROUPS):
        wt = jnp.where(causal, ws_ref[g, :q, :q], 0.0).astype(BF)
        sp = _dot(wt, vn[:, g * gd:(g + 1) * gd].astype(BF)) + bst_ref[:q, g:g + 1]
        yb.append(u[:, g * gd:(g + 1) * gd] * sp)
    ymix_ref[0, :, SSD_INNER:SSD_INNER + GMLP_WIDTH] = jnp.concatenate(yb, axis=1)[:lb].astype(BF)


def _even_kernel_no_v(*refs, q, lb, lc):
    n_in = 14
    ins, outs, scr = refs[:n_in], refs[n_in:n_in + 3], refs[n_in + 3:]
    _even_kernel(*ins, *outs, None, *scr, q=q, lb=lb, lc=lc)


def _even_mixer(proj, dtp, cinit8, sinit, prm, *, q, lb, lc, want_v, layer):
    b, lp, _ = proj.shape
    nchunks = lp // lb
    par = [prm["conv_w"], prm["conv_b"], prm["dt_bias"], prm["a_log"], prm["d_skip"], prm["ssd_gain"],
           prm["ln_g"], prm["ln_b"], prm["ws"], prm["bst"]]
    in_specs = [
        pl.BlockSpec((1, lb, ZXU), lambda i, c: (i, c, 0)),
        pl.BlockSpec((1, lb, LANES), lambda i, c: (i, c, 0)),
        pl.BlockSpec((1, SUBLANES, CONV_DIM), lambda i, c: (i, 0, 0)),
        pl.BlockSpec((1, 1, SSD_INNER, SSD_STATE), lambda i, c: (layer, i, 0, 0)),
    ] + [_full(w.shape) for w in par]
    out_specs = [
        pl.BlockSpec((1, lb, SSD_INNER + GMLP_WIDTH), lambda i, c: (i, c, 0)),
        pl.BlockSpec((1, SSD_CONV - 1, CONV_DIM), lambda i, c: (i, 0, 0)),
        pl.BlockSpec((1, SSD_INNER, SSD_STATE), lambda i, c: (i, 0, 0)),
    ]
    out_shape = [
        jax.ShapeDtypeStruct((b, lp, SSD_INNER + GMLP_WIDTH), BF),
        jax.ShapeDtypeStruct((b, SSD_CONV - 1, CONV_DIM), F32),
        jax.ShapeDtypeStruct((b, SSD_INNER, SSD_STATE), F32),
    ]
    if want_v:
        out_specs.append(pl.BlockSpec((1, lb, GMLP_WIDTH), lambda i, c: (i, c, 0)))
        out_shape.append(jax.ShapeDtypeStruct((b, lp, GMLP_WIDTH), F32))
        body = functools.partial(_even_kernel, q=q, lb=lb, lc=lc)
    else:
        body = functools.partial(_even_kernel_no_v, q=q, lb=lb, lc=lc)
    return pl.pallas_call(
        body,
        grid=(b, nchunks),
        in_specs=in_specs,
        out_specs=out_specs,
        out_shape=out_shape,
        scratch_shapes=[pltpu.VMEM((q + 2 * SUBLANES, CONV_DIM), F32),
                        pltpu.VMEM((SSD_STATE, SSD_INNER), F32)],
        compiler_params=_cp(("parallel", "arbitrary"), vmem_mb=48),
        name="even_mixer",
    )(proj, dtp, cinit8, sinit, *par)


def _mla_proj_kernel(x_ref, g_ref, wdq_ref, wdkv_ref, wdk2_ref, qg_ref, kvg_ref, wqn_ref, wqp_ref,
                     wuk_ref, cos_ref, sin_ref, ckv_ref, kpe_ref, kcat_ref, q_ref):
    xn = _rms(x_ref[...], g_ref[...]).astype(BF)
    cqn = _rms(_dot(xn, wdq_ref[...]), qg_ref[...]).astype(BF)
    ckv = _rms(_dot(xn, wdkv_ref[...]), kvg_ref[...])
    kk = _dot(xn, wdk2_ref[...])
    cos = cos_ref[...]
    sin = sin_ref[...]
    kpe = kk[:, :ROPE] * cos[:, :ROPE] + kk[:, ROPE:] * sin[:, :ROPE]
    ckv_ref[...] = ckv
    kpe_ref[...] = kpe
    kcat_ref[:, :KV_RANK] = ckv.astype(BF)
    kcat_ref[:, KV_RANK:] = kpe.astype(BF)
    qn = _dot(cqn, wqn_ref[...])
    qp = _dot(cqn, wqp_ref[...])
    hr = MLA_HEADS * ROPE
    qpe = qp[:, :hr] * cos + qp[:, hr:] * sin
    for h in range(MLA_HEADS):
        ql = _dot(qn[:, h * NOPE:(h + 1) * NOPE].astype(BF), wuk_ref[h])
        q_ref[0, h, :, :KV_RANK] = (ql * MLA_SCALE).astype(BF)
        q_ref[0, h, :, KV_RANK:] = (qpe[:, h * ROPE:(h + 1) * ROPE] * MLA_SCALE).astype(BF)


def _mla_proj(x, g, prm, cos8, sin8, *, nb):
    t = x.shape[0]
    seq = t // nb
    tm = min(ROW_TILE, seq)
    tpb = seq // tm
    w = [prm["wdq"], prm["wdkv"], prm["wdk2"], prm["q_gain"], prm["kv_gain"], prm["wqn"], prm["wqp"],
         prm["wuk"]]
    return pl.pallas_call(
        _mla_proj_kernel,
        grid=(t // tm,),
        in_specs=[pl.BlockSpec((tm, D), lambda i: (i, 0)), _full((1, D))] + [_full(a.shape) for a in w] + [
            pl.BlockSpec((tm, MLA_HEADS * ROPE), lambda i: (i % tpb, 0)),
            pl.BlockSpec((tm, MLA_HEADS * ROPE), lambda i: (i % tpb, 0)),
        ],
        out_specs=[
            pl.BlockSpec((tm, KV_RANK), lambda i: (i, 0)),
            pl.BlockSpec((tm, ROPE), lambda i: (i, 0)),
            pl.BlockSpec((tm, QK), lambda i: (i, 0)),
            pl.BlockSpec((1, MLA_HEADS, tm, QK), lambda i: (i // tpb, 0, i % tpb, 0)),
        ],
        out_shape=[
            jax.ShapeDtypeStruct((t, KV_RANK), F32),
            jax.ShapeDtypeStruct((t, ROPE), F32),
            jax.ShapeDtypeStruct((t, QK), BF),
            jax.ShapeDtypeStruct((nb, MLA_HEADS, seq, QK), BF),
        ],
        compiler_params=_cp(("parallel",)),
        name="mla_proj",
    )(x, g, *w, cos8, sin8)


HEAD_QK = NOPE + ROPE
LOG2E = 1.4426950408889634


def _mla_proj_prompt_kernel(x_ref, g_ref, wdq_ref, wdkv_ref, wdk2_ref, qg_ref, kvg_ref, wqn_ref, wqp_ref,
                            wkn_ref, wv_ref, cos_ref, sin_ref, ckv_ref, kpe_ref, q_ref, k_ref, v_ref):
    xn = _rms(x_ref[...], g_ref[...]).astype(BF)
    cqn = _rms(_dot(xn, wdq_ref[...]), qg_ref[...]).astype(BF)
    ckv = _rms(_dot(xn, wdkv_ref[...]), kvg_ref[...])
    kk = _dot(xn, wdk2_ref[...])
    cos = cos_ref[...]
    sin = sin_ref[...]
    kpe = kk[:, :ROPE] * cos[:, :ROPE] + kk[:, ROPE:] * sin[:, :ROPE]
    ckv_ref[...] = ckv
    kpe_ref[...] = kpe
    ckv_b = ckv.astype(BF)
    kpe_b = kpe.astype(BF)
    kn = _dot(ckv_b, wkn_ref[...])
    vt = _dot_nt(wv_ref[...], ckv_b)
    qn = _dot(cqn, wqn_ref[...])
    qp = _dot(cqn, wqp_ref[...])
    hr = MLA_HEADS * ROPE
    qpe = qp[:, :hr] * cos + qp[:, hr:] * sin
    qscale = MLA_SCALE * LOG2E
    for h in range(MLA_HEADS):
        q_ref[0, h, :, :NOPE] = (qn[:, h * NOPE:(h + 1) * NOPE] * qscale).astype(BF)
        q_ref[0, h, :, NOPE:] = (qpe[:, h * ROPE:(h + 1) * ROPE] * qscale).astype(BF)
        k_ref[0, h, :, :NOPE] = kn[:, h * NOPE:(h + 1) * NOPE].astype(BF)
        k_ref[0, h, :, NOPE:] = kpe_b
        v_ref[0, h, 0] = vt[h * MLA_V:(h + 1) * MLA_V].astype(BF)


def _mla_proj_prompt(x, g, prm, cos8, sin8, *, nb):
    t = x.shape[0]
    seq = t // nb
    tm = min(ROW_TILE, seq)
    tpb = seq // tm
    w = [prm["wdq"], prm["wdkv"], prm["wdk2"], prm["q_gain"], prm["kv_gain"], prm["wqn"], prm["wqp"],
         prm["wkn"], prm["wv"]]
    head_spec = lambda width: pl.BlockSpec((1, MLA_HEADS, tm, width), lambda i: (i // tpb, 0, i % tpb, 0))
    head_shape = lambda width: jax.ShapeDtypeStruct((nb, MLA_HEADS, seq, width), BF)
    return pl.pallas_call(
        _mla_proj_prompt_kernel,
        grid=(t // tm,),
        in_specs=[pl.BlockSpec((tm, D), lambda i: (i, 0)), _full((1, D))] + [_full(a.shape) for a in w] + [
            pl.BlockSpec((tm, MLA_HEADS * ROPE), lambda i: (i % tpb, 0)),
            pl.BlockSpec((tm, MLA_HEADS * ROPE), lambda i: (i % tpb, 0)),
        ],
        out_specs=[
            pl.BlockSpec((tm, KV_RANK), lambda i: (i, 0)),
            pl.BlockSpec((tm, ROPE), lambda i: (i, 0)),
            head_spec(HEAD_QK), head_spec(HEAD_QK),
            pl.BlockSpec((1, MLA_HEADS, 1, MLA_V, tm), lambda i: (i // tpb, 0, i % tpb, 0, 0)),
        ],
        out_shape=[
            jax.ShapeDtypeStruct((t, KV_RANK), F32),
            jax.ShapeDtypeStruct((t, ROPE), F32),
            head_shape(HEAD_QK), head_shape(HEAD_QK),
            jax.ShapeDtypeStruct((nb, MLA_HEADS, tpb, MLA_V, tm), BF),
        ],
        compiler_params=_cp(("parallel",)),
        name="mla_proj_prompt",
    )(x, g, *w, cos8, sin8)


FLASH_HEADS = 4
DENOM_ROWS = 16


def _flash_kernel(q_ref, k_ref, vt_ref, o_ref, m_ref, acc_ref, *, t):
    qi = pl.program_id(2)
    for hh in range(FLASH_HEADS):
        m_ref[hh] = jnp.full((1, t), NEG, F32)
        acc_ref[hh] = jnp.zeros((MLA_V + DENOM_ROWS, t), F32)
    ones = jnp.ones((DENOM_ROWS, t), BF)

    def block(ki, masked):
        start = pl.multiple_of(ki * t, t)
        for hh in range(FLASH_HEADS):
            kb = k_ref[0, hh, pl.ds(start, t), :]
            st = _dot_nt(kb, q_ref[0, hh])
            if masked:
                key = lax.broadcasted_iota(jnp.int32, (t, t), 0)
                qry = lax.broadcasted_iota(jnp.int32, (t, t), 1)
                st = jnp.where(key <= qry, st, NEG)
            m_old = m_ref[hh]
            m_new = jnp.maximum(m_old, jnp.max(st, axis=0, keepdims=True))
            alpha = jnp.exp2(m_old - m_new)
            pt = jnp.exp2(st - m_new).astype(BF)
            v1 = jnp.concatenate([vt_ref[0, hh, ki], ones], axis=0)
            acc_ref[hh] = alpha * acc_ref[hh] + _dot(v1, pt)
            m_ref[hh] = m_new

    def body(ki, carry):
        block(ki, False)
        return carry

    lax.fori_loop(0, qi, body, 0)
    block(qi, True)
    for hh in range(FLASH_HEADS):
        acc = acc_ref[hh]
        o_t = acc[:MLA_V] / acc[MLA_V:MLA_V + 1]
        o_ref[0, :, hh * MLA_V:(hh + 1) * MLA_V] = o_t.T.astype(BF)


def _flash(q, k, vt):
    nb, _, seq, _ = q.shape
    t = vt.shape[-1]
    nh = FLASH_HEADS
    return pl.pallas_call(
        functools.partial(_flash_kernel, t=t),
        grid=(nb, MLA_HEADS // nh, seq // t),
        in_specs=[
            pl.BlockSpec((1, nh, t, HEAD_QK), lambda b, h, i: (b, h, i, 0)),
            pl.BlockSpec((1, nh, seq, HEAD_QK), lambda b, h, i: (b, h, 0, 0), pipeline_mode=pl.Buffered(1)),
            pl.BlockSpec((1, nh, seq // t, MLA_V, t), lambda b, h, i: (b, h, 0, 0, 0),
                         pipeline_mode=pl.Buffered(1)),
        ],
        out_specs=pl.BlockSpec((1, t, nh * MLA_V), lambda b, h, i: (b, i, h)),
        out_shape=jax.ShapeDtypeStruct((nb, seq, MLA_HEADS * MLA_V), BF),
        scratch_shapes=[pltpu.VMEM((nh, 1, t), F32), pltpu.VMEM((nh, MLA_V + DENOM_ROWS, t), F32)],
        compiler_params=_cp(("parallel", "parallel", "arbitrary"), vmem_mb=48),
        name="mla_flash",
    )(q, k, vt)


def _page_copies(pt_ref, ckv_hbm, kpt_hbm, ckbuf, kpbuf, sem, sample, slot, *, layer, n_pages):
    copies = []
    for p in range(n_pages):
        pg = pt_ref[sample, p]
        copies.append(pltpu.make_async_copy(
            ckv_hbm.at[layer, pg], ckbuf.at[slot, pl.ds(p * PAGE, PAGE), :], sem.at[0, slot]))
        copies.append(pltpu.make_async_copy(
            kpt_hbm.at[layer, pg], kpbuf.at[slot, :, pl.ds(p * PAGE, PAGE)], sem.at[1, slot]))
    return copies


def _decode_kernel(pt_ref, q_ref, nk_ref, ckv_hbm, kpt_hbm, o_ref, ckbuf, kpbuf, sem, *,
                   dec_seq, layer, n_pages):
    b = pl.program_id(0)
    last = pl.num_programs(0) - 1
    slot = b % 2
    copies = functools.partial(_page_copies, pt_ref, ckv_hbm, kpt_hbm, ckbuf, kpbuf, sem,
                               layer=layer, n_pages=n_pages)

    @pl.when(b == 0)
    def _():
        for c in copies(0, 0):
            c.start()

    for c in copies(b, slot):
        c.wait()
    nxt = jnp.minimum(b + 1, last)
    for c in copies(nxt, 1 - slot):
        c.start()

    rows = MLA_HEADS * dec_seq
    qm = q_ref[0]
    ql = qm[:, :KV_RANK]
    qp = qm[:, KV_RANK:]
    ck = ckbuf[slot].astype(BF)
    kp = kpbuf[slot].astype(BF)
    s = _dot_nt(ql, ck) + _dot(qp, kp)
    nk = nk_ref[0]
    kt = lax.broadcasted_iota(jnp.int32, (rows, NEW_KEY_ROWS), 1)
    qt = lax.broadcasted_iota(jnp.int32, (rows, NEW_KEY_ROWS), 0) % dec_seq
    s_new = jnp.where(kt <= qt, _dot_nt(qm, nk), NEG)
    m = jnp.maximum(jnp.max(s, axis=-1, keepdims=True), jnp.max(s_new, axis=-1, keepdims=True))
    p = jnp.exp(s - m)
    p_new = jnp.exp(s_new - m)
    denom = jnp.sum(p, axis=-1, keepdims=True) + jnp.sum(p_new, axis=-1, keepdims=True)
    o_ref[0] = (_dot(p.astype(BF), ck) + _dot(p_new.astype(BF), nk[:, :KV_RANK])) / denom

    @pl.when(b == last)
    def _():
        for c in copies(nxt, 1 - slot):
            c.wait()


def _decode(page_table, q, newk, cache_ckv, cache_kpe_t, layer, dec_seq):
    nb, n_pages = page_table.shape
    rows = MLA_HEADS * dec_seq
    keys = n_pages * PAGE
    grid_spec = pltpu.PrefetchScalarGridSpec(
        num_scalar_prefetch=1,
        grid=(nb,),
        in_specs=[
            pl.BlockSpec((1, rows, QK), lambda b, pt: (b, 0, 0)),
            pl.BlockSpec((1, NEW_KEY_ROWS, QK), lambda b, pt: (b, 0, 0)),
            pl.BlockSpec(memory_space=pl.ANY),
            pl.BlockSpec(memory_space=pl.ANY),
        ],
        out_specs=pl.BlockSpec((1, rows, KV_RANK), lambda b, pt: (b, 0, 0)),
        scratch_shapes=[pltpu.VMEM((2, keys, KV_RANK), F32), pltpu.VMEM((2, ROPE, keys), F32),
                        pltpu.SemaphoreType.DMA((2, 2))],
    )
    return pl.pallas_call(
        functools.partial(_decode_kernel, dec_seq=dec_seq, layer=layer, n_pages=n_pages),
        grid_spec=grid_spec,
        out_shape=jax.ShapeDtypeStruct((nb, rows, KV_RANK), F32),
        compiler_params=_cp(("arbitrary",), vmem_mb=48),
        name="mla_decode",
    )(page_table, q, newk, cache_ckv, cache_kpe_t)


def _mla_out_kernel(o_ref, wuv_ref, wo_ref, r_ref, out_ref):
    parts = [_dot(o_ref[0, h], wuv_ref[h]).astype(BF) for h in range(MLA_HEADS)]
    out_ref[...] = r_ref[...] + _dot(jnp.concatenate(parts, axis=1), wo_ref[...])


def _mla_out(o_lat, wuv, wo, res):
    nb, _, seq, _ = o_lat.shape
    tm = min(ROW_TILE, seq)
    tpb = seq // tm
    t = nb * seq
    return pl.pallas_call(
        _mla_out_kernel,
        grid=(t // tm,),
        in_specs=[
            pl.BlockSpec((1, MLA_HEADS, tm, KV_RANK), lambda i: (i // tpb, 0, i % tpb, 0)),
            _full(wuv.shape), _full(wo.shape),
            pl.BlockSpec((tm, D), lambda i: (i, 0)),
        ],
        out_specs=pl.BlockSpec((tm, D), lambda i: (i, 0)),
        out_shape=jax.ShapeDtypeStruct((t, D), F32),
        compiler_params=_cp(("parallel",)),
        name="mla_out",
    )(o_lat, wuv, wo, res)


def _softmax_rows(s):
    m = jnp.max(s, axis=-1, keepdims=True)
    p = jnp.exp(s - m)
    return p / jnp.sum(p, axis=-1, keepdims=True)


def _mem_prompt_kernel(x_ref, g_ref, wq_ref, k_ref, v_ref, wo_ref, o_ref):
    x = x_ref[...]
    xn = _rms(x, g_ref[...]).astype(BF)
    qm = (_dot(xn, wq_ref[...]) * MEM_HD ** -0.5).astype(BF)
    km = k_ref[0].astype(BF)
    vm = v_ref[0].astype(BF)
    parts = []
    for h in range(MEM_HEADS):
        sl = slice(h * MEM_HD, (h + 1) * MEM_HD)
        p = _softmax_rows(_dot_nt(qm[:, sl], km[:, sl]))
        parts.append(_dot(p.astype(BF), vm[:, sl]).astype(BF))
    o_ref[...] = x + _dot(jnp.concatenate(parts, axis=1), wo_ref[...])


def _mem_prompt(x, g, wq, km, vm, wo):
    t = x.shape[0]
    nb, mt, _ = km.shape
    seq = t // nb
    tm = min(ROW_TILE, seq)
    tpb = seq // tm
    return pl.pallas_call(
        _mem_prompt_kernel,
        grid=(t // tm,),
        in_specs=[
            pl.BlockSpec((tm, D), lambda i: (i, 0)), _full((1, D)), _full(wq.shape),
            pl.BlockSpec((1, mt, MEM_INNER), lambda i: (i // tpb, 0, 0)),
            pl.BlockSpec((1, mt, MEM_INNER), lambda i: (i // tpb, 0, 0)),
            _full(wo.shape),
        ],
        out_specs=pl.BlockSpec((tm, D), lambda i: (i, 0)),
        out_shape=jax.ShapeDtypeStruct((t, D), F32),
        compiler_params=_cp(("parallel",)),
        name="mem_attn_prompt",
    )(x, g, wq, km, vm, wo)


MEM_ROWS = MEM_HEADS * SUBLANES


def _mem_sample_kernel(x_ref, g_ref, wq_ref, k_ref, v_ref, wo_ref, o_ref):
    ns = SAMPLES_PER_STEP
    x = x_ref[...].reshape(ns * SUBLANES, D)
    xn = _rms(x, g_ref[...]).astype(BF)
    qall = _dot(xn, wq_ref[...]) * MEM_HD ** -0.5
    cols = k_ref.shape[2]
    head_of_row = lax.broadcasted_iota(jnp.int32, (MEM_ROWS, cols), 0) // SUBLANES
    head_of_col = lax.broadcasted_iota(jnp.int32, (MEM_ROWS, cols), 1) % MEM_HEADS
    own = head_of_row == head_of_col
    outs = []
    for s in range(ns):
        qs = qall[s * SUBLANES:(s + 1) * SUBLANES]
        qst = jnp.concatenate([qs[:, h * MEM_HD:(h + 1) * MEM_HD] for h in range(MEM_HEADS)], axis=0)
        sc = jnp.where(own, _dot_nt(qst.astype(BF), k_ref[0, s].astype(BF)), NEG)
        o = _dot(_softmax_rows(sc).astype(BF), v_ref[0, s].astype(BF))
        outs.append(jnp.concatenate([o[h * SUBLANES:(h + 1) * SUBLANES] for h in range(MEM_HEADS)], axis=1))
    out = x + _dot(jnp.concatenate(outs, axis=0).astype(BF), wo_ref[...])
    o_ref[...] = out.reshape(ns, SUBLANES, D)


def _mem_sample(x8, g, wq, cache_k, cache_v, wo, layer):
    nb = x8.shape[0]
    rows = cache_k.shape[2]
    ns = SAMPLES_PER_STEP
    return pl.pallas_call(
        _mem_sample_kernel,
        grid=(nb // ns,),
        in_specs=[
            pl.BlockSpec((ns, SUBLANES, D), lambda i: (i, 0, 0)), _full((1, D)), _full(wq.shape),
            pl.BlockSpec((1, ns, rows, MEM_HD), lambda i: (layer, i, 0, 0)),
            pl.BlockSpec((1, ns, rows, MEM_HD), lambda i: (layer, i, 0, 0)),
            _full(wo.shape),
        ],
        out_specs=pl.BlockSpec((ns, SUBLANES, D), lambda i: (i, 0, 0)),
        out_shape=jax.ShapeDtypeStruct((nb, SUBLANES, D), F32),
        compiler_params=_cp(("parallel",), vmem_mb=48),
        name="mem_attn_sample",
    )(x8, g, wq, cache_k, cache_v, wo)


def _router_kernel(x_ref, g_ref, wr_ref, xn_ref, route_ref):
    xn = _rms(x_ref[...], g_ref[...])
    xn_ref[...] = xn
    lane = lax.broadcasted_iota(jnp.int32, (xn.shape[0], LANES), 1).astype(F32)
    lg = jnp.where(lane < N_EXP, _dot_f32(xn, wr_ref[...]), NEG)
    m1 = jnp.max(lg, axis=-1, keepdims=True)
    i1 = jnp.min(jnp.where(lg == m1, lane, float(LANES)), axis=-1, keepdims=True)
    lg2 = jnp.where(lane == i1, NEG, lg)
    m2 = jnp.max(lg2, axis=-1, keepdims=True)
    i2 = jnp.min(jnp.where(lg2 == m2, lane, float(LANES)), axis=-1, keepdims=True)
    e = jnp.exp(m2 - m1)
    g1 = 1.0 / (1.0 + e)
    g2 = e * g1
    route_ref[...] = jnp.where(lane == 0, i1, jnp.where(lane == 1, i2, jnp.where(lane == 2, g1,
                               jnp.where(lane == 3, g2, 0.0))))


def _router(x, g, wr_pad):
    t = x.shape[0]
    tm = min(ROW_TILE, t)
    return pl.pallas_call(
        _router_kernel,
        grid=(t // tm,),
        in_specs=[pl.BlockSpec((tm, D), lambda i: (i, 0)), _full((1, D)), _full(wr_pad.shape)],
        out_specs=[pl.BlockSpec((tm, D), lambda i: (i, 0)), pl.BlockSpec((tm, LANES), lambda i: (i, 0))],
        out_shape=[jax.ShapeDtypeStruct((t, D), F32), jax.ShapeDtypeStruct((t, LANES), F32)],
        compiler_params=_cp(("parallel",)),
        name="router",
    )(x, g, wr_pad)


SC_ROWS = 32


def _sc_mesh():
    return plsc.VectorSubcoreMesh(core_axis_name="c", subcore_axis_name="s")


def _sc_workers():
    info = plsc.get_sparse_core_info()
    return info.num_cores, info.num_cores * info.num_subcores


def _sc_gather_rows(table, idx):
    n = idx.shape[0]
    width = table.shape[1]
    ncores, nw = _sc_workers()
    per_w = n // nw
    n_chunks = per_w // SC_ROWS
    assert per_w * nw == n and n_chunks * SC_ROWS == per_w

    @functools.partial(
        pl.kernel, mesh=_sc_mesh(),
        out_type=jax.ShapeDtypeStruct((n, width), table.dtype),
        scratch_types=[pltpu.VMEM((per_w,), jnp.int32), pltpu.VMEM((SC_ROWS, width), table.dtype),
                       pltpu.SemaphoreType.DMA],
        name="sc_gather_rows",
    )
    def body(table_hbm, idx_hbm, out_hbm, idx_v, rows_v, sem):
        wid = lax.axis_index("s") * ncores + lax.axis_index("c")
        base = wid * per_w
        pltpu.sync_copy(idx_hbm.at[pl.ds(base, per_w)], idx_v)

        @pl.loop(0, n_chunks)
        def _(j):
            off = pl.multiple_of(j * SC_ROWS, SC_ROWS)
            pltpu.async_copy(table_hbm.at[idx_v.at[pl.ds(off, SC_ROWS)]], rows_v, sem).wait()
            pltpu.sync_copy(rows_v, out_hbm.at[pl.ds(base + off, SC_ROWS)])

    return body(table, idx)


def _sc_scatter_rows(src, idx, n_out):
    n = idx.shape[0]
    t, width = src.shape
    ncores, nw = _sc_workers()
    per_w = n // nw
    n_chunks = per_w // SC_ROWS
    assert per_w * nw == n and n_chunks * SC_ROWS == per_w and t % per_w == 0
    idx3 = idx.reshape(nw, n_chunks, SC_ROWS)

    @functools.partial(
        pl.kernel, mesh=_sc_mesh(),
        out_type=jax.ShapeDtypeStruct((n_out, width), src.dtype),
        scratch_types=[pltpu.VMEM((n_chunks, SC_ROWS), jnp.int32), pltpu.VMEM((SC_ROWS, width), src.dtype),
                       pltpu.SemaphoreType.DMA],
        name="sc_scatter_rows",
    )
    def body(src_hbm, idx_hbm, out_hbm, idx_v, rows_v, sem):
        wid = lax.axis_index("s") * ncores + lax.axis_index("c")
        base = lax.rem(wid * per_w, t)
        pltpu.sync_copy(idx_hbm.at[wid], idx_v)

        @pl.loop(0, n_chunks)
        def _(j):
            off = pl.multiple_of(j * SC_ROWS, SC_ROWS)
            pltpu.sync_copy(src_hbm.at[pl.ds(base + off, SC_ROWS)], rows_v)
            pltpu.async_copy(rows_v, out_hbm.at[idx_v.at[j]], sem).wait()

    return body(src, idx3)


MOE_TILE = 1024
MOE_BLOCK = 512
MOE_CHUNK = 256


def _moe_ffn_kernel(te_ref, nused_ref, x_ref, wg_ref, wu_ref, wd_ref, o_ref, xb_ref):
    del te_ref
    i = pl.program_id(0)
    j = pl.program_id(1)

    @pl.when(i < nused_ref[0])
    def _():
        @pl.when(j == 0)
        def _():
            xb_ref[...] = x_ref[...].astype(BF)

        xb = xb_ref[...]
        part = None
        for c in range(MOE_BLOCK // MOE_CHUNK):
            sl = slice(c * MOE_CHUNK, (c + 1) * MOE_CHUNK)
            gate = _dot(xb, wg_ref[0, 0, :, sl].astype(BF))
            up = _dot(xb, wu_ref[0, 0, :, sl].astype(BF))
            contrib = _dot((_silu(gate) * up).astype(BF), wd_ref[0, 0, sl, :].astype(BF))
            part = contrib if part is None else part + contrib

        @pl.when(j == 0)
        def _():
            o_ref[...] = part

        @pl.when(j != 0)
        def _():
            o_ref[...] += part

    @pl.when(i >= nused_ref[0])
    def _():
        o_ref[...] = jnp.zeros_like(o_ref)


def _moe_ffn(tile_expert, nused, xs, wgu, wd, layer, *, tm):
    npad = xs.shape[0]
    nblk = D_FFE // MOE_BLOCK
    grid_spec = pltpu.PrefetchScalarGridSpec(
        num_scalar_prefetch=2,
        grid=(npad // tm, nblk),
        in_specs=[
            pl.BlockSpec((tm, D), lambda i, j, te, nu: (i, 0)),
            pl.BlockSpec((1, 1, D, MOE_BLOCK), lambda i, j, te, nu: (layer, te[i], 0, j)),
            pl.BlockSpec((1, 1, D, MOE_BLOCK), lambda i, j, te, nu: (layer, te[i], 0, nblk + j)),
            pl.BlockSpec((1, 1, MOE_BLOCK, D), lambda i, j, te, nu: (layer, te[i], j, 0)),
        ],
        out_specs=pl.BlockSpec((tm, D), lambda i, j, te, nu: (i, 0)),
        scratch_shapes=[pltpu.VMEM((tm, D), BF)],
    )
    return pl.pallas_call(
        _moe_ffn_kernel,
        grid_spec=grid_spec,
        out_shape=jax.ShapeDtypeStruct((npad, D), F32),
        compiler_params=_cp(("parallel", "arbitrary"), vmem_mb=56),
        name="moe_ffn",
    )(tile_expert, nused, xs, wgu, wgu, wd)


ROUTE_GATE_LANE = 2


def _combine_kernel(h_ref, y0_ref, y1_ref, route_ref, g_ref, o_ref, *, final):
    gl = ROUTE_GATE_LANE
    route = route_ref[...]
    out = h_ref[...] + route[:, gl:gl + 1] * y0_ref[...] + route[:, gl + 1:gl + 2] * y1_ref[...]
    if final:
        out = _rms(out, g_ref[...])
    o_ref[...] = out


def _combine(h, y2, route, g, *, final):
    t = h.shape[0]
    tm = min(ROW_TILE, t)
    nt = t // tm
    return pl.pallas_call(
        functools.partial(_combine_kernel, final=final),
        grid=(nt,),
        in_specs=[
            pl.BlockSpec((tm, D), lambda i: (i, 0)),
            pl.BlockSpec((tm, D), lambda i: (i, 0)),
            pl.BlockSpec((tm, D), lambda i: (nt + i, 0)),
            pl.BlockSpec((tm, LANES), lambda i: (i, 0)),
            _full((1, D)),
        ],
        out_specs=pl.BlockSpec((tm, D), lambda i: (i, 0)),
        out_shape=jax.ShapeDtypeStruct((t, D), F32),
        compiler_params=_cp(("parallel",)),
        name="moe_combine",
    )(h, y2, y2, route, g)


def _moe(h, g, wr_pad, wgu, wd, layer, final_g, *, final):
    t = h.shape[0]
    tm = min(MOE_TILE, 2 * t)
    xn, route = _router(h, g, wr_pad)
    eidx = route[:, :ROUTE_GATE_LANE].astype(jnp.int32)
    e_flat = eidx.T.reshape(-1)
    onehot = (e_flat[:, None] == jnp.arange(N_EXP, dtype=jnp.int32)[None, :]).astype(jnp.int32)
    csum = jnp.cumsum(onehot, axis=0)
    counts = csum[-1]
    rank = jnp.sum(onehot * csum, axis=1) - 1
    padded = ((counts + tm - 1) // tm) * tm
    ends = jnp.cumsum(padded)
    starts = ends - padded
    dest = (jnp.sum(onehot * starts[None, :], axis=1) + rank).astype(jnp.int32)
    n_tiles = (2 * t) // tm + N_EXP
    tile_start = jnp.arange(n_tiles, dtype=jnp.int32) * tm
    tile_expert = jnp.minimum(jnp.sum((tile_start[:, None] >= ends[None, :]).astype(jnp.int32), axis=1),
                              N_EXP - 1).astype(jnp.int32)
    nused = (ends[-1] // tm).astype(jnp.int32).reshape(1)
    xs = _sc_scatter_rows(xn, dest, n_tiles * tm)
    ys = _moe_ffn(tile_expert, nused, xs, wgu, wd, layer, tm=tm)
    y2 = _sc_gather_rows(ys, dest)
    return _combine(h, y2, route, final_g, final=final)


def _rope_tables(pos):
    half = ROPE // 2
    inv_freq = ROPE_BASE ** (-jnp.arange(half, dtype=F32) / half)
    ang = pos.astype(F32)[:, None] * inv_freq
    cos = jnp.cos(ang)
    sin = jnp.sin(ang)
    cc = jnp.concatenate([cos, cos], axis=-1)
    ss = jnp.concatenate([-sin, sin], axis=-1)
    return jnp.tile(cc, (1, MLA_HEADS)), jnp.tile(ss, (1, MLA_HEADS))


def _even_layer(h, g, prm, cinit, sinit, layer, *, nb, sample):
    t = h.shape[0]
    seq = t // nb
    proj = _rms_matmul(h, g, prm["w_zxu"], tn=_largest_tile(ZXU, 1536))
    dtp = _rms_matmul(h, g, prm["w_dt"], tn=LANES, precise=True)
    proj = proj.reshape(nb, seq, ZXU)
    dtp = dtp.reshape(nb, seq, LANES)
    if sample:
        pad = ((0, 0), (0, SUBLANES - seq), (0, 0))
        proj = jnp.pad(proj, pad)
        dtp = jnp.pad(dtp, pad)
        q, lb, lc = SAMPLE_Q, SUBLANES, seq
    else:
        q, lb, lc = CHUNK, CHUNK, CHUNK
    cinit8 = jnp.pad(cinit, ((0, 0), (SUBLANES - (SSD_CONV - 1), 0), (0, 0)))
    outs = _even_mixer(proj, dtp, cinit8, sinit.reshape(-1, nb, SSD_INNER, SSD_STATE), prm,
                       q=q, lb=lb, lc=lc, want_v=sample, layer=layer)
    ymix, cout, sout = outs[:3]
    v = None
    if sample:
        ymix = ymix[:, :seq]
        v = outs[3][:, :seq]
    h = _matmul_res(ymix.reshape(t, SSD_INNER + GMLP_WIDTH), prm["w_out"], h)
    return h, cout, sout.reshape(nb, SSD_HEADS, SSD_HEAD_DIM, SSD_STATE), v


def _prep_even(i, w_in, conv_w, conv_b, dt_bias, a_log, d_skip, ssd_gain, ln_g, ln_b, ws, bs, w_out):
    w = w_in[i]
    o1 = SSD_INNER + CONV_DIM
    w_zxu = jnp.concatenate([w[:, :o1], w[:, o1 + SSD_HEADS:]], axis=1).astype(BF)
    w_dt = jnp.pad(w[:, o1:o1 + SSD_HEADS], ((0, 0), (0, LANES - SSD_HEADS)))
    padl = (0, LANES - SSD_HEADS)
    return dict(
        w_zxu=w_zxu, w_dt=w_dt,
        conv_w=jnp.pad(conv_w[i], ((0, SUBLANES - SSD_CONV), (0, 0))),
        conv_b=conv_b[i][None, :],
        dt_bias=jnp.pad(dt_bias[i], padl)[None, :],
        a_log=jnp.pad(a_log[i], padl)[None, :],
        d_skip=jnp.repeat(d_skip[i], SSD_HEAD_DIM)[None, :],
        ssd_gain=ssd_gain[i][None, :],
        ln_g=ln_g[i][None, :], ln_b=ln_b[i][None, :],
        ws=ws[i], bst=bs[i].T,
        w_out=w_out[i].astype(BF),
    )


def _prep_mla(i, w_down, q_gain, kv_gain, w_uq, w_uk, w_uv, w_o):
    wd = w_down[i]
    wk = wd[:, Q_RANK + KV_RANK:]
    half = ROPE // 2
    rot = lambda a: jnp.concatenate([a[..., half:], a[..., :half]], axis=-1)
    uq = w_uq[i]
    uq_pe = uq[:, :, NOPE:]
    return dict(
        wdq=wd[:, :Q_RANK].astype(BF),
        wdkv=wd[:, Q_RANK:Q_RANK + KV_RANK].astype(BF),
        wdk2=jnp.concatenate([wk, rot(wk)], axis=1).astype(BF),
        q_gain=q_gain[i][None, :], kv_gain=kv_gain[i][None, :],
        wqn=uq[:, :, :NOPE].reshape(Q_RANK, MLA_HEADS * NOPE).astype(BF),
        wqp=jnp.concatenate([uq_pe.reshape(Q_RANK, -1), rot(uq_pe).reshape(Q_RANK, -1)], axis=1).astype(BF),
        wuk=jnp.transpose(w_uk[i], (1, 2, 0)).astype(BF),
        wuv=jnp.transpose(w_uv[i], (1, 0, 2)).astype(BF),
        wkn=w_uk[i].reshape(KV_RANK, MLA_HEADS * NOPE).astype(BF),
        wv=w_uv[i].reshape(KV_RANK, MLA_HEADS * MLA_V).T.astype(BF),
        wo=w_o[i].astype(BF),
    )


def kernel(x_prompt, x_sample, state_ssd, state_conv, cache_mla_ckv, cache_mla_kpe, cache_mem_k, cache_mem_v, page_table, mem_prompt, mix_norm, w_in, conv_w, conv_b, dt_bias, a_log, d_skip, ssd_gain, gmlp_ln_g, gmlp_ln_b, gmlp_ws, gmlp_bs, w_out_even, w_mla_down, mla_q_gain, mla_kv_gain, w_mla_uq, w_mla_uk, w_mla_uv, w_mla_o, xattn_norm, mem_norm, w_mem_q, w_mem_k, w_mem_v, w_mem_o, ffn_norm, w_ffn_gu, w_ffn_down, w_router, w_exp_gu, w_exp_down, final_norm):
    nbp, seq, _ = x_prompt.shape
    nbs, dseq, _ = x_sample.shape
    depth = mix_norm.shape[0]
    past = page_table.shape[1] * PAGE
    mt = mem_prompt.shape[1]
    hp = x_prompt.reshape(nbp * seq, D)
    hs = x_sample.reshape(nbs * dseq, D)
    cos_p, sin_p = _rope_tables(jnp.arange(seq, dtype=jnp.int32))
    cos_s, sin_s = _rope_tables(past + jnp.arange(dseq, dtype=jnp.int32))
    cos_s = jnp.tile(cos_s, (nbs, 1))
    sin_s = jnp.tile(sin_s, (nbs, 1))
    cache_k4 = cache_mem_k.reshape(depth, nbs, mt * MEM_HEADS, MEM_HD)
    cache_v4 = cache_mem_v.reshape(depth, nbs, mt * MEM_HEADS, MEM_HD)
    cache_kpe_t = jnp.swapaxes(cache_mla_kpe, 2, 3)
    final_g = final_norm[None, :]

    p_ssd, p_conv, p_ckv, p_kpe, p_mk, p_mv = [], [], [], [], [], []
    s_ssd, s_conv, s_v, s_ckv, s_kpe = [], [], [], [], []
    for l in range(depth):
        i = l // 2
        g_mix = mix_norm[l][None, :]
        if l % 2 == 0:
            prm = _prep_even(i, w_in, conv_w, conv_b, dt_bias, a_log, d_skip, ssd_gain, gmlp_ln_g,
                             gmlp_ln_b, gmlp_ws, gmlp_bs, w_out_even)
            buf0 = jnp.zeros((nbp, SSD_CONV - 1, CONV_DIM), F32)
            h00 = jnp.zeros((1, nbp, SSD_HEADS, SSD_HEAD_DIM, SSD_STATE), F32)
            hp, buf_p, ssd_p, _ = _even_layer(hp, g_mix, prm, buf0, h00, 0, nb=nbp, sample=False)
            hs, buf_s, ssd_s, v_s = _even_layer(hs, g_mix, prm, state_conv[i], state_ssd, i, nb=nbs,
                                                sample=True)
            p_ssd.append(ssd_p)
            p_conv.append(buf_p)
            s_ssd.append(ssd_s)
            s_conv.append(buf_s)
            s_v.append(v_s)
        else:
            prm = _prep_mla(i, w_mla_down, mla_q_gain, mla_kv_gain, w_mla_uq, w_mla_uk, w_mla_uv, w_mla_o)
            ckv, kpe, qh, kh, vh = _mla_proj_prompt(hp, g_mix, prm, cos_p, sin_p, nb=nbp)
            o_p = _flash(qh, kh, vh)
            hp = _matmul_res(o_p.reshape(nbp * seq, MLA_HEADS * MLA_V), prm["wo"], hp)
            p_ckv.append(ckv.reshape(nbp, seq, KV_RANK))
            p_kpe.append(kpe.reshape(nbp, seq, ROPE))

            ckv_s, kpe_s, kcat_s, q_s = _mla_proj(hs, g_mix, prm, cos_s, sin_s, nb=1)
            q_s = q_s[0].reshape(MLA_HEADS, nbs, dseq, QK).transpose(1, 0, 2, 3).reshape(nbs, MLA_HEADS * dseq, QK)
            newk = jnp.pad(kcat_s.reshape(nbs, dseq, QK), ((0, 0), (0, NEW_KEY_ROWS - dseq), (0, 0)))
            o_s = _decode(page_table, q_s, newk, cache_mla_ckv, cache_kpe_t, i, dseq)
            o_s = o_s.reshape(nbs, MLA_HEADS, dseq, KV_RANK).transpose(1, 0, 2, 3)
            o_s = o_s.reshape(1, MLA_HEADS, nbs * dseq, KV_RANK).astype(BF)
            hs = _mla_out(o_s, prm["wuv"], prm["wo"], hs)
            s_ckv.append(ckv_s.reshape(nbs, dseq, KV_RANK))
            s_kpe.append(kpe_s.reshape(nbs, dseq, ROPE))

        wkv = jnp.concatenate([w_mem_k[l], w_mem_v[l]], axis=1).astype(BF)
        kv = _rms_matmul(mem_prompt.reshape(nbp * mt, D), mem_norm[l][None, :], wkv, tn=MEM_INNER)
        mk_p = kv[:, :MEM_INNER].reshape(nbp, mt, MEM_INNER)
        mv_p = kv[:, MEM_INNER:].reshape(nbp, mt, MEM_INNER)
        g_x = xattn_norm[l][None, :]
        wq = w_mem_q[l].astype(BF)
        wo = w_mem_o[l].astype(BF)
        hp = _mem_prompt(hp, g_x, wq, mk_p, mv_p, wo)
        x8 = jnp.pad(hs.reshape(nbs, dseq, D), ((0, 0), (0, SUBLANES - dseq), (0, 0)))
        hs = _mem_sample(x8, g_x, wq, cache_k4, cache_v4, wo, l)[:, :dseq].reshape(nbs * dseq, D)
        p_mk.append(mk_p.reshape(nbp, mt, MEM_HEADS, MEM_HD))
        p_mv.append(mv_p.reshape(nbp, mt, MEM_HEADS, MEM_HD))

        g_f = ffn_norm[l][None, :]
        if l % 2 == 0:
            wgu = w_ffn_gu[i].astype(BF)
            wd = w_ffn_down[i].astype(BF)
            hp = _ffn(hp, g_f, wgu, wd)
            hs = _ffn(hs, g_f, wgu, wd)
        else:
            wr = jnp.pad(w_router[i], ((0, 0), (0, LANES - N_EXP)))
            final = l == depth - 1
            hp = _moe(hp, g_f, wr, w_exp_gu, w_exp_down, i, final_g, final=final)
            hs = _moe(hs, g_f, wr, w_exp_gu, w_exp_down, i, final_g, final=final)
    if depth % 2 == 1:
        raise NotImplementedError("the final norm is fused into the last routed-expert layer")
    y_prompt = hp.reshape(nbp, seq, D)
    y_sample = hs.reshape(nbs, dseq, D)
    return (y_prompt, y_sample,
            jnp.stack(p_ssd), jnp.stack(p_conv), jnp.stack(p_ckv), jnp.stack(p_kpe),
            jnp.stack(p_mk), jnp.stack(p_mv),
            jnp.stack(s_ssd), jnp.stack(s_conv), jnp.stack(s_v), jnp.stack(s_ckv), jnp.stack(s_kpe))
```

```python
import functools

import jax
import jax.numpy as jnp
from jax import lax
from jax.experimental import pallas as pl
from jax.experimental.pallas import tpu as pltpu
from jax.experimental.pallas import tpu_sc as plsc

F32 = jnp.float32
BF = jnp.bfloat16
EPS = 1e-6
NEG = -1e30

D = 1024
SSD_HEADS = 16
SSD_HEAD_DIM = 64
SSD_INNER = SSD_HEADS * SSD_HEAD_DIM
SSD_GROUPS = 2
SSD_STATE = 128
SSD_CONV = 4
CONV_DIM = SSD_INNER + 2 * SSD_GROUPS * SSD_STATE
GMLP_GROUPS = 8
GMLP_WIDTH = 1024
CHUNK = 128
EVEN_SEQS_PER_STEP = 2
SAMPLE_Q = 16
ZXU = SSD_INNER + CONV_DIM + 2 * GMLP_WIDTH
MLA_HEADS = 8
NOPE = 128
ROPE = 64
MLA_V = 128
Q_RANK = 256
KV_RANK = 256
MLA_SCALE = (NOPE + ROPE) ** -0.5
QK = KV_RANK + ROPE
ROPE_BASE = 10000.0
MEM_HEADS = 4
MEM_HD = 128
MEM_INNER = MEM_HEADS * MEM_HD
D_FF = 2816
N_EXP = 8
D_FFE = 3584
PAGE = 128
LANES = 128
SUBLANES = 8
ROW_TILE = 512
NEW_KEY_ROWS = 16
SAMPLES_PER_STEP = 8


def _cp(sem, vmem_mb=None):
    kw = dict(dimension_semantics=sem)
    if vmem_mb is not None:
        kw["vmem_limit_bytes"] = vmem_mb * 1024 * 1024
    return pltpu.CompilerParams(**kw)


def _rms(x, g):
    return x * lax.rsqrt(jnp.mean(x * x, axis=-1, keepdims=True) + EPS) * g


def _dot(a, b):
    return jnp.dot(a, b, preferred_element_type=F32)


def _dot_nt(a, b):
    return lax.dot_general(a, b, (((1,), (1,)), ((), ())), preferred_element_type=F32)


def _dot_f32(a, b):
    return jnp.dot(a, b, preferred_element_type=F32, precision=lax.Precision.HIGHEST)


def _silu(x):
    return x * jax.nn.sigmoid(x)


def _full(shape):
    n = len(shape)
    return pl.BlockSpec(shape, lambda *_: (0,) * n)


def _largest_tile(n, cap):
    best = LANES
    for t in range(LANES, cap + 1, LANES):
        if n % t == 0:
            best = t
    return best


def _rms_matmul_kernel(x_ref, g_ref, w_ref, o_ref, xn_ref, *, precise):
    @pl.when(pl.program_id(1) == 0)
    def _():
        xn_ref[...] = _rms(x_ref[...], g_ref[...]).astype(xn_ref.dtype)

    if precise:
        o_ref[...] = _dot_f32(xn_ref[...], w_ref[...]).astype(o_ref.dtype)
    else:
        o_ref[...] = _dot(xn_ref[...], w_ref[...]).astype(o_ref.dtype)


def _rms_matmul(x, g, w, *, tn, out_dtype=F32, precise=False):
    t, k = x.shape
    n = w.shape[1]
    tm = min(ROW_TILE, t)
    return pl.pallas_call(
        functools.partial(_rms_matmul_kernel, precise=precise),
        grid=(t // tm, n // tn),
        in_specs=[
            pl.BlockSpec((tm, k), lambda i, j: (i, 0)),
            pl.BlockSpec((1, k), lambda i, j: (0, 0)),
            pl.BlockSpec((k, tn), lambda i, j: (0, j)),
        ],
        out_specs=pl.BlockSpec((tm, tn), lambda i, j: (i, j)),
        out_shape=jax.ShapeDtypeStruct((t, n), out_dtype),
        scratch_shapes=[pltpu.VMEM((tm, k), F32 if precise else BF)],
        compiler_params=_cp(("parallel", "arbitrary")),
        name="rms_matmul",
    )(x, g, w)


def _matmul_res_kernel(a_ref, w_ref, r_ref, o_ref):
    o_ref[...] = r_ref[...] + _dot(a_ref[...].astype(BF), w_ref[...])


def _matmul_res(a, w, res, *, tn=512):
    t, k = a.shape
    n = w.shape[1]
    tm = min(ROW_TILE, t)
    return pl.pallas_call(
        _matmul_res_kernel,
        grid=(t // tm, n // tn),
        in_specs=[
            pl.BlockSpec((tm, k), lambda i, j: (i, 0)),
            pl.BlockSpec((k, tn), lambda i, j: (0, j)),
            pl.BlockSpec((tm, tn), lambda i, j: (i, j)),
        ],
        out_specs=pl.BlockSpec((tm, tn), lambda i, j: (i, j)),
        out_shape=jax.ShapeDtypeStruct((t, n), F32),
        compiler_params=_cp(("parallel", "arbitrary")),
        name="matmul_res",
    )(a, w, res)


FF_CHUNK = 256


def _ffn_kernel(x_ref, g_ref, wgu_ref, wd_ref, o_ref, hid_ref, *, ff):
    x = x_ref[...]
    xn = _rms(x, g_ref[...]).astype(BF)
    for c in range(ff // FF_CHUNK):
        lo = c * FF_CHUNK
        gate = _dot(xn, wgu_ref[:, lo:lo + FF_CHUNK])
        up = _dot(xn, wgu_ref[:, ff + lo:ff + lo + FF_CHUNK])
        hid_ref[:, lo:lo + FF_CHUNK] = (_silu(gate) * up).astype(BF)
    o_ref[...] = x + _dot(hid_ref[...], wd_ref[...])


def _ffn(x, g, wgu, wd):
    t = x.shape[0]
    ff = wd.shape[0]
    tm = min(ROW_TILE, t)
    return pl.pallas_call(
        functools.partial(_ffn_kernel, ff=ff),
        grid=(t // tm,),
        in_specs=[
            pl.BlockSpec((tm, D), lambda i: (i, 0)),
            _full((1, D)),
            pl.BlockSpec((D, 2 * ff), lambda i: (0, 0), pipeline_mode=pl.Buffered(1)),
            pl.BlockSpec((ff, D), lambda i: (0, 0), pipeline_mode=pl.Buffered(1)),
        ],
        out_specs=pl.BlockSpec((tm, D), lambda i: (i, 0)),
        out_shape=jax.ShapeDtypeStruct((t, D), F32),
        scratch_shapes=[pltpu.VMEM((tm, ff), BF)],
        compiler_params=_cp(("parallel",), vmem_mb=48),
        name="ffn",
    )(x, g, wgu, wd)


def _softplus(x):
    return jnp.maximum(x, 0.0) + jnp.log1p(jnp.exp(-jnp.abs(x)))


def _gelu_tanh(x):
    return 0.5 * x * (1.0 + jnp.tanh(0.7978845608028654 * (x + 0.044715 * (x * x * x))))


def _even_kernel(proj_ref, dt_ref, cinit_ref, sinit_ref, cw_ref, cb_ref, dtb_ref, alog_ref,
                 dsk_ref, sg_ref, lng_ref, lnb_ref, ws_ref, bst_ref, e_ref,
                 ymix_ref, cout_ref, sout_ref, v_ref, ext_ref, ht_ref, *, q, lb, lc, nbb):
    c = pl.program_id(1)

    @pl.when(c == 0)
    def _():
        for bb in range(nbb):
            ext_ref[bb, 0:SUBLANES, :] = cinit_ref[bb]
            ht_ref[bb] = sinit_ref[0, bb].T

    for bb in range(nbb):
        _even_block(proj_ref, dt_ref, cw_ref, cb_ref, dtb_ref, alog_ref,
                    dsk_ref, sg_ref, lng_ref, lnb_ref, ws_ref, bst_ref, e_ref,
                    ymix_ref, v_ref, ext_ref, ht_ref, bb, q=q, lb=lb, lc=lc)

    @pl.when(c == pl.num_programs(1) - 1)
    def _():
        for bb in range(nbb):
            cout_ref[bb] = ext_ref[bb, SUBLANES + lc - 3:SUBLANES + lc, :]
            sout_ref[bb] = ht_ref[bb].T


def _even_block(proj_ref, dt_ref, cw_ref, cb_ref, dtb_ref, alog_ref,
                dsk_ref, sg_ref, lng_ref, lnb_ref, ws_ref, bst_ref, e_ref,
                ymix_ref, v_ref, ext_ref, ht_ref, bb, *, q, lb, lc):

    if lb == q:
        p = proj_ref[bb]
        dtr = dt_ref[bb]
    else:
        p = jnp.concatenate([proj_ref[bb], jnp.zeros((q - lb, ZXU), F32)], axis=0)
        dtr = jnp.concatenate([dt_ref[bb], jnp.zeros((q - lb, LANES), F32)], axis=0)
    z = p[:, :SSD_INNER]
    xbc_raw = p[:, SSD_INNER:SSD_INNER + CONV_DIM]
    uv = p[:, SSD_INNER + CONV_DIM:]

    ext_ref[bb, SUBLANES:SUBLANES + q, :] = xbc_raw
    conv = (cb_ref[...] + cw_ref[0:1, :] * ext_ref[bb, 5:5 + q, :] + cw_ref[1:2, :] * ext_ref[bb, 6:6 + q, :]
            + cw_ref[2:3, :] * ext_ref[bb, 7:7 + q, :] + cw_ref[3:4, :] * xbc_raw)
    ext_ref[bb, 0:SUBLANES, :] = ext_ref[bb, q:q + SUBLANES, :]

    xbc = _silu(conv)
    xs = xbc[:, :SSD_INNER]
    gw = SSD_STATE
    bm = [xbc[:, SSD_INNER + g * gw:SSD_INNER + (g + 1) * gw] for g in range(SSD_GROUPS)]
    cm = [xbc[:, SSD_INNER + (SSD_GROUPS + g) * gw:SSD_INNER + (SSD_GROUPS + g + 1) * gw]
          for g in range(SSD_GROUPS)]

    row = lax.broadcasted_iota(jnp.int32, (q, q), 0)
    col = lax.broadcasted_iota(jnp.int32, (q, q), 1)
    causal = row >= col

    dt = _softplus(dtr + dtb_ref[...])
    if lc < q:
        dt = jnp.where(lax.broadcasted_iota(jnp.int32, (q, LANES), 0) < lc, dt, 0.0)
    a = dt * (-jnp.exp(alog_ref[...]))
    a_cum = _dot_f32(causal.astype(F32), a)
    a_cum_t = a_cum.T
    a_last = a_cum[q - 1:q, :]
    decay_end = jnp.exp(a_last - a_cum)
    ea = jnp.exp(a_cum)
    chunk_decay = jnp.exp(a_last)

    cmb = [m.astype(BF) for m in cm]
    cb = [_dot_nt(cmb[g], bm[g].astype(BF)) for g in range(SSD_GROUPS)]
    bt = [bm[g].T.astype(BF) for g in range(SSD_GROUPS)]
    heads_per_group = SSD_HEADS // SSD_GROUPS
    gi = SSD_INNER // SSD_GROUPS

    def per_head_lanes(v):
        hi = v.astype(BF)
        lo = (v - hi.astype(F32)).astype(BF)
        return _dot(hi, e_ref[...]) + _dot(lo, e_ref[...])

    dt_x = per_head_lanes(dt)
    ea_x = per_head_lanes(ea)
    de_x = per_head_lanes(decay_end)
    cd_x = per_head_lanes(jnp.broadcast_to(chunk_decay, (SUBLANES, LANES)))[0:1]
    xdt = xs * dt_x
    xdt_b = xdt.astype(BF)
    xd_b = (xdt * de_x).astype(BF)
    y_off = []
    for g in range(SSD_GROUPS):
        h_old = ht_ref[bb, :, g * gi:(g + 1) * gi]
        y_off.append(_dot(cmb[g], h_old.astype(BF)))
        ht_ref[bb, :, g * gi:(g + 1) * gi] = (h_old * cd_x[:, g * gi:(g + 1) * gi]
                                              + _dot(bt[g], xd_b[:, g * gi:(g + 1) * gi]))

    def decay_weights(r):
        seg = a_cum[:, r:r + 1] - a_cum_t[r:r + 1, :]
        lmat = jnp.where(causal, jnp.exp(jnp.minimum(seg, 0.0)), 0.0)
        return (cb[r // heads_per_group] * lmat).astype(BF)

    first_half = lax.broadcasted_iota(jnp.int32, (q, LANES), 1) < SSD_HEAD_DIM
    y_diag = []
    for k in range(SSD_HEADS // 2):
        xp = xdt_b[:, k * LANES:(k + 1) * LANES]
        y_diag.append(jnp.where(first_half, _dot(decay_weights(2 * k), xp), _dot(decay_weights(2 * k + 1), xp)))
    y = jnp.concatenate(y_diag, axis=1) + jnp.concatenate(y_off, axis=1) * ea_x + dsk_ref[...] * xs
    y = y * _silu(z)
    gi = SSD_INNER // SSD_GROUPS
    yn = [_rms(y[:, g * gi:(g + 1) * gi], sg_ref[:, g * gi:(g + 1) * gi]) for g in range(SSD_GROUPS)]
    ymix_ref[bb, :, 0:SSD_INNER] = jnp.concatenate(yn, axis=1)[:lb].astype(BF)

    uvg = _gelu_tanh(uv)
    u = uvg[:, :GMLP_WIDTH]
    v = uvg[:, GMLP_WIDTH:]
    mu = jnp.mean(v, axis=-1, keepdims=True)
    vc = v - mu
    vn = vc * lax.rsqrt(jnp.mean(vc * vc, axis=-1, keepdims=True) + EPS) * lng_ref[...] + lnb_ref[...]
    if v_ref is not None:
        v_ref[bb] = vn[:lb]
    gd = GMLP_WIDTH // GMLP_GROUPS
    yb = []
    for g in range(GMLP_GROUPS):
        wt = jnp.where(causal, ws_ref[g, :q, :q], 0.0).astype(BF)
        sp = _dot(wt, vn[:, g * gd:(g + 1) * gd].astype(BF)) + bst_ref[:q, g:g + 1]
        yb.append(u[:, g * gd:(g + 1) * gd] * sp)
    ymix_ref[bb, :, SSD_INNER:SSD_INNER + GMLP_WIDTH] = jnp.concatenate(yb, axis=1)[:lb].astype(BF)


def _even_kernel_no_v(*refs, **kw):
    n_in = 15
    ins, outs, scr = refs[:n_in], refs[n_in:n_in + 3], refs[n_in + 3:]
    _even_kernel(*ins, *outs, None, *scr, **kw)


def _even_mixer(proj, dtp, cinit8, sinit, prm, *, q, lb, lc, want_v, layer):
    b, lp, _ = proj.shape
    nchunks = lp // lb
    nbb = EVEN_SEQS_PER_STEP if b % EVEN_SEQS_PER_STEP == 0 else 1
    par = [prm["conv_w"], prm["conv_b"], prm["dt_bias"], prm["a_log"], prm["d_skip"], prm["ssd_gain"],
           prm["ln_g"], prm["ln_b"], prm["ws"], prm["bst"], prm["expand"]]
    in_specs = [
        pl.BlockSpec((nbb, lb, ZXU), lambda i, c: (i, c, 0)),
        pl.BlockSpec((nbb, lb, LANES), lambda i, c: (i, c, 0)),
        pl.BlockSpec((nbb, SUBLANES, CONV_DIM), lambda i, c: (i, 0, 0)),
        pl.BlockSpec((1, nbb, SSD_INNER, SSD_STATE), lambda i, c: (layer, i, 0, 0)),
    ] + [_full(w.shape) for w in par]
    out_specs = [
        pl.BlockSpec((nbb, lb, SSD_INNER + GMLP_WIDTH), lambda i, c: (i, c, 0)),
        pl.BlockSpec((nbb, SSD_CONV - 1, CONV_DIM), lambda i, c: (i, 0, 0)),
        pl.BlockSpec((nbb, SSD_INNER, SSD_STATE), lambda i, c: (i, 0, 0)),
    ]
    out_shape = [
        jax.ShapeDtypeStruct((b, lp, SSD_INNER + GMLP_WIDTH), BF),
        jax.ShapeDtypeStruct((b, SSD_CONV - 1, CONV_DIM), F32),
        jax.ShapeDtypeStruct((b, SSD_INNER, SSD_STATE), F32),
    ]
    if want_v:
        out_specs.append(pl.BlockSpec((nbb, lb, GMLP_WIDTH), lambda i, c: (i, c, 0)))
        out_shape.append(jax.ShapeDtypeStruct((b, lp, GMLP_WIDTH), F32))
        body = functools.partial(_even_kernel, q=q, lb=lb, lc=lc, nbb=nbb)
    else:
        body = functools.partial(_even_kernel_no_v, q=q, lb=lb, lc=lc, nbb=nbb)
    return pl.pallas_call(
        body,
        grid=(b // nbb, nchunks),
        in_specs=in_specs,
        out_specs=out_specs,
        out_shape=out_shape,
        scratch_shapes=[pltpu.VMEM((nbb, q + 2 * SUBLANES, CONV_DIM), F32),
                        pltpu.VMEM((nbb, SSD_STATE, SSD_INNER), F32)],
        compiler_params=_cp(("parallel", "arbitrary"), vmem_mb=48),
        name="even_mixer",
    )(proj, dtp, cinit8, sinit, *par)


def _mla_proj_kernel(x_ref, g_ref, wdq_ref, wdkv_ref, wdk2_ref, qg_ref, kvg_ref, wqn_ref, wqp_ref,
                     wuk_ref, cos_ref, sin_ref, ckv_ref, kpe_ref, kcat_ref, q_ref):
    xn = _rms(x_ref[...], g_ref[...]).astype(BF)
    cqn = _rms(_dot(xn, wdq_ref[...]), qg_ref[...]).astype(BF)
    ckv = _rms(_dot(xn, wdkv_ref[...]), kvg_ref[...])
    kk = _dot(xn, wdk2_ref[...])
    cos = cos_ref[...]
    sin = sin_ref[...]
    kpe = kk[:, :ROPE] * cos[:, :ROPE] + kk[:, ROPE:] * sin[:, :ROPE]
    ckv_ref[...] = ckv
    kpe_ref[...] = kpe
    kcat_ref[:, :KV_RANK] = ckv.astype(BF)
    kcat_ref[:, KV_RANK:] = kpe.astype(BF)
    qn = _dot(cqn, wqn_ref[...])
    qp = _dot(cqn, wqp_ref[...])
    hr = MLA_HEADS * ROPE
    qpe = qp[:, :hr] * cos + qp[:, hr:] * sin
    for h in range(MLA_HEADS):
        ql = _dot(qn[:, h * NOPE:(h + 1) * NOPE].astype(BF), wuk_ref[h])
        q_ref[0, h, :, :KV_RANK] = (ql * MLA_SCALE).astype(BF)
        q_ref[0, h, :, KV_RANK:] = (qpe[:, h * ROPE:(h + 1) * ROPE] * MLA_SCALE).astype(BF)


def _mla_proj(x, g, prm, cos8, sin8, *, nb):
    t = x.shape[0]
    seq = t // nb
    tm = min(ROW_TILE, seq)
    tpb = seq // tm
    w = [prm["wdq"], prm["wdkv"], prm["wdk2"], prm["q_gain"], prm["kv_gain"], prm["wqn"], prm["wqp"],
         prm["wuk"]]
    return pl.pallas_call(
        _mla_proj_kernel,
        grid=(t // tm,),
        in_specs=[pl.BlockSpec((tm, D), lambda i: (i, 0)), _full((1, D))] + [_full(a.shape) for a in w] + [
            pl.BlockSpec((tm, MLA_HEADS * ROPE), lambda i: (i % tpb, 0)),
            pl.BlockSpec((tm, MLA_HEADS * ROPE), lambda i: (i % tpb, 0)),
        ],
        out_specs=[
            pl.BlockSpec((tm, KV_RANK), lambda i: (i, 0)),
            pl.BlockSpec((tm, ROPE), lambda i: (i, 0)),
            pl.BlockSpec((tm, QK), lambda i: (i, 0)),
            pl.BlockSpec((1, MLA_HEADS, tm, QK), lambda i: (i // tpb, 0, i % tpb, 0)),
        ],
        out_shape=[
            jax.ShapeDtypeStruct((t, KV_RANK), F32),
            jax.ShapeDtypeStruct((t, ROPE), F32),
            jax.ShapeDtypeStruct((t, QK), BF),
            jax.ShapeDtypeStruct((nb, MLA_HEADS, seq, QK), BF),
        ],
        compiler_params=_cp(("parallel",)),
        name="mla_proj",
    )(x, g, *w, cos8, sin8)


HEAD_QK = NOPE + ROPE
LOG2E = 1.4426950408889634


def _mla_proj_prompt_kernel(x_ref, g_ref, wdq_ref, wdkv_ref, wdk2_ref, qg_ref, kvg_ref, wqn_ref, wqp_ref,
                            wkn_ref, wv_ref, cos_ref, sin_ref, ckv_ref, kpe_ref, q_ref, k_ref, v_ref):
    xn = _rms(x_ref[...], g_ref[...]).astype(BF)
    cqn = _rms(_dot(xn, wdq_ref[...]), qg_ref[...]).astype(BF)
    ckv = _rms(_dot(xn, wdkv_ref[...]), kvg_ref[...])
    kk = _dot(xn, wdk2_ref[...])
    cos = cos_ref[...]
    sin = sin_ref[...]
    kpe = kk[:, :ROPE] * cos[:, :ROPE] + kk[:, ROPE:] * sin[:, :ROPE]
    ckv_ref[...] = ckv
    kpe_ref[...] = kpe
    ckv_b = ckv.astype(BF)
    kpe_b = kpe.astype(BF)
    kn = _dot(ckv_b, wkn_ref[...])
    vt = _dot_nt(wv_ref[...], ckv_b)
    qn = _dot(cqn, wqn_ref[...])
    qp = _dot(cqn, wqp_ref[...])
    hr = MLA_HEADS * ROPE
    qpe = qp[:, :hr] * cos + qp[:, hr:] * sin
    qscale = MLA_SCALE * LOG2E
    for h in range(MLA_HEADS):
        q_ref[0, h, :, :NOPE] = (qn[:, h * NOPE:(h + 1) * NOPE] * qscale).astype(BF)
        q_ref[0, h, :, NOPE:] = (qpe[:, h * ROPE:(h + 1) * ROPE] * qscale).astype(BF)
        k_ref[0, h, :, :NOPE] = kn[:, h * NOPE:(h + 1) * NOPE].astype(BF)
        k_ref[0, h, :, NOPE:] = kpe_b
        v_ref[0, h, 0] = vt[h * MLA_V:(h + 1) * MLA_V].astype(BF)


def _mla_proj_prompt(x, g, prm, cos8, sin8, *, nb):
    t = x.shape[0]
    seq = t // nb
    tm = min(ROW_TILE, seq)
    tpb = seq // tm
    w = [prm["wdq"], prm["wdkv"], prm["wdk2"], prm["q_gain"], prm["kv_gain"], prm["wqn"], prm["wqp"],
         prm["wkn"], prm["wv"]]
    head_spec = lambda width: pl.BlockSpec((1, MLA_HEADS, tm, width), lambda i: (i // tpb, 0, i % tpb, 0))
    head_shape = lambda width: jax.ShapeDtypeStruct((nb, MLA_HEADS, seq, width), BF)
    return pl.pallas_call(
        _mla_proj_prompt_kernel,
        grid=(t // tm,),
        in_specs=[pl.BlockSpec((tm, D), lambda i: (i, 0)), _full((1, D))] + [_full(a.shape) for a in w] + [
            pl.BlockSpec((tm, MLA_HEADS * ROPE), lambda i: (i % tpb, 0)),
            pl.BlockSpec((tm, MLA_HEADS * ROPE), lambda i: (i % tpb, 0)),
        ],
        out_specs=[
            pl.BlockSpec((tm, KV_RANK), lambda i: (i, 0)),
            pl.BlockSpec((tm, ROPE), lambda i: (i, 0)),
            head_spec(HEAD_QK), head_spec(HEAD_QK),
            pl.BlockSpec((1, MLA_HEADS, 1, MLA_V, tm), lambda i: (i // tpb, 0, i % tpb, 0, 0)),
        ],
        out_shape=[
            jax.ShapeDtypeStruct((t, KV_RANK), F32),
            jax.ShapeDtypeStruct((t, ROPE), F32),
            head_shape(HEAD_QK), head_shape(HEAD_QK),
            jax.ShapeDtypeStruct((nb, MLA_HEADS, tpb, MLA_V, tm), BF),
        ],
        compiler_params=_cp(("parallel",)),
        name="mla_proj_prompt",
    )(x, g, *w, cos8, sin8)


FLASH_HEADS = 4
FLASH_QSPLIT = 1
DENOM_ROWS = 16


def _flash_kernel(q_ref, k_ref, vt_ref, o_ref, m_ref, acc_ref, *, t):
    qi = pl.program_id(2)
    for hh in range(FLASH_HEADS):
        m_ref[hh] = jnp.full((1, t), NEG, F32)
        acc_ref[hh] = jnp.zeros((MLA_V + DENOM_ROWS, t), F32)
    ones = jnp.ones((DENOM_ROWS, t), BF)

    tq = t // FLASH_QSPLIT

    def block(ki, masked):
        start = pl.multiple_of(ki * t, t)
        for hh in range(FLASH_HEADS):
            kb = k_ref[0, hh, pl.ds(start, t), :]
            v1 = jnp.concatenate([vt_ref[0, hh, ki], ones], axis=0)
            for qs in range(FLASH_QSPLIT):
                lanes = slice(qs * tq, (qs + 1) * tq)
                st = _dot_nt(kb, q_ref[0, hh, lanes, :])
                if masked:
                    key = lax.broadcasted_iota(jnp.int32, (t, tq), 0)
                    qry = lax.broadcasted_iota(jnp.int32, (t, tq), 1) + qs * tq
                    st = jnp.where(key <= qry, st, NEG)
                m_old = m_ref[hh, :, lanes]
                m_new = jnp.maximum(m_old, jnp.max(st, axis=0, keepdims=True))
                alpha = jnp.exp2(m_old - m_new)
                pt = jnp.exp2(st - m_new).astype(BF)
                acc_ref[hh, :, lanes] = alpha * acc_ref[hh, :, lanes] + _dot(v1, pt)
                m_ref[hh, :, lanes] = m_new

    def body(ki, carry):
        block(ki, False)
        return carry

    lax.fori_loop(0, qi, body, 0)
    block(qi, True)
    for hh in range(FLASH_HEADS):
        acc = acc_ref[hh]
        o_t = acc[:MLA_V] / acc[MLA_V:MLA_V + 1]
        o_ref[0, :, hh * MLA_V:(hh + 1) * MLA_V] = o_t.T.astype(BF)


def _flash(q, k, vt):
    nb, _, seq, _ = q.shape
    t = vt.shape[-1]
    nh = FLASH_HEADS
    return pl.pallas_call(
        functools.partial(_flash_kernel, t=t),
        grid=(nb, MLA_HEADS // nh, seq // t),
        in_specs=[
            pl.BlockSpec((1, nh, t, HEAD_QK), lambda b, h, i: (b, h, i, 0)),
            pl.BlockSpec((1, nh, seq, HEAD_QK), lambda b, h, i: (b, h, 0, 0), pipeline_mode=pl.Buffered(1)),
            pl.BlockSpec((1, nh, seq // t, MLA_V, t), lambda b, h, i: (b, h, 0, 0, 0),
                         pipeline_mode=pl.Buffered(1)),
        ],
        out_specs=pl.BlockSpec((1, t, nh * MLA_V), lambda b, h, i: (b, i, h)),
        out_shape=jax.ShapeDtypeStruct((nb, seq, MLA_HEADS * MLA_V), BF),
        scratch_shapes=[pltpu.VMEM((nh, 1, t), F32), pltpu.VMEM((nh, MLA_V + DENOM_ROWS, t), F32)],
        compiler_params=_cp(("parallel", "parallel", "arbitrary"), vmem_mb=48),
        name="mla_flash",
    )(q, k, vt)


def _page_copies(pt_ref, ckv_hbm, kpt_hbm, ckbuf, kpbuf, sem, sample, slot, *, layer, n_pages):
    copies = []
    for p in range(n_pages):
        pg = pt_ref[sample, p]
        copies.append(pltpu.make_async_copy(
            ckv_hbm.at[layer, pg], ckbuf.at[slot, pl.ds(p * PAGE, PAGE), :], sem.at[0, slot]))
        copies.append(pltpu.make_async_copy(
            kpt_hbm.at[layer, pg], kpbuf.at[slot, :, pl.ds(p * PAGE, PAGE)], sem.at[1, slot]))
    return copies


def _decode_kernel(pt_ref, q_ref, nk_ref, ckv_hbm, kpt_hbm, o_ref, ckbuf, kpbuf, sem, *,
                   dec_seq, layer, n_pages):
    b = pl.program_id(0)
    last = pl.num_programs(0) - 1
    slot = b % 2
    copies = functools.partial(_page_copies, pt_ref, ckv_hbm, kpt_hbm, ckbuf, kpbuf, sem,
                               layer=layer, n_pages=n_pages)

    @pl.when(b == 0)
    def _():
        for c in copies(0, 0):
            c.start()

    for c in copies(b, slot):
        c.wait()
    nxt = jnp.minimum(b + 1, last)
    for c in copies(nxt, 1 - slot):
        c.start()

    rows = MLA_HEADS * dec_seq
    qm = q_ref[0]
    ql = qm[:, :KV_RANK]
    qp = qm[:, KV_RANK:]
    ck = ckbuf[slot].astype(BF)
    kp = kpbuf[slot].astype(BF)
    s = _dot_nt(ql, ck) + _dot(qp, kp)
    nk = nk_ref[0]
    kt = lax.broadcasted_iota(jnp.int32, (rows, NEW_KEY_ROWS), 1)
    qt = lax.broadcasted_iota(jnp.int32, (rows, NEW_KEY_ROWS), 0) % dec_seq
    s_new = jnp.where(kt <= qt, _dot_nt(qm, nk), NEG)
    m = jnp.maximum(jnp.max(s, axis=-1, keepdims=True), jnp.max(s_new, axis=-1, keepdims=True))
    p = jnp.exp(s - m)
    p_new = jnp.exp(s_new - m)
    denom = jnp.sum(p, axis=-1, keepdims=True) + jnp.sum(p_new, axis=-1, keepdims=True)
    o_ref[0] = (_dot(p.astype(BF), ck) + _dot(p_new.astype(BF), nk[:, :KV_RANK])) / denom

    @pl.when(b == last)
    def _():
        for c in copies(nxt, 1 - slot):
            c.wait()


def _decode(page_table, q, newk, cache_ckv, cache_kpe_t, layer, dec_seq):
    nb, n_pages = page_table.shape
    rows = MLA_HEADS * dec_seq
    keys = n_pages * PAGE
    grid_spec = pltpu.PrefetchScalarGridSpec(
        num_scalar_prefetch=1,
        grid=(nb,),
        in_specs=[
            pl.BlockSpec((1, rows, QK), lambda b, pt: (b, 0, 0)),
            pl.BlockSpec((1, NEW_KEY_ROWS, QK), lambda b, pt: (b, 0, 0)),
            pl.BlockSpec(memory_space=pl.ANY),
            pl.BlockSpec(memory_space=pl.ANY),
        ],
        out_specs=pl.BlockSpec((1, rows, KV_RANK), lambda b, pt: (b, 0, 0)),
        scratch_shapes=[pltpu.VMEM((2, keys, KV_RANK), F32), pltpu.VMEM((2, ROPE, keys), F32),
                        pltpu.SemaphoreType.DMA((2, 2))],
    )
    return pl.pallas_call(
        functools.partial(_decode_kernel, dec_seq=dec_seq, layer=layer, n_pages=n_pages),
        grid_spec=grid_spec,
        out_shape=jax.ShapeDtypeStruct((nb, rows, KV_RANK), F32),
        compiler_params=_cp(("arbitrary",), vmem_mb=48),
        name="mla_decode",
    )(page_table, q, newk, cache_ckv, cache_kpe_t)


def _mla_out_kernel(o_ref, wuv_ref, wo_ref, r_ref, out_ref):
    parts = [_dot(o_ref[0, h], wuv_ref[h]).astype(BF) for h in range(MLA_HEADS)]
    out_ref[...] = r_ref[...] + _dot(jnp.concatenate(parts, axis=1), wo_ref[...])


def _mla_out(o_lat, wuv, wo, res):
    nb, _, seq, _ = o_lat.shape
    tm = min(ROW_TILE, seq)
    tpb = seq // tm
    t = nb * seq
    return pl.pallas_call(
        _mla_out_kernel,
        grid=(t // tm,),
        in_specs=[
            pl.BlockSpec((1, MLA_HEADS, tm, KV_RANK), lambda i: (i // tpb, 0, i % tpb, 0)),
            _full(wuv.shape), _full(wo.shape),
            pl.BlockSpec((tm, D), lambda i: (i, 0)),
        ],
        out_specs=pl.BlockSpec((tm, D), lambda i: (i, 0)),
        out_shape=jax.ShapeDtypeStruct((t, D), F32),
        compiler_params=_cp(("parallel",)),
        name="mla_out",
    )(o_lat, wuv, wo, res)


def _softmax_rows(s):
    m = jnp.max(s, axis=-1, keepdims=True)
    p = jnp.exp(s - m)
    return p / jnp.sum(p, axis=-1, keepdims=True)


def _mem_prompt_kernel(x_ref, g_ref, wq_ref, k_ref, v_ref, wo_ref, o_ref):
    x = x_ref[...]
    xn = _rms(x, g_ref[...]).astype(BF)
    qm = (_dot(xn, wq_ref[...]) * MEM_HD ** -0.5).astype(BF)
    km = k_ref[0].astype(BF)
    vm = v_ref[0].astype(BF)
    parts = []
    for h in range(MEM_HEADS):
        sl = slice(h * MEM_HD, (h + 1) * MEM_HD)
        p = _softmax_rows(_dot_nt(qm[:, sl], km[:, sl]))
        parts.append(_dot(p.astype(BF), vm[:, sl]).astype(BF))
    o_ref[...] = x + _dot(jnp.concatenate(parts, axis=1), wo_ref[...])


def _mem_prompt(x, g, wq, km, vm, wo):
    t = x.shape[0]
    nb, mt, _ = km.shape
    seq = t // nb
    tm = min(ROW_TILE, seq)
    tpb = seq // tm
    return pl.pallas_call(
        _mem_prompt_kernel,
        grid=(t // tm,),
        in_specs=[
            pl.BlockSpec((tm, D), lambda i: (i, 0)), _full((1, D)), _full(wq.shape),
            pl.BlockSpec((1, mt, MEM_INNER), lambda i: (i // tpb, 0, 0)),
            pl.BlockSpec((1, mt, MEM_INNER), lambda i: (i // tpb, 0, 0)),
            _full(wo.shape),
        ],
        out_specs=pl.BlockSpec((tm, D), lambda i: (i, 0)),
        out_shape=jax.ShapeDtypeStruct((t, D), F32),
        compiler_params=_cp(("parallel",)),
        name="mem_attn_prompt",
    )(x, g, wq, km, vm, wo)


MEM_ROWS = MEM_HEADS * SUBLANES


def _mem_sample_kernel(x_ref, g_ref, wq_ref, k_ref, v_ref, wo_ref, o_ref):
    ns = SAMPLES_PER_STEP
    x = x_ref[...].reshape(ns * SUBLANES, D)
    xn = _rms(x, g_ref[...]).astype(BF)
    qall = _dot(xn, wq_ref[...]) * MEM_HD ** -0.5
    cols = k_ref.shape[2]
    head_of_row = lax.broadcasted_iota(jnp.int32, (MEM_ROWS, cols), 0) // SUBLANES
    head_of_col = lax.broadcasted_iota(jnp.int32, (MEM_ROWS, cols), 1) % MEM_HEADS
    own = head_of_row == head_of_col
    outs = []
    for s in range(ns):
        qs = qall[s * SUBLANES:(s + 1) * SUBLANES]
        qst = jnp.concatenate([qs[:, h * MEM_HD:(h + 1) * MEM_HD] for h in range(MEM_HEADS)], axis=0)
        sc = jnp.where(own, _dot_nt(qst.astype(BF), k_ref[0, s].astype(BF)), NEG)
        o = _dot(_softmax_rows(sc).astype(BF), v_ref[0, s].astype(BF))
        outs.append(jnp.concatenate([o[h * SUBLANES:(h + 1) * SUBLANES] for h in range(MEM_HEADS)], axis=1))
    out = x + _dot(jnp.concatenate(outs, axis=0).astype(BF), wo_ref[...])
    o_ref[...] = out.reshape(ns, SUBLANES, D)


def _mem_sample(x8, g, wq, cache_k, cache_v, wo, layer):
    nb = x8.shape[0]
    rows = cache_k.shape[2]
    ns = SAMPLES_PER_STEP
    return pl.pallas_call(
        _mem_sample_kernel,
        grid=(nb // ns,),
        in_specs=[
            pl.BlockSpec((ns, SUBLANES, D), lambda i: (i, 0, 0)), _full((1, D)), _full(wq.shape),
            pl.BlockSpec((1, ns, rows, MEM_HD), lambda i: (layer, i, 0, 0)),
            pl.BlockSpec((1, ns, rows, MEM_HD), lambda i: (layer, i, 0, 0)),
            _full(wo.shape),
        ],
        out_specs=pl.BlockSpec((ns, SUBLANES, D), lambda i: (i, 0, 0)),
        out_shape=jax.ShapeDtypeStruct((nb, SUBLANES, D), F32),
        compiler_params=_cp(("parallel",), vmem_mb=48),
        name="mem_attn_sample",
    )(x8, g, wq, cache_k, cache_v, wo)


def _router_kernel(x_ref, g_ref, wr_ref, xn_ref, route_ref):
    xn = _rms(x_ref[...], g_ref[...])
    xn_ref[...] = xn
    lane = lax.broadcasted_iota(jnp.int32, (xn.shape[0], LANES), 1).astype(F32)
    lg = jnp.where(lane < N_EXP, _dot_f32(xn, wr_ref[...]), NEG)
    m1 = jnp.max(lg, axis=-1, keepdims=True)
    i1 = jnp.min(jnp.where(lg == m1, lane, float(LANES)), axis=-1, keepdims=True)
    lg2 = jnp.where(lane == i1, NEG, lg)
    m2 = jnp.max(lg2, axis=-1, keepdims=True)
    i2 = jnp.min(jnp.where(lg2 == m2, lane, float(LANES)), axis=-1, keepdims=True)
    e = jnp.exp(m2 - m1)
    g1 = 1.0 / (1.0 + e)
    g2 = e * g1
    route_ref[...] = jnp.where(lane == 0, i1, jnp.where(lane == 1, i2, jnp.where(lane == 2, g1,
                               jnp.where(lane == 3, g2, 0.0))))


def _router(x, g, wr_pad):
    t = x.shape[0]
    tm = min(ROW_TILE, t)
    return pl.pallas_call(
        _router_kernel,
        grid=(t // tm,),
        in_specs=[pl.BlockSpec((tm, D), lambda i: (i, 0)), _full((1, D)), _full(wr_pad.shape)],
        out_specs=[pl.BlockSpec((tm, D), lambda i: (i, 0)), pl.BlockSpec((tm, LANES), lambda i: (i, 0))],
        out_shape=[jax.ShapeDtypeStruct((t, D), F32), jax.ShapeDtypeStruct((t, LANES), F32)],
        compiler_params=_cp(("parallel",)),
        name="router",
    )(x, g, wr_pad)


SC_ROWS = 32


def _sc_mesh():
    return plsc.VectorSubcoreMesh(core_axis_name="c", subcore_axis_name="s")


def _sc_workers():
    info = plsc.get_sparse_core_info()
    return info.num_cores, info.num_cores * info.num_subcores


def _sc_gather_rows(table, idx):
    n = idx.shape[0]
    width = table.shape[1]
    ncores, nw = _sc_workers()
    per_w = n // nw
    n_chunks = per_w // SC_ROWS
    assert per_w * nw == n and n_chunks * SC_ROWS == per_w

    @functools.partial(
        pl.kernel, mesh=_sc_mesh(),
        out_type=jax.ShapeDtypeStruct((n, width), table.dtype),
        scratch_types=[pltpu.VMEM((per_w,), jnp.int32), pltpu.VMEM((SC_ROWS, width), table.dtype),
                       pltpu.SemaphoreType.DMA],
        name="sc_gather_rows",
    )
    def body(table_hbm, idx_hbm, out_hbm, idx_v, rows_v, sem):
        wid = lax.axis_index("s") * ncores + lax.axis_index("c")
        base = wid * per_w
        pltpu.sync_copy(idx_hbm.at[pl.ds(base, per_w)], idx_v)

        @pl.loop(0, n_chunks)
        def _(j):
            off = pl.multiple_of(j * SC_ROWS, SC_ROWS)
            pltpu.async_copy(table_hbm.at[idx_v.at[pl.ds(off, SC_ROWS)]], rows_v, sem).wait()
            pltpu.sync_copy(rows_v, out_hbm.at[pl.ds(base + off, SC_ROWS)])

    return body(table, idx)


def _sc_scatter_rows(src, idx, n_out):
    n = idx.shape[0]
    t, width = src.shape
    ncores, nw = _sc_workers()
    per_w = n // nw
    n_chunks = per_w // SC_ROWS
    assert per_w * nw == n and n_chunks * SC_ROWS == per_w and t % per_w == 0
    idx3 = idx.reshape(nw, n_chunks, SC_ROWS)

    @functools.partial(
        pl.kernel, mesh=_sc_mesh(),
        out_type=jax.ShapeDtypeStruct((n_out, width), src.dtype),
        scratch_types=[pltpu.VMEM((n_chunks, SC_ROWS), jnp.int32), pltpu.VMEM((SC_ROWS, width), src.dtype),
                       pltpu.SemaphoreType.DMA],
        name="sc_scatter_rows",
    )
    def body(src_hbm, idx_hbm, out_hbm, idx_v, rows_v, sem):
        wid = lax.axis_index("s") * ncores + lax.axis_index("c")
        base = lax.rem(wid * per_w, t)
        pltpu.sync_copy(idx_hbm.at[wid], idx_v)

        @pl.loop(0, n_chunks)
        def _(j):
            off = pl.multiple_of(j * SC_ROWS, SC_ROWS)
            pltpu.sync_copy(src_hbm.at[pl.ds(base + off, SC_ROWS)], rows_v)
            pltpu.async_copy(rows_v, out_hbm.at[idx_v.at[j]], sem).wait()

    return body(src, idx3)


MOE_TILE = 1024
MOE_TILE_SMALL = 256
MOE_BLOCK = 512
MOE_CHUNK = 256


def _moe_ffn_kernel(te_ref, nused_ref, x_ref, wg_ref, wu_ref, wd_ref, o_ref, xb_ref):
    del te_ref
    i = pl.program_id(0)
    j = pl.program_id(1)

    @pl.when(i < nused_ref[0])
    def _():
        @pl.when(j == 0)
        def _():
            xb_ref[...] = x_ref[...].astype(BF)

        xb = xb_ref[...]
        part = None
        for c in range(MOE_BLOCK // MOE_CHUNK):
            sl = slice(c * MOE_CHUNK, (c + 1) * MOE_CHUNK)
            gate = _dot(xb, wg_ref[0, 0, :, sl].astype(BF))
            up = _dot(xb, wu_ref[0, 0, :, sl].astype(BF))
            contrib = _dot((_silu(gate) * up).astype(BF), wd_ref[0, 0, sl, :].astype(BF))
            part = contrib if part is None else part + contrib

        @pl.when(j == 0)
        def _():
            o_ref[...] = part

        @pl.when(j != 0)
        def _():
            o_ref[...] += part

    @pl.when(i >= nused_ref[0])
    def _():
        o_ref[...] = jnp.zeros_like(o_ref)


def _moe_ffn(tile_expert, nused, xs, wgu, wd, layer, *, tm):
    npad = xs.shape[0]
    nblk = D_FFE // MOE_BLOCK
    grid_spec = pltpu.PrefetchScalarGridSpec(
        num_scalar_prefetch=2,
        grid=(npad // tm, nblk),
        in_specs=[
            pl.BlockSpec((tm, D), lambda i, j, te, nu: (i, 0)),
            pl.BlockSpec((1, 1, D, MOE_BLOCK), lambda i, j, te, nu: (layer, te[i], 0, j)),
            pl.BlockSpec((1, 1, D, MOE_BLOCK), lambda i, j, te, nu: (layer, te[i], 0, nblk + j)),
            pl.BlockSpec((1, 1, MOE_BLOCK, D), lambda i, j, te, nu: (layer, te[i], j, 0)),
        ],
        out_specs=pl.BlockSpec((tm, D), lambda i, j, te, nu: (i, 0)),
        scratch_shapes=[pltpu.VMEM((tm, D), BF)],
    )
    return pl.pallas_call(
        _moe_ffn_kernel,
        grid_spec=grid_spec,
        out_shape=jax.ShapeDtypeStruct((npad, D), F32),
        compiler_params=_cp(("parallel", "arbitrary"), vmem_mb=56),
        name="moe_ffn",
    )(tile_expert, nused, xs, wgu, wgu, wd)


ROUTE_GATE_LANE = 2


def _combine_kernel(h_ref, y0_ref, y1_ref, route_ref, g_ref, o_ref, *, final):
    gl = ROUTE_GATE_LANE
    route = route_ref[...]
    out = h_ref[...] + route[:, gl:gl + 1] * y0_ref[...] + route[:, gl + 1:gl + 2] * y1_ref[...]
    if final:
        out = _rms(out, g_ref[...])
    o_ref[...] = out


def _combine(h, y2, route, g, *, final):
    t = h.shape[0]
    tm = min(ROW_TILE, t)
    nt = t // tm
    return pl.pallas_call(
        functools.partial(_combine_kernel, final=final),
        grid=(nt,),
        in_specs=[
            pl.BlockSpec((tm, D), lambda i: (i, 0)),
            pl.BlockSpec((tm, D), lambda i: (i, 0)),
            pl.BlockSpec((tm, D), lambda i: (nt + i, 0)),
            pl.BlockSpec((tm, LANES), lambda i: (i, 0)),
            _full((1, D)),
        ],
        out_specs=pl.BlockSpec((tm, D), lambda i: (i, 0)),
        out_shape=jax.ShapeDtypeStruct((t, D), F32),
        compiler_params=_cp(("parallel",)),
        name="moe_combine",
    )(h, y2, y2, route, g)


def _moe(h, g, wr_pad, wgu, wd, layer, final_g, *, final):
    t = h.shape[0]
    tm = MOE_TILE if 2 * t >= 2 * N_EXP * MOE_TILE else MOE_TILE_SMALL
    xn, route = _router(h, g, wr_pad)
    eidx = route[:, :ROUTE_GATE_LANE].astype(jnp.int32)
    e_flat = eidx.T.reshape(-1)
    onehot = (e_flat[:, None] == jnp.arange(N_EXP, dtype=jnp.int32)[None, :]).astype(jnp.int32)
    csum = jnp.cumsum(onehot, axis=0)
    counts = csum[-1]
    rank = jnp.sum(onehot * csum, axis=1) - 1
    padded = ((counts + tm - 1) // tm) * tm
    ends = jnp.cumsum(padded)
    starts = ends - padded
    dest = (jnp.sum(onehot * starts[None, :], axis=1) + rank).astype(jnp.int32)
    n_tiles = (2 * t) // tm + N_EXP
    tile_start = jnp.arange(n_tiles, dtype=jnp.int32) * tm
    tile_expert = jnp.minimum(jnp.sum((tile_start[:, None] >= ends[None, :]).astype(jnp.int32), axis=1),
                              N_EXP - 1).astype(jnp.int32)
    nused = (ends[-1] // tm).astype(jnp.int32).reshape(1)
    xs = _sc_scatter_rows(xn, dest, n_tiles * tm)
    ys = _moe_ffn(tile_expert, nused, xs, wgu, wd, layer, tm=tm)
    y2 = _sc_gather_rows(ys, dest)
    return _combine(h, y2, route, final_g, final=final)


def _rope_tables(pos):
    half = ROPE // 2
    inv_freq = ROPE_BASE ** (-jnp.arange(half, dtype=F32) / half)
    ang = pos.astype(F32)[:, None] * inv_freq
    cos = jnp.cos(ang)
    sin = jnp.sin(ang)
    cc = jnp.concatenate([cos, cos], axis=-1)
    ss = jnp.concatenate([-sin, sin], axis=-1)
    return jnp.tile(cc, (1, MLA_HEADS)), jnp.tile(ss, (1, MLA_HEADS))


def _even_layer(h, g, prm, cinit, sinit, layer, *, nb, sample):
    t = h.shape[0]
    seq = t // nb
    proj = _rms_matmul(h, g, prm["w_zxu"], tn=_largest_tile(ZXU, 1536))
    dtp = _rms_matmul(h, g, prm["w_dt"], tn=LANES, precise=True)
    proj = proj.reshape(nb, seq, ZXU)
    dtp = dtp.reshape(nb, seq, LANES)
    if sample:
        pad = ((0, 0), (0, SUBLANES - seq), (0, 0))
        proj = jnp.pad(proj, pad)
        dtp = jnp.pad(dtp, pad)
        q, lb, lc = SAMPLE_Q, SUBLANES, seq
    else:
        q, lb, lc = CHUNK, CHUNK, CHUNK
    cinit8 = jnp.pad(cinit, ((0, 0), (SUBLANES - (SSD_CONV - 1), 0), (0, 0)))
    outs = _even_mixer(proj, dtp, cinit8, sinit.reshape(-1, nb, SSD_INNER, SSD_STATE), prm,
                       q=q, lb=lb, lc=lc, want_v=sample, layer=layer)
    ymix, cout, sout = outs[:3]
    v = None
    if sample:
        ymix = ymix[:, :seq]
        v = outs[3][:, :seq]
    h = _matmul_res(ymix.reshape(t, SSD_INNER + GMLP_WIDTH), prm["w_out"], h)
    return h, cout, sout.reshape(nb, SSD_HEADS, SSD_HEAD_DIM, SSD_STATE), v


def _prep_even(i, w_in, conv_w, conv_b, dt_bias, a_log, d_skip, ssd_gain, ln_g, ln_b, ws, bs, w_out):
    w = w_in[i]
    o1 = SSD_INNER + CONV_DIM
    w_zxu = jnp.concatenate([w[:, :o1], w[:, o1 + SSD_HEADS:]], axis=1).astype(BF)
    w_dt = jnp.pad(w[:, o1:o1 + SSD_HEADS], ((0, 0), (0, LANES - SSD_HEADS)))
    padl = (0, LANES - SSD_HEADS)
    return dict(
        w_zxu=w_zxu, w_dt=w_dt,
        conv_w=jnp.pad(conv_w[i], ((0, SUBLANES - SSD_CONV), (0, 0))),
        conv_b=conv_b[i][None, :],
        dt_bias=jnp.pad(dt_bias[i], padl)[None, :],
        a_log=jnp.pad(a_log[i], padl)[None, :],
        d_skip=jnp.repeat(d_skip[i], SSD_HEAD_DIM)[None, :],
        ssd_gain=ssd_gain[i][None, :],
        ln_g=ln_g[i][None, :], ln_b=ln_b[i][None, :],
        ws=ws[i], bst=bs[i].T,
        expand=(jnp.arange(LANES)[:, None] == jnp.arange(SSD_INNER)[None, :] // SSD_HEAD_DIM).astype(BF),
        w_out=w_out[i].astype(BF),
    )


def _prep_mla(i, w_down, q_gain, kv_gain, w_uq, w_uk, w_uv, w_o):
    wd = w_down[i]
    wk = wd[:, Q_RANK + KV_RANK:]
    half = ROPE // 2
    rot = lambda a: jnp.concatenate([a[..., half:], a[..., :half]], axis=-1)
    uq = w_uq[i]
    uq_pe = uq[:, :, NOPE:]
    return dict(
        wdq=wd[:, :Q_RANK].astype(BF),
        wdkv=wd[:, Q_RANK:Q_RANK + KV_RANK].astype(BF),
        wdk2=jnp.concatenate([wk, rot(wk)], axis=1).astype(BF),
        q_gain=q_gain[i][None, :], kv_gain=kv_gain[i][None, :],
        wqn=uq[:, :, :NOPE].reshape(Q_RANK, MLA_HEADS * NOPE).astype(BF),
        wqp=jnp.concatenate([uq_pe.reshape(Q_RANK, -1), rot(uq_pe).reshape(Q_RANK, -1)], axis=1).astype(BF),
        wuk=jnp.transpose(w_uk[i], (1, 2, 0)).astype(BF),
        wuv=jnp.transpose(w_uv[i], (1, 0, 2)).astype(BF),
        wkn=w_uk[i].reshape(KV_RANK, MLA_HEADS * NOPE).astype(BF),
        wv=w_uv[i].reshape(KV_RANK, MLA_HEADS * MLA_V).T.astype(BF),
        wo=w_o[i].astype(BF),
    )


def kernel(x_prompt, x_sample, state_ssd, state_conv, cache_mla_ckv, cache_mla_kpe, cache_mem_k, cache_mem_v, page_table, mem_prompt, mix_norm, w_in, conv_w, conv_b, dt_bias, a_log, d_skip, ssd_gain, gmlp_ln_g, gmlp_ln_b, gmlp_ws, gmlp_bs, w_out_even, w_mla_down, mla_q_gain, mla_kv_gain, w_mla_uq, w_mla_uk, w_mla_uv, w_mla_o, xattn_norm, mem_norm, w_mem_q, w_mem_k, w_mem_v, w_mem_o, ffn_norm, w_ffn_gu, w_ffn_down, w_router, w_exp_gu, w_exp_down, final_norm):
    nbp, seq, _ = x_prompt.shape
    nbs, dseq, _ = x_sample.shape
    depth = mix_norm.shape[0]
    past = page_table.shape[1] * PAGE
    mt = mem_prompt.shape[1]
    hp = x_prompt.reshape(nbp * seq, D)
    hs = x_sample.reshape(nbs * dseq, D)
    cos_p, sin_p = _rope_tables(jnp.arange(seq, dtype=jnp.int32))
    cos_s, sin_s = _rope_tables(past + jnp.arange(dseq, dtype=jnp.int32))
    cos_s = jnp.tile(cos_s, (nbs, 1))
    sin_s = jnp.tile(sin_s, (nbs, 1))
    cache_k4 = cache_mem_k.reshape(depth, nbs, mt * MEM_HEADS, MEM_HD)
    cache_v4 = cache_mem_v.reshape(depth, nbs, mt * MEM_HEADS, MEM_HD)
    cache_kpe_t = jnp.swapaxes(cache_mla_kpe, 2, 3)
    final_g = final_norm[None, :]

    p_ssd, p_conv, p_ckv, p_kpe, p_mk, p_mv = [], [], [], [], [], []
    s_ssd, s_conv, s_v, s_ckv, s_kpe = [], [], [], [], []
    for l in range(depth):
        i = l // 2
        g_mix = mix_norm[l][None, :]
        if l % 2 == 0:
            prm = _prep_even(i, w_in, conv_w, conv_b, dt_bias, a_log, d_skip, ssd_gain, gmlp_ln_g,
                             gmlp_ln_b, gmlp_ws, gmlp_bs, w_out_even)
            buf0 = jnp.zeros((nbp, SSD_CONV - 1, CONV_DIM), F32)
            h00 = jnp.zeros((1, nbp, SSD_HEADS, SSD_HEAD_DIM, SSD_STATE), F32)
            hp, buf_p, ssd_p, _ = _even_layer(hp, g_mix, prm, buf0, h00, 0, nb=nbp, sample=False)
            hs, buf_s, ssd_s, v_s = _even_layer(hs, g_mix, prm, state_conv[i], state_ssd, i, nb=nbs,
                                                sample=True)
            p_ssd.append(ssd_p)
            p_conv.append(buf_p)
            s_ssd.append(ssd_s)
            s_conv.append(buf_s)
            s_v.append(v_s)
        else:
            prm = _prep_mla(i, w_mla_down, mla_q_gain, mla_kv_gain, w_mla_uq, w_mla_uk, w_mla_uv, w_mla_o)
            ckv, kpe, qh, kh, vh = _mla_proj_prompt(hp, g_mix, prm, cos_p, sin_p, nb=nbp)
            o_p = _flash(qh, kh, vh)
            hp = _matmul_res(o_p.reshape(nbp * seq, MLA_HEADS * MLA_V), prm["wo"], hp)
            p_ckv.append(ckv.reshape(nbp, seq, KV_RANK))
            p_kpe.append(kpe.reshape(nbp, seq, ROPE))

            ckv_s, kpe_s, kcat_s, q_s = _mla_proj(hs, g_mix, prm, cos_s, sin_s, nb=1)
            q_s = q_s[0].reshape(MLA_HEADS, nbs, dseq, QK).transpose(1, 0, 2, 3).reshape(nbs, MLA_HEADS * dseq, QK)
            newk = jnp.pad(kcat_s.reshape(nbs, dseq, QK), ((0, 0), (0, NEW_KEY_ROWS - dseq), (0, 0)))
            o_s = _decode(page_table, q_s, newk, cache_mla_ckv, cache_kpe_t, i, dseq)
            o_s = o_s.reshape(nbs, MLA_HEADS, dseq, KV_RANK).transpose(1, 0, 2, 3)
            o_s = o_s.reshape(1, MLA_HEADS, nbs * dseq, KV_RANK).astype(BF)
            hs = _mla_out(o_s, prm["wuv"], prm["wo"], hs)
            s_ckv.append(ckv_s.reshape(nbs, dseq, KV_RANK))
            s_kpe.append(kpe_s.reshape(nbs, dseq, ROPE))

        wkv = jnp.concatenate([w_mem_k[l], w_mem_v[l]], axis=1).astype(BF)
        kv = _rms_matmul(mem_prompt.reshape(nbp * mt, D), mem_norm[l][None, :], wkv, tn=MEM_INNER)
        mk_p = kv[:, :MEM_INNER].reshape(nbp, mt, MEM_INNER)
        mv_p = kv[:, MEM_INNER:].reshape(nbp, mt, MEM_INNER)
        g_x = xattn_norm[l][None, :]
        wq = w_mem_q[l].astype(BF)
        wo = w_mem_o[l].astype(BF)
        hp = _mem_prompt(hp, g_x, wq, mk_p, mv_p, wo)
        x8 = jnp.pad(hs.reshape(nbs, dseq, D), ((0, 0), (0, SUBLANES - dseq), (0, 0)))
        hs = _mem_sample(x8, g_x, wq, cache_k4, cache_v4, wo, l)[:, :dseq].reshape(nbs * dseq, D)
        p_mk.append(mk_p.reshape(nbp, mt, MEM_HEADS, MEM_HD))
        p_mv.append(mv_p.reshape(nbp, mt, MEM_HEADS, MEM_HD))

        g_f = ffn_norm[l][None, :]
        if l % 2 == 0:
            wgu = w_ffn_gu[i].astype(BF)
            wd = w_ffn_down[i].astype(BF)
            hp = _ffn(hp, g_f, wgu, wd)
            hs = _ffn(hs, g_f, wgu, wd)
        else:
            wr = jnp.pad(w_router[i], ((0, 0), (0, LANES - N_EXP)))
            final = l == depth - 1
            hp = _moe(hp, g_f, wr, w_exp_gu, w_exp_down, i, final_g, final=final)
            hs = _moe(hs, g_f, wr, w_exp_gu, w_exp_down, i, final_g, final=final)
    if depth % 2 == 1:
        raise NotImplementedError("the final norm is fused into the last routed-expert layer")
    y_prompt = hp.reshape(nbp, seq, D)
    y_sample = hs.reshape(nbs, dseq, D)
    return (y_prompt, y_sample,
            jnp.stack(p_ssd), jnp.stack(p_conv), jnp.stack(p_ckv), jnp.stack(p_kpe),
            jnp.stack(p_mk), jnp.stack(p_mv),
            jnp.stack(s_ssd), jnp.stack(s_conv), jnp.stack(s_v), jnp.stack(s_ckv), jnp.stack(s_kpe))
```

```python
import functools

import jax
import jax.numpy as jnp
from jax import lax
from jax.experimental import pallas as pl
from jax.experimental.pallas import tpu as pltpu
from jax.experimental.pallas import tpu_sc as plsc

F32 = jnp.float32
BF = jnp.bfloat16
EPS = 1e-6
NEG = -1e30

D = 1024
SSD_HEADS = 16
SSD_HEAD_DIM = 64
SSD_INNER = SSD_HEADS * SSD_HEAD_DIM
SSD_GROUPS = 2
SSD_STATE = 128
SSD_CONV = 4
CONV_DIM = SSD_INNER + 2 * SSD_GROUPS * SSD_STATE
GMLP_GROUPS = 8
GMLP_WIDTH = 1024
CHUNK = 128
EVEN_SEQS_PER_STEP = 2
SAMPLE_Q = 16
ZXU = SSD_INNER + CONV_DIM + 2 * GMLP_WIDTH
MLA_HEADS = 8
NOPE = 128
ROPE = 64
MLA_V = 128
Q_RANK = 256
KV_RANK = 256
MLA_SCALE = (NOPE + ROPE) ** -0.5
QK = KV_RANK + ROPE
ROPE_BASE = 10000.0
MEM_HEADS = 4
MEM_HD = 128
MEM_INNER = MEM_HEADS * MEM_HD
D_FF = 2816
N_EXP = 8
D_FFE = 3584
PAGE = 128
LANES = 128
SUBLANES = 8
ROW_TILE = 512
NEW_KEY_ROWS = 16
SAMPLES_PER_STEP = 8


def _cp(sem, vmem_mb=None):
    kw = dict(dimension_semantics=sem)
    if vmem_mb is not None:
        kw["vmem_limit_bytes"] = vmem_mb * 1024 * 1024
    return pltpu.CompilerParams(**kw)


def _rms(x, g):
    return x * lax.rsqrt(jnp.mean(x * x, axis=-1, keepdims=True) + EPS) * g


def _dot(a, b):
    return jnp.dot(a, b, preferred_element_type=F32)


def _dot_nt(a, b):
    return lax.dot_general(a, b, (((1,), (1,)), ((), ())), preferred_element_type=F32)


def _dot_f32(a, b):
    return jnp.dot(a, b, preferred_element_type=F32, precision=lax.Precision.HIGHEST)


def _silu(x):
    return x * jax.nn.sigmoid(x)


def _full(shape):
    n = len(shape)
    return pl.BlockSpec(shape, lambda *_: (0,) * n)


def _largest_tile(n, cap):
    best = LANES
    for t in range(LANES, cap + 1, LANES):
        if n % t == 0:
            best = t
    return best


def _rms_matmul_kernel(x_ref, g_ref, w_ref, o_ref, xn_ref, *, precise):
    @pl.when(pl.program_id(1) == 0)
    def _():
        xn_ref[...] = _rms(x_ref[...], g_ref[...]).astype(xn_ref.dtype)

    if precise:
        o_ref[...] = _dot_f32(xn_ref[...], w_ref[...]).astype(o_ref.dtype)
    else:
        o_ref[...] = _dot(xn_ref[...], w_ref[...]).astype(o_ref.dtype)


def _rms_matmul(x, g, w, *, tn, out_dtype=F32, precise=False):
    t, k = x.shape
    n = w.shape[1]
    tm = min(ROW_TILE, t)
    return pl.pallas_call(
        functools.partial(_rms_matmul_kernel, precise=precise),
        grid=(t // tm, n // tn),
        in_specs=[
            pl.BlockSpec((tm, k), lambda i, j: (i, 0)),
            pl.BlockSpec((1, k), lambda i, j: (0, 0)),
            pl.BlockSpec((k, tn), lambda i, j: (0, j)),
        ],
        out_specs=pl.BlockSpec((tm, tn), lambda i, j: (i, j)),
        out_shape=jax.ShapeDtypeStruct((t, n), out_dtype),
        scratch_shapes=[pltpu.VMEM((tm, k), F32 if precise else BF)],
        compiler_params=_cp(("parallel", "arbitrary")),
        name="rms_matmul",
    )(x, g, w)


def _matmul_res_kernel(a_ref, w_ref, r_ref, o_ref):
    o_ref[...] = r_ref[...] + _dot(a_ref[...].astype(BF), w_ref[...])


def _matmul_res(a, w, res):
    t, k = a.shape
    n = w.shape[1]
    tn = _largest_tile(n, 1024)
    tm = min(ROW_TILE, t)
    return pl.pallas_call(
        _matmul_res_kernel,
        grid=(t // tm, n // tn),
        in_specs=[
            pl.BlockSpec((tm, k), lambda i, j: (i, 0)),
            pl.BlockSpec((k, tn), lambda i, j: (0, j)),
            pl.BlockSpec((tm, tn), lambda i, j: (i, j)),
        ],
        out_specs=pl.BlockSpec((tm, tn), lambda i, j: (i, j)),
        out_shape=jax.ShapeDtypeStruct((t, n), F32),
        compiler_params=_cp(("parallel", "arbitrary")),
        name="matmul_res",
    )(a, w, res)


FF_CHUNK = 256


def _ffn_kernel(x_ref, g_ref, wgu_ref, wd_ref, o_ref, hid_ref, *, ff):
    x = x_ref[...]
    xn = _rms(x, g_ref[...]).astype(BF)
    for c in range(ff // FF_CHUNK):
        lo = c * FF_CHUNK
        gate = _dot(xn, wgu_ref[:, lo:lo + FF_CHUNK])
        up = _dot(xn, wgu_ref[:, ff + lo:ff + lo + FF_CHUNK])
        hid_ref[:, lo:lo + FF_CHUNK] = (_silu(gate) * up).astype(BF)
    o_ref[...] = x + _dot(hid_ref[...], wd_ref[...])


def _ffn(x, g, wgu, wd):
    t = x.shape[0]
    ff = wd.shape[0]
    tm = min(ROW_TILE, t)
    return pl.pallas_call(
        functools.partial(_ffn_kernel, ff=ff),
        grid=(t // tm,),
        in_specs=[
            pl.BlockSpec((tm, D), lambda i: (i, 0)),
            _full((1, D)),
            pl.BlockSpec((D, 2 * ff), lambda i: (0, 0), pipeline_mode=pl.Buffered(1)),
            pl.BlockSpec((ff, D), lambda i: (0, 0), pipeline_mode=pl.Buffered(1)),
        ],
        out_specs=pl.BlockSpec((tm, D), lambda i: (i, 0)),
        out_shape=jax.ShapeDtypeStruct((t, D), F32),
        scratch_shapes=[pltpu.VMEM((tm, ff), BF)],
        compiler_params=_cp(("parallel",), vmem_mb=48),
        name="ffn",
    )(x, g, wgu, wd)


def _softplus(x):
    return jnp.maximum(x, 0.0) + jnp.log1p(jnp.exp(-jnp.abs(x)))


def _gelu_tanh(x):
    return 0.5 * x * (1.0 + jnp.tanh(0.7978845608028654 * (x + 0.044715 * (x * x * x))))


def _even_kernel(proj_ref, dt_ref, cinit_ref, sinit_ref, cw_ref, cb_ref, dtb_ref, alog_ref,
                 dsk_ref, sg_ref, lng_ref, lnb_ref, ws_ref, bst_ref, e_ref,
                 ymix_ref, cout_ref, sout_ref, v_ref, ext_ref, ht_ref, *, q, lb, lc, nbb, out_slot):
    c = pl.program_id(1)

    @pl.when(c == 0)
    def _():
        for bb in range(nbb):
            ext_ref[bb, 0:SUBLANES, :] = cinit_ref[bb]
            ht_ref[bb] = sinit_ref[0, bb].T

    for bb in range(nbb):
        _even_block(proj_ref, dt_ref, cw_ref, cb_ref, dtb_ref, alog_ref,
                    dsk_ref, sg_ref, lng_ref, lnb_ref, ws_ref, bst_ref, e_ref,
                    ymix_ref, v_ref, ext_ref, ht_ref, bb, q=q, lb=lb, lc=lc)

    @pl.when(c == pl.num_programs(1) - 1)
    def _():
        for bb in range(nbb):
            cout_ref[bb] = ext_ref[bb, SUBLANES + lc - 3:SUBLANES + lc, :]
            for slot in range(sout_ref.shape[0]):
                if slot == out_slot:
                    sout_ref[slot, bb] = ht_ref[bb].T
                else:
                    sout_ref[slot, bb] = jnp.zeros((SSD_INNER, SSD_STATE), F32)


def _even_block(proj_ref, dt_ref, cw_ref, cb_ref, dtb_ref, alog_ref,
                dsk_ref, sg_ref, lng_ref, lnb_ref, ws_ref, bst_ref, e_ref,
                ymix_ref, v_ref, ext_ref, ht_ref, bb, *, q, lb, lc):

    if lb == q:
        p = proj_ref[bb]
        dtr = dt_ref[bb]
    else:
        p = jnp.concatenate([proj_ref[bb], jnp.zeros((q - lb, ZXU), F32)], axis=0)
        dtr = jnp.concatenate([dt_ref[bb], jnp.zeros((q - lb, LANES), F32)], axis=0)
    z = p[:, :SSD_INNER]
    xbc_raw = p[:, SSD_INNER:SSD_INNER + CONV_DIM]
    uv = p[:, SSD_INNER + CONV_DIM:]

    ext_ref[bb, SUBLANES:SUBLANES + q, :] = xbc_raw
    conv = (cb_ref[...] + cw_ref[0:1, :] * ext_ref[bb, 5:5 + q, :] + cw_ref[1:2, :] * ext_ref[bb, 6:6 + q, :]
            + cw_ref[2:3, :] * ext_ref[bb, 7:7 + q, :] + cw_ref[3:4, :] * xbc_raw)
    ext_ref[bb, 0:SUBLANES, :] = ext_ref[bb, q:q + SUBLANES, :]

    xbc = _silu(conv)
    xs = xbc[:, :SSD_INNER]
    gw = SSD_STATE
    bm = [xbc[:, SSD_INNER + g * gw:SSD_INNER + (g + 1) * gw] for g in range(SSD_GROUPS)]
    cm = [xbc[:, SSD_INNER + (SSD_GROUPS + g) * gw:SSD_INNER + (SSD_GROUPS + g + 1) * gw]
          for g in range(SSD_GROUPS)]

    row = lax.broadcasted_iota(jnp.int32, (q, q), 0)
    col = lax.broadcasted_iota(jnp.int32, (q, q), 1)
    causal = row >= col

    dt = _softplus(dtr + dtb_ref[...])
    if lc < q:
        dt = jnp.where(lax.broadcasted_iota(jnp.int32, (q, LANES), 0) < lc, dt, 0.0)
    a = dt * (-jnp.exp(alog_ref[...]))
    a_cum = _dot_f32(causal.astype(F32), a)
    a_cum_t = a_cum.T
    a_last = a_cum[q - 1:q, :]
    decay_end = jnp.exp(a_last - a_cum)
    ea = jnp.exp(a_cum)
    chunk_decay = jnp.exp(a_last)

    cmb = [m.astype(BF) for m in cm]
    cb = [_dot_nt(cmb[g], bm[g].astype(BF)) for g in range(SSD_GROUPS)]
    bt = [bm[g].T.astype(BF) for g in range(SSD_GROUPS)]
    heads_per_group = SSD_HEADS // SSD_GROUPS
    gi = SSD_INNER // SSD_GROUPS

    def per_head_lanes(v):
        hi = v.astype(BF)
        lo = (v - hi.astype(F32)).astype(BF)
        return _dot(hi, e_ref[...]) + _dot(lo, e_ref[...])

    dt_x = per_head_lanes(dt)
    ea_x = per_head_lanes(ea)
    de_x = per_head_lanes(decay_end)
    cd_x = per_head_lanes(jnp.broadcast_to(chunk_decay, (SUBLANES, LANES)))[0:1]
    xdt = xs * dt_x
    xdt_b = xdt.astype(BF)
    xd_b = (xdt * de_x).astype(BF)
    y_off = []
    for g in range(SSD_GROUPS):
        h_old = ht_ref[bb, :, g * gi:(g + 1) * gi]
        y_off.append(_dot(cmb[g], h_old.astype(BF)))
        ht_ref[bb, :, g * gi:(g + 1) * gi] = (h_old * cd_x[:, g * gi:(g + 1) * gi]
                                              + _dot(bt[g], xd_b[:, g * gi:(g + 1) * gi]))

    def decay_weights(r):
        seg = a_cum[:, r:r + 1] - a_cum_t[r:r + 1, :]
        lmat = jnp.where(causal, jnp.exp(jnp.minimum(seg, 0.0)), 0.0)
        return (cb[r // heads_per_group] * lmat).astype(BF)

    first_half = lax.broadcasted_iota(jnp.int32, (q, LANES), 1) < SSD_HEAD_DIM
    y_diag = []
    for k in range(SSD_HEADS // 2):
        xp = xdt_b[:, k * LANES:(k + 1) * LANES]
        y_diag.append(jnp.where(first_half, _dot(decay_weights(2 * k), xp), _dot(decay_weights(2 * k + 1), xp)))
    y = jnp.concatenate(y_diag, axis=1) + jnp.concatenate(y_off, axis=1) * ea_x + dsk_ref[...] * xs
    y = y * _silu(z)
    gi = SSD_INNER // SSD_GROUPS
    yn = [_rms(y[:, g * gi:(g + 1) * gi], sg_ref[:, g * gi:(g + 1) * gi]) for g in range(SSD_GROUPS)]
    ymix_ref[bb, :, 0:SSD_INNER] = jnp.concatenate(yn, axis=1)[:lb].astype(BF)

    uvg = _gelu_tanh(uv)
    u = uvg[:, :GMLP_WIDTH]
    v = uvg[:, GMLP_WIDTH:]
    mu = jnp.mean(v, axis=-1, keepdims=True)
    vc = v - mu
    vn = vc * lax.rsqrt(jnp.mean(vc * vc, axis=-1, keepdims=True) + EPS) * lng_ref[...] + lnb_ref[...]
    if v_ref is not None:
        v_ref[bb] = vn[:lb]
    gd = GMLP_WIDTH // GMLP_GROUPS
    yb = []
    for g in range(GMLP_GROUPS):
        wt = jnp.where(causal, ws_ref[g, :q, :q], 0.0).astype(BF)
        sp = _dot(wt, vn[:, g * gd:(g + 1) * gd].astype(BF)) + bst_ref[:q, g:g + 1]
        yb.append(u[:, g * gd:(g + 1) * gd] * sp)
    ymix_ref[bb, :, SSD_INNER:SSD_INNER + GMLP_WIDTH] = jnp.concatenate(yb, axis=1)[:lb].astype(BF)


EVEN_INPUTS = 15


def _even_entry(*refs, has_v, has_prev, **kw):
    ins = refs[:EVEN_INPUTS]
    k = EVEN_INPUTS + (1 if has_prev else 0)
    outs = refs[k:k + 3]
    k += 3
    v_ref = refs[k] if has_v else None
    k += 1 if has_v else 0
    _even_kernel(*ins, *outs, v_ref, *refs[k:], **kw)


def _even_mixer(proj, dtp, cinit8, sinit, prm, *, q, lb, lc, want_v, layer, state_layers=1, state_prev=None):
    b, lp, _ = proj.shape
    nchunks = lp // lb
    nbb = EVEN_SEQS_PER_STEP if b % EVEN_SEQS_PER_STEP == 0 else 1
    par = [prm["conv_w"], prm["conv_b"], prm["dt_bias"], prm["a_log"], prm["d_skip"], prm["ssd_gain"],
           prm["ln_g"], prm["ln_b"], prm["ws"], prm["bst"], prm["expand"]]
    in_specs = [
        pl.BlockSpec((nbb, lb, ZXU), lambda i, c: (i, c, 0)),
        pl.BlockSpec((nbb, lb, LANES), lambda i, c: (i, c, 0)),
        pl.BlockSpec((nbb, SUBLANES, CONV_DIM), lambda i, c: (i, 0, 0)),
        pl.BlockSpec((1, nbb, SSD_INNER, SSD_STATE), lambda i, c: (layer, i, 0, 0)),
    ] + [_full(w.shape) for w in par]
    out_specs = [
        pl.BlockSpec((nbb, lb, SSD_INNER + GMLP_WIDTH), lambda i, c: (i, c, 0)),
        pl.BlockSpec((nbb, SSD_CONV - 1, CONV_DIM), lambda i, c: (i, 0, 0)),
        (pl.BlockSpec((1, nbb, SSD_INNER, SSD_STATE), lambda i, c: (layer, i, 0, 0))
         if state_prev is not None else
         pl.BlockSpec((state_layers, nbb, SSD_INNER, SSD_STATE), lambda i, c: (0, i, 0, 0))),
    ]
    out_slot = 0 if state_prev is not None else (layer if state_layers > 1 else 0)
    out_shape = [
        jax.ShapeDtypeStruct((b, lp, SSD_INNER + GMLP_WIDTH), BF),
        jax.ShapeDtypeStruct((b, SSD_CONV - 1, CONV_DIM), F32),
        jax.ShapeDtypeStruct((state_layers, b, SSD_INNER, SSD_STATE), F32),
    ]
    if want_v:
        out_specs.append(pl.BlockSpec((nbb, lb, GMLP_WIDTH), lambda i, c: (i, c, 0)))
        out_shape.append(jax.ShapeDtypeStruct((b, lp, GMLP_WIDTH), F32))
    args = [proj, dtp, cinit8, sinit, *par]
    aliases = {}
    if state_prev is not None:
        in_specs.append(pl.BlockSpec(memory_space=pl.ANY))
        aliases = {len(args): 2}
        args.append(state_prev)
    return pl.pallas_call(
        functools.partial(_even_entry, has_v=want_v, has_prev=state_prev is not None,
                          q=q, lb=lb, lc=lc, nbb=nbb, out_slot=out_slot),
        grid=(b // nbb, nchunks),
        in_specs=in_specs,
        out_specs=out_specs,
        out_shape=out_shape,
        input_output_aliases=aliases,
        scratch_shapes=[pltpu.VMEM((nbb, q + 2 * SUBLANES, CONV_DIM), F32),
                        pltpu.VMEM((nbb, SSD_STATE, SSD_INNER), F32)],
        compiler_params=_cp(("parallel", "arbitrary"), vmem_mb=48),
        name="even_mixer",
    )(*args)


def _mla_proj_kernel(x_ref, g_ref, wdq_ref, wdkv_ref, wdk2_ref, qg_ref, kvg_ref, wqn_ref, wqp_ref,
                     wuk_ref, cos_ref, sin_ref, ckv_ref, kpe_ref, kcat_ref, q_ref):
    xn = _rms(x_ref[...], g_ref[...]).astype(BF)
    cqn = _rms(_dot(xn, wdq_ref[...]), qg_ref[...]).astype(BF)
    ckv = _rms(_dot(xn, wdkv_ref[...]), kvg_ref[...])
    kk = _dot(xn, wdk2_ref[...])
    cos = cos_ref[...]
    sin = sin_ref[...]
    kpe = kk[:, :ROPE] * cos[:, :ROPE] + kk[:, ROPE:] * sin[:, :ROPE]
    ckv_ref[...] = ckv
    kpe_ref[...] = kpe
    kcat_ref[:, :KV_RANK] = ckv.astype(BF)
    kcat_ref[:, KV_RANK:] = kpe.astype(BF)
    qn = _dot(cqn, wqn_ref[...])
    qp = _dot(cqn, wqp_ref[...])
    hr = MLA_HEADS * ROPE
    qpe = qp[:, :hr] * cos + qp[:, hr:] * sin
    for h in range(MLA_HEADS):
        ql = _dot(qn[:, h * NOPE:(h + 1) * NOPE].astype(BF), wuk_ref[h])
        q_ref[0, h, :, :KV_RANK] = (ql * MLA_SCALE).astype(BF)
        q_ref[0, h, :, KV_RANK:] = (qpe[:, h * ROPE:(h + 1) * ROPE] * MLA_SCALE).astype(BF)


def _mla_proj(x, g, prm, cos8, sin8, *, nb):
    t = x.shape[0]
    seq = t // nb
    tm = min(ROW_TILE, seq)
    tpb = seq // tm
    w = [prm["wdq"], prm["wdkv"], prm["wdk2"], prm["q_gain"], prm["kv_gain"], prm["wqn"], prm["wqp"],
         prm["wuk"]]
    return pl.pallas_call(
        _mla_proj_kernel,
        grid=(t // tm,),
        in_specs=[pl.BlockSpec((tm, D), lambda i: (i, 0)), _full((1, D))] + [_full(a.shape) for a in w] + [
            pl.BlockSpec((tm, MLA_HEADS * ROPE), lambda i: (i % tpb, 0)),
            pl.BlockSpec((tm, MLA_HEADS * ROPE), lambda i: (i % tpb, 0)),
        ],
        out_specs=[
            pl.BlockSpec((tm, KV_RANK), lambda i: (i, 0)),
            pl.BlockSpec((tm, ROPE), lambda i: (i, 0)),
            pl.BlockSpec((tm, QK), lambda i: (i, 0)),
            pl.BlockSpec((1, MLA_HEADS, tm, QK), lambda i: (i // tpb, 0, i % tpb, 0)),
        ],
        out_shape=[
            jax.ShapeDtypeStruct((t, KV_RANK), F32),
            jax.ShapeDtypeStruct((t, ROPE), F32),
            jax.ShapeDtypeStruct((t, QK), BF),
            jax.ShapeDtypeStruct((nb, MLA_HEADS, seq, QK), BF),
        ],
        compiler_params=_cp(("parallel",)),
        name="mla_proj",
    )(x, g, *w, cos8, sin8)


HEAD_QK = NOPE + ROPE
LOG2E = 1.4426950408889634


def _mla_proj_prompt_kernel(x_ref, g_ref, wdq_ref, wdkv_ref, wdk2_ref, qg_ref, kvg_ref, wqn_ref, wqp_ref,
                            wkn_ref, wv_ref, cos_ref, sin_ref, ckv_ref, kpe_ref, q_ref, k_ref, v_ref):
    xn = _rms(x_ref[...], g_ref[...]).astype(BF)
    cqn = _rms(_dot(xn, wdq_ref[...]), qg_ref[...]).astype(BF)
    ckv = _rms(_dot(xn, wdkv_ref[...]), kvg_ref[...])
    kk = _dot(xn, wdk2_ref[...])
    cos = cos_ref[...]
    sin = sin_ref[...]
    kpe = kk[:, :ROPE] * cos[:, :ROPE] + kk[:, ROPE:] * sin[:, :ROPE]
    ckv_ref[...] = ckv
    kpe_ref[...] = kpe
    ckv_b = ckv.astype(BF)
    kpe_b = kpe.astype(BF)
    kn = _dot(ckv_b, wkn_ref[...])
    vt = _dot_nt(wv_ref[...], ckv_b)
    qn = _dot(cqn, wqn_ref[...])
    qp = _dot(cqn, wqp_ref[...])
    hr = MLA_HEADS * ROPE
    qpe = qp[:, :hr] * cos + qp[:, hr:] * sin
    qscale = MLA_SCALE * LOG2E
    for h in range(MLA_HEADS):
        q_ref[0, h, :, :NOPE] = (qn[:, h * NOPE:(h + 1) * NOPE] * qscale).astype(BF)
        q_ref[0, h, :, NOPE:] = (qpe[:, h * ROPE:(h + 1) * ROPE] * qscale).astype(BF)
        k_ref[0, h, :, :NOPE] = kn[:, h * NOPE:(h + 1) * NOPE].astype(BF)
        k_ref[0, h, :, NOPE:] = kpe_b
        v_ref[0, h, 0] = vt[h * MLA_V:(h + 1) * MLA_V].astype(BF)


def _mla_proj_prompt(x, g, prm, cos8, sin8, *, nb):
    t = x.shape[0]
    seq = t // nb
    tm = min(ROW_TILE, seq)
    tpb = seq // tm
    w = [prm["wdq"], prm["wdkv"], prm["wdk2"], prm["q_gain"], prm["kv_gain"], prm["wqn"], prm["wqp"],
         prm["wkn"], prm["wv"]]
    head_spec = lambda width: pl.BlockSpec((1, MLA_HEADS, tm, width), lambda i: (i // tpb, 0, i % tpb, 0))
    head_shape = lambda width: jax.ShapeDtypeStruct((nb, MLA_HEADS, seq, width), BF)
    return pl.pallas_call(
        _mla_proj_prompt_kernel,
        grid=(t // tm,),
        in_specs=[pl.BlockSpec((tm, D), lambda i: (i, 0)), _full((1, D))] + [_full(a.shape) for a in w] + [
            pl.BlockSpec((tm, MLA_HEADS * ROPE), lambda i: (i % tpb, 0)),
            pl.BlockSpec((tm, MLA_HEADS * ROPE), lambda i: (i % tpb, 0)),
        ],
        out_specs=[
            pl.BlockSpec((tm, KV_RANK), lambda i: (i, 0)),
            pl.BlockSpec((tm, ROPE), lambda i: (i, 0)),
            head_spec(HEAD_QK), head_spec(HEAD_QK),
            pl.BlockSpec((1, MLA_HEADS, 1, MLA_V, tm), lambda i: (i // tpb, 0, i % tpb, 0, 0)),
        ],
        out_shape=[
            jax.ShapeDtypeStruct((t, KV_RANK), F32),
            jax.ShapeDtypeStruct((t, ROPE), F32),
            head_shape(HEAD_QK), head_shape(HEAD_QK),
            jax.ShapeDtypeStruct((nb, MLA_HEADS, tpb, MLA_V, tm), BF),
        ],
        compiler_params=_cp(("parallel",)),
        name="mla_proj_prompt",
    )(x, g, *w, cos8, sin8)


FLASH_HEADS = 4
FLASH_QSPLIT = 1
DENOM_ROWS = 16


def _flash_kernel(q_ref, k_ref, vt_ref, o_ref, m_ref, acc_ref, *, t):
    qi = pl.program_id(2)
    for hh in range(FLASH_HEADS):
        m_ref[hh] = jnp.full((1, t), NEG, F32)
        acc_ref[hh] = jnp.zeros((MLA_V + DENOM_ROWS, t), F32)
    ones = jnp.ones((DENOM_ROWS, t), BF)

    tq = t // FLASH_QSPLIT

    def block(ki, masked):
        start = pl.multiple_of(ki * t, t)
        for hh in range(FLASH_HEADS):
            kb = k_ref[0, hh, pl.ds(start, t), :]
            v1 = jnp.concatenate([vt_ref[0, hh, ki], ones], axis=0)
            for qs in range(FLASH_QSPLIT):
                lanes = slice(qs * tq, (qs + 1) * tq)
                st = _dot_nt(kb, q_ref[0, hh, lanes, :])
                if masked:
                    key = lax.broadcasted_iota(jnp.int32, (t, tq), 0)
                    qry = lax.broadcasted_iota(jnp.int32, (t, tq), 1) + qs * tq
                    st = jnp.where(key <= qry, st, NEG)
                m_old = m_ref[hh, :, lanes]
                m_new = jnp.maximum(m_old, jnp.max(st, axis=0, keepdims=True))
                alpha = jnp.exp2(m_old - m_new)
                pt = jnp.exp2(st - m_new).astype(BF)
                acc_ref[hh, :, lanes] = alpha * acc_ref[hh, :, lanes] + _dot(v1, pt)
                m_ref[hh, :, lanes] = m_new

    def body(ki, carry):
        block(ki, False)
        return carry

    lax.fori_loop(0, qi, body, 0)
    block(qi, True)
    for hh in range(FLASH_HEADS):
        acc = acc_ref[hh]
        o_t = acc[:MLA_V] / acc[MLA_V:MLA_V + 1]
        o_ref[0, :, hh * MLA_V:(hh + 1) * MLA_V] = o_t.T.astype(BF)


def _flash(q, k, vt):
    nb, _, seq, _ = q.shape
    t = vt.shape[-1]
    nh = FLASH_HEADS
    return pl.pallas_call(
        functools.partial(_flash_kernel, t=t),
        grid=(nb, MLA_HEADS // nh, seq // t),
        in_specs=[
            pl.BlockSpec((1, nh, t, HEAD_QK), lambda b, h, i: (b, h, i, 0)),
            pl.BlockSpec((1, nh, seq, HEAD_QK), lambda b, h, i: (b, h, 0, 0), pipeline_mode=pl.Buffered(1)),
            pl.BlockSpec((1, nh, seq // t, MLA_V, t), lambda b, h, i: (b, h, 0, 0, 0),
                         pipeline_mode=pl.Buffered(1)),
        ],
        out_specs=pl.BlockSpec((1, t, nh * MLA_V), lambda b, h, i: (b, i, h)),
        out_shape=jax.ShapeDtypeStruct((nb, seq, MLA_HEADS * MLA_V), BF),
        scratch_shapes=[pltpu.VMEM((nh, 1, t), F32), pltpu.VMEM((nh, MLA_V + DENOM_ROWS, t), F32)],
        compiler_params=_cp(("parallel", "parallel", "arbitrary"), vmem_mb=48),
        name="mla_flash",
    )(q, k, vt)


def _page_copies(pt_ref, ckv_hbm, kpt_hbm, ckbuf, kpbuf, sem, sample, slot, *, layer, n_pages):
    copies = []
    for p in range(n_pages):
        pg = pt_ref[sample, p]
        copies.append(pltpu.make_async_copy(
            ckv_hbm.at[layer, pg], ckbuf.at[slot, pl.ds(p * PAGE, PAGE), :], sem.at[0, slot]))
        copies.append(pltpu.make_async_copy(
            kpt_hbm.at[layer, pg], kpbuf.at[slot, :, pl.ds(p * PAGE, PAGE)], sem.at[1, slot]))
    return copies


def _decode_kernel(pt_ref, q_ref, nk_ref, ckv_hbm, kpt_hbm, o_ref, ckbuf, kpbuf, sem, *,
                   dec_seq, layer, n_pages):
    b = pl.program_id(0)
    last = pl.num_programs(0) - 1
    slot = b % 2
    copies = functools.partial(_page_copies, pt_ref, ckv_hbm, kpt_hbm, ckbuf, kpbuf, sem,
                               layer=layer, n_pages=n_pages)

    @pl.when(b == 0)
    def _():
        for c in copies(0, 0):
            c.start()

    for c in copies(b, slot):
        c.wait()
    nxt = jnp.minimum(b + 1, last)
    for c in copies(nxt, 1 - slot):
        c.start()

    rows = MLA_HEADS * dec_seq
    qm = q_ref[0]
    ql = qm[:, :KV_RANK]
    qp = qm[:, KV_RANK:]
    ck = ckbuf[slot].astype(BF)
    kp = kpbuf[slot].astype(BF)
    s = _dot_nt(ql, ck) + _dot(qp, kp)
    nk = nk_ref[0]
    kt = lax.broadcasted_iota(jnp.int32, (rows, NEW_KEY_ROWS), 1)
    qt = lax.broadcasted_iota(jnp.int32, (rows, NEW_KEY_ROWS), 0) % dec_seq
    s_new = jnp.where(kt <= qt, _dot_nt(qm, nk), NEG)
    m = jnp.maximum(jnp.max(s, axis=-1, keepdims=True), jnp.max(s_new, axis=-1, keepdims=True))
    p = jnp.exp(s - m)
    p_new = jnp.exp(s_new - m)
    denom = jnp.sum(p, axis=-1, keepdims=True) + jnp.sum(p_new, axis=-1, keepdims=True)
    o_ref[0] = (_dot(p.astype(BF), ck) + _dot(p_new.astype(BF), nk[:, :KV_RANK])) / denom

    @pl.when(b == last)
    def _():
        for c in copies(nxt, 1 - slot):
            c.wait()


def _decode(page_table, q, newk, cache_ckv, cache_kpe_t, layer, dec_seq):
    nb, n_pages = page_table.shape
    rows = MLA_HEADS * dec_seq
    keys = n_pages * PAGE
    grid_spec = pltpu.PrefetchScalarGridSpec(
        num_scalar_prefetch=1,
        grid=(nb,),
        in_specs=[
            pl.BlockSpec((1, rows, QK), lambda b, pt: (b, 0, 0)),
            pl.BlockSpec((1, NEW_KEY_ROWS, QK), lambda b, pt: (b, 0, 0)),
            pl.BlockSpec(memory_space=pl.ANY),
            pl.BlockSpec(memory_space=pl.ANY),
        ],
        out_specs=pl.BlockSpec((1, rows, KV_RANK), lambda b, pt: (b, 0, 0)),
        scratch_shapes=[pltpu.VMEM((2, keys, KV_RANK), F32), pltpu.VMEM((2, ROPE, keys), F32),
                        pltpu.SemaphoreType.DMA((2, 2))],
    )
    return pl.pallas_call(
        functools.partial(_decode_kernel, dec_seq=dec_seq, layer=layer, n_pages=n_pages),
        grid_spec=grid_spec,
        out_shape=jax.ShapeDtypeStruct((nb, rows, KV_RANK), F32),
        compiler_params=_cp(("arbitrary",), vmem_mb=48),
        name="mla_decode",
    )(page_table, q, newk, cache_ckv, cache_kpe_t)


def _mla_out_kernel(o_ref, wuv_ref, wo_ref, r_ref, out_ref):
    parts = [_dot(o_ref[0, h], wuv_ref[h]).astype(BF) for h in range(MLA_HEADS)]
    out_ref[...] = r_ref[...] + _dot(jnp.concatenate(parts, axis=1), wo_ref[...])


def _mla_out(o_lat, wuv, wo, res):
    nb, _, seq, _ = o_lat.shape
    tm = min(ROW_TILE, seq)
    tpb = seq // tm
    t = nb * seq
    return pl.pallas_call(
        _mla_out_kernel,
        grid=(t // tm,),
        in_specs=[
            pl.BlockSpec((1, MLA_HEADS, tm, KV_RANK), lambda i: (i // tpb, 0, i % tpb, 0)),
            _full(wuv.shape), _full(wo.shape),
            pl.BlockSpec((tm, D), lambda i: (i, 0)),
        ],
        out_specs=pl.BlockSpec((tm, D), lambda i: (i, 0)),
        out_shape=jax.ShapeDtypeStruct((t, D), F32),
        compiler_params=_cp(("parallel",)),
        name="mla_out",
    )(o_lat, wuv, wo, res)


def _softmax_rows(s):
    m = jnp.max(s, axis=-1, keepdims=True)
    p = jnp.exp(s - m)
    return p / jnp.sum(p, axis=-1, keepdims=True)


def _mem_prompt_kernel(x_ref, g_ref, wq_ref, k_ref, v_ref, wo_ref, o_ref):
    x = x_ref[...]
    xn = _rms(x, g_ref[...]).astype(BF)
    qm = (_dot(xn, wq_ref[...]) * MEM_HD ** -0.5).astype(BF)
    km = k_ref[0].astype(BF)
    vm = v_ref[0].astype(BF)
    parts = []
    for h in range(MEM_HEADS):
        sl = slice(h * MEM_HD, (h + 1) * MEM_HD)
        p = _softmax_rows(_dot_nt(qm[:, sl], km[:, sl]))
        parts.append(_dot(p.astype(BF), vm[:, sl]).astype(BF))
    o_ref[...] = x + _dot(jnp.concatenate(parts, axis=1), wo_ref[...])


def _mem_prompt(x, g, wq, km, vm, wo):
    t = x.shape[0]
    nb, mt, _ = km.shape
    seq = t // nb
    tm = min(ROW_TILE, seq)
    tpb = seq // tm
    return pl.pallas_call(
        _mem_prompt_kernel,
        grid=(t // tm,),
        in_specs=[
            pl.BlockSpec((tm, D), lambda i: (i, 0)), _full((1, D)), _full(wq.shape),
            pl.BlockSpec((1, mt, MEM_INNER), lambda i: (i // tpb, 0, 0)),
            pl.BlockSpec((1, mt, MEM_INNER), lambda i: (i // tpb, 0, 0)),
            _full(wo.shape),
        ],
        out_specs=pl.BlockSpec((tm, D), lambda i: (i, 0)),
        out_shape=jax.ShapeDtypeStruct((t, D), F32),
        compiler_params=_cp(("parallel",)),
        name="mem_attn_prompt",
    )(x, g, wq, km, vm, wo)


MEM_ROWS = MEM_HEADS * SUBLANES


def _mem_sample_kernel(x_ref, g_ref, wq_ref, k_ref, v_ref, wo_ref, o_ref):
    ns = SAMPLES_PER_STEP
    x = x_ref[...].reshape(ns * SUBLANES, D)
    xn = _rms(x, g_ref[...]).astype(BF)
    qall = _dot(xn, wq_ref[...]) * MEM_HD ** -0.5
    cols = k_ref.shape[2]
    head_of_row = lax.broadcasted_iota(jnp.int32, (MEM_ROWS, cols), 0) // SUBLANES
    head_of_col = lax.broadcasted_iota(jnp.int32, (MEM_ROWS, cols), 1) % MEM_HEADS
    own = head_of_row == head_of_col
    outs = []
    for s in range(ns):
        qs = qall[s * SUBLANES:(s + 1) * SUBLANES]
        qst = jnp.concatenate([qs[:, h * MEM_HD:(h + 1) * MEM_HD] for h in range(MEM_HEADS)], axis=0)
        sc = jnp.where(own, _dot_nt(qst.astype(BF), k_ref[0, s].astype(BF)), NEG)
        o = _dot(_softmax_rows(sc).astype(BF), v_ref[0, s].astype(BF))
        outs.append(jnp.concatenate([o[h * SUBLANES:(h + 1) * SUBLANES] for h in range(MEM_HEADS)], axis=1))
    out = x + _dot(jnp.concatenate(outs, axis=0).astype(BF), wo_ref[...])
    o_ref[...] = out.reshape(ns, SUBLANES, D)


def _mem_sample(x8, g, wq, cache_k, cache_v, wo, layer):
    nb = x8.shape[0]
    rows = cache_k.shape[2]
    ns = SAMPLES_PER_STEP
    return pl.pallas_call(
        _mem_sample_kernel,
        grid=(nb // ns,),
        in_specs=[
            pl.BlockSpec((ns, SUBLANES, D), lambda i: (i, 0, 0)), _full((1, D)), _full(wq.shape),
            pl.BlockSpec((1, ns, rows, MEM_HD), lambda i: (layer, i, 0, 0)),
            pl.BlockSpec((1, ns, rows, MEM_HD), lambda i: (layer, i, 0, 0)),
            _full(wo.shape),
        ],
        out_specs=pl.BlockSpec((ns, SUBLANES, D), lambda i: (i, 0, 0)),
        out_shape=jax.ShapeDtypeStruct((nb, SUBLANES, D), F32),
        compiler_params=_cp(("parallel",), vmem_mb=48),
        name="mem_attn_sample",
    )(x8, g, wq, cache_k, cache_v, wo)


def _router_kernel(x_ref, g_ref, wr_ref, xn_ref, route_ref):
    xn = _rms(x_ref[...], g_ref[...])
    xn_ref[...] = xn
    lane = lax.broadcasted_iota(jnp.int32, (xn.shape[0], LANES), 1).astype(F32)
    lg = jnp.where(lane < N_EXP, _dot_f32(xn, wr_ref[...]), NEG)
    m1 = jnp.max(lg, axis=-1, keepdims=True)
    i1 = jnp.min(jnp.where(lg == m1, lane, float(LANES)), axis=-1, keepdims=True)
    lg2 = jnp.where(lane == i1, NEG, lg)
    m2 = jnp.max(lg2, axis=-1, keepdims=True)
    i2 = jnp.min(jnp.where(lg2 == m2, lane, float(LANES)), axis=-1, keepdims=True)
    e = jnp.exp(m2 - m1)
    g1 = 1.0 / (1.0 + e)
    g2 = e * g1
    route_ref[...] = jnp.where(lane == 0, i1, jnp.where(lane == 1, i2, jnp.where(lane == 2, g1,
                               jnp.where(lane == 3, g2, 0.0))))


def _router(x, g, wr_pad):
    t = x.shape[0]
    tm = min(ROW_TILE, t)
    return pl.pallas_call(
        _router_kernel,
        grid=(t // tm,),
        in_specs=[pl.BlockSpec((tm, D), lambda i: (i, 0)), _full((1, D)), _full(wr_pad.shape)],
        out_specs=[pl.BlockSpec((tm, D), lambda i: (i, 0)), pl.BlockSpec((tm, LANES), lambda i: (i, 0))],
        out_shape=[jax.ShapeDtypeStruct((t, D), F32), jax.ShapeDtypeStruct((t, LANES), F32)],
        compiler_params=_cp(("parallel",)),
        name="router",
    )(x, g, wr_pad)


SC_ROWS = 32


def _sc_mesh():
    return plsc.VectorSubcoreMesh(core_axis_name="c", subcore_axis_name="s")


def _sc_workers():
    info = plsc.get_sparse_core_info()
    return info.num_cores, info.num_cores * info.num_subcores


def _sc_gather_rows(table, idx):
    n = idx.shape[0]
    width = table.shape[1]
    ncores, nw = _sc_workers()
    per_w = n // nw
    n_chunks = per_w // SC_ROWS
    assert per_w * nw == n and n_chunks * SC_ROWS == per_w

    @functools.partial(
        pl.kernel, mesh=_sc_mesh(),
        out_type=jax.ShapeDtypeStruct((n, width), table.dtype),
        scratch_types=[pltpu.VMEM((per_w,), jnp.int32), pltpu.VMEM((SC_ROWS, width), table.dtype),
                       pltpu.SemaphoreType.DMA],
        name="sc_gather_rows",
    )
    def body(table_hbm, idx_hbm, out_hbm, idx_v, rows_v, sem):
        wid = lax.axis_index("s") * ncores + lax.axis_index("c")
        base = wid * per_w
        pltpu.sync_copy(idx_hbm.at[pl.ds(base, per_w)], idx_v)

        @pl.loop(0, n_chunks)
        def _(j):
            off = pl.multiple_of(j * SC_ROWS, SC_ROWS)
            pltpu.async_copy(table_hbm.at[idx_v.at[pl.ds(off, SC_ROWS)]], rows_v, sem).wait()
            pltpu.sync_copy(rows_v, out_hbm.at[pl.ds(base + off, SC_ROWS)])

    return body(table, idx)


def _sc_scatter_rows(src, idx, n_out):
    n = idx.shape[0]
    t, width = src.shape
    ncores, nw = _sc_workers()
    per_w = n // nw
    n_chunks = per_w // SC_ROWS
    assert per_w * nw == n and n_chunks * SC_ROWS == per_w and t % per_w == 0
    idx3 = idx.reshape(nw, n_chunks, SC_ROWS)

    @functools.partial(
        pl.kernel, mesh=_sc_mesh(),
        out_type=jax.ShapeDtypeStruct((n_out, width), src.dtype),
        scratch_types=[pltpu.VMEM((n_chunks, SC_ROWS), jnp.int32), pltpu.VMEM((SC_ROWS, width), src.dtype),
                       pltpu.SemaphoreType.DMA],
        name="sc_scatter_rows",
    )
    def body(src_hbm, idx_hbm, out_hbm, idx_v, rows_v, sem):
        wid = lax.axis_index("s") * ncores + lax.axis_index("c")
        base = lax.rem(wid * per_w, t)
        pltpu.sync_copy(idx_hbm.at[wid], idx_v)

        @pl.loop(0, n_chunks)
        def _(j):
            off = pl.multiple_of(j * SC_ROWS, SC_ROWS)
            pltpu.sync_copy(src_hbm.at[pl.ds(base + off, SC_ROWS)], rows_v)
            pltpu.async_copy(rows_v, out_hbm.at[idx_v.at[j]], sem).wait()

    return body(src, idx3)


MOE_TILE = 1024
MOE_TILE_SMALL = 256
MOE_BLOCK = 512
MOE_CHUNK = 256


def _moe_ffn_kernel(te_ref, nused_ref, x_ref, wg_ref, wu_ref, wd_ref, o_ref, xb_ref):
    del te_ref
    i = pl.program_id(0)
    j = pl.program_id(1)

    @pl.when(i < nused_ref[0])
    def _():
        @pl.when(j == 0)
        def _():
            xb_ref[...] = x_ref[...].astype(BF)

        xb = xb_ref[...]
        part = None
        for c in range(MOE_BLOCK // MOE_CHUNK):
            sl = slice(c * MOE_CHUNK, (c + 1) * MOE_CHUNK)
            gate = _dot(xb, wg_ref[0, 0, :, sl].astype(BF))
            up = _dot(xb, wu_ref[0, 0, :, sl].astype(BF))
            contrib = _dot((_silu(gate) * up).astype(BF), wd_ref[0, 0, sl, :].astype(BF))
            part = contrib if part is None else part + contrib

        @pl.when(j == 0)
        def _():
            o_ref[...] = part

        @pl.when(j != 0)
        def _():
            o_ref[...] += part

    @pl.when(i >= nused_ref[0])
    def _():
        o_ref[...] = jnp.zeros_like(o_ref)


def _moe_ffn(tile_expert, nused, xs, wgu, wd, layer, *, tm):
    npad = xs.shape[0]
    nblk = D_FFE // MOE_BLOCK

    def blk(i, j, nu):
        return jnp.where(i < nu[0], j, nblk - 1)

    grid_spec = pltpu.PrefetchScalarGridSpec(
        num_scalar_prefetch=2,
        grid=(npad // tm, nblk),
        in_specs=[
            pl.BlockSpec((tm, D), lambda i, j, te, nu: (jnp.minimum(i, nu[0] - 1), 0)),
            pl.BlockSpec((1, 1, D, MOE_BLOCK), lambda i, j, te, nu: (layer, te[i], 0, blk(i, j, nu))),
            pl.BlockSpec((1, 1, D, MOE_BLOCK), lambda i, j, te, nu: (layer, te[i], 0, nblk + blk(i, j, nu))),
            pl.BlockSpec((1, 1, MOE_BLOCK, D), lambda i, j, te, nu: (layer, te[i], blk(i, j, nu), 0)),
        ],
        out_specs=pl.BlockSpec((tm, D), lambda i, j, te, nu: (i, 0)),
        scratch_shapes=[pltpu.VMEM((tm, D), BF)],
    )
    return pl.pallas_call(
        _moe_ffn_kernel,
        grid_spec=grid_spec,
        out_shape=jax.ShapeDtypeStruct((npad, D), F32),
        compiler_params=_cp(("parallel", "arbitrary"), vmem_mb=56),
        name="moe_ffn",
    )(tile_expert, nused, xs, wgu, wgu, wd)


ROUTE_GATE_LANE = 2


def _combine_kernel(h_ref, y0_ref, y1_ref, route_ref, g_ref, o_ref, *, final):
    gl = ROUTE_GATE_LANE
    route = route_ref[...]
    out = h_ref[...] + route[:, gl:gl + 1] * y0_ref[...] + route[:, gl + 1:gl + 2] * y1_ref[...]
    if final:
        out = _rms(out, g_ref[...])
    o_ref[...] = out


def _combine(h, y2, route, g, *, final):
    t = h.shape[0]
    tm = min(ROW_TILE, t)
    nt = t // tm
    return pl.pallas_call(
        functools.partial(_combine_kernel, final=final),
        grid=(nt,),
        in_specs=[
            pl.BlockSpec((tm, D), lambda i: (i, 0)),
            pl.BlockSpec((tm, D), lambda i: (i, 0)),
            pl.BlockSpec((tm, D), lambda i: (nt + i, 0)),
            pl.BlockSpec((tm, LANES), lambda i: (i, 0)),
            _full((1, D)),
        ],
        out_specs=pl.BlockSpec((tm, D), lambda i: (i, 0)),
        out_shape=jax.ShapeDtypeStruct((t, D), F32),
        compiler_params=_cp(("parallel",)),
        name="moe_combine",
    )(h, y2, y2, route, g)


def _moe(h, g, wr_pad, wgu, wd, layer, final_g, *, final):
    t = h.shape[0]
    tm = MOE_TILE if 2 * t >= 2 * N_EXP * MOE_TILE else MOE_TILE_SMALL
    xn, route = _router(h, g, wr_pad)
    eidx = route[:, :ROUTE_GATE_LANE].astype(jnp.int32)
    e_flat = eidx.T.reshape(-1)
    onehot = (e_flat[:, None] == jnp.arange(N_EXP, dtype=jnp.int32)[None, :]).astype(jnp.int32)
    csum = jnp.cumsum(onehot, axis=0)
    counts = csum[-1]
    rank = jnp.sum(onehot * csum, axis=1) - 1
    padded = ((counts + tm - 1) // tm) * tm
    ends = jnp.cumsum(padded)
    starts = ends - padded
    dest = (jnp.sum(onehot * starts[None, :], axis=1) + rank).astype(jnp.int32)
    n_tiles = -(-2 * t // tm) + N_EXP
    tile_start = jnp.arange(n_tiles, dtype=jnp.int32) * tm
    tile_expert = jnp.minimum(jnp.sum((tile_start[:, None] >= ends[None, :]).astype(jnp.int32), axis=1),
                              N_EXP - 1).astype(jnp.int32)
    nused = (ends[-1] // tm).astype(jnp.int32).reshape(1)
    tile_expert = jnp.where(jnp.arange(n_tiles) < nused[0], tile_expert, tile_expert[nused[0] - 1])
    xs = _sc_scatter_rows(xn, dest, n_tiles * tm)
    ys = _moe_ffn(tile_expert, nused, xs, wgu, wd, layer, tm=tm)
    y2 = _sc_gather_rows(ys, dest)
    return _combine(h, y2, route, final_g, final=final)


def _rope_tables(pos):
    half = ROPE // 2
    inv_freq = ROPE_BASE ** (-jnp.arange(half, dtype=F32) / half)
    ang = pos.astype(F32)[:, None] * inv_freq
    cos = jnp.cos(ang)
    sin = jnp.sin(ang)
    cc = jnp.concatenate([cos, cos], axis=-1)
    ss = jnp.concatenate([-sin, sin], axis=-1)
    return jnp.tile(cc, (1, MLA_HEADS)), jnp.tile(ss, (1, MLA_HEADS))


def _even_layer(h, g, prm, cinit, sinit, layer, *, nb, sample, state_layers=1, state_prev=None):
    t = h.shape[0]
    seq = t // nb
    proj = _rms_matmul(h, g, prm["w_zxu"], tn=_largest_tile(ZXU, 1536))
    dtp = _rms_matmul(h, g, prm["w_dt"], tn=LANES, precise=True)
    proj = proj.reshape(nb, seq, ZXU)
    dtp = dtp.reshape(nb, seq, LANES)
    if sample:
        pad = ((0, 0), (0, SUBLANES - seq), (0, 0))
        proj = jnp.pad(proj, pad)
        dtp = jnp.pad(dtp, pad)
        q, lb, lc = SAMPLE_Q, SUBLANES, seq
    else:
        q, lb, lc = CHUNK, CHUNK, CHUNK
    cinit8 = jnp.pad(cinit, ((0, 0), (SUBLANES - (SSD_CONV - 1), 0), (0, 0)))
    outs = _even_mixer(proj, dtp, cinit8, sinit.reshape(-1, nb, SSD_INNER, SSD_STATE), prm,
                       q=q, lb=lb, lc=lc, want_v=sample, layer=layer,
                       state_layers=state_layers, state_prev=state_prev)
    ymix, cout, sout = outs[:3]
    v = None
    if sample:
        ymix = ymix[:, :seq]
        v = outs[3][:, :seq]
    h = _matmul_res(ymix.reshape(t, SSD_INNER + GMLP_WIDTH), prm["w_out"], h)
    return h, cout, sout, v


def _prep_even(i, w_in, conv_w, conv_b, dt_bias, a_log, d_skip, ssd_gain, ln_g, ln_b, ws, bs, w_out):
    w = w_in[i]
    o1 = SSD_INNER + CONV_DIM
    w_zxu = jnp.concatenate([w[:, :o1], w[:, o1 + SSD_HEADS:]], axis=1).astype(BF)
    w_dt = jnp.pad(w[:, o1:o1 + SSD_HEADS], ((0, 0), (0, LANES - SSD_HEADS)))
    padl = (0, LANES - SSD_HEADS)
    return dict(
        w_zxu=w_zxu, w_dt=w_dt,
        conv_w=jnp.pad(conv_w[i], ((0, SUBLANES - SSD_CONV), (0, 0))),
        conv_b=conv_b[i][None, :],
        dt_bias=jnp.pad(dt_bias[i], padl)[None, :],
        a_log=jnp.pad(a_log[i], padl)[None, :],
        d_skip=jnp.repeat(d_skip[i], SSD_HEAD_DIM)[None, :],
        ssd_gain=ssd_gain[i][None, :],
        ln_g=ln_g[i][None, :], ln_b=ln_b[i][None, :],
        ws=ws[i], bst=bs[i].T,
        expand=(jnp.arange(LANES)[:, None] == jnp.arange(SSD_INNER)[None, :] // SSD_HEAD_DIM).astype(BF),
        w_out=w_out[i].astype(BF),
    )


def _prep_mla(i, w_down, q_gain, kv_gain, w_uq, w_uk, w_uv, w_o):
    wd = w_down[i]
    wk = wd[:, Q_RANK + KV_RANK:]
    half = ROPE // 2
    rot = lambda a: jnp.concatenate([a[..., half:], a[..., :half]], axis=-1)
    uq = w_uq[i]
    uq_pe = uq[:, :, NOPE:]
    return dict(
        wdq=wd[:, :Q_RANK].astype(BF),
        wdkv=wd[:, Q_RANK:Q_RANK + KV_RANK].astype(BF),
        wdk2=jnp.concatenate([wk, rot(wk)], axis=1).astype(BF),
        q_gain=q_gain[i][None, :], kv_gain=kv_gain[i][None, :],
        wqn=uq[:, :, :NOPE].reshape(Q_RANK, MLA_HEADS * NOPE).astype(BF),
        wqp=jnp.concatenate([uq_pe.reshape(Q_RANK, -1), rot(uq_pe).reshape(Q_RANK, -1)], axis=1).astype(BF),
        wuk=jnp.transpose(w_uk[i], (1, 2, 0)).astype(BF),
        wuv=jnp.transpose(w_uv[i], (1, 0, 2)).astype(BF),
        wkn=w_uk[i].reshape(KV_RANK, MLA_HEADS * NOPE).astype(BF),
        wv=w_uv[i].reshape(KV_RANK, MLA_HEADS * MLA_V).T.astype(BF),
        wo=w_o[i].astype(BF),
    )


def kernel(x_prompt, x_sample, state_ssd, state_conv, cache_mla_ckv, cache_mla_kpe, cache_mem_k, cache_mem_v, page_table, mem_prompt, mix_norm, w_in, conv_w, conv_b, dt_bias, a_log, d_skip, ssd_gain, gmlp_ln_g, gmlp_ln_b, gmlp_ws, gmlp_bs, w_out_even, w_mla_down, mla_q_gain, mla_kv_gain, w_mla_uq, w_mla_uk, w_mla_uv, w_mla_o, xattn_norm, mem_norm, w_mem_q, w_mem_k, w_mem_v, w_mem_o, ffn_norm, w_ffn_gu, w_ffn_down, w_router, w_exp_gu, w_exp_down, final_norm):
    nbp, seq, _ = x_prompt.shape
    nbs, dseq, _ = x_sample.shape
    depth = mix_norm.shape[0]
    past = page_table.shape[1] * PAGE
    mt = mem_prompt.shape[1]
    hp = x_prompt.reshape(nbp * seq, D)
    hs = x_sample.reshape(nbs * dseq, D)
    cos_p, sin_p = _rope_tables(jnp.arange(seq, dtype=jnp.int32))
    cos_s, sin_s = _rope_tables(past + jnp.arange(dseq, dtype=jnp.int32))
    cos_s = jnp.tile(cos_s, (nbs, 1))
    sin_s = jnp.tile(sin_s, (nbs, 1))
    cache_k4 = cache_mem_k.reshape(depth, nbs, mt * MEM_HEADS, MEM_HD)
    cache_v4 = cache_mem_v.reshape(depth, nbs, mt * MEM_HEADS, MEM_HD)
    cache_kpe_t = jnp.swapaxes(cache_mla_kpe, 2, 3)
    final_g = final_norm[None, :]

    p_ssd, p_conv, p_ckv, p_kpe, p_mk, p_mv = [], [], [], [], [], []
    s_conv, s_v, s_ckv, s_kpe = [], [], [], []
    n_even = (depth + 1) // 2
    s_state = None
    for l in range(depth):
        i = l // 2
        g_mix = mix_norm[l][None, :]
        if l % 2 == 0:
            prm = _prep_even(i, w_in, conv_w, conv_b, dt_bias, a_log, d_skip, ssd_gain, gmlp_ln_g,
                             gmlp_ln_b, gmlp_ws, gmlp_bs, w_out_even)
            buf0 = jnp.zeros((nbp, SSD_CONV - 1, CONV_DIM), F32)
            h00 = jnp.zeros((1, nbp, SSD_HEADS, SSD_HEAD_DIM, SSD_STATE), F32)
            hp, buf_p, ssd_p, _ = _even_layer(hp, g_mix, prm, buf0, h00, 0, nb=nbp, sample=False)
            hs, buf_s, s_state, v_s = _even_layer(hs, g_mix, prm, state_conv[i], state_ssd, i, nb=nbs,
                                                  sample=True, state_layers=n_even, state_prev=s_state)
            p_ssd.append(ssd_p.reshape(nbp, SSD_HEADS, SSD_HEAD_DIM, SSD_STATE))
            p_conv.append(buf_p)
            s_conv.append(buf_s)
            s_v.append(v_s)
        else:
            prm = _prep_mla(i, w_mla_down, mla_q_gain, mla_kv_gain, w_mla_uq, w_mla_uk, w_mla_uv, w_mla_o)
            ckv, kpe, qh, kh, vh = _mla_proj_prompt(hp, g_mix, prm, cos_p, sin_p, nb=nbp)
            o_p = _flash(qh, kh, vh)
            hp = _matmul_res(o_p.reshape(nbp * seq, MLA_HEADS * MLA_V), prm["wo"], hp)
            p_ckv.append(ckv.reshape(nbp, seq, KV_RANK))
            p_kpe.append(kpe.reshape(nbp, seq, ROPE))

            ckv_s, kpe_s, kcat_s, q_s = _mla_proj(hs, g_mix, prm, cos_s, sin_s, nb=1)
            q_s = q_s[0].reshape(MLA_HEADS, nbs, dseq, QK).transpose(1, 0, 2, 3).reshape(nbs, MLA_HEADS * dseq, QK)
            newk = jnp.pad(kcat_s.reshape(nbs, dseq, QK), ((0, 0), (0, NEW_KEY_ROWS - dseq), (0, 0)))
            o_s = _decode(page_table, q_s, newk, cache_mla_ckv, cache_kpe_t, i, dseq)
            o_s = o_s.reshape(nbs, MLA_HEADS, dseq, KV_RANK).transpose(1, 0, 2, 3)
            o_s = o_s.reshape(1, MLA_HEADS, nbs * dseq, KV_RANK).astype(BF)
            hs = _mla_out(o_s, prm["wuv"], prm["wo"], hs)
            s_ckv.append(ckv_s.reshape(nbs, dseq, KV_RANK))
            s_kpe.append(kpe_s.reshape(nbs, dseq, ROPE))

        wkv = jnp.concatenate([w_mem_k[l], w_mem_v[l]], axis=1).astype(BF)
        kv = _rms_matmul(mem_prompt.reshape(nbp * mt, D), mem_norm[l][None, :], wkv, tn=MEM_INNER)
        mk_p = kv[:, :MEM_INNER].reshape(nbp, mt, MEM_INNER)
        mv_p = kv[:, MEM_INNER:].reshape(nbp, mt, MEM_INNER)
        g_x = xattn_norm[l][None, :]
        wq = w_mem_q[l].astype(BF)
        wo = w_mem_o[l].astype(BF)
        hp = _mem_prompt(hp, g_x, wq, mk_p, mv_p, wo)
        x8 = jnp.pad(hs.reshape(nbs, dseq, D), ((0, 0), (0, SUBLANES - dseq), (0, 0)))
        hs = _mem_sample(x8, g_x, wq, cache_k4, cache_v4, wo, l)[:, :dseq].reshape(nbs * dseq, D)
        p_mk.append(mk_p.reshape(nbp, mt, MEM_HEADS, MEM_HD))
        p_mv.append(mv_p.reshape(nbp, mt, MEM_HEADS, MEM_HD))

        g_f = ffn_norm[l][None, :]
        if l % 2 == 0:
            wgu = w_ffn_gu[i].astype(BF)
            wd = w_ffn_down[i].astype(BF)
            hp = _ffn(hp, g_f, wgu, wd)
            hs = _ffn(hs, g_f, wgu, wd)
        else:
            wr = jnp.pad(w_router[i], ((0, 0), (0, LANES - N_EXP)))
            final = l == depth - 1
            hp = _moe(hp, g_f, wr, w_exp_gu, w_exp_down, i, final_g, final=final)
            hs = _moe(hs, g_f, wr, w_exp_gu, w_exp_down, i, final_g, final=final)
    if depth % 2 == 1:
        raise NotImplementedError("the final norm is fused into the last routed-expert layer")
    y_prompt = hp.reshape(nbp, seq, D)
    y_sample = hs.reshape(nbs, dseq, D)
    return (y_prompt, y_sample,
            jnp.stack(p_ssd), jnp.stack(p_conv), jnp.stack(p_ckv), jnp.stack(p_kpe),
            jnp.stack(p_mk), jnp.stack(p_mv),
            s_state.reshape(n_even, nbs, SSD_HEADS, SSD_HEAD_DIM, SSD_STATE), jnp.stack(s_conv), jnp.stack(s_v), jnp.stack(s_ckv), jnp.stack(s_kpe))
```

```python
import functools

import jax
import jax.numpy as jnp
from jax import lax
from jax.experimental import pallas as pl
from jax.experimental.pallas import tpu as pltpu
from jax.experimental.pallas import tpu_sc as plsc

F32 = jnp.float32
BF = jnp.bfloat16
EPS = 1e-6
NEG = -1e30

D = 1024
SSD_HEADS = 16
SSD_HEAD_DIM = 64
SSD_INNER = SSD_HEADS * SSD_HEAD_DIM
SSD_GROUPS = 2
SSD_STATE = 128
SSD_CONV = 4
CONV_DIM = SSD_INNER + 2 * SSD_GROUPS * SSD_STATE
GMLP_GROUPS = 8
GMLP_WIDTH = 1024
CHUNK = 128
EVEN_SEQS_PER_STEP = 2
SAMPLE_Q = 16
ZXU = SSD_INNER + CONV_DIM + 2 * GMLP_WIDTH
MLA_HEADS = 8
NOPE = 128
ROPE = 64
MLA_V = 128
Q_RANK = 256
KV_RANK = 256
MLA_SCALE = (NOPE + ROPE) ** -0.5
QK = KV_RANK + ROPE
ROPE_BASE = 10000.0
MEM_HEADS = 4
MEM_HD = 128
MEM_INNER = MEM_HEADS * MEM_HD
D_FF = 2816
N_EXP = 8
D_FFE = 3584
PAGE = 128
LANES = 128
SUBLANES = 8
ROW_TILE = 512
NEW_KEY_ROWS = 16
SAMPLES_PER_STEP = 8


def _cp(sem, vmem_mb=None):
    kw = dict(dimension_semantics=sem)
    if vmem_mb is not None:
        kw["vmem_limit_bytes"] = vmem_mb * 1024 * 1024
    return pltpu.CompilerParams(**kw)


def _rms(x, g):
    return x * lax.rsqrt(jnp.mean(x * x, axis=-1, keepdims=True) + EPS) * g


def _dot(a, b):
    return jnp.dot(a, b, preferred_element_type=F32)


def _dot_nt(a, b):
    return lax.dot_general(a, b, (((1,), (1,)), ((), ())), preferred_element_type=F32)


def _dot_f32(a, b):
    return jnp.dot(a, b, preferred_element_type=F32, precision=lax.Precision.HIGHEST)


def _silu(x):
    return x * jax.nn.sigmoid(x)


def _full(shape):
    n = len(shape)
    return pl.BlockSpec(shape, lambda *_: (0,) * n)


def _largest_tile(n, cap):
    best = LANES
    for t in range(LANES, cap + 1, LANES):
        if n % t == 0:
            best = t
    return best


def _rms_matmul_kernel(x_ref, g_ref, w_ref, o_ref, xn_ref, *, precise):
    @pl.when(pl.program_id(1) == 0)
    def _():
        xn_ref[...] = _rms(x_ref[...], g_ref[...]).astype(xn_ref.dtype)

    if precise:
        o_ref[...] = _dot_f32(xn_ref[...], w_ref[...]).astype(o_ref.dtype)
    else:
        o_ref[...] = _dot(xn_ref[...], w_ref[...]).astype(o_ref.dtype)


def _rms_matmul(x, g, w, *, tn, out_dtype=F32, precise=False):
    t, k = x.shape
    n = w.shape[1]
    tm = min(ROW_TILE, t)
    return pl.pallas_call(
        functools.partial(_rms_matmul_kernel, precise=precise),
        grid=(t // tm, n // tn),
        in_specs=[
            pl.BlockSpec((tm, k), lambda i, j: (i, 0)),
            pl.BlockSpec((1, k), lambda i, j: (0, 0)),
            pl.BlockSpec((k, tn), lambda i, j: (0, j)),
        ],
        out_specs=pl.BlockSpec((tm, tn), lambda i, j: (i, j)),
        out_shape=jax.ShapeDtypeStruct((t, n), out_dtype),
        scratch_shapes=[pltpu.VMEM((tm, k), F32 if precise else BF)],
        compiler_params=_cp(("parallel", "arbitrary")),
        name="rms_matmul",
    )(x, g, w)


def _matmul_res_kernel(a_ref, w_ref, r_ref, o_ref):
    o_ref[...] = r_ref[...] + _dot(a_ref[...].astype(BF), w_ref[...])


def _matmul_res(a, w, res):
    t, k = a.shape
    n = w.shape[1]
    tn = _largest_tile(n, 1024)
    tm = min(ROW_TILE, t)
    return pl.pallas_call(
        _matmul_res_kernel,
        grid=(t // tm, n // tn),
        in_specs=[
            pl.BlockSpec((tm, k), lambda i, j: (i, 0)),
            pl.BlockSpec((k, tn), lambda i, j: (0, j)),
            pl.BlockSpec((tm, tn), lambda i, j: (i, j)),
        ],
        out_specs=pl.BlockSpec((tm, tn), lambda i, j: (i, j)),
        out_shape=jax.ShapeDtypeStruct((t, n), F32),
        compiler_params=_cp(("parallel", "arbitrary")),
        name="matmul_res",
    )(a, w, res)


FF_CHUNK = 256


def _ffn_kernel(x_ref, g_ref, wgu_ref, wd_ref, o_ref, hid_ref, *, ff):
    x = x_ref[...]
    xn = _rms(x, g_ref[...]).astype(BF)
    for c in range(ff // FF_CHUNK):
        lo = c * FF_CHUNK
        gate = _dot(xn, wgu_ref[:, lo:lo + FF_CHUNK])
        up = _dot(xn, wgu_ref[:, ff + lo:ff + lo + FF_CHUNK])
        hid_ref[:, lo:lo + FF_CHUNK] = (_silu(gate) * up).astype(BF)
    o_ref[...] = x + _dot(hid_ref[...], wd_ref[...])


def _ffn(x, g, wgu, wd):
    t = x.shape[0]
    ff = wd.shape[0]
    tm = min(ROW_TILE, t)
    return pl.pallas_call(
        functools.partial(_ffn_kernel, ff=ff),
        grid=(t // tm,),
        in_specs=[
            pl.BlockSpec((tm, D), lambda i: (i, 0)),
            _full((1, D)),
            pl.BlockSpec((D, 2 * ff), lambda i: (0, 0), pipeline_mode=pl.Buffered(1)),
            pl.BlockSpec((ff, D), lambda i: (0, 0), pipeline_mode=pl.Buffered(1)),
        ],
        out_specs=pl.BlockSpec((tm, D), lambda i: (i, 0)),
        out_shape=jax.ShapeDtypeStruct((t, D), F32),
        scratch_shapes=[pltpu.VMEM((tm, ff), BF)],
        compiler_params=_cp(("parallel",), vmem_mb=48),
        name="ffn",
    )(x, g, wgu, wd)


def _softplus(x):
    return jnp.maximum(x, 0.0) + jnp.log1p(jnp.exp(-jnp.abs(x)))


def _gelu_tanh(x):
    return 0.5 * x * (1.0 + jnp.tanh(0.7978845608028654 * (x + 0.044715 * (x * x * x))))


def _even_kernel(proj_ref, dt_ref, cinit_ref, sinit_ref, cw_ref, cb_ref, dtb_ref, alog_ref,
                 dsk_ref, sg_ref, lng_ref, lnb_ref, ws_ref, bst_ref, e_ref,
                 ymix_ref, cout_ref, sout_ref, v_ref, ext_ref, ht_ref, *, q, lb, lc, nbb, out_slot):
    c = pl.program_id(1)

    @pl.when(c == 0)
    def _():
        for bb in range(nbb):
            ext_ref[bb, 0:SUBLANES, :] = cinit_ref[bb]
            ht_ref[bb] = sinit_ref[0, bb].T

    for bb in range(nbb):
        _even_block(proj_ref, dt_ref, cw_ref, cb_ref, dtb_ref, alog_ref,
                    dsk_ref, sg_ref, lng_ref, lnb_ref, ws_ref, bst_ref, e_ref,
                    ymix_ref, v_ref, ext_ref, ht_ref, bb, q=q, lb=lb, lc=lc)

    @pl.when(c == pl.num_programs(1) - 1)
    def _():
        for bb in range(nbb):
            cout_ref[bb] = ext_ref[bb, SUBLANES + lc - 3:SUBLANES + lc, :]
            for slot in range(sout_ref.shape[0]):
                if slot == out_slot:
                    sout_ref[slot, bb] = ht_ref[bb].T
                else:
                    sout_ref[slot, bb] = jnp.zeros((SSD_INNER, SSD_STATE), F32)


def _even_block(proj_ref, dt_ref, cw_ref, cb_ref, dtb_ref, alog_ref,
                dsk_ref, sg_ref, lng_ref, lnb_ref, ws_ref, bst_ref, e_ref,
                ymix_ref, v_ref, ext_ref, ht_ref, bb, *, q, lb, lc):

    if lb == q:
        p = proj_ref[bb]
        dtr = dt_ref[bb]
    else:
        p = jnp.concatenate([proj_ref[bb], jnp.zeros((q - lb, ZXU), F32)], axis=0)
        dtr = jnp.concatenate([dt_ref[bb], jnp.zeros((q - lb, LANES), F32)], axis=0)
    z = p[:, :SSD_INNER]
    xbc_raw = p[:, SSD_INNER:SSD_INNER + CONV_DIM]
    uv = p[:, SSD_INNER + CONV_DIM:]

    ext_ref[bb, SUBLANES:SUBLANES + q, :] = xbc_raw
    conv = (cb_ref[...] + cw_ref[0:1, :] * ext_ref[bb, 5:5 + q, :] + cw_ref[1:2, :] * ext_ref[bb, 6:6 + q, :]
            + cw_ref[2:3, :] * ext_ref[bb, 7:7 + q, :] + cw_ref[3:4, :] * xbc_raw)
    ext_ref[bb, 0:SUBLANES, :] = ext_ref[bb, q:q + SUBLANES, :]

    xbc = _silu(conv)
    xs = xbc[:, :SSD_INNER]
    gw = SSD_STATE
    bm = [xbc[:, SSD_INNER + g * gw:SSD_INNER + (g + 1) * gw] for g in range(SSD_GROUPS)]
    cm = [xbc[:, SSD_INNER + (SSD_GROUPS + g) * gw:SSD_INNER + (SSD_GROUPS + g + 1) * gw]
          for g in range(SSD_GROUPS)]

    row = lax.broadcasted_iota(jnp.int32, (q, q), 0)
    col = lax.broadcasted_iota(jnp.int32, (q, q), 1)
    causal = row >= col

    dt = _softplus(dtr + dtb_ref[...])
    if lc < q:
        dt = jnp.where(lax.broadcasted_iota(jnp.int32, (q, LANES), 0) < lc, dt, 0.0)
    a = dt * (-jnp.exp(alog_ref[...]))
    a_cum = _dot_f32(causal.astype(F32), a)
    a_cum_t = a_cum.T
    a_last = a_cum[q - 1:q, :]
    decay_end = jnp.exp(a_last - a_cum)
    ea = jnp.exp(a_cum)
    chunk_decay = jnp.exp(a_last)

    cmb = [m.astype(BF) for m in cm]
    cb = [_dot_nt(cmb[g], bm[g].astype(BF)) for g in range(SSD_GROUPS)]
    bt = [bm[g].T.astype(BF) for g in range(SSD_GROUPS)]
    heads_per_group = SSD_HEADS // SSD_GROUPS
    gi = SSD_INNER // SSD_GROUPS

    def per_head_lanes(v):
        hi = v.astype(BF)
        lo = (v - hi.astype(F32)).astype(BF)
        return _dot(hi, e_ref[...]) + _dot(lo, e_ref[...])

    dt_x = per_head_lanes(dt)
    ea_x = per_head_lanes(ea)
    de_x = per_head_lanes(decay_end)
    cd_x = per_head_lanes(jnp.broadcast_to(chunk_decay, (SUBLANES, LANES)))[0:1]
    xdt = xs * dt_x
    xdt_b = xdt.astype(BF)
    xd_b = (xdt * de_x).astype(BF)
    y_off = []
    for g in range(SSD_GROUPS):
        h_old = ht_ref[bb, :, g * gi:(g + 1) * gi]
        y_off.append(_dot(cmb[g], h_old.astype(BF)))
        ht_ref[bb, :, g * gi:(g + 1) * gi] = (h_old * cd_x[:, g * gi:(g + 1) * gi]
                                              + _dot(bt[g], xd_b[:, g * gi:(g + 1) * gi]))

    def decay_weights(r):
        seg = a_cum[:, r:r + 1] - a_cum_t[r:r + 1, :]
        lmat = jnp.where(causal, jnp.exp(jnp.minimum(seg, 0.0)), 0.0)
        return (cb[r // heads_per_group] * lmat).astype(BF)

    first_half = lax.broadcasted_iota(jnp.int32, (q, LANES), 1) < SSD_HEAD_DIM
    y_diag = []
    for k in range(SSD_HEADS // 2):
        xp = xdt_b[:, k * LANES:(k + 1) * LANES]
        y_diag.append(jnp.where(first_half, _dot(decay_weights(2 * k), xp), _dot(decay_weights(2 * k + 1), xp)))
    y = jnp.concatenate(y_diag, axis=1) + jnp.concatenate(y_off, axis=1) * ea_x + dsk_ref[...] * xs
    y = y * _silu(z)
    gi = SSD_INNER // SSD_GROUPS
    yn = [_rms(y[:, g * gi:(g + 1) * gi], sg_ref[:, g * gi:(g + 1) * gi]) for g in range(SSD_GROUPS)]
    ymix_ref[bb, :, 0:SSD_INNER] = jnp.concatenate(yn, axis=1)[:lb].astype(BF)

    uvg = _gelu_tanh(uv)
    u = uvg[:, :GMLP_WIDTH]
    v = uvg[:, GMLP_WIDTH:]
    mu = jnp.mean(v, axis=-1, keepdims=True)
    vc = v - mu
    vn = vc * lax.rsqrt(jnp.mean(vc * vc, axis=-1, keepdims=True) + EPS) * lng_ref[...] + lnb_ref[...]
    if v_ref is not None:
        v_ref[bb] = vn[:lb]
    gd = GMLP_WIDTH // GMLP_GROUPS
    yb = []
    for g in range(GMLP_GROUPS):
        wt = jnp.where(causal, ws_ref[g, :q, :q], 0.0).astype(BF)
        sp = _dot(wt, vn[:, g * gd:(g + 1) * gd].astype(BF)) + bst_ref[:q, g:g + 1]
        yb.append(u[:, g * gd:(g + 1) * gd] * sp)
    ymix_ref[bb, :, SSD_INNER:SSD_INNER + GMLP_WIDTH] = jnp.concatenate(yb, axis=1)[:lb].astype(BF)


EVEN_INPUTS = 15


def _even_entry(*refs, has_v, has_prev, **kw):
    ins = refs[:EVEN_INPUTS]
    k = EVEN_INPUTS + (1 if has_prev else 0)
    outs = refs[k:k + 3]
    k += 3
    v_ref = refs[k] if has_v else None
    k += 1 if has_v else 0
    _even_kernel(*ins, *outs, v_ref, *refs[k:], **kw)


def _even_mixer(proj, dtp, cinit8, sinit, prm, *, q, lb, lc, want_v, layer, state_layers=1, state_prev=None):
    b, lp, _ = proj.shape
    nchunks = lp // lb
    nbb = EVEN_SEQS_PER_STEP if b % EVEN_SEQS_PER_STEP == 0 else 1
    par = [prm["conv_w"], prm["conv_b"], prm["dt_bias"], prm["a_log"], prm["d_skip"], prm["ssd_gain"],
           prm["ln_g"], prm["ln_b"], prm["ws"], prm["bst"], prm["expand"]]
    in_specs = [
        pl.BlockSpec((nbb, lb, ZXU), lambda i, c: (i, c, 0)),
        pl.BlockSpec((nbb, lb, LANES), lambda i, c: (i, c, 0)),
        pl.BlockSpec((nbb, SUBLANES, CONV_DIM), lambda i, c: (i, 0, 0)),
        pl.BlockSpec((1, nbb, SSD_INNER, SSD_STATE), lambda i, c: (layer, i, 0, 0)),
    ] + [_full(w.shape) for w in par]
    out_specs = [
        pl.BlockSpec((nbb, lb, SSD_INNER + GMLP_WIDTH), lambda i, c: (i, c, 0)),
        pl.BlockSpec((nbb, SSD_CONV - 1, CONV_DIM), lambda i, c: (i, 0, 0)),
        (pl.BlockSpec((1, nbb, SSD_INNER, SSD_STATE), lambda i, c: (layer, i, 0, 0))
         if state_prev is not None else
         pl.BlockSpec((state_layers, nbb, SSD_INNER, SSD_STATE), lambda i, c: (0, i, 0, 0))),
    ]
    out_slot = 0 if state_prev is not None else (layer if state_layers > 1 else 0)
    out_shape = [
        jax.ShapeDtypeStruct((b, lp, SSD_INNER + GMLP_WIDTH), BF),
        jax.ShapeDtypeStruct((b, SSD_CONV - 1, CONV_DIM), F32),
        jax.ShapeDtypeStruct((state_layers, b, SSD_INNER, SSD_STATE), F32),
    ]
    if want_v:
        out_specs.append(pl.BlockSpec((nbb, lb, GMLP_WIDTH), lambda i, c: (i, c, 0)))
        out_shape.append(jax.ShapeDtypeStruct((b, lp, GMLP_WIDTH), F32))
    args = [proj, dtp, cinit8, sinit, *par]
    aliases = {}
    if state_prev is not None:
        in_specs.append(pl.BlockSpec(memory_space=pl.ANY))
        aliases = {len(args): 2}
        args.append(state_prev)
    return pl.pallas_call(
        functools.partial(_even_entry, has_v=want_v, has_prev=state_prev is not None,
                          q=q, lb=lb, lc=lc, nbb=nbb, out_slot=out_slot),
        grid=(b // nbb, nchunks),
        in_specs=in_specs,
        out_specs=out_specs,
        out_shape=out_shape,
        input_output_aliases=aliases,
        scratch_shapes=[pltpu.VMEM((nbb, q + 2 * SUBLANES, CONV_DIM), F32),
                        pltpu.VMEM((nbb, SSD_STATE, SSD_INNER), F32)],
        compiler_params=_cp(("parallel", "arbitrary"), vmem_mb=48),
        name="even_mixer",
    )(*args)


def _mla_proj_kernel(x_ref, g_ref, wdq_ref, wdkv_ref, wdk2_ref, qg_ref, kvg_ref, wqn_ref, wqp_ref,
                     wuk_ref, cos_ref, sin_ref, ckv_ref, kpe_ref, kcat_ref, q_ref):
    xn = _rms(x_ref[...], g_ref[...]).astype(BF)
    cqn = _rms(_dot(xn, wdq_ref[...]), qg_ref[...]).astype(BF)
    ckv = _rms(_dot(xn, wdkv_ref[...]), kvg_ref[...])
    kk = _dot(xn, wdk2_ref[...])
    cos = cos_ref[...]
    sin = sin_ref[...]
    kpe = kk[:, :ROPE] * cos[:, :ROPE] + kk[:, ROPE:] * sin[:, :ROPE]
    ckv_ref[...] = ckv
    kpe_ref[...] = kpe
    kcat_ref[:, :KV_RANK] = ckv.astype(BF)
    kcat_ref[:, KV_RANK:] = kpe.astype(BF)
    qn = _dot(cqn, wqn_ref[...])
    qp = _dot(cqn, wqp_ref[...])
    hr = MLA_HEADS * ROPE
    qpe = qp[:, :hr] * cos + qp[:, hr:] * sin
    for h in range(MLA_HEADS):
        ql = _dot(qn[:, h * NOPE:(h + 1) * NOPE].astype(BF), wuk_ref[h])
        q_ref[0, h, :, :KV_RANK] = (ql * MLA_SCALE).astype(BF)
        q_ref[0, h, :, KV_RANK:] = (qpe[:, h * ROPE:(h + 1) * ROPE] * MLA_SCALE).astype(BF)


def _mla_proj(x, g, prm, cos8, sin8, *, nb):
    t = x.shape[0]
    seq = t // nb
    tm = min(ROW_TILE, seq)
    tpb = seq // tm
    w = [prm["wdq"], prm["wdkv"], prm["wdk2"], prm["q_gain"], prm["kv_gain"], prm["wqn"], prm["wqp"],
         prm["wuk"]]
    return pl.pallas_call(
        _mla_proj_kernel,
        grid=(t // tm,),
        in_specs=[pl.BlockSpec((tm, D), lambda i: (i, 0)), _full((1, D))] + [_full(a.shape) for a in w] + [
            pl.BlockSpec((tm, MLA_HEADS * ROPE), lambda i: (i % tpb, 0)),
            pl.BlockSpec((tm, MLA_HEADS * ROPE), lambda i: (i % tpb, 0)),
        ],
        out_specs=[
            pl.BlockSpec((tm, KV_RANK), lambda i: (i, 0)),
            pl.BlockSpec((tm, ROPE), lambda i: (i, 0)),
            pl.BlockSpec((tm, QK), lambda i: (i, 0)),
            pl.BlockSpec((1, MLA_HEADS, tm, QK), lambda i: (i // tpb, 0, i % tpb, 0)),
        ],
        out_shape=[
            jax.ShapeDtypeStruct((t, KV_RANK), F32),
            jax.ShapeDtypeStruct((t, ROPE), F32),
            jax.ShapeDtypeStruct((t, QK), BF),
            jax.ShapeDtypeStruct((nb, MLA_HEADS, seq, QK), BF),
        ],
        compiler_params=_cp(("parallel",)),
        name="mla_proj",
    )(x, g, *w, cos8, sin8)


HEAD_QK = NOPE + ROPE
LOG2E = 1.4426950408889634


def _mla_proj_prompt_kernel(x_ref, g_ref, wdq_ref, wdkv_ref, wdk2_ref, qg_ref, kvg_ref, wqn_ref, wqp_ref,
                            wkn_ref, wv_ref, cos_ref, sin_ref, ckv_ref, kpe_ref, q_ref, k_ref, v_ref):
    xn = _rms(x_ref[...], g_ref[...]).astype(BF)
    cqn = _rms(_dot(xn, wdq_ref[...]), qg_ref[...]).astype(BF)
    ckv = _rms(_dot(xn, wdkv_ref[...]), kvg_ref[...])
    kk = _dot(xn, wdk2_ref[...])
    cos = cos_ref[...]
    sin = sin_ref[...]
    kpe = kk[:, :ROPE] * cos[:, :ROPE] + kk[:, ROPE:] * sin[:, :ROPE]
    ckv_ref[...] = ckv
    kpe_ref[...] = kpe
    ckv_b = ckv.astype(BF)
    kpe_b = kpe.astype(BF)
    kn = _dot(ckv_b, wkn_ref[...])
    vt = _dot_nt(wv_ref[...], ckv_b)
    qn = _dot(cqn, wqn_ref[...])
    qp = _dot(cqn, wqp_ref[...])
    hr = MLA_HEADS * ROPE
    qpe = qp[:, :hr] * cos + qp[:, hr:] * sin
    qscale = MLA_SCALE * LOG2E
    for h in range(MLA_HEADS):
        q_ref[0, h, :, :NOPE] = (qn[:, h * NOPE:(h + 1) * NOPE] * qscale).astype(BF)
        q_ref[0, h, :, NOPE:] = (qpe[:, h * ROPE:(h + 1) * ROPE] * qscale).astype(BF)
        k_ref[0, h, :, :NOPE] = kn[:, h * NOPE:(h + 1) * NOPE].astype(BF)
        k_ref[0, h, :, NOPE:] = kpe_b
        v_ref[0, h, 0] = vt[h * MLA_V:(h + 1) * MLA_V].astype(BF)


def _mla_proj_prompt(x, g, prm, cos8, sin8, *, nb):
    t = x.shape[0]
    seq = t // nb
    tm = min(ROW_TILE, seq)
    tpb = seq // tm
    w = [prm["wdq"], prm["wdkv"], prm["wdk2"], prm["q_gain"], prm["kv_gain"], prm["wqn"], prm["wqp"],
         prm["wkn"], prm["wv"]]
    head_spec = lambda width: pl.BlockSpec((1, MLA_HEADS, tm, width), lambda i: (i // tpb, 0, i % tpb, 0))
    head_shape = lambda width: jax.ShapeDtypeStruct((nb, MLA_HEADS, seq, width), BF)
    return pl.pallas_call(
        _mla_proj_prompt_kernel,
        grid=(t // tm,),
        in_specs=[pl.BlockSpec((tm, D), lambda i: (i, 0)), _full((1, D))] + [_full(a.shape) for a in w] + [
            pl.BlockSpec((tm, MLA_HEADS * ROPE), lambda i: (i % tpb, 0)),
            pl.BlockSpec((tm, MLA_HEADS * ROPE), lambda i: (i % tpb, 0)),
        ],
        out_specs=[
            pl.BlockSpec((tm, KV_RANK), lambda i: (i, 0)),
            pl.BlockSpec((tm, ROPE), lambda i: (i, 0)),
            head_spec(HEAD_QK), head_spec(HEAD_QK),
            pl.BlockSpec((1, MLA_HEADS, 1, MLA_V, tm), lambda i: (i // tpb, 0, i % tpb, 0, 0)),
        ],
        out_shape=[
            jax.ShapeDtypeStruct((t, KV_RANK), F32),
            jax.ShapeDtypeStruct((t, ROPE), F32),
            head_shape(HEAD_QK), head_shape(HEAD_QK),
            jax.ShapeDtypeStruct((nb, MLA_HEADS, tpb, MLA_V, tm), BF),
        ],
        compiler_params=_cp(("parallel",)),
        name="mla_proj_prompt",
    )(x, g, *w, cos8, sin8)


FLASH_HEADS = 4
FLASH_KBLOCKS = 4
FLASH_QSPLIT = 1
DENOM_ROWS = 16


def _flash_kernel(q_ref, k_ref, vt_ref, o_ref, m_ref, acc_ref, *, t):
    qi = pl.program_id(2)
    for hh in range(FLASH_HEADS):
        m_ref[hh] = jnp.full((1, t), NEG, F32)
        acc_ref[hh] = jnp.zeros((MLA_V + DENOM_ROWS, t), F32)
    ones = jnp.ones((DENOM_ROWS, t), BF)

    tq = t // FLASH_QSPLIT

    def block(ki, nblk, masked):
        start = pl.multiple_of(ki * t, t)
        ones_n = jnp.concatenate([ones] * nblk, axis=1)
        for hh in range(FLASH_HEADS):
            kb = k_ref[0, hh, pl.ds(start, nblk * t), :]
            vt = jnp.concatenate([vt_ref[0, hh, ki + n] for n in range(nblk)], axis=1)
            v1 = jnp.concatenate([vt, ones_n], axis=0)
            for qs in range(FLASH_QSPLIT):
                lanes = slice(qs * tq, (qs + 1) * tq)
                st = _dot_nt(kb, q_ref[0, hh, lanes, :])
                if masked:
                    key = lax.broadcasted_iota(jnp.int32, (t, tq), 0)
                    qry = lax.broadcasted_iota(jnp.int32, (t, tq), 1) + qs * tq
                    st = jnp.where(key <= qry, st, NEG)
                m_old = m_ref[hh, :, lanes]
                m_new = jnp.maximum(m_old, jnp.max(st, axis=0, keepdims=True))
                alpha = jnp.exp2(m_old - m_new)
                pt = jnp.exp2(st - m_new).astype(BF)
                acc_ref[hh, :, lanes] = alpha * acc_ref[hh, :, lanes] + _dot(v1, pt)
                m_ref[hh, :, lanes] = m_new

    def body(kk, carry):
        block(kk * FLASH_KBLOCKS, FLASH_KBLOCKS, False)
        return carry

    lax.fori_loop(0, qi // FLASH_KBLOCKS, body, 0)
    for rem in range(1, FLASH_KBLOCKS):
        @pl.when(qi % FLASH_KBLOCKS >= rem)
        def _():
            block(qi - qi % FLASH_KBLOCKS + rem - 1, 1, False)
    block(qi, 1, True)
    for hh in range(FLASH_HEADS):
        acc = acc_ref[hh]
        o_t = acc[:MLA_V] / acc[MLA_V:MLA_V + 1]
        o_ref[0, :, hh * MLA_V:(hh + 1) * MLA_V] = o_t.T.astype(BF)


def _flash(q, k, vt):
    nb, _, seq, _ = q.shape
    t = vt.shape[-1]
    nh = FLASH_HEADS
    return pl.pallas_call(
        functools.partial(_flash_kernel, t=t),
        grid=(nb, MLA_HEADS // nh, seq // t),
        in_specs=[
            pl.BlockSpec((1, nh, t, HEAD_QK), lambda b, h, i: (b, h, i, 0)),
            pl.BlockSpec((1, nh, seq, HEAD_QK), lambda b, h, i: (b, h, 0, 0), pipeline_mode=pl.Buffered(1)),
            pl.BlockSpec((1, nh, seq // t, MLA_V, t), lambda b, h, i: (b, h, 0, 0, 0),
                         pipeline_mode=pl.Buffered(1)),
        ],
        out_specs=pl.BlockSpec((1, t, nh * MLA_V), lambda b, h, i: (b, i, h)),
        out_shape=jax.ShapeDtypeStruct((nb, seq, MLA_HEADS * MLA_V), BF),
        scratch_shapes=[pltpu.VMEM((nh, 1, t), F32), pltpu.VMEM((nh, MLA_V + DENOM_ROWS, t), F32)],
        compiler_params=_cp(("parallel", "parallel", "arbitrary"), vmem_mb=48),
        name="mla_flash",
    )(q, k, vt)


def _page_copies(pt_ref, ckv_hbm, kpt_hbm, ckbuf, kpbuf, sem, sample, slot, *, layer, n_pages):
    copies = []
    for p in range(n_pages):
        pg = pt_ref[sample, p]
        copies.append(pltpu.make_async_copy(
            ckv_hbm.at[layer, pg], ckbuf.at[slot, pl.ds(p * PAGE, PAGE), :], sem.at[0, slot]))
        copies.append(pltpu.make_async_copy(
            kpt_hbm.at[layer, pg], kpbuf.at[slot, :, pl.ds(p * PAGE, PAGE)], sem.at[1, slot]))
    return copies


def _decode_kernel(pt_ref, q_ref, nk_ref, ckv_hbm, kpt_hbm, o_ref, ckbuf, kpbuf, sem, *,
                   dec_seq, layer, n_pages):
    b = pl.program_id(0)
    last = pl.num_programs(0) - 1
    slot = b % 2
    copies = functools.partial(_page_copies, pt_ref, ckv_hbm, kpt_hbm, ckbuf, kpbuf, sem,
                               layer=layer, n_pages=n_pages)

    @pl.when(b == 0)
    def _():
        for c in copies(0, 0):
            c.start()

    for c in copies(b, slot):
        c.wait()
    nxt = jnp.minimum(b + 1, last)
    for c in copies(nxt, 1 - slot):
        c.start()

    rows = MLA_HEADS * dec_seq
    qm = q_ref[0]
    ql = qm[:, :KV_RANK]
    qp = qm[:, KV_RANK:]
    ck = ckbuf[slot].astype(BF)
    kp = kpbuf[slot].astype(BF)
    s = _dot_nt(ql, ck) + _dot(qp, kp)
    nk = nk_ref[0]
    kt = lax.broadcasted_iota(jnp.int32, (rows, NEW_KEY_ROWS), 1)
    qt = lax.broadcasted_iota(jnp.int32, (rows, NEW_KEY_ROWS), 0) % dec_seq
    s_new = jnp.where(kt <= qt, _dot_nt(qm, nk), NEG)
    m = jnp.maximum(jnp.max(s, axis=-1, keepdims=True), jnp.max(s_new, axis=-1, keepdims=True))
    p = jnp.exp(s - m)
    p_new = jnp.exp(s_new - m)
    denom = jnp.sum(p, axis=-1, keepdims=True) + jnp.sum(p_new, axis=-1, keepdims=True)
    o_ref[0] = (_dot(p.astype(BF), ck) + _dot(p_new.astype(BF), nk[:, :KV_RANK])) / denom

    @pl.when(b == last)
    def _():
        for c in copies(nxt, 1 - slot):
            c.wait()


def _decode(page_table, q, newk, cache_ckv, cache_kpe_t, layer, dec_seq):
    nb, n_pages = page_table.shape
    rows = MLA_HEADS * dec_seq
    keys = n_pages * PAGE
    grid_spec = pltpu.PrefetchScalarGridSpec(
        num_scalar_prefetch=1,
        grid=(nb,),
        in_specs=[
            pl.BlockSpec((1, rows, QK), lambda b, pt: (b, 0, 0)),
            pl.BlockSpec((1, NEW_KEY_ROWS, QK), lambda b, pt: (b, 0, 0)),
            pl.BlockSpec(memory_space=pl.ANY),
            pl.BlockSpec(memory_space=pl.ANY),
        ],
        out_specs=pl.BlockSpec((1, rows, KV_RANK), lambda b, pt: (b, 0, 0)),
        scratch_shapes=[pltpu.VMEM((2, keys, KV_RANK), F32), pltpu.VMEM((2, ROPE, keys), F32),
                        pltpu.SemaphoreType.DMA((2, 2))],
    )
    return pl.pallas_call(
        functools.partial(_decode_kernel, dec_seq=dec_seq, layer=layer, n_pages=n_pages),
        grid_spec=grid_spec,
        out_shape=jax.ShapeDtypeStruct((nb, rows, KV_RANK), F32),
        compiler_params=_cp(("arbitrary",), vmem_mb=48),
        name="mla_decode",
    )(page_table, q, newk, cache_ckv, cache_kpe_t)


def _mla_out_kernel(o_ref, wuv_ref, wo_ref, r_ref, out_ref):
    parts = [_dot(o_ref[0, h], wuv_ref[h]).astype(BF) for h in range(MLA_HEADS)]
    out_ref[...] = r_ref[...] + _dot(jnp.concatenate(parts, axis=1), wo_ref[...])


def _mla_out(o_lat, wuv, wo, res):
    nb, _, seq, _ = o_lat.shape
    tm = min(ROW_TILE, seq)
    tpb = seq // tm
    t = nb * seq
    return pl.pallas_call(
        _mla_out_kernel,
        grid=(t // tm,),
        in_specs=[
            pl.BlockSpec((1, MLA_HEADS, tm, KV_RANK), lambda i: (i // tpb, 0, i % tpb, 0)),
            _full(wuv.shape), _full(wo.shape),
            pl.BlockSpec((tm, D), lambda i: (i, 0)),
        ],
        out_specs=pl.BlockSpec((tm, D), lambda i: (i, 0)),
        out_shape=jax.ShapeDtypeStruct((t, D), F32),
        compiler_params=_cp(("parallel",)),
        name="mla_out",
    )(o_lat, wuv, wo, res)


def _softmax_rows(s):
    m = jnp.max(s, axis=-1, keepdims=True)
    p = jnp.exp(s - m)
    return p / jnp.sum(p, axis=-1, keepdims=True)


def _mem_prompt_kernel(x_ref, g_ref, wq_ref, k_ref, v_ref, wo_ref, o_ref):
    x = x_ref[...]
    xn = _rms(x, g_ref[...]).astype(BF)
    qm = (_dot(xn, wq_ref[...]) * MEM_HD ** -0.5).astype(BF)
    km = k_ref[0].astype(BF)
    vm = v_ref[0].astype(BF)
    parts = []
    for h in range(MEM_HEADS):
        sl = slice(h * MEM_HD, (h + 1) * MEM_HD)
        p = _softmax_rows(_dot_nt(qm[:, sl], km[:, sl]))
        parts.append(_dot(p.astype(BF), vm[:, sl]).astype(BF))
    o_ref[...] = x + _dot(jnp.concatenate(parts, axis=1), wo_ref[...])


def _mem_prompt(x, g, wq, km, vm, wo):
    t = x.shape[0]
    nb, mt, _ = km.shape
    seq = t // nb
    tm = min(ROW_TILE, seq)
    tpb = seq // tm
    return pl.pallas_call(
        _mem_prompt_kernel,
        grid=(t // tm,),
        in_specs=[
            pl.BlockSpec((tm, D), lambda i: (i, 0)), _full((1, D)), _full(wq.shape),
            pl.BlockSpec((1, mt, MEM_INNER), lambda i: (i // tpb, 0, 0)),
            pl.BlockSpec((1, mt, MEM_INNER), lambda i: (i // tpb, 0, 0)),
            _full(wo.shape),
        ],
        out_specs=pl.BlockSpec((tm, D), lambda i: (i, 0)),
        out_shape=jax.ShapeDtypeStruct((t, D), F32),
        compiler_params=_cp(("parallel",)),
        name="mem_attn_prompt",
    )(x, g, wq, km, vm, wo)


MEM_ROWS = MEM_HEADS * SUBLANES


def _mem_sample_kernel(x_ref, g_ref, wq_ref, k_ref, v_ref, wo_ref, o_ref):
    ns = SAMPLES_PER_STEP
    x = x_ref[...].reshape(ns * SUBLANES, D)
    xn = _rms(x, g_ref[...]).astype(BF)
    qall = _dot(xn, wq_ref[...]) * MEM_HD ** -0.5
    cols = k_ref.shape[2]
    head_of_row = lax.broadcasted_iota(jnp.int32, (MEM_ROWS, cols), 0) // SUBLANES
    head_of_col = lax.broadcasted_iota(jnp.int32, (MEM_ROWS, cols), 1) % MEM_HEADS
    own = head_of_row == head_of_col
    outs = []
    for s in range(ns):
        qs = qall[s * SUBLANES:(s + 1) * SUBLANES]
        qst = jnp.concatenate([qs[:, h * MEM_HD:(h + 1) * MEM_HD] for h in range(MEM_HEADS)], axis=0)
        sc = jnp.where(own, _dot_nt(qst.astype(BF), k_ref[0, s].astype(BF)), NEG)
        o = _dot(_softmax_rows(sc).astype(BF), v_ref[0, s].astype(BF))
        outs.append(jnp.concatenate([o[h * SUBLANES:(h + 1) * SUBLANES] for h in range(MEM_HEADS)], axis=1))
    out = x + _dot(jnp.concatenate(outs, axis=0).astype(BF), wo_ref[...])
    o_ref[...] = out.reshape(ns, SUBLANES, D)


def _mem_sample(x8, g, wq, cache_k, cache_v, wo, layer):
    nb = x8.shape[0]
    rows = cache_k.shape[2]
    ns = SAMPLES_PER_STEP
    return pl.pallas_call(
        _mem_sample_kernel,
        grid=(nb // ns,),
        in_specs=[
            pl.BlockSpec((ns, SUBLANES, D), lambda i: (i, 0, 0)), _full((1, D)), _full(wq.shape),
            pl.BlockSpec((1, ns, rows, MEM_HD), lambda i: (layer, i, 0, 0)),
            pl.BlockSpec((1, ns, rows, MEM_HD), lambda i: (layer, i, 0, 0)),
            _full(wo.shape),
        ],
        out_specs=pl.BlockSpec((ns, SUBLANES, D), lambda i: (i, 0, 0)),
        out_shape=jax.ShapeDtypeStruct((nb, SUBLANES, D), F32),
        compiler_params=_cp(("parallel",), vmem_mb=48),
        name="mem_attn_sample",
    )(x8, g, wq, cache_k, cache_v, wo)


def _router_kernel(x_ref, g_ref, wr_ref, xn_ref, route_ref):
    xn = _rms(x_ref[...], g_ref[...])
    xn_ref[...] = xn
    lane = lax.broadcasted_iota(jnp.int32, (xn.shape[0], LANES), 1).astype(F32)
    lg = jnp.where(lane < N_EXP, _dot_f32(xn, wr_ref[...]), NEG)
    m1 = jnp.max(lg, axis=-1, keepdims=True)
    i1 = jnp.min(jnp.where(lg == m1, lane, float(LANES)), axis=-1, keepdims=True)
    lg2 = jnp.where(lane == i1, NEG, lg)
    m2 = jnp.max(lg2, axis=-1, keepdims=True)
    i2 = jnp.min(jnp.where(lg2 == m2, lane, float(LANES)), axis=-1, keepdims=True)
    e = jnp.exp(m2 - m1)
    g1 = 1.0 / (1.0 + e)
    g2 = e * g1
    route_ref[...] = jnp.where(lane == 0, i1, jnp.where(lane == 1, i2, jnp.where(lane == 2, g1,
                               jnp.where(lane == 3, g2, 0.0))))


def _router(x, g, wr_pad):
    t = x.shape[0]
    tm = min(ROW_TILE, t)
    return pl.pallas_call(
        _router_kernel,
        grid=(t // tm,),
        in_specs=[pl.BlockSpec((tm, D), lambda i: (i, 0)), _full((1, D)), _full(wr_pad.shape)],
        out_specs=[pl.BlockSpec((tm, D), lambda i: (i, 0)), pl.BlockSpec((tm, LANES), lambda i: (i, 0))],
        out_shape=[jax.ShapeDtypeStruct((t, D), F32), jax.ShapeDtypeStruct((t, LANES), F32)],
        compiler_params=_cp(("parallel",)),
        name="router",
    )(x, g, wr_pad)


SC_ROWS = 32


def _sc_mesh():
    return plsc.VectorSubcoreMesh(core_axis_name="c", subcore_axis_name="s")


def _sc_workers():
    info = plsc.get_sparse_core_info()
    return info.num_cores, info.num_cores * info.num_subcores


def _sc_gather_rows(table, idx):
    n = idx.shape[0]
    width = table.shape[1]
    ncores, nw = _sc_workers()
    per_w = n // nw
    n_chunks = per_w // SC_ROWS
    assert per_w * nw == n and n_chunks * SC_ROWS == per_w

    @functools.partial(
        pl.kernel, mesh=_sc_mesh(),
        out_type=jax.ShapeDtypeStruct((n, width), table.dtype),
        scratch_types=[pltpu.VMEM((per_w,), jnp.int32), pltpu.VMEM((SC_ROWS, width), table.dtype),
                       pltpu.SemaphoreType.DMA],
        name="sc_gather_rows",
    )
    def body(table_hbm, idx_hbm, out_hbm, idx_v, rows_v, sem):
        wid = lax.axis_index("s") * ncores + lax.axis_index("c")
        base = wid * per_w
        pltpu.sync_copy(idx_hbm.at[pl.ds(base, per_w)], idx_v)

        @pl.loop(0, n_chunks)
        def _(j):
            off = pl.multiple_of(j * SC_ROWS, SC_ROWS)
            pltpu.async_copy(table_hbm.at[idx_v.at[pl.ds(off, SC_ROWS)]], rows_v, sem).wait()
            pltpu.sync_copy(rows_v, out_hbm.at[pl.ds(base + off, SC_ROWS)])

    return body(table, idx)


def _sc_scatter_rows(src, idx, n_out):
    n = idx.shape[0]
    t, width = src.shape
    ncores, nw = _sc_workers()
    per_w = n // nw
    n_chunks = per_w // SC_ROWS
    assert per_w * nw == n and n_chunks * SC_ROWS == per_w and t % per_w == 0
    idx3 = idx.reshape(nw, n_chunks, SC_ROWS)

    @functools.partial(
        pl.kernel, mesh=_sc_mesh(),
        out_type=jax.ShapeDtypeStruct((n_out, width), src.dtype),
        scratch_types=[pltpu.VMEM((n_chunks, SC_ROWS), jnp.int32), pltpu.VMEM((SC_ROWS, width), src.dtype),
                       pltpu.SemaphoreType.DMA],
        name="sc_scatter_rows",
    )
    def body(src_hbm, idx_hbm, out_hbm, idx_v, rows_v, sem):
        wid = lax.axis_index("s") * ncores + lax.axis_index("c")
        base = lax.rem(wid * per_w, t)
        pltpu.sync_copy(idx_hbm.at[wid], idx_v)

        @pl.loop(0, n_chunks)
        def _(j):
            off = pl.multiple_of(j * SC_ROWS, SC_ROWS)
            pltpu.sync_copy(src_hbm.at[pl.ds(base + off, SC_ROWS)], rows_v)
            pltpu.async_copy(rows_v, out_hbm.at[idx_v.at[j]], sem).wait()

    return body(src, idx3)


MOE_TILE = 1024
MOE_TILE_SMALL = 256
MOE_BLOCK = 512
MOE_BLOCK_SMALL = 896
MOE_CHUNK = 256


def _moe_ffn_kernel(te_ref, nused_ref, x_ref, wg_ref, wu_ref, wd_ref, o_ref, xb_ref):
    del te_ref
    i = pl.program_id(0)
    j = pl.program_id(1)

    @pl.when(i < nused_ref[0])
    def _():
        @pl.when(j == 0)
        def _():
            xb_ref[...] = x_ref[...].astype(BF)

        xb = xb_ref[...]
        part = None
        block = wd_ref.shape[2]
        chunk = MOE_CHUNK if block % MOE_CHUNK == 0 else block
        for c in range(block // chunk):
            sl = slice(c * chunk, (c + 1) * chunk)
            gate = _dot(xb, wg_ref[0, 0, :, sl].astype(BF))
            up = _dot(xb, wu_ref[0, 0, :, sl].astype(BF))
            contrib = _dot((_silu(gate) * up).astype(BF), wd_ref[0, 0, sl, :].astype(BF))
            part = contrib if part is None else part + contrib

        @pl.when(j == 0)
        def _():
            o_ref[...] = part

        @pl.when(j != 0)
        def _():
            o_ref[...] += part

    @pl.when(i >= nused_ref[0])
    def _():
        o_ref[...] = jnp.zeros_like(o_ref)


def _moe_ffn(tile_expert, nused, xs, wgu, wd, layer, *, tm):
    npad = xs.shape[0]
    block = MOE_BLOCK if tm == MOE_TILE else MOE_BLOCK_SMALL
    nblk = D_FFE // block

    def blk(i, j, nu):
        return jnp.where(i < nu[0], j, nblk - 1)

    grid_spec = pltpu.PrefetchScalarGridSpec(
        num_scalar_prefetch=2,
        grid=(npad // tm, nblk),
        in_specs=[
            pl.BlockSpec((tm, D), lambda i, j, te, nu: (jnp.minimum(i, nu[0] - 1), 0)),
            pl.BlockSpec((1, 1, D, block), lambda i, j, te, nu: (layer, te[i], 0, blk(i, j, nu))),
            pl.BlockSpec((1, 1, D, block), lambda i, j, te, nu: (layer, te[i], 0, nblk + blk(i, j, nu))),
            pl.BlockSpec((1, 1, block, D), lambda i, j, te, nu: (layer, te[i], blk(i, j, nu), 0)),
        ],
        out_specs=pl.BlockSpec((tm, D), lambda i, j, te, nu: (i, 0)),
        scratch_shapes=[pltpu.VMEM((tm, D), BF)],
    )
    return pl.pallas_call(
        _moe_ffn_kernel,
        grid_spec=grid_spec,
        out_shape=jax.ShapeDtypeStruct((npad, D), F32),
        compiler_params=_cp(("parallel", "arbitrary"), vmem_mb=56),
        name="moe_ffn",
    )(tile_expert, nused, xs, wgu, wgu, wd)


ROUTE_GATE_LANE = 2


def _combine_kernel(h_ref, y0_ref, y1_ref, route_ref, g_ref, o_ref, *, final):
    gl = ROUTE_GATE_LANE
    route = route_ref[...]
    out = h_ref[...] + route[:, gl:gl + 1] * y0_ref[...] + route[:, gl + 1:gl + 2] * y1_ref[...]
    if final:
        out = _rms(out, g_ref[...])
    o_ref[...] = out


def _combine(h, y2, route, g, *, final):
    t = h.shape[0]
    tm = min(ROW_TILE, t)
    nt = t // tm
    return pl.pallas_call(
        functools.partial(_combine_kernel, final=final),
        grid=(nt,),
        in_specs=[
            pl.BlockSpec((tm, D), lambda i: (i, 0)),
            pl.BlockSpec((tm, D), lambda i: (i, 0)),
            pl.BlockSpec((tm, D), lambda i: (nt + i, 0)),
            pl.BlockSpec((tm, LANES), lambda i: (i, 0)),
            _full((1, D)),
        ],
        out_specs=pl.BlockSpec((tm, D), lambda i: (i, 0)),
        out_shape=jax.ShapeDtypeStruct((t, D), F32),
        compiler_params=_cp(("parallel",)),
        name="moe_combine",
    )(h, y2, y2, route, g)


def _moe(h, g, wr_pad, wgu, wd, layer, final_g, *, final):
    t = h.shape[0]
    tm = MOE_TILE if 2 * t >= 2 * N_EXP * MOE_TILE else MOE_TILE_SMALL
    xn, route = _router(h, g, wr_pad)
    eidx = route[:, :ROUTE_GATE_LANE].astype(jnp.int32)
    e_flat = eidx.T.reshape(-1)
    onehot = (e_flat[:, None] == jnp.arange(N_EXP, dtype=jnp.int32)[None, :]).astype(jnp.int32)
    csum = jnp.cumsum(onehot, axis=0)
    counts = csum[-1]
    rank = jnp.sum(onehot * csum, axis=1) - 1
    padded = ((counts + tm - 1) // tm) * tm
    ends = jnp.cumsum(padded)
    starts = ends - padded
    dest = (jnp.sum(onehot * starts[None, :], axis=1) + rank).astype(jnp.int32)
    n_tiles = -(-2 * t // tm) + N_EXP
    tile_start = jnp.arange(n_tiles, dtype=jnp.int32) * tm
    tile_expert = jnp.minimum(jnp.sum((tile_start[:, None] >= ends[None, :]).astype(jnp.int32), axis=1),
                              N_EXP - 1).astype(jnp.int32)
    nused = (ends[-1] // tm).astype(jnp.int32).reshape(1)
    tile_expert = jnp.where(jnp.arange(n_tiles) < nused[0], tile_expert, tile_expert[nused[0] - 1])
    xs = _sc_scatter_rows(xn, dest, n_tiles * tm)
    ys = _moe_ffn(tile_expert, nused, xs, wgu, wd, layer, tm=tm)
    y2 = _sc_gather_rows(ys, dest)
    return _combine(h, y2, route, final_g, final=final)


def _rope_tables(pos):
    half = ROPE // 2
    inv_freq = ROPE_BASE ** (-jnp.arange(half, dtype=F32) / half)
    ang = pos.astype(F32)[:, None] * inv_freq
    cos = jnp.cos(ang)
    sin = jnp.sin(ang)
    cc = jnp.concatenate([cos, cos], axis=-1)
    ss = jnp.concatenate([-sin, sin], axis=-1)
    return jnp.tile(cc, (1, MLA_HEADS)), jnp.tile(ss, (1, MLA_HEADS))


def _even_layer(h, g, prm, cinit, sinit, layer, *, nb, sample, state_layers=1, state_prev=None):
    t = h.shape[0]
    seq = t // nb
    proj = _rms_matmul(h, g, prm["w_zxu"], tn=_largest_tile(ZXU, 1536))
    dtp = _rms_matmul(h, g, prm["w_dt"], tn=LANES, precise=True)
    proj = proj.reshape(nb, seq, ZXU)
    dtp = dtp.reshape(nb, seq, LANES)
    if sample:
        pad = ((0, 0), (0, SUBLANES - seq), (0, 0))
        proj = jnp.pad(proj, pad)
        dtp = jnp.pad(dtp, pad)
        q, lb, lc = SAMPLE_Q, SUBLANES, seq
    else:
        q, lb, lc = CHUNK, CHUNK, CHUNK
    cinit8 = jnp.pad(cinit, ((0, 0), (SUBLANES - (SSD_CONV - 1), 0), (0, 0)))
    outs = _even_mixer(proj, dtp, cinit8, sinit.reshape(-1, nb, SSD_INNER, SSD_STATE), prm,
                       q=q, lb=lb, lc=lc, want_v=sample, layer=layer,
                       state_layers=state_layers, state_prev=state_prev)
    ymix, cout, sout = outs[:3]
    v = None
    if sample:
        ymix = ymix[:, :seq]
        v = outs[3][:, :seq]
    h = _matmul_res(ymix.reshape(t, SSD_INNER + GMLP_WIDTH), prm["w_out"], h)
    return h, cout, sout, v


def _prep_even(i, w_in, conv_w, conv_b, dt_bias, a_log, d_skip, ssd_gain, ln_g, ln_b, ws, bs, w_out):
    w = w_in[i]
    o1 = SSD_INNER + CONV_DIM
    w_zxu = jnp.concatenate([w[:, :o1], w[:, o1 + SSD_HEADS:]], axis=1).astype(BF)
    w_dt = jnp.pad(w[:, o1:o1 + SSD_HEADS], ((0, 0), (0, LANES - SSD_HEADS)))
    padl = (0, LANES - SSD_HEADS)
    return dict(
        w_zxu=w_zxu, w_dt=w_dt,
        conv_w=jnp.pad(conv_w[i], ((0, SUBLANES - SSD_CONV), (0, 0))),
        conv_b=conv_b[i][None, :],
        dt_bias=jnp.pad(dt_bias[i], padl)[None, :],
        a_log=jnp.pad(a_log[i], padl)[None, :],
        d_skip=jnp.repeat(d_skip[i], SSD_HEAD_DIM)[None, :],
        ssd_gain=ssd_gain[i][None, :],
        ln_g=ln_g[i][None, :], ln_b=ln_b[i][None, :],
        ws=ws[i], bst=bs[i].T,
        expand=(jnp.arange(LANES)[:, None] == jnp.arange(SSD_INNER)[None, :] // SSD_HEAD_DIM).astype(BF),
        w_out=w_out[i].astype(BF),
    )


def _prep_mla(i, w_down, q_gain, kv_gain, w_uq, w_uk, w_uv, w_o):
    wd = w_down[i]
    wk = wd[:, Q_RANK + KV_RANK:]
    half = ROPE // 2
    rot = lambda a: jnp.concatenate([a[..., half:], a[..., :half]], axis=-1)
    uq = w_uq[i]
    uq_pe = uq[:, :, NOPE:]
    return dict(
        wdq=wd[:, :Q_RANK].astype(BF),
        wdkv=wd[:, Q_RANK:Q_RANK + KV_RANK].astype(BF),
        wdk2=jnp.concatenate([wk, rot(wk)], axis=1).astype(BF),
        q_gain=q_gain[i][None, :], kv_gain=kv_gain[i][None, :],
        wqn=uq[:, :, :NOPE].reshape(Q_RANK, MLA_HEADS * NOPE).astype(BF),
        wqp=jnp.concatenate([uq_pe.reshape(Q_RANK, -1), rot(uq_pe).reshape(Q_RANK, -1)], axis=1).astype(BF),
        wuk=jnp.transpose(w_uk[i], (1, 2, 0)).astype(BF),
        wuv=jnp.transpose(w_uv[i], (1, 0, 2)).astype(BF),
        wkn=w_uk[i].reshape(KV_RANK, MLA_HEADS * NOPE).astype(BF),
        wv=w_uv[i].reshape(KV_RANK, MLA_HEADS * MLA_V).T.astype(BF),
        wo=w_o[i].astype(BF),
    )


def kernel(x_prompt, x_sample, state_ssd, state_conv, cache_mla_ckv, cache_mla_kpe, cache_mem_k, cache_mem_v, page_table, mem_prompt, mix_norm, w_in, conv_w, conv_b, dt_bias, a_log, d_skip, ssd_gain, gmlp_ln_g, gmlp_ln_b, gmlp_ws, gmlp_bs, w_out_even, w_mla_down, mla_q_gain, mla_kv_gain, w_mla_uq, w_mla_uk, w_mla_uv, w_mla_o, xattn_norm, mem_norm, w_mem_q, w_mem_k, w_mem_v, w_mem_o, ffn_norm, w_ffn_gu, w_ffn_down, w_router, w_exp_gu, w_exp_down, final_norm):
    nbp, seq, _ = x_prompt.shape
    nbs, dseq, _ = x_sample.shape
    depth = mix_norm.shape[0]
    past = page_table.shape[1] * PAGE
    mt = mem_prompt.shape[1]
    hp = x_prompt.reshape(nbp * seq, D)
    hs = x_sample.reshape(nbs * dseq, D)
    cos_p, sin_p = _rope_tables(jnp.arange(seq, dtype=jnp.int32))
    cos_s, sin_s = _rope_tables(past + jnp.arange(dseq, dtype=jnp.int32))
    cos_s = jnp.tile(cos_s, (nbs, 1))
    sin_s = jnp.tile(sin_s, (nbs, 1))
    cache_k4 = cache_mem_k.reshape(depth, nbs, mt * MEM_HEADS, MEM_HD)
    cache_v4 = cache_mem_v.reshape(depth, nbs, mt * MEM_HEADS, MEM_HD)
    cache_kpe_t = jnp.swapaxes(cache_mla_kpe, 2, 3)
    final_g = final_norm[None, :]

    p_ssd, p_conv, p_ckv, p_kpe, p_mk, p_mv = [], [], [], [], [], []
    s_conv, s_v, s_ckv, s_kpe = [], [], [], []
    n_even = (depth + 1) // 2
    s_state = None
    for l in range(depth):
        i = l // 2
        g_mix = mix_norm[l][None, :]
        if l % 2 == 0:
            prm = _prep_even(i, w_in, conv_w, conv_b, dt_bias, a_log, d_skip, ssd_gain, gmlp_ln_g,
                             gmlp_ln_b, gmlp_ws, gmlp_bs, w_out_even)
            buf0 = jnp.zeros((nbp, SSD_CONV - 1, CONV_DIM), F32)
            h00 = jnp.zeros((1, nbp, SSD_HEADS, SSD_HEAD_DIM, SSD_STATE), F32)
            hp, buf_p, ssd_p, _ = _even_layer(hp, g_mix, prm, buf0, h00, 0, nb=nbp, sample=False)
            hs, buf_s, s_state, v_s = _even_layer(hs, g_mix, prm, state_conv[i], state_ssd, i, nb=nbs,
                                                  sample=True, state_layers=n_even, state_prev=s_state)
            p_ssd.append(ssd_p.reshape(nbp, SSD_HEADS, SSD_HEAD_DIM, SSD_STATE))
            p_conv.append(buf_p)
            s_conv.append(buf_s)
            s_v.append(v_s)
        else:
            prm = _prep_mla(i, w_mla_down, mla_q_gain, mla_kv_gain, w_mla_uq, w_mla_uk, w_mla_uv, w_mla_o)
            ckv, kpe, qh, kh, vh = _mla_proj_prompt(hp, g_mix, prm, cos_p, sin_p, nb=nbp)
            o_p = _flash(qh, kh, vh)
            hp = _matmul_res(o_p.reshape(nbp * seq, MLA_HEADS * MLA_V), prm["wo"], hp)
            p_ckv.append(ckv.reshape(nbp, seq, KV_RANK))
            p_kpe.append(kpe.reshape(nbp, seq, ROPE))

            ckv_s, kpe_s, kcat_s, q_s = _mla_proj(hs, g_mix, prm, cos_s, sin_s, nb=1)
            q_s = q_s[0].reshape(MLA_HEADS, nbs, dseq, QK).transpose(1, 0, 2, 3).reshape(nbs, MLA_HEADS * dseq, QK)
            newk = jnp.pad(kcat_s.reshape(nbs, dseq, QK), ((0, 0), (0, NEW_KEY_ROWS - dseq), (0, 0)))
            o_s = _decode(page_table, q_s, newk, cache_mla_ckv, cache_kpe_t, i, dseq)
            o_s = o_s.reshape(nbs, MLA_HEADS, dseq, KV_RANK).transpose(1, 0, 2, 3)
            o_s = o_s.reshape(1, MLA_HEADS, nbs * dseq, KV_RANK).astype(BF)
            hs = _mla_out(o_s, prm["wuv"], prm["wo"], hs)
            s_ckv.append(ckv_s.reshape(nbs, dseq, KV_RANK))
            s_kpe.append(kpe_s.reshape(nbs, dseq, ROPE))

        wkv = jnp.concatenate([w_mem_k[l], w_mem_v[l]], axis=1).astype(BF)
        kv = _rms_matmul(mem_prompt.reshape(nbp * mt, D), mem_norm[l][None, :], wkv, tn=MEM_INNER)
        mk_p = kv[:, :MEM_INNER].reshape(nbp, mt, MEM_INNER)
        mv_p = kv[:, MEM_INNER:].reshape(nbp, mt, MEM_INNER)
        g_x = xattn_norm[l][None, :]
        wq = w_mem_q[l].astype(BF)
        wo = w_mem_o[l].astype(BF)
        hp = _mem_prompt(hp, g_x, wq, mk_p, mv_p, wo)
        x8 = jnp.pad(hs.reshape(nbs, dseq, D), ((0, 0), (0, SUBLANES - dseq), (0, 0)))
        hs = _mem_sample(x8, g_x, wq, cache_k4, cache_v4, wo, l)[:, :dseq].reshape(nbs * dseq, D)
        p_mk.append(mk_p.reshape(nbp, mt, MEM_HEADS, MEM_HD))
        p_mv.append(mv_p.reshape(nbp, mt, MEM_HEADS, MEM_HD))

        g_f = ffn_norm[l][None, :]
        if l % 2 == 0:
            wgu = w_ffn_gu[i].astype(BF)
            wd = w_ffn_down[i].astype(BF)
            hp = _ffn(hp, g_f, wgu, wd)
            hs = _ffn(hs, g_f, wgu, wd)
        else:
            wr = jnp.pad(w_router[i], ((0, 0), (0, LANES - N_EXP)))
            final = l == depth - 1
            hp = _moe(hp, g_f, wr, w_exp_gu, w_exp_down, i, final_g, final=final)
            hs = _moe(hs, g_f, wr, w_exp_gu, w_exp_down, i, final_g, final=final)
    if depth % 2 == 1:
        raise NotImplementedError("the final norm is fused into the last routed-expert layer")
    y_prompt = hp.reshape(nbp, seq, D)
    y_sample = hs.reshape(nbs, dseq, D)
    return (y_prompt, y_sample,
            jnp.stack(p_ssd), jnp.stack(p_conv), jnp.stack(p_ckv), jnp.stack(p_kpe),
            jnp.stack(p_mk), jnp.stack(p_mv),
            s_state.reshape(n_even, nbs, SSD_HEADS, SSD_HEAD_DIM, SSD_STATE), jnp.stack(s_conv), jnp.stack(s_v), jnp.stack(s_ckv), jnp.stack(s_kpe))
```

```python
import functools

import jax
import jax.numpy as jnp
from jax import lax
from jax.experimental import pallas as pl
from jax.experimental.pallas import tpu as pltpu
from jax.experimental.pallas import tpu_sc as plsc

F32 = jnp.float32
BF = jnp.bfloat16
EPS = 1e-6
NEG = -1e30

D = 1024
SSD_HEADS = 16
SSD_HEAD_DIM = 64
SSD_INNER = SSD_HEADS * SSD_HEAD_DIM
SSD_GROUPS = 2
SSD_STATE = 128
SSD_CONV = 4
CONV_DIM = SSD_INNER + 2 * SSD_GROUPS * SSD_STATE
GMLP_GROUPS = 8
GMLP_WIDTH = 1024
CHUNK = 128
EVEN_SEQS_PER_STEP = 2
SAMPLE_Q = 16
ZXU = SSD_INNER + CONV_DIM + 2 * GMLP_WIDTH
MLA_HEADS = 8
NOPE = 128
ROPE = 64
MLA_V = 128
Q_RANK = 256
KV_RANK = 256
MLA_SCALE = (NOPE + ROPE) ** -0.5
QK = KV_RANK + ROPE
ROPE_BASE = 10000.0
MEM_HEADS = 4
MEM_HD = 128
MEM_INNER = MEM_HEADS * MEM_HD
D_FF = 2816
N_EXP = 8
D_FFE = 3584
PAGE = 128
LANES = 128
SUBLANES = 8
ROW_TILE = 512
DECODE_SLICES = 4
NEW_KEY_ROWS = 16
SAMPLES_PER_STEP = 8


def _cp(sem, vmem_mb=None):
    kw = dict(dimension_semantics=sem)
    if vmem_mb is not None:
        kw["vmem_limit_bytes"] = vmem_mb * 1024 * 1024
    return pltpu.CompilerParams(**kw)


def _rms(x, g):
    return x * lax.rsqrt(jnp.mean(x * x, axis=-1, keepdims=True) + EPS) * g


def _dot(a, b):
    return jnp.dot(a, b, preferred_element_type=F32)


def _dot_nt(a, b):
    return lax.dot_general(a, b, (((1,), (1,)), ((), ())), preferred_element_type=F32)


def _dot_f32(a, b):
    return jnp.dot(a, b, preferred_element_type=F32, precision=lax.Precision.HIGHEST)


def _silu(x):
    return x * jax.nn.sigmoid(x)


def _full(shape):
    n = len(shape)
    return pl.BlockSpec(shape, lambda *_: (0,) * n)


def _largest_tile(n, cap):
    best = LANES
    for t in range(LANES, cap + 1, LANES):
        if n % t == 0:
            best = t
    return best


def _rms_matmul_kernel(x_ref, g_ref, w_ref, o_ref, xn_ref):
    @pl.when(pl.program_id(1) == 0)
    def _():
        xn_ref[...] = _rms(x_ref[...], g_ref[...]).astype(BF)

    o_ref[...] = _dot(xn_ref[...], w_ref[...])


def _rms_matmul(x, g, w, *, tn):
    t, k = x.shape
    n = w.shape[1]
    tm = min(ROW_TILE, t)
    return pl.pallas_call(
        _rms_matmul_kernel,
        grid=(t // tm, n // tn),
        in_specs=[
            pl.BlockSpec((tm, k), lambda i, j: (i, 0)),
            pl.BlockSpec((1, k), lambda i, j: (0, 0)),
            pl.BlockSpec((k, tn), lambda i, j: (0, j)),
        ],
        out_specs=pl.BlockSpec((tm, tn), lambda i, j: (i, j)),
        out_shape=jax.ShapeDtypeStruct((t, n), F32),
        scratch_shapes=[pltpu.VMEM((tm, k), BF)],
        compiler_params=_cp(("parallel", "arbitrary")),
        name="rms_matmul",
    )(x, g, w)


def _even_proj_kernel(x_ref, g_ref, w_ref, wdt_ref, o_ref, dt_ref, xn_ref):
    @pl.when(pl.program_id(1) == 0)
    def _():
        xn = _rms(x_ref[...], g_ref[...])
        xn_ref[...] = xn.astype(BF)
        dt_ref[...] = _dot_f32(xn, wdt_ref[...])

    o_ref[...] = _dot(xn_ref[...], w_ref[...])


def _even_proj(x, g, w, wdt, *, tn):
    t, k = x.shape
    n = w.shape[1]
    tm = min(ROW_TILE, t)
    return pl.pallas_call(
        _even_proj_kernel,
        grid=(t // tm, n // tn),
        in_specs=[
            pl.BlockSpec((tm, k), lambda i, j: (i, 0)),
            pl.BlockSpec((1, k), lambda i, j: (0, 0)),
            pl.BlockSpec((k, tn), lambda i, j: (0, j)),
            pl.BlockSpec((k, LANES), lambda i, j: (0, 0)),
        ],
        out_specs=[pl.BlockSpec((tm, tn), lambda i, j: (i, j)), pl.BlockSpec((tm, LANES), lambda i, j: (i, 0))],
        out_shape=[jax.ShapeDtypeStruct((t, n), F32), jax.ShapeDtypeStruct((t, LANES), F32)],
        scratch_shapes=[pltpu.VMEM((tm, k), BF)],
        compiler_params=_cp(("parallel", "arbitrary")),
        name="even_proj",
    )(x, g, w, wdt)


def _matmul_res_kernel(a_ref, w_ref, r_ref, o_ref):
    o_ref[...] = r_ref[...] + _dot(a_ref[...].astype(BF), w_ref[...])


def _matmul_res(a, w, res):
    t, k = a.shape
    n = w.shape[1]
    tn = _largest_tile(n, 1024)
    tm = min(ROW_TILE, t)
    return pl.pallas_call(
        _matmul_res_kernel,
        grid=(t // tm, n // tn),
        in_specs=[
            pl.BlockSpec((tm, k), lambda i, j: (i, 0)),
            pl.BlockSpec((k, tn), lambda i, j: (0, j)),
            pl.BlockSpec((tm, tn), lambda i, j: (i, j)),
        ],
        out_specs=pl.BlockSpec((tm, tn), lambda i, j: (i, j)),
        out_shape=jax.ShapeDtypeStruct((t, n), F32),
        compiler_params=_cp(("parallel", "arbitrary")),
        name="matmul_res",
    )(a, w, res)


FF_CHUNK = 256


def _ffn_kernel(x_ref, g_ref, wgu_ref, wd_ref, o_ref, hid_ref, *, ff):
    x = x_ref[...]
    xn = _rms(x, g_ref[...]).astype(BF)
    for c in range(ff // FF_CHUNK):
        lo = c * FF_CHUNK
        gate = _dot(xn, wgu_ref[:, lo:lo + FF_CHUNK])
        up = _dot(xn, wgu_ref[:, ff + lo:ff + lo + FF_CHUNK])
        hid_ref[:, lo:lo + FF_CHUNK] = (_silu(gate) * up).astype(BF)
    o_ref[...] = x + _dot(hid_ref[...], wd_ref[...])


def _ffn(x, g, wgu, wd):
    t = x.shape[0]
    ff = wd.shape[0]
    tm = min(ROW_TILE, t)
    return pl.pallas_call(
        functools.partial(_ffn_kernel, ff=ff),
        grid=(t // tm,),
        in_specs=[
            pl.BlockSpec((tm, D), lambda i: (i, 0)),
            _full((1, D)),
            pl.BlockSpec((D, 2 * ff), lambda i: (0, 0), pipeline_mode=pl.Buffered(1)),
            pl.BlockSpec((ff, D), lambda i: (0, 0), pipeline_mode=pl.Buffered(1)),
        ],
        out_specs=pl.BlockSpec((tm, D), lambda i: (i, 0)),
        out_shape=jax.ShapeDtypeStruct((t, D), F32),
        scratch_shapes=[pltpu.VMEM((tm, ff), BF)],
        compiler_params=_cp(("parallel",), vmem_mb=48),
        name="ffn",
    )(x, g, wgu, wd)


def _softplus(x):
    return jnp.maximum(x, 0.0) + jnp.log1p(jnp.exp(-jnp.abs(x)))


def _gelu_tanh(x):
    return 0.5 * x * (1.0 + jnp.tanh(0.7978845608028654 * (x + 0.044715 * (x * x * x))))


def _even_kernel(proj_ref, dt_ref, cinit_ref, sinit_ref, cw_ref, cb_ref, dtb_ref, alog_ref,
                 dsk_ref, sg_ref, lng_ref, lnb_ref, ws_ref, bst_ref, e_ref,
                 ymix_ref, cout_ref, sout_ref, v_ref, ext_ref, ht_ref, *, q, lb, lc, nbb, out_slot):
    c = pl.program_id(1)

    @pl.when(c == 0)
    def _():
        for bb in range(nbb):
            ext_ref[bb, 0:SUBLANES, :] = cinit_ref[bb]
            ht_ref[bb] = sinit_ref[0, bb].T

    for bb in range(nbb):
        _even_block(proj_ref, dt_ref, cw_ref, cb_ref, dtb_ref, alog_ref,
                    dsk_ref, sg_ref, lng_ref, lnb_ref, ws_ref, bst_ref, e_ref,
                    ymix_ref, v_ref, ext_ref, ht_ref, bb, q=q, lb=lb, lc=lc)

    @pl.when(c == pl.num_programs(1) - 1)
    def _():
        for bb in range(nbb):
            cout_ref[bb] = ext_ref[bb, SUBLANES + lc - 3:SUBLANES + lc, :]
            for slot in range(sout_ref.shape[0]):
                if slot == out_slot:
                    sout_ref[slot, bb] = ht_ref[bb].T
                else:
                    sout_ref[slot, bb] = jnp.zeros((SSD_INNER, SSD_STATE), F32)


def _even_block(proj_ref, dt_ref, cw_ref, cb_ref, dtb_ref, alog_ref,
                dsk_ref, sg_ref, lng_ref, lnb_ref, ws_ref, bst_ref, e_ref,
                ymix_ref, v_ref, ext_ref, ht_ref, bb, *, q, lb, lc):

    if lb == q:
        p = proj_ref[bb]
        dtr = dt_ref[bb]
    else:
        p = jnp.concatenate([proj_ref[bb], jnp.zeros((q - lb, ZXU), F32)], axis=0)
        dtr = jnp.concatenate([dt_ref[bb], jnp.zeros((q - lb, LANES), F32)], axis=0)
    z = p[:, :SSD_INNER]
    xbc_raw = p[:, SSD_INNER:SSD_INNER + CONV_DIM]
    uv = p[:, SSD_INNER + CONV_DIM:]

    ext_ref[bb, SUBLANES:SUBLANES + q, :] = xbc_raw
    conv = (cb_ref[...] + cw_ref[0:1, :] * ext_ref[bb, 5:5 + q, :] + cw_ref[1:2, :] * ext_ref[bb, 6:6 + q, :]
            + cw_ref[2:3, :] * ext_ref[bb, 7:7 + q, :] + cw_ref[3:4, :] * xbc_raw)
    ext_ref[bb, 0:SUBLANES, :] = ext_ref[bb, q:q + SUBLANES, :]

    xbc = _silu(conv)
    xs = xbc[:, :SSD_INNER]
    gw = SSD_STATE
    bm = [xbc[:, SSD_INNER + g * gw:SSD_INNER + (g + 1) * gw] for g in range(SSD_GROUPS)]
    cm = [xbc[:, SSD_INNER + (SSD_GROUPS + g) * gw:SSD_INNER + (SSD_GROUPS + g + 1) * gw]
          for g in range(SSD_GROUPS)]

    row = lax.broadcasted_iota(jnp.int32, (q, q), 0)
    col = lax.broadcasted_iota(jnp.int32, (q, q), 1)
    causal = row >= col

    dt = _softplus(dtr + dtb_ref[...])
    if lc < q:
        dt = jnp.where(lax.broadcasted_iota(jnp.int32, (q, LANES), 0) < lc, dt, 0.0)
    a = dt * (-jnp.exp(alog_ref[...]))
    a_cum = _dot_f32(causal.astype(F32), a)
    a_cum_t = a_cum.T
    a_last = a_cum[q - 1:q, :]
    decay_end = jnp.exp(a_last - a_cum)
    ea = jnp.exp(a_cum)
    chunk_decay = jnp.exp(a_last)

    cmb = [m.astype(BF) for m in cm]
    cb = [_dot_nt(cmb[g], bm[g].astype(BF)) for g in range(SSD_GROUPS)]
    bt = [bm[g].T.astype(BF) for g in range(SSD_GROUPS)]
    heads_per_group = SSD_HEADS // SSD_GROUPS
    gi = SSD_INNER // SSD_GROUPS

    def per_head_lanes(v):
        hi = v.astype(BF)
        lo = (v - hi.astype(F32)).astype(BF)
        return _dot(hi, e_ref[...]) + _dot(lo, e_ref[...])

    dt_x = per_head_lanes(dt)
    ea_x = per_head_lanes(ea)
    de_x = per_head_lanes(decay_end)
    cd_x = per_head_lanes(jnp.broadcast_to(chunk_decay, (SUBLANES, LANES)))[0:1]
    xdt = xs * dt_x
    xdt_b = xdt.astype(BF)
    xd_b = (xdt * de_x).astype(BF)
    y_off = []
    for g in range(SSD_GROUPS):
        h_old = ht_ref[bb, :, g * gi:(g + 1) * gi]
        y_off.append(_dot(cmb[g], h_old.astype(BF)))
        ht_ref[bb, :, g * gi:(g + 1) * gi] = (h_old * cd_x[:, g * gi:(g + 1) * gi]
                                              + _dot(bt[g], xd_b[:, g * gi:(g + 1) * gi]))

    def decay_weights(r):
        seg = a_cum[:, r:r + 1] - a_cum_t[r:r + 1, :]
        lmat = jnp.where(causal, jnp.exp(jnp.minimum(seg, 0.0)), 0.0)
        return (cb[r // heads_per_group] * lmat).astype(BF)

    first_half = lax.broadcasted_iota(jnp.int32, (q, LANES), 1) < SSD_HEAD_DIM
    y_diag = []
    for k in range(SSD_HEADS // 2):
        xp = xdt_b[:, k * LANES:(k + 1) * LANES]
        y_diag.append(jnp.where(first_half, _dot(decay_weights(2 * k), xp), _dot(decay_weights(2 * k + 1), xp)))
    y = jnp.concatenate(y_diag, axis=1) + jnp.concatenate(y_off, axis=1) * ea_x + dsk_ref[...] * xs
    y = y * _silu(z)
    gi = SSD_INNER // SSD_GROUPS
    yn = [_rms(y[:, g * gi:(g + 1) * gi], sg_ref[:, g * gi:(g + 1) * gi]) for g in range(SSD_GROUPS)]
    ymix_ref[bb, :, 0:SSD_INNER] = jnp.concatenate(yn, axis=1)[:lb].astype(BF)

    uvg = _gelu_tanh(uv)
    u = uvg[:, :GMLP_WIDTH]
    v = uvg[:, GMLP_WIDTH:]
    mu = jnp.mean(v, axis=-1, keepdims=True)
    vc = v - mu
    vn = vc * lax.rsqrt(jnp.mean(vc * vc, axis=-1, keepdims=True) + EPS) * lng_ref[...] + lnb_ref[...]
    if v_ref is not None:
        v_ref[bb] = vn[:lb]
    gd = GMLP_WIDTH // GMLP_GROUPS
    yb = []
    for g in range(GMLP_GROUPS):
        wt = jnp.where(causal, ws_ref[g, :q, :q], 0.0).astype(BF)
        sp = _dot(wt, vn[:, g * gd:(g + 1) * gd].astype(BF)) + bst_ref[:q, g:g + 1]
        yb.append(u[:, g * gd:(g + 1) * gd] * sp)
    ymix_ref[bb, :, SSD_INNER:SSD_INNER + GMLP_WIDTH] = jnp.concatenate(yb, axis=1)[:lb].astype(BF)


EVEN_INPUTS = 15


def _even_entry(*refs, has_v, has_prev, **kw):
    ins = refs[:EVEN_INPUTS]
    k = EVEN_INPUTS + (1 if has_prev else 0)
    outs = refs[k:k + 3]
    k += 3
    v_ref = refs[k] if has_v else None
    k += 1 if has_v else 0
    _even_kernel(*ins, *outs, v_ref, *refs[k:], **kw)


def _even_mixer(proj, dtp, cinit8, sinit, prm, *, q, lb, lc, want_v, layer, state_layers=1, state_prev=None):
    b, lp, _ = proj.shape
    nchunks = lp // lb
    nbb = EVEN_SEQS_PER_STEP if b % EVEN_SEQS_PER_STEP == 0 else 1
    par = [prm["conv_w"], prm["conv_b"], prm["dt_bias"], prm["a_log"], prm["d_skip"], prm["ssd_gain"],
           prm["ln_g"], prm["ln_b"], prm["ws"], prm["bst"], prm["expand"]]
    in_specs = [
        pl.BlockSpec((nbb, lb, ZXU), lambda i, c: (i, c, 0)),
        pl.BlockSpec((nbb, lb, LANES), lambda i, c: (i, c, 0)),
        pl.BlockSpec((nbb, SUBLANES, CONV_DIM), lambda i, c: (i, 0, 0)),
        pl.BlockSpec((1, nbb, SSD_INNER, SSD_STATE), lambda i, c: (layer, i, 0, 0)),
    ] + [_full(w.shape) for w in par]
    out_specs = [
        pl.BlockSpec((nbb, lb, SSD_INNER + GMLP_WIDTH), lambda i, c: (i, c, 0)),
        pl.BlockSpec((nbb, SSD_CONV - 1, CONV_DIM), lambda i, c: (i, 0, 0)),
        (pl.BlockSpec((1, nbb, SSD_INNER, SSD_STATE), lambda i, c: (layer, i, 0, 0))
         if state_prev is not None else
         pl.BlockSpec((state_layers, nbb, SSD_INNER, SSD_STATE), lambda i, c: (0, i, 0, 0))),
    ]
    out_slot = 0 if state_prev is not None else (layer if state_layers > 1 else 0)
    out_shape = [
        jax.ShapeDtypeStruct((b, lp, SSD_INNER + GMLP_WIDTH), BF),
        jax.ShapeDtypeStruct((b, SSD_CONV - 1, CONV_DIM), F32),
        jax.ShapeDtypeStruct((state_layers, b, SSD_INNER, SSD_STATE), F32),
    ]
    if want_v:
        out_specs.append(pl.BlockSpec((nbb, lb, GMLP_WIDTH), lambda i, c: (i, c, 0)))
        out_shape.append(jax.ShapeDtypeStruct((b, lp, GMLP_WIDTH), F32))
    args = [proj, dtp, cinit8, sinit, *par]
    aliases = {}
    if state_prev is not None:
        in_specs.append(pl.BlockSpec(memory_space=pl.ANY))
        aliases = {len(args): 2}
        args.append(state_prev)
    return pl.pallas_call(
        functools.partial(_even_entry, has_v=want_v, has_prev=state_prev is not None,
                          q=q, lb=lb, lc=lc, nbb=nbb, out_slot=out_slot),
        grid=(b // nbb, nchunks),
        in_specs=in_specs,
        out_specs=out_specs,
        out_shape=out_shape,
        input_output_aliases=aliases,
        scratch_shapes=[pltpu.VMEM((nbb, q + 2 * SUBLANES, CONV_DIM), F32),
                        pltpu.VMEM((nbb, SSD_STATE, SSD_INNER), F32)],
        compiler_params=_cp(("parallel", "arbitrary"), vmem_mb=48),
        name="even_mixer",
    )(*args)


def _mla_proj_kernel(x_ref, g_ref, wdq_ref, wdkv_ref, wdk2_ref, qg_ref, kvg_ref, wqn_ref, wqp_ref,
                     wuk_ref, cos_ref, sin_ref, ckv_ref, kpe_ref, kcat_ref, q_ref):
    xn = _rms(x_ref[...], g_ref[...]).astype(BF)
    cqn = _rms(_dot(xn, wdq_ref[...]), qg_ref[...]).astype(BF)
    ckv = _rms(_dot(xn, wdkv_ref[...]), kvg_ref[...])
    kk = _dot(xn, wdk2_ref[...])
    cos = cos_ref[...]
    sin = sin_ref[...]
    kpe = kk[:, :ROPE] * cos[:, :ROPE] + kk[:, ROPE:] * sin[:, :ROPE]
    ckv_ref[...] = ckv
    kpe_ref[...] = kpe
    kcat_ref[:, :KV_RANK] = ckv.astype(BF)
    kcat_ref[:, KV_RANK:] = kpe.astype(BF)
    qn = _dot(cqn, wqn_ref[...])
    qp = _dot(cqn, wqp_ref[...])
    hr = MLA_HEADS * ROPE
    qpe = qp[:, :hr] * cos + qp[:, hr:] * sin
    for h in range(MLA_HEADS):
        ql = _dot(qn[:, h * NOPE:(h + 1) * NOPE].astype(BF), wuk_ref[h])
        q_ref[0, h, :, :KV_RANK] = (ql * MLA_SCALE).astype(BF)
        q_ref[0, h, :, KV_RANK:] = (qpe[:, h * ROPE:(h + 1) * ROPE] * MLA_SCALE).astype(BF)


def _mla_proj(x, g, prm, cos8, sin8, *, nb):
    t = x.shape[0]
    seq = t // nb
    tm = min(ROW_TILE, seq)
    tpb = seq // tm
    w = [prm["wdq"], prm["wdkv"], prm["wdk2"], prm["q_gain"], prm["kv_gain"], prm["wqn"], prm["wqp"],
         prm["wuk"]]
    return pl.pallas_call(
        _mla_proj_kernel,
        grid=(t // tm,),
        in_specs=[pl.BlockSpec((tm, D), lambda i: (i, 0)), _full((1, D))] + [_full(a.shape) for a in w] + [
            pl.BlockSpec((tm, MLA_HEADS * ROPE), lambda i: (i % tpb, 0)),
            pl.BlockSpec((tm, MLA_HEADS * ROPE), lambda i: (i % tpb, 0)),
        ],
        out_specs=[
            pl.BlockSpec((tm, KV_RANK), lambda i: (i, 0)),
            pl.BlockSpec((tm, ROPE), lambda i: (i, 0)),
            pl.BlockSpec((tm, QK), lambda i: (i, 0)),
            pl.BlockSpec((1, MLA_HEADS, tm, QK), lambda i: (i // tpb, 0, i % tpb, 0)),
        ],
        out_shape=[
            jax.ShapeDtypeStruct((t, KV_RANK), F32),
            jax.ShapeDtypeStruct((t, ROPE), F32),
            jax.ShapeDtypeStruct((t, QK), BF),
            jax.ShapeDtypeStruct((nb, MLA_HEADS, seq, QK), BF),
        ],
        compiler_params=_cp(("parallel",)),
        name="mla_proj",
    )(x, g, *w, cos8, sin8)


HEAD_QK = NOPE + ROPE
LOG2E = 1.4426950408889634


def _mla_proj_prompt_kernel(x_ref, g_ref, wdq_ref, wdkv_ref, wdk2_ref, qg_ref, kvg_ref, wqn_ref, wqp_ref,
                            wkn_ref, wv_ref, cos_ref, sin_ref, ckv_ref, kpe_ref, q_ref, k_ref, v_ref):
    xn = _rms(x_ref[...], g_ref[...]).astype(BF)
    cqn = _rms(_dot(xn, wdq_ref[...]), qg_ref[...]).astype(BF)
    ckv = _rms(_dot(xn, wdkv_ref[...]), kvg_ref[...])
    kk = _dot(xn, wdk2_ref[...])
    cos = cos_ref[...]
    sin = sin_ref[...]
    kpe = kk[:, :ROPE] * cos[:, :ROPE] + kk[:, ROPE:] * sin[:, :ROPE]
    ckv_ref[...] = ckv
    kpe_ref[...] = kpe
    ckv_b = ckv.astype(BF)
    kpe_b = kpe.astype(BF)
    kn = _dot(ckv_b, wkn_ref[...])
    vt = _dot_nt(wv_ref[...], ckv_b)
    qn = _dot(cqn, wqn_ref[...])
    qp = _dot(cqn, wqp_ref[...])
    hr = MLA_HEADS * ROPE
    qpe = qp[:, :hr] * cos + qp[:, hr:] * sin
    qscale = MLA_SCALE * LOG2E
    for h in range(MLA_HEADS):
        q_ref[0, h, :, :NOPE] = (qn[:, h * NOPE:(h + 1) * NOPE] * qscale).astype(BF)
        q_ref[0, h, :, NOPE:] = (qpe[:, h * ROPE:(h + 1) * ROPE] * qscale).astype(BF)
        k_ref[0, h, :, :NOPE] = kn[:, h * NOPE:(h + 1) * NOPE].astype(BF)
        k_ref[0, h, :, NOPE:] = kpe_b
        v_ref[0, h, 0] = vt[h * MLA_V:(h + 1) * MLA_V].astype(BF)


def _mla_proj_prompt(x, g, prm, cos8, sin8, *, nb):
    t = x.shape[0]
    seq = t // nb
    tm = min(ROW_TILE, seq)
    tpb = seq // tm
    w = [prm["wdq"], prm["wdkv"], prm["wdk2"], prm["q_gain"], prm["kv_gain"], prm["wqn"], prm["wqp"],
         prm["wkn"], prm["wv"]]
    head_spec = lambda width: pl.BlockSpec((1, MLA_HEADS, tm, width), lambda i: (i // tpb, 0, i % tpb, 0))
    head_shape = lambda width: jax.ShapeDtypeStruct((nb, MLA_HEADS, seq, width), BF)
    return pl.pallas_call(
        _mla_proj_prompt_kernel,
        grid=(t // tm,),
        in_specs=[pl.BlockSpec((tm, D), lambda i: (i, 0)), _full((1, D))] + [_full(a.shape) for a in w] + [
            pl.BlockSpec((tm, MLA_HEADS * ROPE), lambda i: (i % tpb, 0)),
            pl.BlockSpec((tm, MLA_HEADS * ROPE), lambda i: (i % tpb, 0)),
        ],
        out_specs=[
            pl.BlockSpec((tm, KV_RANK), lambda i: (i, 0)),
            pl.BlockSpec((tm, ROPE), lambda i: (i, 0)),
            head_spec(HEAD_QK), head_spec(HEAD_QK),
            pl.BlockSpec((1, MLA_HEADS, 1, MLA_V, tm), lambda i: (i // tpb, 0, i % tpb, 0, 0)),
        ],
        out_shape=[
            jax.ShapeDtypeStruct((t, KV_RANK), F32),
            jax.ShapeDtypeStruct((t, ROPE), F32),
            head_shape(HEAD_QK), head_shape(HEAD_QK),
            jax.ShapeDtypeStruct((nb, MLA_HEADS, tpb, MLA_V, tm), BF),
        ],
        compiler_params=_cp(("parallel",)),
        name="mla_proj_prompt",
    )(x, g, *w, cos8, sin8)


FLASH_HEADS = 4
FLASH_KBLOCKS = 4
FLASH_QSPLIT = 1
DENOM_ROWS = 16


def _flash_kernel(q_ref, k_ref, vt_ref, o_ref, m_ref, acc_ref, *, t):
    qi = pl.program_id(2)
    for hh in range(FLASH_HEADS):
        m_ref[hh] = jnp.full((1, t), NEG, F32)
        acc_ref[hh] = jnp.zeros((MLA_V + DENOM_ROWS, t), F32)
    ones = jnp.ones((DENOM_ROWS, t), BF)

    tq = t // FLASH_QSPLIT

    def block(ki, nblk, masked):
        start = pl.multiple_of(ki * t, t)
        ones_n = jnp.concatenate([ones] * nblk, axis=1)
        for hh in range(FLASH_HEADS):
            kb = k_ref[0, hh, pl.ds(start, nblk * t), :]
            vt = jnp.concatenate([vt_ref[0, hh, ki + n] for n in range(nblk)], axis=1)
            v1 = jnp.concatenate([vt, ones_n], axis=0)
            for qs in range(FLASH_QSPLIT):
                lanes = slice(qs * tq, (qs + 1) * tq)
                st = _dot_nt(kb, q_ref[0, hh, lanes, :])
                if masked:
                    key = lax.broadcasted_iota(jnp.int32, (t, tq), 0)
                    qry = lax.broadcasted_iota(jnp.int32, (t, tq), 1) + qs * tq
                    st = jnp.where(key <= qry, st, NEG)
                m_old = m_ref[hh, :, lanes]
                m_new = jnp.maximum(m_old, jnp.max(st, axis=0, keepdims=True))
                alpha = jnp.exp2(m_old - m_new)
                pt = jnp.exp2(st - m_new).astype(BF)
                acc_ref[hh, :, lanes] = alpha * acc_ref[hh, :, lanes] + _dot(v1, pt)
                m_ref[hh, :, lanes] = m_new

    def body(kk, carry):
        block(kk * FLASH_KBLOCKS, FLASH_KBLOCKS, False)
        return carry

    lax.fori_loop(0, qi // FLASH_KBLOCKS, body, 0)
    rem = qi % FLASH_KBLOCKS
    base = qi - rem
    size = FLASH_KBLOCKS // 2
    while size >= 1:
        @pl.when(rem & size != 0)
        def _(size=size):
            block(base + (rem & ~(2 * size - 1)), size, False)
        size //= 2
    block(qi, 1, True)
    for hh in range(FLASH_HEADS):
        acc = acc_ref[hh]
        o_t = acc[:MLA_V] / acc[MLA_V:MLA_V + 1]
        o_ref[0, :, hh * MLA_V:(hh + 1) * MLA_V] = o_t.T.astype(BF)


def _flash(q, k, vt):
    nb, _, seq, _ = q.shape
    t = vt.shape[-1]
    nh = FLASH_HEADS
    return pl.pallas_call(
        functools.partial(_flash_kernel, t=t),
        grid=(nb, MLA_HEADS // nh, seq // t),
        in_specs=[
            pl.BlockSpec((1, nh, t, HEAD_QK), lambda b, h, i: (b, h, i, 0)),
            pl.BlockSpec((1, nh, seq, HEAD_QK), lambda b, h, i: (b, h, 0, 0), pipeline_mode=pl.Buffered(1)),
            pl.BlockSpec((1, nh, seq // t, MLA_V, t), lambda b, h, i: (b, h, 0, 0, 0),
                         pipeline_mode=pl.Buffered(1)),
        ],
        out_specs=pl.BlockSpec((1, t, nh * MLA_V), lambda b, h, i: (b, i, h)),
        out_shape=jax.ShapeDtypeStruct((nb, seq, MLA_HEADS * MLA_V), BF),
        scratch_shapes=[pltpu.VMEM((nh, 1, t), F32), pltpu.VMEM((nh, MLA_V + DENOM_ROWS, t), F32)],
        compiler_params=_cp(("parallel", "parallel", "arbitrary"), vmem_mb=48),
        name="mla_flash",
    )(q, k, vt)


def _page_copies(pt_ref, ckv_hbm, kpt_hbm, ckbuf, kpbuf, sem, sample, slot, *, layer, n_pages):
    copies = []
    for p in range(n_pages):
        pg = pt_ref[sample, p]
        copies.append(pltpu.make_async_copy(
            ckv_hbm.at[layer, pg], ckbuf.at[slot, pl.ds(p * PAGE, PAGE), :], sem.at[0, slot]))
        copies.append(pltpu.make_async_copy(
            kpt_hbm.at[layer, pg], kpbuf.at[slot, :, pl.ds(p * PAGE, PAGE)], sem.at[1, slot]))
    return copies


def _wait_grouped(copies):
    for c in copies[0::2]:
        c.wait()
    for c in copies[1::2]:
        c.wait()


def _decode_kernel(pt_ref, q_ref, nk_ref, ckv_hbm, kpt_hbm, o_ref, ckbuf, kpbuf, sem, *,
                   dec_seq, layer, n_pages):
    b = pl.program_id(0)
    last = pl.num_programs(0) - 1
    slot = b % 2
    copies = functools.partial(_page_copies, pt_ref, ckv_hbm, kpt_hbm, ckbuf, kpbuf, sem,
                               layer=layer, n_pages=n_pages)

    @pl.when(b == 0)
    def _():
        for c in copies(0, 0):
            c.start()

    _wait_grouped(copies(b, slot))
    nxt = jnp.minimum(b + 1, last)
    for c in copies(nxt, 1 - slot):
        c.start()

    rows = MLA_HEADS * dec_seq
    qm = q_ref[0]
    ql = qm[:, :KV_RANK]
    qp = qm[:, KV_RANK:]
    n_slices = DECODE_SLICES if n_pages % DECODE_SLICES == 0 else 1
    width = n_pages * PAGE // n_slices
    cks, scores = [], []
    for c in range(n_slices):
        ck = ckbuf[slot, c * width:(c + 1) * width, :].astype(BF)
        kp = kpbuf[slot, :, c * width:(c + 1) * width].astype(BF)
        cks.append(ck)
        scores.append(_dot_nt(ql, ck) + _dot(qp, kp))
    nk = nk_ref[0]
    kt = lax.broadcasted_iota(jnp.int32, (rows, NEW_KEY_ROWS), 1)
    qt = lax.broadcasted_iota(jnp.int32, (rows, NEW_KEY_ROWS), 0) % dec_seq
    s_new = jnp.where(kt <= qt, _dot_nt(qm, nk), NEG)
    m = jnp.max(s_new, axis=-1, keepdims=True)
    for s in scores:
        m = jnp.maximum(m, jnp.max(s, axis=-1, keepdims=True))
    p_new = jnp.exp(s_new - m)
    denom = jnp.sum(p_new, axis=-1, keepdims=True)
    acc = _dot(p_new.astype(BF), nk[:, :KV_RANK])
    for s, ck in zip(scores, cks):
        p = jnp.exp(s - m)
        denom = denom + jnp.sum(p, axis=-1, keepdims=True)
        acc = acc + _dot(p.astype(BF), ck)
    o_ref[0] = acc / denom

    @pl.when(b == last)
    def _():
        _wait_grouped(copies(nxt, 1 - slot))


def _decode(page_table, q, newk, cache_ckv, cache_kpe_t, layer, dec_seq):
    nb, n_pages = page_table.shape
    rows = MLA_HEADS * dec_seq
    keys = n_pages * PAGE
    grid_spec = pltpu.PrefetchScalarGridSpec(
        num_scalar_prefetch=1,
        grid=(nb,),
        in_specs=[
            pl.BlockSpec((1, rows, QK), lambda b, pt: (b, 0, 0)),
            pl.BlockSpec((1, NEW_KEY_ROWS, QK), lambda b, pt: (b, 0, 0)),
            pl.BlockSpec(memory_space=pl.ANY),
            pl.BlockSpec(memory_space=pl.ANY),
        ],
        out_specs=pl.BlockSpec((1, rows, KV_RANK), lambda b, pt: (b, 0, 0)),
        scratch_shapes=[pltpu.VMEM((2, keys, KV_RANK), F32), pltpu.VMEM((2, ROPE, keys), F32),
                        pltpu.SemaphoreType.DMA((2, 2))],
    )
    return pl.pallas_call(
        functools.partial(_decode_kernel, dec_seq=dec_seq, layer=layer, n_pages=n_pages),
        grid_spec=grid_spec,
        out_shape=jax.ShapeDtypeStruct((nb, rows, KV_RANK), F32),
        compiler_params=_cp(("arbitrary",), vmem_mb=48),
        name="mla_decode",
    )(page_table, q, newk, cache_ckv, cache_kpe_t)


def _mla_out_kernel(o_ref, wuv_ref, wo_ref, r_ref, out_ref):
    parts = [_dot(o_ref[0, h], wuv_ref[h]).astype(BF) for h in range(MLA_HEADS)]
    out_ref[...] = r_ref[...] + _dot(jnp.concatenate(parts, axis=1), wo_ref[...])


def _mla_out(o_lat, wuv, wo, res):
    nb, _, seq, _ = o_lat.shape
    tm = min(ROW_TILE, seq)
    tpb = seq // tm
    t = nb * seq
    return pl.pallas_call(
        _mla_out_kernel,
        grid=(t // tm,),
        in_specs=[
            pl.BlockSpec((1, MLA_HEADS, tm, KV_RANK), lambda i: (i // tpb, 0, i % tpb, 0)),
            _full(wuv.shape), _full(wo.shape),
            pl.BlockSpec((tm, D), lambda i: (i, 0)),
        ],
        out_specs=pl.BlockSpec((tm, D), lambda i: (i, 0)),
        out_shape=jax.ShapeDtypeStruct((t, D), F32),
        compiler_params=_cp(("parallel",)),
        name="mla_out",
    )(o_lat, wuv, wo, res)


def _softmax_rows(s):
    m = jnp.max(s, axis=-1, keepdims=True)
    p = jnp.exp(s - m)
    return p / jnp.sum(p, axis=-1, keepdims=True)


def _mem_prompt_kernel(x_ref, g_ref, wq_ref, k_ref, v_ref, wo_ref, o_ref):
    x = x_ref[...]
    xn = _rms(x, g_ref[...]).astype(BF)
    qm = (_dot(xn, wq_ref[...]) * MEM_HD ** -0.5).astype(BF)
    km = k_ref[0].astype(BF)
    vm = v_ref[0].astype(BF)
    parts = []
    for h in range(MEM_HEADS):
        sl = slice(h * MEM_HD, (h + 1) * MEM_HD)
        p = _softmax_rows(_dot_nt(qm[:, sl], km[:, sl]))
        parts.append(_dot(p.astype(BF), vm[:, sl]).astype(BF))
    o_ref[...] = x + _dot(jnp.concatenate(parts, axis=1), wo_ref[...])


def _mem_prompt(x, g, wq, km, vm, wo):
    t = x.shape[0]
    nb, mt, _ = km.shape
    seq = t // nb
    tm = min(ROW_TILE, seq)
    tpb = seq // tm
    return pl.pallas_call(
        _mem_prompt_kernel,
        grid=(t // tm,),
        in_specs=[
            pl.BlockSpec((tm, D), lambda i: (i, 0)), _full((1, D)), _full(wq.shape),
            pl.BlockSpec((1, mt, MEM_INNER), lambda i: (i // tpb, 0, 0)),
            pl.BlockSpec((1, mt, MEM_INNER), lambda i: (i // tpb, 0, 0)),
            _full(wo.shape),
        ],
        out_specs=pl.BlockSpec((tm, D), lambda i: (i, 0)),
        out_shape=jax.ShapeDtypeStruct((t, D), F32),
        compiler_params=_cp(("parallel",)),
        name="mem_attn_prompt",
    )(x, g, wq, km, vm, wo)


MEM_ROWS = MEM_HEADS * SUBLANES


def _mem_sample_kernel(x_ref, g_ref, wq_ref, k_ref, v_ref, wo_ref, o_ref):
    ns = SAMPLES_PER_STEP
    x = x_ref[...].reshape(ns * SUBLANES, D)
    xn = _rms(x, g_ref[...]).astype(BF)
    qall = _dot(xn, wq_ref[...]) * MEM_HD ** -0.5
    cols = k_ref.shape[2]
    head_of_row = lax.broadcasted_iota(jnp.int32, (MEM_ROWS, cols), 0) // SUBLANES
    head_of_col = lax.broadcasted_iota(jnp.int32, (MEM_ROWS, cols), 1) % MEM_HEADS
    own = head_of_row == head_of_col
    outs = []
    for s in range(ns):
        qs = qall[s * SUBLANES:(s + 1) * SUBLANES]
        qst = jnp.concatenate([qs[:, h * MEM_HD:(h + 1) * MEM_HD] for h in range(MEM_HEADS)], axis=0)
        sc = jnp.where(own, _dot_nt(qst.astype(BF), k_ref[0, s].astype(BF)), NEG)
        o = _dot(_softmax_rows(sc).astype(BF), v_ref[0, s].astype(BF))
        outs.append(jnp.concatenate([o[h * SUBLANES:(h + 1) * SUBLANES] for h in range(MEM_HEADS)], axis=1))
    out = x + _dot(jnp.concatenate(outs, axis=0).astype(BF), wo_ref[...])
    o_ref[...] = out.reshape(ns, SUBLANES, D)


def _mem_sample(x8, g, wq, cache_k, cache_v, wo, layer):
    nb = x8.shape[0]
    rows = cache_k.shape[2]
    ns = SAMPLES_PER_STEP
    return pl.pallas_call(
        _mem_sample_kernel,
        grid=(nb // ns,),
        in_specs=[
            pl.BlockSpec((ns, SUBLANES, D), lambda i: (i, 0, 0)), _full((1, D)), _full(wq.shape),
            pl.BlockSpec((1, ns, rows, MEM_HD), lambda i: (layer, i, 0, 0)),
            pl.BlockSpec((1, ns, rows, MEM_HD), lambda i: (layer, i, 0, 0)),
            _full(wo.shape),
        ],
        out_specs=pl.BlockSpec((ns, SUBLANES, D), lambda i: (i, 0, 0)),
        out_shape=jax.ShapeDtypeStruct((nb, SUBLANES, D), F32),
        compiler_params=_cp(("parallel",), vmem_mb=48),
        name="mem_attn_sample",
    )(x8, g, wq, cache_k, cache_v, wo)


def _router_kernel(x_ref, g_ref, wr_ref, xn_ref, route_ref):
    xn = _rms(x_ref[...], g_ref[...])
    xn_ref[...] = xn
    lane = lax.broadcasted_iota(jnp.int32, (xn.shape[0], LANES), 1).astype(F32)
    lg = jnp.where(lane < N_EXP, _dot_f32(xn, wr_ref[...]), NEG)
    m1 = jnp.max(lg, axis=-1, keepdims=True)
    i1 = jnp.min(jnp.where(lg == m1, lane, float(LANES)), axis=-1, keepdims=True)
    lg2 = jnp.where(lane == i1, NEG, lg)
    m2 = jnp.max(lg2, axis=-1, keepdims=True)
    i2 = jnp.min(jnp.where(lg2 == m2, lane, float(LANES)), axis=-1, keepdims=True)
    e = jnp.exp(m2 - m1)
    g1 = 1.0 / (1.0 + e)
    g2 = e * g1
    route_ref[...] = jnp.where(lane == 0, i1, jnp.where(lane == 1, i2, jnp.where(lane == 2, g1,
                               jnp.where(lane == 3, g2, 0.0))))


def _router(x, g, wr_pad):
    t = x.shape[0]
    tm = min(ROW_TILE, t)
    return pl.pallas_call(
        _router_kernel,
        grid=(t // tm,),
        in_specs=[pl.BlockSpec((tm, D), lambda i: (i, 0)), _full((1, D)), _full(wr_pad.shape)],
        out_specs=[pl.BlockSpec((tm, D), lambda i: (i, 0)), pl.BlockSpec((tm, LANES), lambda i: (i, 0))],
        out_shape=[jax.ShapeDtypeStruct((t, D), F32), jax.ShapeDtypeStruct((t, LANES), F32)],
        compiler_params=_cp(("parallel",)),
        name="router",
    )(x, g, wr_pad)


SC_ROWS = 32


def _sc_mesh():
    return plsc.VectorSubcoreMesh(core_axis_name="c", subcore_axis_name="s")


def _sc_workers():
    info = plsc.get_sparse_core_info()
    return info.num_cores, info.num_cores * info.num_subcores


def _sc_gather_rows(table, idx):
    n = idx.shape[0]
    width = table.shape[1]
    ncores, nw = _sc_workers()
    per_w = n // nw
    n_chunks = per_w // SC_ROWS
    assert per_w * nw == n and n_chunks * SC_ROWS == per_w

    @functools.partial(
        pl.kernel, mesh=_sc_mesh(),
        out_type=jax.ShapeDtypeStruct((n, width), table.dtype),
        scratch_types=[pltpu.VMEM((per_w,), jnp.int32), pltpu.VMEM((SC_ROWS, width), table.dtype),
                       pltpu.SemaphoreType.DMA],
        name="sc_gather_rows",
    )
    def body(table_hbm, idx_hbm, out_hbm, idx_v, rows_v, sem):
        wid = lax.axis_index("s") * ncores + lax.axis_index("c")
        base = wid * per_w
        pltpu.sync_copy(idx_hbm.at[pl.ds(base, per_w)], idx_v)

        @pl.loop(0, n_chunks)
        def _(j):
            off = pl.multiple_of(j * SC_ROWS, SC_ROWS)
            pltpu.async_copy(table_hbm.at[idx_v.at[pl.ds(off, SC_ROWS)]], rows_v, sem).wait()
            pltpu.sync_copy(rows_v, out_hbm.at[pl.ds(base + off, SC_ROWS)])

    return body(table, idx)


def _sc_scatter_rows(src, idx, n_out):
    n = idx.shape[0]
    t, width = src.shape
    ncores, nw = _sc_workers()
    per_w = n // nw
    n_chunks = per_w // SC_ROWS
    assert per_w * nw == n and n_chunks * SC_ROWS == per_w and t % per_w == 0
    idx3 = idx.reshape(nw, n_chunks, SC_ROWS)

    @functools.partial(
        pl.kernel, mesh=_sc_mesh(),
        out_type=jax.ShapeDtypeStruct((n_out, width), src.dtype),
        scratch_types=[pltpu.VMEM((n_chunks, SC_ROWS), jnp.int32), pltpu.VMEM((SC_ROWS, width), src.dtype),
                       pltpu.SemaphoreType.DMA],
        name="sc_scatter_rows",
    )
    def body(src_hbm, idx_hbm, out_hbm, idx_v, rows_v, sem):
        wid = lax.axis_index("s") * ncores + lax.axis_index("c")
        base = lax.rem(wid * per_w, t)
        pltpu.sync_copy(idx_hbm.at[wid], idx_v)

        @pl.loop(0, n_chunks)
        def _(j):
            off = pl.multiple_of(j * SC_ROWS, SC_ROWS)
            pltpu.sync_copy(src_hbm.at[pl.ds(base + off, SC_ROWS)], rows_v)
            pltpu.async_copy(rows_v, out_hbm.at[idx_v.at[j]], sem).wait()

    return body(src, idx3)


MOE_TILE = 1024
MOE_TILE_SMALL = 256
MOE_BLOCK = 512
MOE_CHUNK = 256


def _moe_ffn_kernel(te_ref, nused_ref, x_ref, wg_ref, wu_ref, wd_ref, o_ref, xb_ref):
    del te_ref
    i = pl.program_id(0)
    j = pl.program_id(1)

    @pl.when(i < nused_ref[0])
    def _():
        @pl.when(j == 0)
        def _():
            xb_ref[...] = x_ref[...].astype(BF)

        xb = xb_ref[...]
        part = None
        block = wd_ref.shape[2]
        for lo in range(0, block, MOE_CHUNK):
            sl = slice(lo, min(lo + MOE_CHUNK, block))
            gate = _dot(xb, wg_ref[0, 0, :, sl].astype(BF))
            up = _dot(xb, wu_ref[0, 0, :, sl].astype(BF))
            contrib = _dot((_silu(gate) * up).astype(BF), wd_ref[0, 0, sl, :].astype(BF))
            part = contrib if part is None else part + contrib

        @pl.when(j == 0)
        def _():
            o_ref[...] = part

        @pl.when(j != 0)
        def _():
            o_ref[...] += part

    @pl.when(i >= nused_ref[0])
    def _():
        o_ref[...] = jnp.zeros_like(o_ref)


def _moe_ffn(tile_expert, nused, xs, wgu, wd, layer, *, tm):
    npad = xs.shape[0]
    block = MOE_BLOCK
    nblk = D_FFE // block

    def blk(i, j, nu):
        return jnp.where(i < nu[0], j, nblk - 1)

    grid_spec = pltpu.PrefetchScalarGridSpec(
        num_scalar_prefetch=2,
        grid=(npad // tm, nblk),
        in_specs=[
            pl.BlockSpec((tm, D), lambda i, j, te, nu: (jnp.minimum(i, nu[0] - 1), 0)),
            pl.BlockSpec((1, 1, D, block), lambda i, j, te, nu: (layer, te[i], 0, blk(i, j, nu))),
            pl.BlockSpec((1, 1, D, block), lambda i, j, te, nu: (layer, te[i], 0, nblk + blk(i, j, nu))),
            pl.BlockSpec((1, 1, block, D), lambda i, j, te, nu: (layer, te[i], blk(i, j, nu), 0)),
        ],
        out_specs=pl.BlockSpec((tm, D), lambda i, j, te, nu: (i, 0)),
        scratch_shapes=[pltpu.VMEM((tm, D), BF)],
    )
    return pl.pallas_call(
        _moe_ffn_kernel,
        grid_spec=grid_spec,
        out_shape=jax.ShapeDtypeStruct((npad, D), F32),
        compiler_params=_cp(("parallel", "arbitrary"), vmem_mb=56),
        name="moe_ffn",
    )(tile_expert, nused, xs, wgu, wgu, wd)


ROUTE_GATE_LANE = 2


def _combine_kernel(h_ref, y0_ref, y1_ref, route_ref, g_ref, o_ref, *, final):
    gl = ROUTE_GATE_LANE
    route = route_ref[...]
    out = h_ref[...] + route[:, gl:gl + 1] * y0_ref[...] + route[:, gl + 1:gl + 2] * y1_ref[...]
    if final:
        out = _rms(out, g_ref[...])
    o_ref[...] = out


def _combine(h, y2, route, g, *, final):
    t = h.shape[0]
    tm = min(ROW_TILE, t)
    nt = t // tm
    return pl.pallas_call(
        functools.partial(_combine_kernel, final=final),
        grid=(nt,),
        in_specs=[
            pl.BlockSpec((tm, D), lambda i: (i, 0)),
            pl.BlockSpec((tm, D), lambda i: (i, 0)),
            pl.BlockSpec((tm, D), lambda i: (nt + i, 0)),
            pl.BlockSpec((tm, LANES), lambda i: (i, 0)),
            _full((1, D)),
        ],
        out_specs=pl.BlockSpec((tm, D), lambda i: (i, 0)),
        out_shape=jax.ShapeDtypeStruct((t, D), F32),
        compiler_params=_cp(("parallel",)),
        name="moe_combine",
    )(h, y2, y2, route, g)


def _moe(h, g, wr_pad, wgu, wd, layer, final_g, *, final):
    t = h.shape[0]
    tm = MOE_TILE if 2 * t >= 2 * N_EXP * MOE_TILE else MOE_TILE_SMALL
    xn, route = _router(h, g, wr_pad)
    eidx = route[:, :ROUTE_GATE_LANE].astype(jnp.int32)
    e_flat = eidx.T.reshape(-1)
    onehot = (e_flat[:, None] == jnp.arange(N_EXP, dtype=jnp.int32)[None, :]).astype(jnp.int32)
    csum = jnp.cumsum(onehot, axis=0)
    counts = csum[-1]
    rank = jnp.sum(onehot * csum, axis=1) - 1
    padded = ((counts + tm - 1) // tm) * tm
    ends = jnp.cumsum(padded)
    starts = ends - padded
    dest = (jnp.sum(onehot * starts[None, :], axis=1) + rank).astype(jnp.int32)
    n_tiles = -(-2 * t // tm) + N_EXP
    tile_start = jnp.arange(n_tiles, dtype=jnp.int32) * tm
    tile_expert = jnp.minimum(jnp.sum((tile_start[:, None] >= ends[None, :]).astype(jnp.int32), axis=1),
                              N_EXP - 1).astype(jnp.int32)
    nused = (ends[-1] // tm).astype(jnp.int32).reshape(1)
    tile_expert = jnp.where(jnp.arange(n_tiles) < nused[0], tile_expert, tile_expert[nused[0] - 1])
    xs = _sc_scatter_rows(xn, dest, n_tiles * tm)
    ys = _moe_ffn(tile_expert, nused, xs, wgu, wd, layer, tm=tm)
    y2 = _sc_gather_rows(ys, dest)
    return _combine(h, y2, route, final_g, final=final)


def _rope_tables(pos):
    half = ROPE // 2
    inv_freq = ROPE_BASE ** (-jnp.arange(half, dtype=F32) / half)
    ang = pos.astype(F32)[:, None] * inv_freq
    cos = jnp.cos(ang)
    sin = jnp.sin(ang)
    cc = jnp.concatenate([cos, cos], axis=-1)
    ss = jnp.concatenate([-sin, sin], axis=-1)
    return jnp.tile(cc, (1, MLA_HEADS)), jnp.tile(ss, (1, MLA_HEADS))


def _even_layer(h, g, prm, cinit, sinit, layer, *, nb, sample, state_layers=1, state_prev=None):
    t = h.shape[0]
    seq = t // nb
    proj, dtp = _even_proj(h, g, prm["w_zxu"], prm["w_dt"], tn=_largest_tile(ZXU, 1536))
    proj = proj.reshape(nb, seq, ZXU)
    dtp = dtp.reshape(nb, seq, LANES)
    if sample:
        pad = ((0, 0), (0, SUBLANES - seq), (0, 0))
        proj = jnp.pad(proj, pad)
        dtp = jnp.pad(dtp, pad)
        q, lb, lc = SAMPLE_Q, SUBLANES, seq
    else:
        q, lb, lc = CHUNK, CHUNK, CHUNK
    cinit8 = jnp.pad(cinit, ((0, 0), (SUBLANES - (SSD_CONV - 1), 0), (0, 0)))
    outs = _even_mixer(proj, dtp, cinit8, sinit.reshape(-1, nb, SSD_INNER, SSD_STATE), prm,
                       q=q, lb=lb, lc=lc, want_v=sample, layer=layer,
                       state_layers=state_layers, state_prev=state_prev)
    ymix, cout, sout = outs[:3]
    v = None
    if sample:
        ymix = ymix[:, :seq]
        v = outs[3][:, :seq]
    h = _matmul_res(ymix.reshape(t, SSD_INNER + GMLP_WIDTH), prm["w_out"], h)
    return h, cout, sout, v


def _prep_even(i, w_in, conv_w, conv_b, dt_bias, a_log, d_skip, ssd_gain, ln_g, ln_b, ws, bs, w_out):
    w = w_in[i]
    o1 = SSD_INNER + CONV_DIM
    w_zxu = jnp.concatenate([w[:, :o1], w[:, o1 + SSD_HEADS:]], axis=1).astype(BF)
    w_dt = jnp.pad(w[:, o1:o1 + SSD_HEADS], ((0, 0), (0, LANES - SSD_HEADS)))
    padl = (0, LANES - SSD_HEADS)
    return dict(
        w_zxu=w_zxu, w_dt=w_dt,
        conv_w=jnp.pad(conv_w[i], ((0, SUBLANES - SSD_CONV), (0, 0))),
        conv_b=conv_b[i][None, :],
        dt_bias=jnp.pad(dt_bias[i], padl)[None, :],
        a_log=jnp.pad(a_log[i], padl)[None, :],
        d_skip=jnp.repeat(d_skip[i], SSD_HEAD_DIM)[None, :],
        ssd_gain=ssd_gain[i][None, :],
        ln_g=ln_g[i][None, :], ln_b=ln_b[i][None, :],
        ws=ws[i], bst=bs[i].T,
        expand=(jnp.arange(LANES)[:, None] == jnp.arange(SSD_INNER)[None, :] // SSD_HEAD_DIM).astype(BF),
        w_out=w_out[i].astype(BF),
    )


def _prep_mla(i, w_down, q_gain, kv_gain, w_uq, w_uk, w_uv, w_o):
    wd = w_down[i]
    wk = wd[:, Q_RANK + KV_RANK:]
    half = ROPE // 2
    rot = lambda a: jnp.concatenate([a[..., half:], a[..., :half]], axis=-1)
    uq = w_uq[i]
    uq_pe = uq[:, :, NOPE:]
    return dict(
        wdq=wd[:, :Q_RANK].astype(BF),
        wdkv=wd[:, Q_RANK:Q_RANK + KV_RANK].astype(BF),
        wdk2=jnp.concatenate([wk, rot(wk)], axis=1).astype(BF),
        q_gain=q_gain[i][None, :], kv_gain=kv_gain[i][None, :],
        wqn=uq[:, :, :NOPE].reshape(Q_RANK, MLA_HEADS * NOPE).astype(BF),
        wqp=jnp.concatenate([uq_pe.reshape(Q_RANK, -1), rot(uq_pe).reshape(Q_RANK, -1)], axis=1).astype(BF),
        wuk=jnp.transpose(w_uk[i], (1, 2, 0)).astype(BF),
        wuv=jnp.transpose(w_uv[i], (1, 0, 2)).astype(BF),
        wkn=w_uk[i].reshape(KV_RANK, MLA_HEADS * NOPE).astype(BF),
        wv=w_uv[i].reshape(KV_RANK, MLA_HEADS * MLA_V).T.astype(BF),
        wo=w_o[i].astype(BF),
    )


def kernel(x_prompt, x_sample, state_ssd, state_conv, cache_mla_ckv, cache_mla_kpe, cache_mem_k, cache_mem_v, page_table, mem_prompt, mix_norm, w_in, conv_w, conv_b, dt_bias, a_log, d_skip, ssd_gain, gmlp_ln_g, gmlp_ln_b, gmlp_ws, gmlp_bs, w_out_even, w_mla_down, mla_q_gain, mla_kv_gain, w_mla_uq, w_mla_uk, w_mla_uv, w_mla_o, xattn_norm, mem_norm, w_mem_q, w_mem_k, w_mem_v, w_mem_o, ffn_norm, w_ffn_gu, w_ffn_down, w_router, w_exp_gu, w_exp_down, final_norm):
    nbp, seq, _ = x_prompt.shape
    nbs, dseq, _ = x_sample.shape
    depth = mix_norm.shape[0]
    past = page_table.shape[1] * PAGE
    mt = mem_prompt.shape[1]
    hp = x_prompt.reshape(nbp * seq, D)
    hs = x_sample.reshape(nbs * dseq, D)
    cos_p, sin_p = _rope_tables(jnp.arange(seq, dtype=jnp.int32))
    cos_s, sin_s = _rope_tables(past + jnp.arange(dseq, dtype=jnp.int32))
    cos_s = jnp.tile(cos_s, (nbs, 1))
    sin_s = jnp.tile(sin_s, (nbs, 1))
    cache_k4 = cache_mem_k.reshape(depth, nbs, mt * MEM_HEADS, MEM_HD)
    cache_v4 = cache_mem_v.reshape(depth, nbs, mt * MEM_HEADS, MEM_HD)
    cache_kpe_t = jnp.swapaxes(cache_mla_kpe, 2, 3)
    final_g = final_norm[None, :]

    p_ssd, p_conv, p_ckv, p_kpe, p_mk, p_mv = [], [], [], [], [], []
    s_conv, s_v, s_ckv, s_kpe = [], [], [], []
    n_even = (depth + 1) // 2
    s_state = None
    for l in range(depth):
        i = l // 2
        g_mix = mix_norm[l][None, :]
        if l % 2 == 0:
            prm = _prep_even(i, w_in, conv_w, conv_b, dt_bias, a_log, d_skip, ssd_gain, gmlp_ln_g,
                             gmlp_ln_b, gmlp_ws, gmlp_bs, w_out_even)
            buf0 = jnp.zeros((nbp, SSD_CONV - 1, CONV_DIM), F32)
            h00 = jnp.zeros((1, nbp, SSD_HEADS, SSD_HEAD_DIM, SSD_STATE), F32)
            hp, buf_p, ssd_p, _ = _even_layer(hp, g_mix, prm, buf0, h00, 0, nb=nbp, sample=False)
            hs, buf_s, s_state, v_s = _even_layer(hs, g_mix, prm, state_conv[i], state_ssd, i, nb=nbs,
                                                  sample=True, state_layers=n_even, state_prev=s_state)
            p_ssd.append(ssd_p.reshape(nbp, SSD_HEADS, SSD_HEAD_DIM, SSD_STATE))
            p_conv.append(buf_p)
            s_conv.append(buf_s)
            s_v.append(v_s)
        else:
            prm = _prep_mla(i, w_mla_down, mla_q_gain, mla_kv_gain, w_mla_uq, w_mla_uk, w_mla_uv, w_mla_o)
            ckv, kpe, qh, kh, vh = _mla_proj_prompt(hp, g_mix, prm, cos_p, sin_p, nb=nbp)
            o_p = _flash(qh, kh, vh)
            hp = _matmul_res(o_p.reshape(nbp * seq, MLA_HEADS * MLA_V), prm["wo"], hp)
            p_ckv.append(ckv.reshape(nbp, seq, KV_RANK))
            p_kpe.append(kpe.reshape(nbp, seq, ROPE))

            ckv_s, kpe_s, kcat_s, q_s = _mla_proj(hs, g_mix, prm, cos_s, sin_s, nb=1)
            q_s = q_s[0].reshape(MLA_HEADS, nbs, dseq, QK).transpose(1, 0, 2, 3).reshape(nbs, MLA_HEADS * dseq, QK)
            newk = jnp.pad(kcat_s.reshape(nbs, dseq, QK), ((0, 0), (0, NEW_KEY_ROWS - dseq), (0, 0)))
            o_s = _decode(page_table, q_s, newk, cache_mla_ckv, cache_kpe_t, i, dseq)
            o_s = o_s.reshape(nbs, MLA_HEADS, dseq, KV_RANK).transpose(1, 0, 2, 3)
            o_s = o_s.reshape(1, MLA_HEADS, nbs * dseq, KV_RANK).astype(BF)
            hs = _mla_out(o_s, prm["wuv"], prm["wo"], hs)
            s_ckv.append(ckv_s.reshape(nbs, dseq, KV_RANK))
            s_kpe.append(kpe_s.reshape(nbs, dseq, ROPE))

        wkv = jnp.concatenate([w_mem_k[l], w_mem_v[l]], axis=1).astype(BF)
        kv = _rms_matmul(mem_prompt.reshape(nbp * mt, D), mem_norm[l][None, :], wkv, tn=MEM_INNER)
        mk_p = kv[:, :MEM_INNER].reshape(nbp, mt, MEM_INNER)
        mv_p = kv[:, MEM_INNER:].reshape(nbp, mt, MEM_INNER)
        g_x = xattn_norm[l][None, :]
        wq = w_mem_q[l].astype(BF)
        wo = w_mem_o[l].astype(BF)
        hp = _mem_prompt(hp, g_x, wq, mk_p, mv_p, wo)
        x8 = jnp.pad(hs.reshape(nbs, dseq, D), ((0, 0), (0, SUBLANES - dseq), (0, 0)))
        hs = _mem_sample(x8, g_x, wq, cache_k4, cache_v4, wo, l)[:, :dseq].reshape(nbs * dseq, D)
        p_mk.append(mk_p.reshape(nbp, mt, MEM_HEADS, MEM_HD))
        p_mv.append(mv_p.reshape(nbp, mt, MEM_HEADS, MEM_HD))

        g_f = ffn_norm[l][None, :]
        if l % 2 == 0:
            wgu = w_ffn_gu[i].astype(BF)
            wd = w_ffn_down[i].astype(BF)
            hp = _ffn(hp, g_f, wgu, wd)
            hs = _ffn(hs, g_f, wgu, wd)
        else:
            wr = jnp.pad(w_router[i], ((0, 0), (0, LANES - N_EXP)))
            final = l == depth - 1
            hp = _moe(hp, g_f, wr, w_exp_gu, w_exp_down, i, final_g, final=final)
            hs = _moe(hs, g_f, wr, w_exp_gu, w_exp_down, i, final_g, final=final)
    if depth % 2 == 1:
        raise NotImplementedError("the final norm is fused into the last routed-expert layer")
    y_prompt = hp.reshape(nbp, seq, D)
    y_sample = hs.reshape(nbs, dseq, D)
    return (y_prompt, y_sample,
            jnp.stack(p_ssd), jnp.stack(p_conv), jnp.stack(p_ckv), jnp.stack(p_kpe),
            jnp.stack(p_mk), jnp.stack(p_mv),
            s_state.reshape(n_even, nbs, SSD_HEADS, SSD_HEAD_DIM, SSD_STATE), jnp.stack(s_conv), jnp.stack(s_v), jnp.stack(s_ckv), jnp.stack(s_kpe))
```

```python
import functools

import jax
import jax.numpy as jnp
from jax import lax
from jax.experimental import pallas as pl
from jax.experimental.pallas import tpu as pltpu
from jax.experimental.pallas import tpu_sc as plsc

F32 = jnp.float32
BF = jnp.bfloat16
EPS = 1e-6
NEG = -1e30

D = 1024
SSD_HEADS = 16
SSD_HEAD_DIM = 64
SSD_INNER = SSD_HEADS * SSD_HEAD_DIM
SSD_GROUPS = 2
SSD_STATE = 128
SSD_CONV = 4
CONV_DIM = SSD_INNER + 2 * SSD_GROUPS * SSD_STATE
GMLP_GROUPS = 8
GMLP_WIDTH = 1024
CHUNK = 128
EVEN_SEQS_PER_STEP = 2
SAMPLE_Q = 16
ZXU = SSD_INNER + CONV_DIM + 2 * GMLP_WIDTH
MLA_HEADS = 8
NOPE = 128
ROPE = 64
MLA_V = 128
Q_RANK = 256
KV_RANK = 256
MLA_SCALE = (NOPE + ROPE) ** -0.5
QK = KV_RANK + ROPE
ROPE_BASE = 10000.0
MEM_HEADS = 4
MEM_HD = 128
MEM_INNER = MEM_HEADS * MEM_HD
D_FF = 2816
N_EXP = 8
D_FFE = 3584
PAGE = 128
LANES = 128
SUBLANES = 8
ROW_TILE = 512
DECODE_SLICES = 4
NEW_KEY_ROWS = 16
SAMPLES_PER_STEP = 8


def _cp(sem, vmem_mb=None):
    kw = dict(dimension_semantics=sem)
    if vmem_mb is not None:
        kw["vmem_limit_bytes"] = vmem_mb * 1024 * 1024
    return pltpu.CompilerParams(**kw)


def _rms(x, g):
    return x * lax.rsqrt(jnp.mean(x * x, axis=-1, keepdims=True) + EPS) * g


def _dot(a, b):
    return jnp.dot(a, b, preferred_element_type=F32)


def _dot_nt(a, b):
    return lax.dot_general(a, b, (((1,), (1,)), ((), ())), preferred_element_type=F32)


def _dot_f32(a, b):
    return jnp.dot(a, b, preferred_element_type=F32, precision=lax.Precision.HIGHEST)


def _silu(x):
    return x * jax.nn.sigmoid(x)


def _full(shape):
    n = len(shape)
    return pl.BlockSpec(shape, lambda *_: (0,) * n)


def _largest_tile(n, cap):
    best = LANES
    for t in range(LANES, cap + 1, LANES):
        if n % t == 0:
            best = t
    return best


def _rms_matmul_kernel(x_ref, g_ref, w_ref, o_ref, xn_ref):
    @pl.when(pl.program_id(1) == 0)
    def _():
        xn_ref[...] = _rms(x_ref[...], g_ref[...]).astype(BF)

    o_ref[...] = _dot(xn_ref[...], w_ref[...])


def _rms_matmul(x, g, w, *, tn):
    t, k = x.shape
    n = w.shape[1]
    tm = min(ROW_TILE, t)
    return pl.pallas_call(
        _rms_matmul_kernel,
        grid=(t // tm, n // tn),
        in_specs=[
            pl.BlockSpec((tm, k), lambda i, j: (i, 0)),
            pl.BlockSpec((1, k), lambda i, j: (0, 0)),
            pl.BlockSpec((k, tn), lambda i, j: (0, j)),
        ],
        out_specs=pl.BlockSpec((tm, tn), lambda i, j: (i, j)),
        out_shape=jax.ShapeDtypeStruct((t, n), F32),
        scratch_shapes=[pltpu.VMEM((tm, k), BF)],
        compiler_params=_cp(("parallel", "arbitrary")),
        name="rms_matmul",
    )(x, g, w)


EVEN_PROJ_CHUNK = 512


def _even_proj_kernel(x_ref, g_ref, w_ref, wdt_ref, o_ref, dt_ref):
    xn = _rms(x_ref[...], g_ref[...])
    dt_ref[...] = _dot_f32(xn, wdt_ref[...])
    xb = xn.astype(BF)
    for lo in range(0, o_ref.shape[1], EVEN_PROJ_CHUNK):
        o_ref[:, lo:lo + EVEN_PROJ_CHUNK] = _dot(xb, w_ref[:, lo:lo + EVEN_PROJ_CHUNK])


def _even_proj(x, g, w, wdt):
    t, k = x.shape
    n = w.shape[1]
    tm = min(ROW_TILE, t)
    return pl.pallas_call(
        _even_proj_kernel,
        grid=(t // tm,),
        in_specs=[
            pl.BlockSpec((tm, k), lambda i: (i, 0)),
            _full((1, k)),
            pl.BlockSpec((k, n), lambda i: (0, 0), pipeline_mode=pl.Buffered(1)),
            pl.BlockSpec((k, LANES), lambda i: (0, 0), pipeline_mode=pl.Buffered(1)),
        ],
        out_specs=[pl.BlockSpec((tm, n), lambda i: (i, 0)), pl.BlockSpec((tm, LANES), lambda i: (i, 0))],
        out_shape=[jax.ShapeDtypeStruct((t, n), F32), jax.ShapeDtypeStruct((t, LANES), F32)],
        compiler_params=_cp(("parallel",), vmem_mb=48),
        name="even_proj",
    )(x, g, w, wdt)


def _matmul_res_kernel(a_ref, w_ref, r_ref, o_ref):
    o_ref[...] = r_ref[...] + _dot(a_ref[...].astype(BF), w_ref[...])


def _matmul_res(a, w, res):
    t, k = a.shape
    n = w.shape[1]
    tn = _largest_tile(n, 1024)
    tm = min(ROW_TILE, t)
    return pl.pallas_call(
        _matmul_res_kernel,
        grid=(t // tm, n // tn),
        in_specs=[
            pl.BlockSpec((tm, k), lambda i, j: (i, 0)),
            pl.BlockSpec((k, tn), lambda i, j: (0, j)),
            pl.BlockSpec((tm, tn), lambda i, j: (i, j)),
        ],
        out_specs=pl.BlockSpec((tm, tn), lambda i, j: (i, j)),
        out_shape=jax.ShapeDtypeStruct((t, n), F32),
        compiler_params=_cp(("parallel", "arbitrary")),
        name="matmul_res",
    )(a, w, res)


FF_CHUNK = 256


def _ffn_kernel(x_ref, g_ref, wgu_ref, wd_ref, o_ref, hid_ref, *, ff):
    x = x_ref[...]
    xn = _rms(x, g_ref[...]).astype(BF)
    for c in range(ff // FF_CHUNK):
        lo = c * FF_CHUNK
        gate = _dot(xn, wgu_ref[:, lo:lo + FF_CHUNK])
        up = _dot(xn, wgu_ref[:, ff + lo:ff + lo + FF_CHUNK])
        hid_ref[:, lo:lo + FF_CHUNK] = (_silu(gate) * up).astype(BF)
    o_ref[...] = x + _dot(hid_ref[...], wd_ref[...])


def _ffn(x, g, wgu, wd):
    t = x.shape[0]
    ff = wd.shape[0]
    tm = min(ROW_TILE, t)
    return pl.pallas_call(
        functools.partial(_ffn_kernel, ff=ff),
        grid=(t // tm,),
        in_specs=[
            pl.BlockSpec((tm, D), lambda i: (i, 0)),
            _full((1, D)),
            pl.BlockSpec((D, 2 * ff), lambda i: (0, 0), pipeline_mode=pl.Buffered(1)),
            pl.BlockSpec((ff, D), lambda i: (0, 0), pipeline_mode=pl.Buffered(1)),
        ],
        out_specs=pl.BlockSpec((tm, D), lambda i: (i, 0)),
        out_shape=jax.ShapeDtypeStruct((t, D), F32),
        scratch_shapes=[pltpu.VMEM((tm, ff), BF)],
        compiler_params=_cp(("parallel",), vmem_mb=48),
        name="ffn",
    )(x, g, wgu, wd)


def _softplus(x):
    return jnp.maximum(x, 0.0) + jnp.log1p(jnp.exp(-jnp.abs(x)))


def _gelu_tanh(x):
    return 0.5 * x * (1.0 + jnp.tanh(0.7978845608028654 * (x + 0.044715 * (x * x * x))))


def _even_kernel(proj_ref, dt_ref, cinit_ref, sinit_ref, cw_ref, cb_ref, dtb_ref, alog_ref,
                 dsk_ref, sg_ref, lng_ref, lnb_ref, ws_ref, bst_ref, e_ref,
                 ymix_ref, cout_ref, sout_ref, v_ref, ext_ref, ht_ref, *, q, lb, lc, nbb, out_slot):
    c = pl.program_id(1)

    @pl.when(c == 0)
    def _():
        for bb in range(nbb):
            ext_ref[bb, 0:SUBLANES, :] = cinit_ref[bb]
            ht_ref[bb] = sinit_ref[0, bb].T

    for bb in range(nbb):
        _even_block(proj_ref, dt_ref, cw_ref, cb_ref, dtb_ref, alog_ref,
                    dsk_ref, sg_ref, lng_ref, lnb_ref, ws_ref, bst_ref, e_ref,
                    ymix_ref, v_ref, ext_ref, ht_ref, bb, q=q, lb=lb, lc=lc)

    @pl.when(c == pl.num_programs(1) - 1)
    def _():
        for bb in range(nbb):
            cout_ref[bb] = ext_ref[bb, SUBLANES + lc - 3:SUBLANES + lc, :]
            for slot in range(sout_ref.shape[0]):
                if slot == out_slot:
                    sout_ref[slot, bb] = ht_ref[bb].T
                else:
                    sout_ref[slot, bb] = jnp.zeros((SSD_INNER, SSD_STATE), F32)


def _even_block(proj_ref, dt_ref, cw_ref, cb_ref, dtb_ref, alog_ref,
                dsk_ref, sg_ref, lng_ref, lnb_ref, ws_ref, bst_ref, e_ref,
                ymix_ref, v_ref, ext_ref, ht_ref, bb, *, q, lb, lc):

    if lb == q:
        p = proj_ref[bb]
        dtr = dt_ref[bb]
    else:
        p = jnp.concatenate([proj_ref[bb], jnp.zeros((q - lb, ZXU), F32)], axis=0)
        dtr = jnp.concatenate([dt_ref[bb], jnp.zeros((q - lb, LANES), F32)], axis=0)
    z = p[:, :SSD_INNER]
    xbc_raw = p[:, SSD_INNER:SSD_INNER + CONV_DIM]
    uv = p[:, SSD_INNER + CONV_DIM:]

    ext_ref[bb, SUBLANES:SUBLANES + q, :] = xbc_raw
    conv = (cb_ref[...] + cw_ref[0:1, :] * ext_ref[bb, 5:5 + q, :] + cw_ref[1:2, :] * ext_ref[bb, 6:6 + q, :]
            + cw_ref[2:3, :] * ext_ref[bb, 7:7 + q, :] + cw_ref[3:4, :] * xbc_raw)
    ext_ref[bb, 0:SUBLANES, :] = ext_ref[bb, q:q + SUBLANES, :]

    xbc = _silu(conv)
    xs = xbc[:, :SSD_INNER]
    gw = SSD_STATE
    bm = [xbc[:, SSD_INNER + g * gw:SSD_INNER + (g + 1) * gw] for g in range(SSD_GROUPS)]
    cm = [xbc[:, SSD_INNER + (SSD_GROUPS + g) * gw:SSD_INNER + (SSD_GROUPS + g + 1) * gw]
          for g in range(SSD_GROUPS)]

    row = lax.broadcasted_iota(jnp.int32, (q, q), 0)
    col = lax.broadcasted_iota(jnp.int32, (q, q), 1)
    causal = row >= col

    dt = _softplus(dtr + dtb_ref[...])
    if lc < q:
        dt = jnp.where(lax.broadcasted_iota(jnp.int32, (q, LANES), 0) < lc, dt, 0.0)
    a = dt * (-jnp.exp(alog_ref[...]))
    a_cum = _dot_f32(causal.astype(F32), a)
    a_cum_t = a_cum.T
    a_last = a_cum[q - 1:q, :]
    decay_end = jnp.exp(a_last - a_cum)
    ea = jnp.exp(a_cum)
    chunk_decay = jnp.exp(a_last)

    cmb = [m.astype(BF) for m in cm]
    cb = [_dot_nt(cmb[g], bm[g].astype(BF)) for g in range(SSD_GROUPS)]
    bt = [bm[g].T.astype(BF) for g in range(SSD_GROUPS)]
    heads_per_group = SSD_HEADS // SSD_GROUPS
    gi = SSD_INNER // SSD_GROUPS

    def per_head_lanes(v):
        hi = v.astype(BF)
        lo = (v - hi.astype(F32)).astype(BF)
        return _dot(hi, e_ref[...]) + _dot(lo, e_ref[...])

    dt_x = per_head_lanes(dt)
    ea_x = per_head_lanes(ea)
    de_x = per_head_lanes(decay_end)
    cd_x = per_head_lanes(jnp.broadcast_to(chunk_decay, (SUBLANES, LANES)))[0:1]
    xdt = xs * dt_x
    xdt_b = xdt.astype(BF)
    xd_b = (xdt * de_x).astype(BF)
    y_off = []
    for g in range(SSD_GROUPS):
        h_old = ht_ref[bb, :, g * gi:(g + 1) * gi]
        y_off.append(_dot(cmb[g], h_old.astype(BF)))
        ht_ref[bb, :, g * gi:(g + 1) * gi] = (h_old * cd_x[:, g * gi:(g + 1) * gi]
                                              + _dot(bt[g], xd_b[:, g * gi:(g + 1) * gi]))

    def decay_weights(r):
        seg = a_cum[:, r:r + 1] - a_cum_t[r:r + 1, :]
        lmat = jnp.where(causal, jnp.exp(jnp.minimum(seg, 0.0)), 0.0)
        return (cb[r // heads_per_group] * lmat).astype(BF)

    first_half = lax.broadcasted_iota(jnp.int32, (q, LANES), 1) < SSD_HEAD_DIM
    y_diag = []
    for k in range(SSD_HEADS // 2):
        xp = xdt_b[:, k * LANES:(k + 1) * LANES]
        y_diag.append(jnp.where(first_half, _dot(decay_weights(2 * k), xp), _dot(decay_weights(2 * k + 1), xp)))
    y = jnp.concatenate(y_diag, axis=1) + jnp.concatenate(y_off, axis=1) * ea_x + dsk_ref[...] * xs
    y = y * _silu(z)
    gi = SSD_INNER // SSD_GROUPS
    yn = [_rms(y[:, g * gi:(g + 1) * gi], sg_ref[:, g * gi:(g + 1) * gi]) for g in range(SSD_GROUPS)]
    ymix_ref[bb, :, 0:SSD_INNER] = jnp.concatenate(yn, axis=1)[:lb].astype(BF)

    uvg = _gelu_tanh(uv)
    u = uvg[:, :GMLP_WIDTH]
    v = uvg[:, GMLP_WIDTH:]
    mu = jnp.mean(v, axis=-1, keepdims=True)
    vc = v - mu
    vn = vc * lax.rsqrt(jnp.mean(vc * vc, axis=-1, keepdims=True) + EPS) * lng_ref[...] + lnb_ref[...]
    if v_ref is not None:
        v_ref[bb] = vn[:lb]
    gd = GMLP_WIDTH // GMLP_GROUPS
    yb = []
    for g in range(GMLP_GROUPS):
        wt = jnp.where(causal, ws_ref[g, :q, :q], 0.0).astype(BF)
        sp = _dot(wt, vn[:, g * gd:(g + 1) * gd].astype(BF)) + bst_ref[:q, g:g + 1]
        yb.append(u[:, g * gd:(g + 1) * gd] * sp)
    ymix_ref[bb, :, SSD_INNER:SSD_INNER + GMLP_WIDTH] = jnp.concatenate(yb, axis=1)[:lb].astype(BF)


EVEN_INPUTS = 15


def _even_entry(*refs, has_v, has_prev, **kw):
    ins = refs[:EVEN_INPUTS]
    k = EVEN_INPUTS + (1 if has_prev else 0)
    outs = refs[k:k + 3]
    k += 3
    v_ref = refs[k] if has_v else None
    k += 1 if has_v else 0
    _even_kernel(*ins, *outs, v_ref, *refs[k:], **kw)


def _even_mixer(proj, dtp, cinit8, sinit, prm, *, q, lb, lc, want_v, layer, state_layers=1, state_prev=None):
    b, lp, _ = proj.shape
    nchunks = lp // lb
    nbb = EVEN_SEQS_PER_STEP if b % EVEN_SEQS_PER_STEP == 0 else 1
    par = [prm["conv_w"], prm["conv_b"], prm["dt_bias"], prm["a_log"], prm["d_skip"], prm["ssd_gain"],
           prm["ln_g"], prm["ln_b"], prm["ws"], prm["bst"], prm["expand"]]
    in_specs = [
        pl.BlockSpec((nbb, lb, ZXU), lambda i, c: (i, c, 0)),
        pl.BlockSpec((nbb, lb, LANES), lambda i, c: (i, c, 0)),
        pl.BlockSpec((nbb, SUBLANES, CONV_DIM), lambda i, c: (i, 0, 0)),
        pl.BlockSpec((1, nbb, SSD_INNER, SSD_STATE), lambda i, c: (layer, i, 0, 0)),
    ] + [_full(w.shape) for w in par]
    out_specs = [
        pl.BlockSpec((nbb, lb, SSD_INNER + GMLP_WIDTH), lambda i, c: (i, c, 0)),
        pl.BlockSpec((nbb, SSD_CONV - 1, CONV_DIM), lambda i, c: (i, 0, 0)),
        (pl.BlockSpec((1, nbb, SSD_INNER, SSD_STATE), lambda i, c: (layer, i, 0, 0))
         if state_prev is not None else
         pl.BlockSpec((state_layers, nbb, SSD_INNER, SSD_STATE), lambda i, c: (0, i, 0, 0))),
    ]
    out_slot = 0 if state_prev is not None else (layer if state_layers > 1 else 0)
    out_shape = [
        jax.ShapeDtypeStruct((b, lp, SSD_INNER + GMLP_WIDTH), BF),
        jax.ShapeDtypeStruct((b, SSD_CONV - 1, CONV_DIM), F32),
        jax.ShapeDtypeStruct((state_layers, b, SSD_INNER, SSD_STATE), F32),
    ]
    if want_v:
        out_specs.append(pl.BlockSpec((nbb, lb, GMLP_WIDTH), lambda i, c: (i, c, 0)))
        out_shape.append(jax.ShapeDtypeStruct((b, lp, GMLP_WIDTH), F32))
    args = [proj, dtp, cinit8, sinit, *par]
    aliases = {}
    if state_prev is not None:
        in_specs.append(pl.BlockSpec(memory_space=pl.ANY))
        aliases = {len(args): 2}
        args.append(state_prev)
    return pl.pallas_call(
        functools.partial(_even_entry, has_v=want_v, has_prev=state_prev is not None,
                          q=q, lb=lb, lc=lc, nbb=nbb, out_slot=out_slot),
        grid=(b // nbb, nchunks),
        in_specs=in_specs,
        out_specs=out_specs,
        out_shape=out_shape,
        input_output_aliases=aliases,
        scratch_shapes=[pltpu.VMEM((nbb, q + 2 * SUBLANES, CONV_DIM), F32),
                        pltpu.VMEM((nbb, SSD_STATE, SSD_INNER), F32)],
        compiler_params=_cp(("parallel", "arbitrary"), vmem_mb=48),
        name="even_mixer",
    )(*args)


def _mla_proj_kernel(x_ref, g_ref, wdq_ref, wdkv_ref, wdk2_ref, qg_ref, kvg_ref, wqn_ref, wqp_ref,
                     wuk_ref, cos_ref, sin_ref, ckv_ref, kpe_ref, kcat_ref, q_ref):
    xn = _rms(x_ref[...], g_ref[...]).astype(BF)
    cqn = _rms(_dot(xn, wdq_ref[...]), qg_ref[...]).astype(BF)
    ckv = _rms(_dot(xn, wdkv_ref[...]), kvg_ref[...])
    kk = _dot(xn, wdk2_ref[...])
    cos = cos_ref[...]
    sin = sin_ref[...]
    kpe = kk[:, :ROPE] * cos[:, :ROPE] + kk[:, ROPE:] * sin[:, :ROPE]
    ckv_ref[...] = ckv
    kpe_ref[...] = kpe
    kcat_ref[:, :KV_RANK] = ckv.astype(BF)
    kcat_ref[:, KV_RANK:] = kpe.astype(BF)
    qn = _dot(cqn, wqn_ref[...])
    qp = _dot(cqn, wqp_ref[...])
    hr = MLA_HEADS * ROPE
    qpe = qp[:, :hr] * cos + qp[:, hr:] * sin
    for h in range(MLA_HEADS):
        ql = _dot(qn[:, h * NOPE:(h + 1) * NOPE].astype(BF), wuk_ref[h])
        q_ref[0, h, :, :KV_RANK] = (ql * MLA_SCALE).astype(BF)
        q_ref[0, h, :, KV_RANK:] = (qpe[:, h * ROPE:(h + 1) * ROPE] * MLA_SCALE).astype(BF)


def _mla_proj(x, g, prm, cos8, sin8, *, nb):
    t = x.shape[0]
    seq = t // nb
    tm = min(ROW_TILE, seq)
    tpb = seq // tm
    w = [prm["wdq"], prm["wdkv"], prm["wdk2"], prm["q_gain"], prm["kv_gain"], prm["wqn"], prm["wqp"],
         prm["wuk"]]
    return pl.pallas_call(
        _mla_proj_kernel,
        grid=(t // tm,),
        in_specs=[pl.BlockSpec((tm, D), lambda i: (i, 0)), _full((1, D))] + [_full(a.shape) for a in w] + [
            pl.BlockSpec((tm, MLA_HEADS * ROPE), lambda i: (i % tpb, 0)),
            pl.BlockSpec((tm, MLA_HEADS * ROPE), lambda i: (i % tpb, 0)),
        ],
        out_specs=[
            pl.BlockSpec((tm, KV_RANK), lambda i: (i, 0)),
            pl.BlockSpec((tm, ROPE), lambda i: (i, 0)),
            pl.BlockSpec((tm, QK), lambda i: (i, 0)),
            pl.BlockSpec((1, MLA_HEADS, tm, QK), lambda i: (i // tpb, 0, i % tpb, 0)),
        ],
        out_shape=[
            jax.ShapeDtypeStruct((t, KV_RANK), F32),
            jax.ShapeDtypeStruct((t, ROPE), F32),
            jax.ShapeDtypeStruct((t, QK), BF),
            jax.ShapeDtypeStruct((nb, MLA_HEADS, seq, QK), BF),
        ],
        compiler_params=_cp(("parallel",)),
        name="mla_proj",
    )(x, g, *w, cos8, sin8)


HEAD_QK = NOPE + ROPE
LOG2E = 1.4426950408889634


def _mla_proj_prompt_kernel(x_ref, g_ref, wdq_ref, wdkv_ref, wdk2_ref, qg_ref, kvg_ref, wqn_ref, wqp_ref,
                            wkn_ref, wv_ref, cos_ref, sin_ref, ckv_ref, kpe_ref, q_ref, k_ref, v_ref):
    xn = _rms(x_ref[...], g_ref[...]).astype(BF)
    cqn = _rms(_dot(xn, wdq_ref[...]), qg_ref[...]).astype(BF)
    ckv = _rms(_dot(xn, wdkv_ref[...]), kvg_ref[...])
    kk = _dot(xn, wdk2_ref[...])
    cos = cos_ref[...]
    sin = sin_ref[...]
    kpe = kk[:, :ROPE] * cos[:, :ROPE] + kk[:, ROPE:] * sin[:, :ROPE]
    ckv_ref[...] = ckv
    kpe_ref[...] = kpe
    ckv_b = ckv.astype(BF)
    kpe_b = kpe.astype(BF)
    kn = _dot(ckv_b, wkn_ref[...])
    vt = _dot_nt(wv_ref[...], ckv_b)
    qn = _dot(cqn, wqn_ref[...])
    qp = _dot(cqn, wqp_ref[...])
    hr = MLA_HEADS * ROPE
    qpe = qp[:, :hr] * cos + qp[:, hr:] * sin
    qscale = MLA_SCALE * LOG2E
    for h in range(MLA_HEADS):
        q_ref[0, h, :, :NOPE] = (qn[:, h * NOPE:(h + 1) * NOPE] * qscale).astype(BF)
        q_ref[0, h, :, NOPE:] = (qpe[:, h * ROPE:(h + 1) * ROPE] * qscale).astype(BF)
        k_ref[0, h, :, :NOPE] = kn[:, h * NOPE:(h + 1) * NOPE].astype(BF)
        k_ref[0, h, :, NOPE:] = kpe_b
        v_ref[0, h, 0] = vt[h * MLA_V:(h + 1) * MLA_V].astype(BF)


def _mla_proj_prompt(x, g, prm, cos8, sin8, *, nb):
    t = x.shape[0]
    seq = t // nb
    tm = min(ROW_TILE, seq)
    tpb = seq // tm
    w = [prm["wdq"], prm["wdkv"], prm["wdk2"], prm["q_gain"], prm["kv_gain"], prm["wqn"], prm["wqp"],
         prm["wkn"], prm["wv"]]
    head_spec = lambda width: pl.BlockSpec((1, MLA_HEADS, tm, width), lambda i: (i // tpb, 0, i % tpb, 0))
    head_shape = lambda width: jax.ShapeDtypeStruct((nb, MLA_HEADS, seq, width), BF)
    return pl.pallas_call(
        _mla_proj_prompt_kernel,
        grid=(t // tm,),
        in_specs=[pl.BlockSpec((tm, D), lambda i: (i, 0)), _full((1, D))] + [_full(a.shape) for a in w] + [
            pl.BlockSpec((tm, MLA_HEADS * ROPE), lambda i: (i % tpb, 0)),
            pl.BlockSpec((tm, MLA_HEADS * ROPE), lambda i: (i % tpb, 0)),
        ],
        out_specs=[
            pl.BlockSpec((tm, KV_RANK), lambda i: (i, 0)),
            pl.BlockSpec((tm, ROPE), lambda i: (i, 0)),
            head_spec(HEAD_QK), head_spec(HEAD_QK),
            pl.BlockSpec((1, MLA_HEADS, 1, MLA_V, tm), lambda i: (i // tpb, 0, i % tpb, 0, 0)),
        ],
        out_shape=[
            jax.ShapeDtypeStruct((t, KV_RANK), F32),
            jax.ShapeDtypeStruct((t, ROPE), F32),
            head_shape(HEAD_QK), head_shape(HEAD_QK),
            jax.ShapeDtypeStruct((nb, MLA_HEADS, tpb, MLA_V, tm), BF),
        ],
        compiler_params=_cp(("parallel",)),
        name="mla_proj_prompt",
    )(x, g, *w, cos8, sin8)


FLASH_HEADS = 4
FLASH_KBLOCKS = 4
FLASH_QSPLIT = 1
DENOM_ROWS = 16


def _flash_kernel(q_ref, k_ref, vt_ref, o_ref, m_ref, acc_ref, *, t):
    qi = pl.program_id(2)
    for hh in range(FLASH_HEADS):
        m_ref[hh] = jnp.full((1, t), NEG, F32)
        acc_ref[hh] = jnp.zeros((MLA_V + DENOM_ROWS, t), F32)
    ones = jnp.ones((DENOM_ROWS, t), BF)

    tq = t // FLASH_QSPLIT

    def block(ki, nblk, masked):
        start = pl.multiple_of(ki * t, t)
        ones_n = jnp.concatenate([ones] * nblk, axis=1)
        for hh in range(FLASH_HEADS):
            kb = k_ref[0, hh, pl.ds(start, nblk * t), :]
            vt = jnp.concatenate([vt_ref[0, hh, ki + n] for n in range(nblk)], axis=1)
            v1 = jnp.concatenate([vt, ones_n], axis=0)
            for qs in range(FLASH_QSPLIT):
                lanes = slice(qs * tq, (qs + 1) * tq)
                st = _dot_nt(kb, q_ref[0, hh, lanes, :])
                if masked:
                    key = lax.broadcasted_iota(jnp.int32, (t, tq), 0)
                    qry = lax.broadcasted_iota(jnp.int32, (t, tq), 1) + qs * tq
                    st = jnp.where(key <= qry, st, NEG)
                m_old = m_ref[hh, :, lanes]
                m_new = jnp.maximum(m_old, jnp.max(st, axis=0, keepdims=True))
                alpha = jnp.exp2(m_old - m_new)
                pt = jnp.exp2(st - m_new).astype(BF)
                acc_ref[hh, :, lanes] = alpha * acc_ref[hh, :, lanes] + _dot(v1, pt)
                m_ref[hh, :, lanes] = m_new

    def body(kk, carry):
        block(kk * FLASH_KBLOCKS, FLASH_KBLOCKS, False)
        return carry

    lax.fori_loop(0, qi // FLASH_KBLOCKS, body, 0)
    rem = qi % FLASH_KBLOCKS
    base = qi - rem
    size = FLASH_KBLOCKS // 2
    while size >= 1:
        @pl.when(rem & size != 0)
        def _(size=size):
            block(base + (rem & ~(2 * size - 1)), size, False)
        size //= 2
    block(qi, 1, True)
    for hh in range(FLASH_HEADS):
        acc = acc_ref[hh]
        o_t = acc[:MLA_V] / acc[MLA_V:MLA_V + 1]
        o_ref[0, :, hh * MLA_V:(hh + 1) * MLA_V] = o_t.T.astype(BF)


def _flash(q, k, vt):
    nb, _, seq, _ = q.shape
    t = vt.shape[-1]
    nh = FLASH_HEADS
    return pl.pallas_call(
        functools.partial(_flash_kernel, t=t),
        grid=(nb, MLA_HEADS // nh, seq // t),
        in_specs=[
            pl.BlockSpec((1, nh, t, HEAD_QK), lambda b, h, i: (b, h, i, 0)),
            pl.BlockSpec((1, nh, seq, HEAD_QK), lambda b, h, i: (b, h, 0, 0), pipeline_mode=pl.Buffered(1)),
            pl.BlockSpec((1, nh, seq // t, MLA_V, t), lambda b, h, i: (b, h, 0, 0, 0),
                         pipeline_mode=pl.Buffered(1)),
        ],
        out_specs=pl.BlockSpec((1, t, nh * MLA_V), lambda b, h, i: (b, i, h)),
        out_shape=jax.ShapeDtypeStruct((nb, seq, MLA_HEADS * MLA_V), BF),
        scratch_shapes=[pltpu.VMEM((nh, 1, t), F32), pltpu.VMEM((nh, MLA_V + DENOM_ROWS, t), F32)],
        compiler_params=_cp(("parallel", "parallel", "arbitrary"), vmem_mb=48),
        name="mla_flash",
    )(q, k, vt)


def _page_copies(pt_ref, ckv_hbm, kpt_hbm, ckbuf, kpbuf, sem, sample, slot, *, layer, n_pages):
    copies = []
    for p in range(n_pages):
        pg = pt_ref[sample, p]
        copies.append(pltpu.make_async_copy(
            ckv_hbm.at[layer, pg], ckbuf.at[slot, pl.ds(p * PAGE, PAGE), :], sem.at[0, slot]))
        copies.append(pltpu.make_async_copy(
            kpt_hbm.at[layer, pg], kpbuf.at[slot, :, pl.ds(p * PAGE, PAGE)], sem.at[1, slot]))
    return copies


def _wait_grouped(copies):
    for c in copies[0::2]:
        c.wait()
    for c in copies[1::2]:
        c.wait()


def _decode_kernel(pt_ref, q_ref, nk_ref, ckv_hbm, kpt_hbm, o_ref, ckbuf, kpbuf, sem, *,
                   dec_seq, layer, n_pages):
    b = pl.program_id(0)
    last = pl.num_programs(0) - 1
    slot = b % 2
    copies = functools.partial(_page_copies, pt_ref, ckv_hbm, kpt_hbm, ckbuf, kpbuf, sem,
                               layer=layer, n_pages=n_pages)

    @pl.when(b == 0)
    def _():
        for c in copies(0, 0):
            c.start()

    _wait_grouped(copies(b, slot))
    nxt = jnp.minimum(b + 1, last)
    for c in copies(nxt, 1 - slot):
        c.start()

    rows = MLA_HEADS * dec_seq
    qm = q_ref[0]
    ql = qm[:, :KV_RANK]
    qp = qm[:, KV_RANK:]
    n_slices = DECODE_SLICES if n_pages % DECODE_SLICES == 0 else 1
    width = n_pages * PAGE // n_slices
    cks, scores = [], []
    for c in range(n_slices):
        ck = ckbuf[slot, c * width:(c + 1) * width, :].astype(BF)
        kp = kpbuf[slot, :, c * width:(c + 1) * width].astype(BF)
        cks.append(ck)
        scores.append(_dot_nt(ql, ck) + _dot(qp, kp))
    nk = nk_ref[0]
    kt = lax.broadcasted_iota(jnp.int32, (rows, NEW_KEY_ROWS), 1)
    qt = lax.broadcasted_iota(jnp.int32, (rows, NEW_KEY_ROWS), 0) % dec_seq
    s_new = jnp.where(kt <= qt, _dot_nt(qm, nk), NEG)
    m = jnp.max(s_new, axis=-1, keepdims=True)
    for s in scores:
        m = jnp.maximum(m, jnp.max(s, axis=-1, keepdims=True))
    p_new = jnp.exp(s_new - m)
    denom = jnp.sum(p_new, axis=-1, keepdims=True)
    acc = _dot(p_new.astype(BF), nk[:, :KV_RANK])
    for s, ck in zip(scores, cks):
        p = jnp.exp(s - m)
        denom = denom + jnp.sum(p, axis=-1, keepdims=True)
        acc = acc + _dot(p.astype(BF), ck)
    o_ref[0] = acc / denom

    @pl.when(b == last)
    def _():
        _wait_grouped(copies(nxt, 1 - slot))


def _decode(page_table, q, newk, cache_ckv, cache_kpe_t, layer, dec_seq):
    nb, n_pages = page_table.shape
    rows = MLA_HEADS * dec_seq
    keys = n_pages * PAGE
    grid_spec = pltpu.PrefetchScalarGridSpec(
        num_scalar_prefetch=1,
        grid=(nb,),
        in_specs=[
            pl.BlockSpec((1, rows, QK), lambda b, pt: (b, 0, 0)),
            pl.BlockSpec((1, NEW_KEY_ROWS, QK), lambda b, pt: (b, 0, 0)),
            pl.BlockSpec(memory_space=pl.ANY),
            pl.BlockSpec(memory_space=pl.ANY),
        ],
        out_specs=pl.BlockSpec((1, rows, KV_RANK), lambda b, pt: (b, 0, 0)),
        scratch_shapes=[pltpu.VMEM((2, keys, KV_RANK), F32), pltpu.VMEM((2, ROPE, keys), F32),
                        pltpu.SemaphoreType.DMA((2, 2))],
    )
    return pl.pallas_call(
        functools.partial(_decode_kernel, dec_seq=dec_seq, layer=layer, n_pages=n_pages),
        grid_spec=grid_spec,
        out_shape=jax.ShapeDtypeStruct((nb, rows, KV_RANK), F32),
        compiler_params=_cp(("arbitrary",), vmem_mb=48),
        name="mla_decode",
    )(page_table, q, newk, cache_ckv, cache_kpe_t)


def _mla_out_kernel(o_ref, wuv_ref, wo_ref, r_ref, out_ref):
    parts = [_dot(o_ref[0, h], wuv_ref[h]).astype(BF) for h in range(MLA_HEADS)]
    out_ref[...] = r_ref[...] + _dot(jnp.concatenate(parts, axis=1), wo_ref[...])


def _mla_out(o_lat, wuv, wo, res):
    nb, _, seq, _ = o_lat.shape
    tm = min(ROW_TILE, seq)
    tpb = seq // tm
    t = nb * seq
    return pl.pallas_call(
        _mla_out_kernel,
        grid=(t // tm,),
        in_specs=[
            pl.BlockSpec((1, MLA_HEADS, tm, KV_RANK), lambda i: (i // tpb, 0, i % tpb, 0)),
            _full(wuv.shape), _full(wo.shape),
            pl.BlockSpec((tm, D), lambda i: (i, 0)),
        ],
        out_specs=pl.BlockSpec((tm, D), lambda i: (i, 0)),
        out_shape=jax.ShapeDtypeStruct((t, D), F32),
        compiler_params=_cp(("parallel",)),
        name="mla_out",
    )(o_lat, wuv, wo, res)


def _softmax_rows(s):
    m = jnp.max(s, axis=-1, keepdims=True)
    p = jnp.exp(s - m)
    return p / jnp.sum(p, axis=-1, keepdims=True)


def _mem_prompt_kernel(x_ref, g_ref, wq_ref, k_ref, v_ref, wo_ref, o_ref):
    x = x_ref[...]
    xn = _rms(x, g_ref[...]).astype(BF)
    qm = (_dot(xn, wq_ref[...]) * MEM_HD ** -0.5).astype(BF)
    km = k_ref[0].astype(BF)
    vm = v_ref[0].astype(BF)
    parts = []
    for h in range(MEM_HEADS):
        sl = slice(h * MEM_HD, (h + 1) * MEM_HD)
        p = _softmax_rows(_dot_nt(qm[:, sl], km[:, sl]))
        parts.append(_dot(p.astype(BF), vm[:, sl]).astype(BF))
    o_ref[...] = x + _dot(jnp.concatenate(parts, axis=1), wo_ref[...])


def _mem_prompt(x, g, wq, km, vm, wo):
    t = x.shape[0]
    nb, mt, _ = km.shape
    seq = t // nb
    tm = min(ROW_TILE, seq)
    tpb = seq // tm
    return pl.pallas_call(
        _mem_prompt_kernel,
        grid=(t // tm,),
        in_specs=[
            pl.BlockSpec((tm, D), lambda i: (i, 0)), _full((1, D)), _full(wq.shape),
            pl.BlockSpec((1, mt, MEM_INNER), lambda i: (i // tpb, 0, 0)),
            pl.BlockSpec((1, mt, MEM_INNER), lambda i: (i // tpb, 0, 0)),
            _full(wo.shape),
        ],
        out_specs=pl.BlockSpec((tm, D), lambda i: (i, 0)),
        out_shape=jax.ShapeDtypeStruct((t, D), F32),
        compiler_params=_cp(("parallel",)),
        name="mem_attn_prompt",
    )(x, g, wq, km, vm, wo)


MEM_ROWS = MEM_HEADS * SUBLANES


def _mem_sample_kernel(x_ref, g_ref, wq_ref, k_ref, v_ref, wo_ref, o_ref):
    ns = SAMPLES_PER_STEP
    x = x_ref[...].reshape(ns * SUBLANES, D)
    xn = _rms(x, g_ref[...]).astype(BF)
    qall = _dot(xn, wq_ref[...]) * MEM_HD ** -0.5
    cols = k_ref.shape[2]
    head_of_row = lax.broadcasted_iota(jnp.int32, (MEM_ROWS, cols), 0) // SUBLANES
    head_of_col = lax.broadcasted_iota(jnp.int32, (MEM_ROWS, cols), 1) % MEM_HEADS
    own = head_of_row == head_of_col
    outs = []
    for s in range(ns):
        qs = qall[s * SUBLANES:(s + 1) * SUBLANES]
        qst = jnp.concatenate([qs[:, h * MEM_HD:(h + 1) * MEM_HD] for h in range(MEM_HEADS)], axis=0)
        sc = jnp.where(own, _dot_nt(qst.astype(BF), k_ref[0, s].astype(BF)), NEG)
        o = _dot(_softmax_rows(sc).astype(BF), v_ref[0, s].astype(BF))
        outs.append(jnp.concatenate([o[h * SUBLANES:(h + 1) * SUBLANES] for h in range(MEM_HEADS)], axis=1))
    out = x + _dot(jnp.concatenate(outs, axis=0).astype(BF), wo_ref[...])
    o_ref[...] = out.reshape(ns, SUBLANES, D)


def _mem_sample(x8, g, wq, cache_k, cache_v, wo, layer):
    nb = x8.shape[0]
    rows = cache_k.shape[2]
    ns = SAMPLES_PER_STEP
    return pl.pallas_call(
        _mem_sample_kernel,
        grid=(nb // ns,),
        in_specs=[
            pl.BlockSpec((ns, SUBLANES, D), lambda i: (i, 0, 0)), _full((1, D)), _full(wq.shape),
            pl.BlockSpec((1, ns, rows, MEM_HD), lambda i: (layer, i, 0, 0)),
            pl.BlockSpec((1, ns, rows, MEM_HD), lambda i: (layer, i, 0, 0)),
            _full(wo.shape),
        ],
        out_specs=pl.BlockSpec((ns, SUBLANES, D), lambda i: (i, 0, 0)),
        out_shape=jax.ShapeDtypeStruct((nb, SUBLANES, D), F32),
        compiler_params=_cp(("parallel",), vmem_mb=48),
        name="mem_attn_sample",
    )(x8, g, wq, cache_k, cache_v, wo)


def _router_kernel(x_ref, g_ref, wr_ref, xn_ref, route_ref):
    xn = _rms(x_ref[...], g_ref[...])
    xn_ref[...] = xn
    lane = lax.broadcasted_iota(jnp.int32, (xn.shape[0], LANES), 1).astype(F32)
    lg = jnp.where(lane < N_EXP, _dot_f32(xn, wr_ref[...]), NEG)
    m1 = jnp.max(lg, axis=-1, keepdims=True)
    i1 = jnp.min(jnp.where(lg == m1, lane, float(LANES)), axis=-1, keepdims=True)
    lg2 = jnp.where(lane == i1, NEG, lg)
    m2 = jnp.max(lg2, axis=-1, keepdims=True)
    i2 = jnp.min(jnp.where(lg2 == m2, lane, float(LANES)), axis=-1, keepdims=True)
    e = jnp.exp(m2 - m1)
    g1 = 1.0 / (1.0 + e)
    g2 = e * g1
    route_ref[...] = jnp.where(lane == 0, i1, jnp.where(lane == 1, i2, jnp.where(lane == 2, g1,
                               jnp.where(lane == 3, g2, 0.0))))


def _router(x, g, wr_pad):
    t = x.shape[0]
    tm = min(ROW_TILE, t)
    return pl.pallas_call(
        _router_kernel,
        grid=(t // tm,),
        in_specs=[pl.BlockSpec((tm, D), lambda i: (i, 0)), _full((1, D)), _full(wr_pad.shape)],
        out_specs=[pl.BlockSpec((tm, D), lambda i: (i, 0)), pl.BlockSpec((tm, LANES), lambda i: (i, 0))],
        out_shape=[jax.ShapeDtypeStruct((t, D), F32), jax.ShapeDtypeStruct((t, LANES), F32)],
        compiler_params=_cp(("parallel",)),
        name="router",
    )(x, g, wr_pad)


SC_ROWS = 32


def _sc_mesh():
    return plsc.VectorSubcoreMesh(core_axis_name="c", subcore_axis_name="s")


def _sc_workers():
    info = plsc.get_sparse_core_info()
    return info.num_cores, info.num_cores * info.num_subcores


def _sc_gather_rows(table, idx):
    n = idx.shape[0]
    width = table.shape[1]
    ncores, nw = _sc_workers()
    per_w = n // nw
    n_chunks = per_w // SC_ROWS
    assert per_w * nw == n and n_chunks * SC_ROWS == per_w

    @functools.partial(
        pl.kernel, mesh=_sc_mesh(),
        out_type=jax.ShapeDtypeStruct((n, width), table.dtype),
        scratch_types=[pltpu.VMEM((per_w,), jnp.int32), pltpu.VMEM((SC_ROWS, width), table.dtype),
                       pltpu.SemaphoreType.DMA],
        name="sc_gather_rows",
    )
    def body(table_hbm, idx_hbm, out_hbm, idx_v, rows_v, sem):
        wid = lax.axis_index("s") * ncores + lax.axis_index("c")
        base = wid * per_w
        pltpu.sync_copy(idx_hbm.at[pl.ds(base, per_w)], idx_v)

        @pl.loop(0, n_chunks)
        def _(j):
            off = pl.multiple_of(j * SC_ROWS, SC_ROWS)
            pltpu.async_copy(table_hbm.at[idx_v.at[pl.ds(off, SC_ROWS)]], rows_v, sem).wait()
            pltpu.sync_copy(rows_v, out_hbm.at[pl.ds(base + off, SC_ROWS)])

    return body(table, idx)


def _sc_scatter_rows(src, idx, n_out):
    n = idx.shape[0]
    t, width = src.shape
    ncores, nw = _sc_workers()
    per_w = n // nw
    n_chunks = per_w // SC_ROWS
    assert per_w * nw == n and n_chunks * SC_ROWS == per_w and t % per_w == 0
    idx3 = idx.reshape(nw, n_chunks, SC_ROWS)

    @functools.partial(
        pl.kernel, mesh=_sc_mesh(),
        out_type=jax.ShapeDtypeStruct((n_out, width), src.dtype),
        scratch_types=[pltpu.VMEM((n_chunks, SC_ROWS), jnp.int32), pltpu.VMEM((SC_ROWS, width), src.dtype),
                       pltpu.SemaphoreType.DMA],
        name="sc_scatter_rows",
    )
    def body(src_hbm, idx_hbm, out_hbm, idx_v, rows_v, sem):
        wid = lax.axis_index("s") * ncores + lax.axis_index("c")
        base = lax.rem(wid * per_w, t)
        pltpu.sync_copy(idx_hbm.at[wid], idx_v)

        @pl.loop(0, n_chunks)
        def _(j):
            off = pl.multiple_of(j * SC_ROWS, SC_ROWS)
            pltpu.sync_copy(src_hbm.at[pl.ds(base + off, SC_ROWS)], rows_v)
            pltpu.async_copy(rows_v, out_hbm.at[idx_v.at[j]], sem).wait()

    return body(src, idx3)


MOE_TILE = 1024
MOE_TILE_SMALL = 256
MOE_BLOCK = 512
MOE_CHUNK = 256


def _moe_ffn_kernel(te_ref, nused_ref, x_ref, wg_ref, wu_ref, wd_ref, o_ref, xb_ref):
    del te_ref
    i = pl.program_id(0)
    j = pl.program_id(1)

    @pl.when(i < nused_ref[0])
    def _():
        @pl.when(j == 0)
        def _():
            xb_ref[...] = x_ref[...].astype(BF)
            o_ref[...] = jnp.zeros_like(o_ref)

        xb = xb_ref[...]
        part = None
        block = wd_ref.shape[2]
        for lo in range(0, block, MOE_CHUNK):
            sl = slice(lo, min(lo + MOE_CHUNK, block))
            gate = _dot(xb, wg_ref[0, 0, :, sl].astype(BF))
            up = _dot(xb, wu_ref[0, 0, :, sl].astype(BF))
            contrib = _dot((_silu(gate) * up).astype(BF), wd_ref[0, 0, sl, :].astype(BF))
            part = contrib if part is None else part + contrib
        o_ref[...] += part

    @pl.when(i >= nused_ref[0])
    def _():
        o_ref[...] = jnp.zeros_like(o_ref)


def _moe_ffn(tile_expert, nused, xs, wgu, wd, layer, *, tm):
    npad = xs.shape[0]
    block = MOE_BLOCK
    nblk = D_FFE // block

    def blk(i, j, nu):
        return jnp.where(i < nu[0], j, nblk - 1)

    grid_spec = pltpu.PrefetchScalarGridSpec(
        num_scalar_prefetch=2,
        grid=(npad // tm, nblk),
        in_specs=[
            pl.BlockSpec((tm, D), lambda i, j, te, nu: (jnp.minimum(i, nu[0] - 1), 0)),
            pl.BlockSpec((1, 1, D, block), lambda i, j, te, nu: (layer, te[i], 0, blk(i, j, nu))),
            pl.BlockSpec((1, 1, D, block), lambda i, j, te, nu: (layer, te[i], 0, nblk + blk(i, j, nu))),
            pl.BlockSpec((1, 1, block, D), lambda i, j, te, nu: (layer, te[i], blk(i, j, nu), 0)),
        ],
        out_specs=pl.BlockSpec((tm, D), lambda i, j, te, nu: (i, 0)),
        scratch_shapes=[pltpu.VMEM((tm, D), BF)],
    )
    return pl.pallas_call(
        _moe_ffn_kernel,
        grid_spec=grid_spec,
        out_shape=jax.ShapeDtypeStruct((npad, D), F32),
        compiler_params=_cp(("parallel", "arbitrary"), vmem_mb=56),
        name="moe_ffn",
    )(tile_expert, nused, xs, wgu, wgu, wd)


ROUTE_GATE_LANE = 2


def _combine_kernel(h_ref, y0_ref, y1_ref, route_ref, g_ref, o_ref, *, final):
    gl = ROUTE_GATE_LANE
    route = route_ref[...]
    out = h_ref[...] + route[:, gl:gl + 1] * y0_ref[...] + route[:, gl + 1:gl + 2] * y1_ref[...]
    if final:
        out = _rms(out, g_ref[...])
    o_ref[...] = out


def _combine(h, y2, route, g, *, final):
    t = h.shape[0]
    tm = min(ROW_TILE, t)
    nt = t // tm
    return pl.pallas_call(
        functools.partial(_combine_kernel, final=final),
        grid=(nt,),
        in_specs=[
            pl.BlockSpec((tm, D), lambda i: (i, 0)),
            pl.BlockSpec((tm, D), lambda i: (i, 0)),
            pl.BlockSpec((tm, D), lambda i: (nt + i, 0)),
            pl.BlockSpec((tm, LANES), lambda i: (i, 0)),
            _full((1, D)),
        ],
        out_specs=pl.BlockSpec((tm, D), lambda i: (i, 0)),
        out_shape=jax.ShapeDtypeStruct((t, D), F32),
        compiler_params=_cp(("parallel",)),
        name="moe_combine",
    )(h, y2, y2, route, g)


def _moe(h, g, wr_pad, wgu, wd, layer, final_g, *, final):
    t = h.shape[0]
    tm = MOE_TILE if 2 * t >= 2 * N_EXP * MOE_TILE else MOE_TILE_SMALL
    xn, route = _router(h, g, wr_pad)
    eidx = route[:, :ROUTE_GATE_LANE].astype(jnp.int32)
    e_flat = eidx.T.reshape(-1)
    onehot = (e_flat[:, None] == jnp.arange(N_EXP, dtype=jnp.int32)[None, :]).astype(jnp.int32)
    csum = jnp.cumsum(onehot, axis=0)
    counts = csum[-1]
    rank = jnp.sum(onehot * csum, axis=1) - 1
    padded = ((counts + tm - 1) // tm) * tm
    ends = jnp.cumsum(padded)
    starts = ends - padded
    dest = (jnp.sum(onehot * starts[None, :], axis=1) + rank).astype(jnp.int32)
    n_tiles = -(-2 * t // tm) + N_EXP
    tile_start = jnp.arange(n_tiles, dtype=jnp.int32) * tm
    tile_expert = jnp.minimum(jnp.sum((tile_start[:, None] >= ends[None, :]).astype(jnp.int32), axis=1),
                              N_EXP - 1).astype(jnp.int32)
    nused = (ends[-1] // tm).astype(jnp.int32).reshape(1)
    tile_expert = jnp.where(jnp.arange(n_tiles) < nused[0], tile_expert, tile_expert[nused[0] - 1])
    xs = _sc_scatter_rows(xn, dest, n_tiles * tm)
    ys = _moe_ffn(tile_expert, nused, xs, wgu, wd, layer, tm=tm)
    y2 = _sc_gather_rows(ys, dest)
    return _combine(h, y2, route, final_g, final=final)


def _rope_tables(pos):
    half = ROPE // 2
    inv_freq = ROPE_BASE ** (-jnp.arange(half, dtype=F32) / half)
    ang = pos.astype(F32)[:, None] * inv_freq
    cos = jnp.cos(ang)
    sin = jnp.sin(ang)
    cc = jnp.concatenate([cos, cos], axis=-1)
    ss = jnp.concatenate([-sin, sin], axis=-1)
    return jnp.tile(cc, (1, MLA_HEADS)), jnp.tile(ss, (1, MLA_HEADS))


def _even_layer(h, g, prm, cinit, sinit, layer, *, nb, sample, state_layers=1, state_prev=None):
    t = h.shape[0]
    seq = t // nb
    proj, dtp = _even_proj(h, g, prm["w_zxu"], prm["w_dt"])
    proj = proj.reshape(nb, seq, ZXU)
    dtp = dtp.reshape(nb, seq, LANES)
    if sample:
        pad = ((0, 0), (0, SUBLANES - seq), (0, 0))
        proj = jnp.pad(proj, pad)
        dtp = jnp.pad(dtp, pad)
        q, lb, lc = SAMPLE_Q, SUBLANES, seq
    else:
        q, lb, lc = CHUNK, CHUNK, CHUNK
    cinit8 = jnp.pad(cinit, ((0, 0), (SUBLANES - (SSD_CONV - 1), 0), (0, 0)))
    outs = _even_mixer(proj, dtp, cinit8, sinit.reshape(-1, nb, SSD_INNER, SSD_STATE), prm,
                       q=q, lb=lb, lc=lc, want_v=sample, layer=layer,
                       state_layers=state_layers, state_prev=state_prev)
    ymix, cout, sout = outs[:3]
    v = None
    if sample:
        ymix = ymix[:, :seq]
        v = outs[3][:, :seq]
    h = _matmul_res(ymix.reshape(t, SSD_INNER + GMLP_WIDTH), prm["w_out"], h)
    return h, cout, sout, v


def _prep_even(i, w_in, conv_w, conv_b, dt_bias, a_log, d_skip, ssd_gain, ln_g, ln_b, ws, bs, w_out):
    w = w_in[i]
    o1 = SSD_INNER + CONV_DIM
    w_zxu = jnp.concatenate([w[:, :o1], w[:, o1 + SSD_HEADS:]], axis=1).astype(BF)
    w_dt = jnp.pad(w[:, o1:o1 + SSD_HEADS], ((0, 0), (0, LANES - SSD_HEADS)))
    padl = (0, LANES - SSD_HEADS)
    return dict(
        w_zxu=w_zxu, w_dt=w_dt,
        conv_w=jnp.pad(conv_w[i], ((0, SUBLANES - SSD_CONV), (0, 0))),
        conv_b=conv_b[i][None, :],
        dt_bias=jnp.pad(dt_bias[i], padl)[None, :],
        a_log=jnp.pad(a_log[i], padl)[None, :],
        d_skip=jnp.repeat(d_skip[i], SSD_HEAD_DIM)[None, :],
        ssd_gain=ssd_gain[i][None, :],
        ln_g=ln_g[i][None, :], ln_b=ln_b[i][None, :],
        ws=ws[i], bst=bs[i].T,
        expand=(jnp.arange(LANES)[:, None] == jnp.arange(SSD_INNER)[None, :] // SSD_HEAD_DIM).astype(BF),
        w_out=w_out[i].astype(BF),
    )


def _prep_mla(i, w_down, q_gain, kv_gain, w_uq, w_uk, w_uv, w_o):
    wd = w_down[i]
    wk = wd[:, Q_RANK + KV_RANK:]
    half = ROPE // 2
    rot = lambda a: jnp.concatenate([a[..., half:], a[..., :half]], axis=-1)
    uq = w_uq[i]
    uq_pe = uq[:, :, NOPE:]
    return dict(
        wdq=wd[:, :Q_RANK].astype(BF),
        wdkv=wd[:, Q_RANK:Q_RANK + KV_RANK].astype(BF),
        wdk2=jnp.concatenate([wk, rot(wk)], axis=1).astype(BF),
        q_gain=q_gain[i][None, :], kv_gain=kv_gain[i][None, :],
        wqn=uq[:, :, :NOPE].reshape(Q_RANK, MLA_HEADS * NOPE).astype(BF),
        wqp=jnp.concatenate([uq_pe.reshape(Q_RANK, -1), rot(uq_pe).reshape(Q_RANK, -1)], axis=1).astype(BF),
        wuk=jnp.transpose(w_uk[i], (1, 2, 0)).astype(BF),
        wuv=jnp.transpose(w_uv[i], (1, 0, 2)).astype(BF),
        wkn=w_uk[i].reshape(KV_RANK, MLA_HEADS * NOPE).astype(BF),
        wv=w_uv[i].reshape(KV_RANK, MLA_HEADS * MLA_V).T.astype(BF),
        wo=w_o[i].astype(BF),
    )


def kernel(x_prompt, x_sample, state_ssd, state_conv, cache_mla_ckv, cache_mla_kpe, cache_mem_k, cache_mem_v, page_table, mem_prompt, mix_norm, w_in, conv_w, conv_b, dt_bias, a_log, d_skip, ssd_gain, gmlp_ln_g, gmlp_ln_b, gmlp_ws, gmlp_bs, w_out_even, w_mla_down, mla_q_gain, mla_kv_gain, w_mla_uq, w_mla_uk, w_mla_uv, w_mla_o, xattn_norm, mem_norm, w_mem_q, w_mem_k, w_mem_v, w_mem_o, ffn_norm, w_ffn_gu, w_ffn_down, w_router, w_exp_gu, w_exp_down, final_norm):
    nbp, seq, _ = x_prompt.shape
    nbs, dseq, _ = x_sample.shape
    depth = mix_norm.shape[0]
    past = page_table.shape[1] * PAGE
    mt = mem_prompt.shape[1]
    hp = x_prompt.reshape(nbp * seq, D)
    hs = x_sample.reshape(nbs * dseq, D)
    cos_p, sin_p = _rope_tables(jnp.arange(seq, dtype=jnp.int32))
    cos_s, sin_s = _rope_tables(past + jnp.arange(dseq, dtype=jnp.int32))
    cos_s = jnp.tile(cos_s, (nbs, 1))
    sin_s = jnp.tile(sin_s, (nbs, 1))
    cache_k4 = cache_mem_k.reshape(depth, nbs, mt * MEM_HEADS, MEM_HD)
    cache_v4 = cache_mem_v.reshape(depth, nbs, mt * MEM_HEADS, MEM_HD)
    cache_kpe_t = jnp.swapaxes(cache_mla_kpe, 2, 3)
    final_g = final_norm[None, :]

    p_ssd, p_conv, p_ckv, p_kpe, p_mk, p_mv = [], [], [], [], [], []
    s_conv, s_v, s_ckv, s_kpe = [], [], [], []
    n_even = (depth + 1) // 2
    s_state = None
    for l in range(depth):
        i = l // 2
        g_mix = mix_norm[l][None, :]
        if l % 2 == 0:
            prm = _prep_even(i, w_in, conv_w, conv_b, dt_bias, a_log, d_skip, ssd_gain, gmlp_ln_g,
                             gmlp_ln_b, gmlp_ws, gmlp_bs, w_out_even)
            buf0 = jnp.zeros((nbp, SSD_CONV - 1, CONV_DIM), F32)
            h00 = jnp.zeros((1, nbp, SSD_HEADS, SSD_HEAD_DIM, SSD_STATE), F32)
            hp, buf_p, ssd_p, _ = _even_layer(hp, g_mix, prm, buf0, h00, 0, nb=nbp, sample=False)
            hs, buf_s, s_state, v_s = _even_layer(hs, g_mix, prm, state_conv[i], state_ssd, i, nb=nbs,
                                                  sample=True, state_layers=n_even, state_prev=s_state)
            p_ssd.append(ssd_p.reshape(nbp, SSD_HEADS, SSD_HEAD_DIM, SSD_STATE))
            p_conv.append(buf_p)
            s_conv.append(buf_s)
            s_v.append(v_s)
        else:
            prm = _prep_mla(i, w_mla_down, mla_q_gain, mla_kv_gain, w_mla_uq, w_mla_uk, w_mla_uv, w_mla_o)
            ckv, kpe, qh, kh, vh = _mla_proj_prompt(hp, g_mix, prm, cos_p, sin_p, nb=nbp)
            o_p = _flash(qh, kh, vh)
            hp = _matmul_res(o_p.reshape(nbp * seq, MLA_HEADS * MLA_V), prm["wo"], hp)
            p_ckv.append(ckv.reshape(nbp, seq, KV_RANK))
            p_kpe.append(kpe.reshape(nbp, seq, ROPE))

            ckv_s, kpe_s, kcat_s, q_s = _mla_proj(hs, g_mix, prm, cos_s, sin_s, nb=1)
            q_s = q_s[0].reshape(MLA_HEADS, nbs, dseq, QK).transpose(1, 0, 2, 3).reshape(nbs, MLA_HEADS * dseq, QK)
            newk = jnp.pad(kcat_s.reshape(nbs, dseq, QK), ((0, 0), (0, NEW_KEY_ROWS - dseq), (0, 0)))
            o_s = _decode(page_table, q_s, newk, cache_mla_ckv, cache_kpe_t, i, dseq)
            o_s = o_s.reshape(nbs, MLA_HEADS, dseq, KV_RANK).transpose(1, 0, 2, 3)
            o_s = o_s.reshape(1, MLA_HEADS, nbs * dseq, KV_RANK).astype(BF)
            hs = _mla_out(o_s, prm["wuv"], prm["wo"], hs)
            s_ckv.append(ckv_s.reshape(nbs, dseq, KV_RANK))
            s_kpe.append(kpe_s.reshape(nbs, dseq, ROPE))

        wkv = jnp.concatenate([w_mem_k[l], w_mem_v[l]], axis=1).astype(BF)
        kv = _rms_matmul(mem_prompt.reshape(nbp * mt, D), mem_norm[l][None, :], wkv, tn=MEM_INNER)
        mk_p = kv[:, :MEM_INNER].reshape(nbp, mt, MEM_INNER)
        mv_p = kv[:, MEM_INNER:].reshape(nbp, mt, MEM_INNER)
        g_x = xattn_norm[l][None, :]
        wq = w_mem_q[l].astype(BF)
        wo = w_mem_o[l].astype(BF)
        hp = _mem_prompt(hp, g_x, wq, mk_p, mv_p, wo)
        x8 = jnp.pad(hs.reshape(nbs, dseq, D), ((0, 0), (0, SUBLANES - dseq), (0, 0)))
        hs = _mem_sample(x8, g_x, wq, cache_k4, cache_v4, wo, l)[:, :dseq].reshape(nbs * dseq, D)
        p_mk.append(mk_p.reshape(nbp, mt, MEM_HEADS, MEM_HD))
        p_mv.append(mv_p.reshape(nbp, mt, MEM_HEADS, MEM_HD))

        g_f = ffn_norm[l][None, :]
        if l % 2 == 0:
            wgu = w_ffn_gu[i].astype(BF)
            wd = w_ffn_down[i].astype(BF)
            hp = _ffn(hp, g_f, wgu, wd)
            hs = _ffn(hs, g_f, wgu, wd)
        else:
            wr = jnp.pad(w_router[i], ((0, 0), (0, LANES - N_EXP)))
            final = l == depth - 1
            hp = _moe(hp, g_f, wr, w_exp_gu, w_exp_down, i, final_g, final=final)
            hs = _moe(hs, g_f, wr, w_exp_gu, w_exp_down, i, final_g, final=final)
    if depth % 2 == 1:
        raise NotImplementedError("the final norm is fused into the last routed-expert layer")
    y_prompt = hp.reshape(nbp, seq, D)
    y_sample = hs.reshape(nbs, dseq, D)
    return (y_prompt, y_sample,
            jnp.stack(p_ssd), jnp.stack(p_conv), jnp.stack(p_ckv), jnp.stack(p_kpe),
            jnp.stack(p_mk), jnp.stack(p_mv),
            s_state.reshape(n_even, nbs, SSD_HEADS, SSD_HEAD_DIM, SSD_STATE), jnp.stack(s_conv), jnp.stack(s_v), jnp.stack(s_ckv), jnp.stack(s_kpe))
```

```python
import functools

import jax
import jax.numpy as jnp
from jax import lax
from jax.experimental import pallas as pl
from jax.experimental.pallas import tpu as pltpu
from jax.experimental.pallas import tpu_sc as plsc

F32 = jnp.float32
BF = jnp.bfloat16
EPS = 1e-6
NEG = -1e30

D = 1024
SSD_HEADS = 16
SSD_HEAD_DIM = 64
SSD_INNER = SSD_HEADS * SSD_HEAD_DIM
SSD_GROUPS = 2
SSD_STATE = 128
SSD_CONV = 4
CONV_DIM = SSD_INNER + 2 * SSD_GROUPS * SSD_STATE
GMLP_GROUPS = 8
GMLP_WIDTH = 1024
CHUNK = 128
EVEN_SEQS_PER_STEP = 2
SAMPLE_Q = 16
ZXU = SSD_INNER + CONV_DIM + 2 * GMLP_WIDTH
MLA_HEADS = 8
NOPE = 128
ROPE = 64
MLA_V = 128
Q_RANK = 256
KV_RANK = 256
MLA_SCALE = (NOPE + ROPE) ** -0.5
QK = KV_RANK + ROPE
ROPE_BASE = 10000.0
MEM_HEADS = 4
MEM_HD = 128
MEM_INNER = MEM_HEADS * MEM_HD
N_EXP = 8
D_FFE = 3584
PAGE = 128
LANES = 128
SUBLANES = 8
ROW_TILE = 512
DECODE_SLICES = 4
NEW_KEY_ROWS = 16
SAMPLES_PER_STEP = 8


def _cp(sem, vmem_mb=None):
    kw = dict(dimension_semantics=sem)
    if vmem_mb is not None:
        kw["vmem_limit_bytes"] = vmem_mb * 1024 * 1024
    return pltpu.CompilerParams(**kw)


def _rms(x, g):
    return x * lax.rsqrt(jnp.mean(x * x, axis=-1, keepdims=True) + EPS) * g


def _dot(a, b):
    return jnp.dot(a, b, preferred_element_type=F32)


def _dot_nt(a, b):
    return lax.dot_general(a, b, (((1,), (1,)), ((), ())), preferred_element_type=F32)


def _dot_f32(a, b):
    return jnp.dot(a, b, preferred_element_type=F32, precision=lax.Precision.HIGHEST)


def _silu(x):
    return x * jax.nn.sigmoid(x)


def _full(shape):
    n = len(shape)
    return pl.BlockSpec(shape, lambda *_: (0,) * n)


def _largest_tile(n, cap):
    best = LANES
    for t in range(LANES, cap + 1, LANES):
        if n % t == 0:
            best = t
    return best


def _rms_matmul_kernel(x_ref, g_ref, w_ref, o_ref, xn_ref):
    @pl.when(pl.program_id(1) == 0)
    def _():
        xn_ref[...] = _rms(x_ref[...], g_ref[...]).astype(BF)

    o_ref[...] = _dot(xn_ref[...], w_ref[...])


def _rms_matmul(x, g, w, *, tn):
    t, k = x.shape
    n = w.shape[1]
    tm = min(ROW_TILE, t)
    return pl.pallas_call(
        _rms_matmul_kernel,
        grid=(t // tm, n // tn),
        in_specs=[
            pl.BlockSpec((tm, k), lambda i, j: (i, 0)),
            pl.BlockSpec((1, k), lambda i, j: (0, 0)),
            pl.BlockSpec((k, tn), lambda i, j: (0, j)),
        ],
        out_specs=pl.BlockSpec((tm, tn), lambda i, j: (i, j)),
        out_shape=jax.ShapeDtypeStruct((t, n), F32),
        scratch_shapes=[pltpu.VMEM((tm, k), BF)],
        compiler_params=_cp(("parallel", "arbitrary")),
        name="rms_matmul",
    )(x, g, w)


EVEN_PROJ_CHUNK = 512


def _even_proj_kernel(x_ref, g_ref, w_ref, wdt_ref, o_ref, dt_ref):
    xn = _rms(x_ref[...], g_ref[...])
    dt_ref[...] = _dot_f32(xn, wdt_ref[...])
    xb = xn.astype(BF)
    for lo in range(0, o_ref.shape[1], EVEN_PROJ_CHUNK):
        o_ref[:, lo:lo + EVEN_PROJ_CHUNK] = _dot(xb, w_ref[:, lo:lo + EVEN_PROJ_CHUNK])


def _even_proj(x, g, w, wdt):
    t, k = x.shape
    n = w.shape[1]
    tm = min(ROW_TILE, t)
    return pl.pallas_call(
        _even_proj_kernel,
        grid=(t // tm,),
        in_specs=[
            pl.BlockSpec((tm, k), lambda i: (i, 0)),
            _full((1, k)),
            pl.BlockSpec((k, n), lambda i: (0, 0), pipeline_mode=pl.Buffered(1)),
            pl.BlockSpec((k, LANES), lambda i: (0, 0), pipeline_mode=pl.Buffered(1)),
        ],
        out_specs=[pl.BlockSpec((tm, n), lambda i: (i, 0)), pl.BlockSpec((tm, LANES), lambda i: (i, 0))],
        out_shape=[jax.ShapeDtypeStruct((t, n), F32), jax.ShapeDtypeStruct((t, LANES), F32)],
        compiler_params=_cp(("parallel",), vmem_mb=48),
        name="even_proj",
    )(x, g, w, wdt)


def _matmul_res_kernel(a_ref, w_ref, r_ref, o_ref):
    o_ref[...] = r_ref[...] + _dot(a_ref[...].astype(BF), w_ref[...])


def _matmul_res(a, w, res):
    t, k = a.shape
    n = w.shape[1]
    tn = _largest_tile(n, 1024)
    tm = min(ROW_TILE, t)
    return pl.pallas_call(
        _matmul_res_kernel,
        grid=(t // tm, n // tn),
        in_specs=[
            pl.BlockSpec((tm, k), lambda i, j: (i, 0)),
            pl.BlockSpec((k, tn), lambda i, j: (0, j)),
            pl.BlockSpec((tm, tn), lambda i, j: (i, j)),
        ],
        out_specs=pl.BlockSpec((tm, tn), lambda i, j: (i, j)),
        out_shape=jax.ShapeDtypeStruct((t, n), F32),
        compiler_params=_cp(("parallel", "arbitrary")),
        name="matmul_res",
    )(a, w, res)


FF_CHUNK = 256


def _ffn_kernel(x_ref, g_ref, wgu_ref, wd_ref, o_ref, hid_ref, *, ff):
    x = x_ref[...]
    xn = _rms(x, g_ref[...]).astype(BF)
    for c in range(ff // FF_CHUNK):
        lo = c * FF_CHUNK
        gate = _dot(xn, wgu_ref[:, lo:lo + FF_CHUNK])
        up = _dot(xn, wgu_ref[:, ff + lo:ff + lo + FF_CHUNK])
        hid_ref[:, lo:lo + FF_CHUNK] = (_silu(gate) * up).astype(BF)
    o_ref[...] = x + _dot(hid_ref[...], wd_ref[...])


def _ffn(x, g, wgu, wd):
    t = x.shape[0]
    ff = wd.shape[0]
    tm = min(ROW_TILE, t)
    return pl.pallas_call(
        functools.partial(_ffn_kernel, ff=ff),
        grid=(t // tm,),
        in_specs=[
            pl.BlockSpec((tm, D), lambda i: (i, 0)),
            _full((1, D)),
            pl.BlockSpec((D, 2 * ff), lambda i: (0, 0), pipeline_mode=pl.Buffered(1)),
            pl.BlockSpec((ff, D), lambda i: (0, 0), pipeline_mode=pl.Buffered(1)),
        ],
        out_specs=pl.BlockSpec((tm, D), lambda i: (i, 0)),
        out_shape=jax.ShapeDtypeStruct((t, D), F32),
        scratch_shapes=[pltpu.VMEM((tm, ff), BF)],
        compiler_params=_cp(("parallel",), vmem_mb=48),
        name="ffn",
    )(x, g, wgu, wd)


def _softplus(x):
    return jnp.maximum(x, 0.0) + jnp.log1p(jnp.exp(-jnp.abs(x)))


def _gelu_tanh(x):
    return 0.5 * x * (1.0 + jnp.tanh(0.7978845608028654 * (x + 0.044715 * (x * x * x))))


def _even_kernel(proj_ref, dt_ref, cinit_ref, sinit_ref, cw_ref, cb_ref, dtb_ref, alog_ref,
                 dsk_ref, sg_ref, lng_ref, lnb_ref, ws_ref, bst_ref, e_ref,
                 ymix_ref, cout_ref, sout_ref, v_ref, ext_ref, ht_ref, *, q, lb, lc, nbb, out_slot):
    c = pl.program_id(1)

    @pl.when(c == 0)
    def _():
        for bb in range(nbb):
            ext_ref[bb, 0:SUBLANES, :] = cinit_ref[bb]
            ht_ref[bb] = sinit_ref[0, bb].T

    for bb in range(nbb):
        _even_block(proj_ref, dt_ref, cw_ref, cb_ref, dtb_ref, alog_ref,
                    dsk_ref, sg_ref, lng_ref, lnb_ref, ws_ref, bst_ref, e_ref,
                    ymix_ref, v_ref, ext_ref, ht_ref, bb, q=q, lb=lb, lc=lc)

    @pl.when(c == pl.num_programs(1) - 1)
    def _():
        for bb in range(nbb):
            cout_ref[bb] = ext_ref[bb, SUBLANES + lc - 3:SUBLANES + lc, :]
            for slot in range(sout_ref.shape[0]):
                if slot == out_slot:
                    sout_ref[slot, bb] = ht_ref[bb].T
                else:
                    sout_ref[slot, bb] = jnp.zeros((SSD_INNER, SSD_STATE), F32)


def _even_block(proj_ref, dt_ref, cw_ref, cb_ref, dtb_ref, alog_ref,
                dsk_ref, sg_ref, lng_ref, lnb_ref, ws_ref, bst_ref, e_ref,
                ymix_ref, v_ref, ext_ref, ht_ref, bb, *, q, lb, lc):

    if lb == q:
        p = proj_ref[bb]
        dtr = dt_ref[bb]
    else:
        p = jnp.concatenate([proj_ref[bb], jnp.zeros((q - lb, ZXU), F32)], axis=0)
        dtr = jnp.concatenate([dt_ref[bb], jnp.zeros((q - lb, LANES), F32)], axis=0)
    z = p[:, :SSD_INNER]
    xbc_raw = p[:, SSD_INNER:SSD_INNER + CONV_DIM]
    uv = p[:, SSD_INNER + CONV_DIM:]

    ext_ref[bb, SUBLANES:SUBLANES + q, :] = xbc_raw
    conv = (cb_ref[...] + cw_ref[0:1, :] * ext_ref[bb, 5:5 + q, :] + cw_ref[1:2, :] * ext_ref[bb, 6:6 + q, :]
            + cw_ref[2:3, :] * ext_ref[bb, 7:7 + q, :] + cw_ref[3:4, :] * xbc_raw)
    ext_ref[bb, 0:SUBLANES, :] = ext_ref[bb, q:q + SUBLANES, :]

    xbc = _silu(conv)
    xs = xbc[:, :SSD_INNER]
    gw = SSD_STATE
    bm = [xbc[:, SSD_INNER + g * gw:SSD_INNER + (g + 1) * gw] for g in range(SSD_GROUPS)]
    cm = [xbc[:, SSD_INNER + (SSD_GROUPS + g) * gw:SSD_INNER + (SSD_GROUPS + g + 1) * gw]
          for g in range(SSD_GROUPS)]

    row = lax.broadcasted_iota(jnp.int32, (q, q), 0)
    col = lax.broadcasted_iota(jnp.int32, (q, q), 1)
    causal = row >= col

    dt = _softplus(dtr + dtb_ref[...])
    if lc < q:
        dt = jnp.where(lax.broadcasted_iota(jnp.int32, (q, LANES), 0) < lc, dt, 0.0)
    a = dt * (-jnp.exp(alog_ref[...]))
    a_cum = _dot_f32(causal.astype(F32), a)
    a_cum_t = a_cum.T
    a_last = a_cum[q - 1:q, :]
    decay_end = jnp.exp(a_last - a_cum)
    ea = jnp.exp(a_cum)
    chunk_decay = jnp.exp(a_last)

    cmb = [m.astype(BF) for m in cm]
    cb = [_dot_nt(cmb[g], bm[g].astype(BF)) for g in range(SSD_GROUPS)]
    bt = [bm[g].T.astype(BF) for g in range(SSD_GROUPS)]
    heads_per_group = SSD_HEADS // SSD_GROUPS
    gi = SSD_INNER // SSD_GROUPS

    def per_head_lanes(v):
        hi = v.astype(BF)
        lo = (v - hi.astype(F32)).astype(BF)
        return _dot(hi, e_ref[...]) + _dot(lo, e_ref[...])

    dt_x = per_head_lanes(dt)
    ea_x = per_head_lanes(ea)
    de_x = per_head_lanes(decay_end)
    cd_x = per_head_lanes(jnp.broadcast_to(chunk_decay, (SUBLANES, LANES)))[0:1]
    xdt = xs * dt_x
    xdt_b = xdt.astype(BF)
    xd_b = (xdt * de_x).astype(BF)
    y_off = []
    for g in range(SSD_GROUPS):
        h_old = ht_ref[bb, :, g * gi:(g + 1) * gi]
        y_off.append(_dot(cmb[g], h_old.astype(BF)))
        ht_ref[bb, :, g * gi:(g + 1) * gi] = (h_old * cd_x[:, g * gi:(g + 1) * gi]
                                              + _dot(bt[g], xd_b[:, g * gi:(g + 1) * gi]))

    def decay_weights(r):
        seg = a_cum[:, r:r + 1] - a_cum_t[r:r + 1, :]
        lmat = jnp.where(causal, jnp.exp(jnp.minimum(seg, 0.0)), 0.0)
        return (cb[r // heads_per_group] * lmat).astype(BF)

    first_half = lax.broadcasted_iota(jnp.int32, (q, LANES), 1) < SSD_HEAD_DIM
    y_diag = []
    for k in range(SSD_HEADS // 2):
        xp = xdt_b[:, k * LANES:(k + 1) * LANES]
        y_diag.append(jnp.where(first_half, _dot(decay_weights(2 * k), xp), _dot(decay_weights(2 * k + 1), xp)))
    y = jnp.concatenate(y_diag, axis=1) + jnp.concatenate(y_off, axis=1) * ea_x + dsk_ref[...] * xs
    y = y * _silu(z)
    gi = SSD_INNER // SSD_GROUPS
    yn = [_rms(y[:, g * gi:(g + 1) * gi], sg_ref[:, g * gi:(g + 1) * gi]) for g in range(SSD_GROUPS)]
    ymix_ref[bb, :, 0:SSD_INNER] = jnp.concatenate(yn, axis=1)[:lb].astype(BF)

    uvg = _gelu_tanh(uv)
    u = uvg[:, :GMLP_WIDTH]
    v = uvg[:, GMLP_WIDTH:]
    mu = jnp.mean(v, axis=-1, keepdims=True)
    vc = v - mu
    vn = vc * lax.rsqrt(jnp.mean(vc * vc, axis=-1, keepdims=True) + EPS) * lng_ref[...] + lnb_ref[...]
    if v_ref is not None:
        v_ref[bb] = vn[:lb]
    gd = GMLP_WIDTH // GMLP_GROUPS
    yb = []
    for g in range(GMLP_GROUPS):
        wt = jnp.where(causal, ws_ref[g, :q, :q], 0.0).astype(BF)
        sp = _dot(wt, vn[:, g * gd:(g + 1) * gd].astype(BF)) + bst_ref[:q, g:g + 1]
        yb.append(u[:, g * gd:(g + 1) * gd] * sp)
    ymix_ref[bb, :, SSD_INNER:SSD_INNER + GMLP_WIDTH] = jnp.concatenate(yb, axis=1)[:lb].astype(BF)


EVEN_INPUTS = 15


def _even_entry(*refs, has_v, has_prev, **kw):
    ins = refs[:EVEN_INPUTS]
    k = EVEN_INPUTS + (1 if has_prev else 0)
    outs = refs[k:k + 3]
    k += 3
    v_ref = refs[k] if has_v else None
    k += 1 if has_v else 0
    _even_kernel(*ins, *outs, v_ref, *refs[k:], **kw)


def _even_mixer(proj, dtp, cinit8, sinit, prm, *, q, lb, lc, want_v, layer, state_layers=1, state_prev=None):
    b, lp, _ = proj.shape
    nchunks = lp // lb
    nbb = EVEN_SEQS_PER_STEP if b % EVEN_SEQS_PER_STEP == 0 else 1
    par = [prm["conv_w"], prm["conv_b"], prm["dt_bias"], prm["a_log"], prm["d_skip"], prm["ssd_gain"],
           prm["ln_g"], prm["ln_b"], prm["ws"], prm["bst"], prm["expand"]]
    in_specs = [
        pl.BlockSpec((nbb, lb, ZXU), lambda i, c: (i, c, 0)),
        pl.BlockSpec((nbb, lb, LANES), lambda i, c: (i, c, 0)),
        pl.BlockSpec((nbb, SUBLANES, CONV_DIM), lambda i, c: (i, 0, 0)),
        pl.BlockSpec((1, nbb, SSD_INNER, SSD_STATE), lambda i, c: (layer, i, 0, 0)),
    ] + [_full(w.shape) for w in par]
    out_specs = [
        pl.BlockSpec((nbb, lb, SSD_INNER + GMLP_WIDTH), lambda i, c: (i, c, 0)),
        pl.BlockSpec((nbb, SSD_CONV - 1, CONV_DIM), lambda i, c: (i, 0, 0)),
        (pl.BlockSpec((1, nbb, SSD_INNER, SSD_STATE), lambda i, c: (layer, i, 0, 0))
         if state_prev is not None else
         pl.BlockSpec((state_layers, nbb, SSD_INNER, SSD_STATE), lambda i, c: (0, i, 0, 0))),
    ]
    out_slot = 0 if state_prev is not None else (layer if state_layers > 1 else 0)
    out_shape = [
        jax.ShapeDtypeStruct((b, lp, SSD_INNER + GMLP_WIDTH), BF),
        jax.ShapeDtypeStruct((b, SSD_CONV - 1, CONV_DIM), F32),
        jax.ShapeDtypeStruct((state_layers, b, SSD_INNER, SSD_STATE), F32),
    ]
    if want_v:
        out_specs.append(pl.BlockSpec((nbb, lb, GMLP_WIDTH), lambda i, c: (i, c, 0)))
        out_shape.append(jax.ShapeDtypeStruct((b, lp, GMLP_WIDTH), F32))
    args = [proj, dtp, cinit8, sinit, *par]
    aliases = {}
    if state_prev is not None:
        in_specs.append(pl.BlockSpec(memory_space=pl.ANY))
        aliases = {len(args): 2}
        args.append(state_prev)
    return pl.pallas_call(
        functools.partial(_even_entry, has_v=want_v, has_prev=state_prev is not None,
                          q=q, lb=lb, lc=lc, nbb=nbb, out_slot=out_slot),
        grid=(b // nbb, nchunks),
        in_specs=in_specs,
        out_specs=out_specs,
        out_shape=out_shape,
        input_output_aliases=aliases,
        scratch_shapes=[pltpu.VMEM((nbb, q + 2 * SUBLANES, CONV_DIM), F32),
                        pltpu.VMEM((nbb, SSD_STATE, SSD_INNER), F32)],
        compiler_params=_cp(("parallel", "arbitrary"), vmem_mb=48),
        name="even_mixer",
    )(*args)


def _mla_proj_kernel(x_ref, g_ref, wdq_ref, wdkv_ref, wdk2_ref, qg_ref, kvg_ref, wqn_ref, wqp_ref,
                     wuk_ref, cos_ref, sin_ref, ckv_ref, kpe_ref, kcat_ref, q_ref):
    xn = _rms(x_ref[...], g_ref[...]).astype(BF)
    cqn = _rms(_dot(xn, wdq_ref[...]), qg_ref[...]).astype(BF)
    ckv = _rms(_dot(xn, wdkv_ref[...]), kvg_ref[...])
    kk = _dot(xn, wdk2_ref[...])
    cos = cos_ref[...]
    sin = sin_ref[...]
    kpe = kk[:, :ROPE] * cos[:, :ROPE] + kk[:, ROPE:] * sin[:, :ROPE]
    ckv_ref[...] = ckv
    kpe_ref[...] = kpe
    kcat_ref[:, :KV_RANK] = ckv.astype(BF)
    kcat_ref[:, KV_RANK:] = kpe.astype(BF)
    qn = _dot(cqn, wqn_ref[...])
    qp = _dot(cqn, wqp_ref[...])
    hr = MLA_HEADS * ROPE
    qpe = qp[:, :hr] * cos + qp[:, hr:] * sin
    for h in range(MLA_HEADS):
        ql = _dot(qn[:, h * NOPE:(h + 1) * NOPE].astype(BF), wuk_ref[h])
        q_ref[0, h, :, :KV_RANK] = (ql * MLA_SCALE).astype(BF)
        q_ref[0, h, :, KV_RANK:] = (qpe[:, h * ROPE:(h + 1) * ROPE] * MLA_SCALE).astype(BF)


def _mla_proj(x, g, prm, cos8, sin8, *, nb):
    t = x.shape[0]
    seq = t // nb
    tm = min(ROW_TILE, seq)
    tpb = seq // tm
    w = [prm["wdq"], prm["wdkv"], prm["wdk2"], prm["q_gain"], prm["kv_gain"], prm["wqn"], prm["wqp"],
         prm["wuk"]]
    return pl.pallas_call(
        _mla_proj_kernel,
        grid=(t // tm,),
        in_specs=[pl.BlockSpec((tm, D), lambda i: (i, 0)), _full((1, D))] + [_full(a.shape) for a in w] + [
            pl.BlockSpec((tm, MLA_HEADS * ROPE), lambda i: (i % tpb, 0)),
            pl.BlockSpec((tm, MLA_HEADS * ROPE), lambda i: (i % tpb, 0)),
        ],
        out_specs=[
            pl.BlockSpec((tm, KV_RANK), lambda i: (i, 0)),
            pl.BlockSpec((tm, ROPE), lambda i: (i, 0)),
            pl.BlockSpec((tm, QK), lambda i: (i, 0)),
            pl.BlockSpec((1, MLA_HEADS, tm, QK), lambda i: (i // tpb, 0, i % tpb, 0)),
        ],
        out_shape=[
            jax.ShapeDtypeStruct((t, KV_RANK), F32),
            jax.ShapeDtypeStruct((t, ROPE), F32),
            jax.ShapeDtypeStruct((t, QK), BF),
            jax.ShapeDtypeStruct((nb, MLA_HEADS, seq, QK), BF),
        ],
        compiler_params=_cp(("parallel",)),
        name="mla_proj",
    )(x, g, *w, cos8, sin8)


HEAD_QK = NOPE + ROPE
LOG2E = 1.4426950408889634


def _mla_proj_prompt_kernel(x_ref, g_ref, wdq_ref, wdkv_ref, wdk2_ref, qg_ref, kvg_ref, wqn_ref, wqp_ref,
                            wkn_ref, wv_ref, cos_ref, sin_ref, ckv_ref, kpe_ref, q_ref, k_ref, v_ref):
    xn = _rms(x_ref[...], g_ref[...]).astype(BF)
    cqn = _rms(_dot(xn, wdq_ref[...]), qg_ref[...]).astype(BF)
    ckv = _rms(_dot(xn, wdkv_ref[...]), kvg_ref[...])
    kk = _dot(xn, wdk2_ref[...])
    cos = cos_ref[...]
    sin = sin_ref[...]
    kpe = kk[:, :ROPE] * cos[:, :ROPE] + kk[:, ROPE:] * sin[:, :ROPE]
    ckv_ref[...] = ckv
    kpe_ref[...] = kpe
    ckv_b = ckv.astype(BF)
    kpe_b = kpe.astype(BF)
    kn = _dot(ckv_b, wkn_ref[...])
    vt = _dot_nt(wv_ref[...], ckv_b)
    qn = _dot(cqn, wqn_ref[...])
    qp = _dot(cqn, wqp_ref[...])
    hr = MLA_HEADS * ROPE
    qpe = qp[:, :hr] * cos + qp[:, hr:] * sin
    qscale = MLA_SCALE * LOG2E
    for h in range(MLA_HEADS):
        q_ref[0, h, :, :NOPE] = (qn[:, h * NOPE:(h + 1) * NOPE] * qscale).astype(BF)
        q_ref[0, h, :, NOPE:] = (qpe[:, h * ROPE:(h + 1) * ROPE] * qscale).astype(BF)
        k_ref[0, h, :, :NOPE] = kn[:, h * NOPE:(h + 1) * NOPE].astype(BF)
        k_ref[0, h, :, NOPE:] = kpe_b
        v_ref[0, h, 0] = vt[h * MLA_V:(h + 1) * MLA_V].astype(BF)


def _mla_proj_prompt(x, g, prm, cos8, sin8, *, nb):
    t = x.shape[0]
    seq = t // nb
    tm = min(ROW_TILE, seq)
    tpb = seq // tm
    w = [prm["wdq"], prm["wdkv"], prm["wdk2"], prm["q_gain"], prm["kv_gain"], prm["wqn"], prm["wqp"],
         prm["wkn"], prm["wv"]]
    head_spec = lambda width: pl.BlockSpec((1, MLA_HEADS, tm, width), lambda i: (i // tpb, 0, i % tpb, 0))
    head_shape = lambda width: jax.ShapeDtypeStruct((nb, MLA_HEADS, seq, width), BF)
    return pl.pallas_call(
        _mla_proj_prompt_kernel,
        grid=(t // tm,),
        in_specs=[pl.BlockSpec((tm, D), lambda i: (i, 0)), _full((1, D))] + [_full(a.shape) for a in w] + [
            pl.BlockSpec((tm, MLA_HEADS * ROPE), lambda i: (i % tpb, 0)),
            pl.BlockSpec((tm, MLA_HEADS * ROPE), lambda i: (i % tpb, 0)),
        ],
        out_specs=[
            pl.BlockSpec((tm, KV_RANK), lambda i: (i, 0)),
            pl.BlockSpec((tm, ROPE), lambda i: (i, 0)),
            head_spec(HEAD_QK), head_spec(HEAD_QK),
            pl.BlockSpec((1, MLA_HEADS, 1, MLA_V, tm), lambda i: (i // tpb, 0, i % tpb, 0, 0)),
        ],
        out_shape=[
            jax.ShapeDtypeStruct((t, KV_RANK), F32),
            jax.ShapeDtypeStruct((t, ROPE), F32),
            head_shape(HEAD_QK), head_shape(HEAD_QK),
            jax.ShapeDtypeStruct((nb, MLA_HEADS, tpb, MLA_V, tm), BF),
        ],
        compiler_params=_cp(("parallel",)),
        name="mla_proj_prompt",
    )(x, g, *w, cos8, sin8)


FLASH_HEADS = 4
FLASH_KBLOCKS = 4
DENOM_ROWS = 16


def _flash_kernel(q_ref, k_ref, vt_ref, o_ref, m_ref, acc_ref, *, t):
    qi = pl.program_id(2)
    for hh in range(FLASH_HEADS):
        m_ref[hh] = jnp.full((1, t), NEG, F32)
        acc_ref[hh] = jnp.zeros((MLA_V + DENOM_ROWS, t), F32)
    ones = jnp.ones((DENOM_ROWS, t), BF)

    def block(ki, nblk, masked):
        start = pl.multiple_of(ki * t, t)
        ones_n = jnp.concatenate([ones] * nblk, axis=1)
        for hh in range(FLASH_HEADS):
            kb = k_ref[0, hh, pl.ds(start, nblk * t), :]
            vt = jnp.concatenate([vt_ref[0, hh, ki + n] for n in range(nblk)], axis=1)
            v1 = jnp.concatenate([vt, ones_n], axis=0)
            st = _dot_nt(kb, q_ref[0, hh])
            if masked:
                key = lax.broadcasted_iota(jnp.int32, (t, t), 0)
                qry = lax.broadcasted_iota(jnp.int32, (t, t), 1)
                st = jnp.where(key <= qry, st, NEG)
            m_old = m_ref[hh]
            m_new = jnp.maximum(m_old, jnp.max(st, axis=0, keepdims=True))
            alpha = jnp.exp2(m_old - m_new)
            pt = jnp.exp2(st - m_new).astype(BF)
            acc_ref[hh] = alpha * acc_ref[hh] + _dot(v1, pt)
            m_ref[hh] = m_new

    def body(kk, carry):
        block(kk * FLASH_KBLOCKS, FLASH_KBLOCKS, False)
        return carry

    lax.fori_loop(0, qi // FLASH_KBLOCKS, body, 0)
    rem = qi % FLASH_KBLOCKS
    base = qi - rem
    size = FLASH_KBLOCKS // 2
    while size >= 1:
        @pl.when(rem & size != 0)
        def _(size=size):
            block(base + (rem & ~(2 * size - 1)), size, False)
        size //= 2
    block(qi, 1, True)
    for hh in range(FLASH_HEADS):
        acc = acc_ref[hh]
        o_t = acc[:MLA_V] / acc[MLA_V:MLA_V + 1]
        o_ref[0, :, hh * MLA_V:(hh + 1) * MLA_V] = o_t.T.astype(BF)


def _flash(q, k, vt):
    nb, _, seq, _ = q.shape
    t = vt.shape[-1]
    nh = FLASH_HEADS
    return pl.pallas_call(
        functools.partial(_flash_kernel, t=t),
        grid=(nb, MLA_HEADS // nh, seq // t),
        in_specs=[
            pl.BlockSpec((1, nh, t, HEAD_QK), lambda b, h, i: (b, h, i, 0)),
            pl.BlockSpec((1, nh, seq, HEAD_QK), lambda b, h, i: (b, h, 0, 0), pipeline_mode=pl.Buffered(1)),
            pl.BlockSpec((1, nh, seq // t, MLA_V, t), lambda b, h, i: (b, h, 0, 0, 0),
                         pipeline_mode=pl.Buffered(1)),
        ],
        out_specs=pl.BlockSpec((1, t, nh * MLA_V), lambda b, h, i: (b, i, h)),
        out_shape=jax.ShapeDtypeStruct((nb, seq, MLA_HEADS * MLA_V), BF),
        scratch_shapes=[pltpu.VMEM((nh, 1, t), F32), pltpu.VMEM((nh, MLA_V + DENOM_ROWS, t), F32)],
        compiler_params=_cp(("parallel", "parallel", "arbitrary"), vmem_mb=48),
        name="mla_flash",
    )(q, k, vt)


def _page_copies(pt_ref, ckv_hbm, kpt_hbm, ckbuf, kpbuf, sem, sample, slot, *, layer, n_pages):
    copies = []
    for p in range(n_pages):
        pg = pt_ref[sample, p]
        copies.append(pltpu.make_async_copy(
            ckv_hbm.at[layer, pg], ckbuf.at[slot, pl.ds(p * PAGE, PAGE), :], sem.at[0, slot]))
        copies.append(pltpu.make_async_copy(
            kpt_hbm.at[layer, pg], kpbuf.at[slot, :, pl.ds(p * PAGE, PAGE)], sem.at[1, slot]))
    return copies


def _wait_grouped(copies):
    for c in copies[0::2]:
        c.wait()
    for c in copies[1::2]:
        c.wait()


def _decode_kernel(pt_ref, q_ref, nk_ref, ckv_hbm, kpt_hbm, o_ref, ckbuf, kpbuf, sem, *,
                   dec_seq, layer, n_pages):
    b = pl.program_id(0)
    last = pl.num_programs(0) - 1
    slot = b % 2
    copies = functools.partial(_page_copies, pt_ref, ckv_hbm, kpt_hbm, ckbuf, kpbuf, sem,
                               layer=layer, n_pages=n_pages)

    @pl.when(b == 0)
    def _():
        for c in copies(0, 0):
            c.start()

    _wait_grouped(copies(b, slot))
    nxt = jnp.minimum(b + 1, last)
    for c in copies(nxt, 1 - slot):
        c.start()

    rows = MLA_HEADS * dec_seq
    qm = q_ref[0]
    ql = qm[:, :KV_RANK]
    qp = qm[:, KV_RANK:]
    n_slices = DECODE_SLICES if n_pages % DECODE_SLICES == 0 else 1
    width = n_pages * PAGE // n_slices
    cks, scores = [], []
    for c in range(n_slices):
        ck = ckbuf[slot, c * width:(c + 1) * width, :].astype(BF)
        kp = kpbuf[slot, :, c * width:(c + 1) * width].astype(BF)
        cks.append(ck)
        scores.append(_dot_nt(ql, ck) + _dot(qp, kp))
    nk = nk_ref[0]
    kt = lax.broadcasted_iota(jnp.int32, (rows, NEW_KEY_ROWS), 1)
    qt = lax.broadcasted_iota(jnp.int32, (rows, NEW_KEY_ROWS), 0) % dec_seq
    s_new = jnp.where(kt <= qt, _dot_nt(qm, nk), NEG)
    m = jnp.max(s_new, axis=-1, keepdims=True)
    for s in scores:
        m = jnp.maximum(m, jnp.max(s, axis=-1, keepdims=True))
    p_new = jnp.exp(s_new - m)
    denom = jnp.sum(p_new, axis=-1, keepdims=True)
    acc = _dot(p_new.astype(BF), nk[:, :KV_RANK])
    for s, ck in zip(scores, cks):
        p = jnp.exp(s - m)
        denom = denom + jnp.sum(p, axis=-1, keepdims=True)
        acc = acc + _dot(p.astype(BF), ck)
    o_ref[0] = acc / denom

    @pl.when(b == last)
    def _():
        _wait_grouped(copies(nxt, 1 - slot))


def _decode(page_table, q, newk, cache_ckv, cache_kpe_t, layer, dec_seq):
    nb, n_pages = page_table.shape
    rows = MLA_HEADS * dec_seq
    keys = n_pages * PAGE
    grid_spec = pltpu.PrefetchScalarGridSpec(
        num_scalar_prefetch=1,
        grid=(nb,),
        in_specs=[
            pl.BlockSpec((1, rows, QK), lambda b, pt: (b, 0, 0)),
            pl.BlockSpec((1, NEW_KEY_ROWS, QK), lambda b, pt: (b, 0, 0)),
            pl.BlockSpec(memory_space=pl.ANY),
            pl.BlockSpec(memory_space=pl.ANY),
        ],
        out_specs=pl.BlockSpec((1, rows, KV_RANK), lambda b, pt: (b, 0, 0)),
        scratch_shapes=[pltpu.VMEM((2, keys, KV_RANK), F32), pltpu.VMEM((2, ROPE, keys), F32),
                        pltpu.SemaphoreType.DMA((2, 2))],
    )
    return pl.pallas_call(
        functools.partial(_decode_kernel, dec_seq=dec_seq, layer=layer, n_pages=n_pages),
        grid_spec=grid_spec,
        out_shape=jax.ShapeDtypeStruct((nb, rows, KV_RANK), F32),
        compiler_params=_cp(("arbitrary",), vmem_mb=48),
        name="mla_decode",
    )(page_table, q, newk, cache_ckv, cache_kpe_t)


def _mla_out_kernel(o_ref, wuv_ref, wo_ref, r_ref, out_ref):
    parts = [_dot(o_ref[0, h], wuv_ref[h]).astype(BF) for h in range(MLA_HEADS)]
    out_ref[...] = r_ref[...] + _dot(jnp.concatenate(parts, axis=1), wo_ref[...])


def _mla_out(o_lat, wuv, wo, res):
    nb, _, seq, _ = o_lat.shape
    tm = min(ROW_TILE, seq)
    tpb = seq // tm
    t = nb * seq
    return pl.pallas_call(
        _mla_out_kernel,
        grid=(t // tm,),
        in_specs=[
            pl.BlockSpec((1, MLA_HEADS, tm, KV_RANK), lambda i: (i // tpb, 0, i % tpb, 0)),
            _full(wuv.shape), _full(wo.shape),
            pl.BlockSpec((tm, D), lambda i: (i, 0)),
        ],
        out_specs=pl.BlockSpec((tm, D), lambda i: (i, 0)),
        out_shape=jax.ShapeDtypeStruct((t, D), F32),
        compiler_params=_cp(("parallel",)),
        name="mla_out",
    )(o_lat, wuv, wo, res)


def _softmax_rows(s):
    m = jnp.max(s, axis=-1, keepdims=True)
    p = jnp.exp(s - m)
    return p / jnp.sum(p, axis=-1, keepdims=True)


def _mem_prompt_kernel(x_ref, g_ref, wq_ref, k_ref, v_ref, wo_ref, o_ref):
    x = x_ref[...]
    xn = _rms(x, g_ref[...]).astype(BF)
    qm = (_dot(xn, wq_ref[...]) * MEM_HD ** -0.5).astype(BF)
    km = k_ref[0].astype(BF)
    vm = v_ref[0].astype(BF)
    parts = []
    for h in range(MEM_HEADS):
        sl = slice(h * MEM_HD, (h + 1) * MEM_HD)
        p = _softmax_rows(_dot_nt(qm[:, sl], km[:, sl]))
        parts.append(_dot(p.astype(BF), vm[:, sl]).astype(BF))
    o_ref[...] = x + _dot(jnp.concatenate(parts, axis=1), wo_ref[...])


def _mem_prompt(x, g, wq, km, vm, wo):
    t = x.shape[0]
    nb, mt, _ = km.shape
    seq = t // nb
    tm = min(ROW_TILE, seq)
    tpb = seq // tm
    return pl.pallas_call(
        _mem_prompt_kernel,
        grid=(t // tm,),
        in_specs=[
            pl.BlockSpec((tm, D), lambda i: (i, 0)), _full((1, D)), _full(wq.shape),
            pl.BlockSpec((1, mt, MEM_INNER), lambda i: (i // tpb, 0, 0)),
            pl.BlockSpec((1, mt, MEM_INNER), lambda i: (i // tpb, 0, 0)),
            _full(wo.shape),
        ],
        out_specs=pl.BlockSpec((tm, D), lambda i: (i, 0)),
        out_shape=jax.ShapeDtypeStruct((t, D), F32),
        compiler_params=_cp(("parallel",)),
        name="mem_attn_prompt",
    )(x, g, wq, km, vm, wo)


MEM_ROWS = MEM_HEADS * SUBLANES


def _mem_sample_kernel(x_ref, g_ref, wq_ref, k_ref, v_ref, wo_ref, o_ref):
    ns = SAMPLES_PER_STEP
    x = x_ref[...].reshape(ns * SUBLANES, D)
    xn = _rms(x, g_ref[...]).astype(BF)
    qall = _dot(xn, wq_ref[...]) * MEM_HD ** -0.5
    cols = k_ref.shape[2]
    head_of_row = lax.broadcasted_iota(jnp.int32, (MEM_ROWS, cols), 0) // SUBLANES
    head_of_col = lax.broadcasted_iota(jnp.int32, (MEM_ROWS, cols), 1) % MEM_HEADS
    own = head_of_row == head_of_col
    outs = []
    for s in range(ns):
        qs = qall[s * SUBLANES:(s + 1) * SUBLANES]
        qst = jnp.concatenate([qs[:, h * MEM_HD:(h + 1) * MEM_HD] for h in range(MEM_HEADS)], axis=0)
        sc = jnp.where(own, _dot_nt(qst.astype(BF), k_ref[0, s].astype(BF)), NEG)
        o = _dot(_softmax_rows(sc).astype(BF), v_ref[0, s].astype(BF))
        outs.append(jnp.concatenate([o[h * SUBLANES:(h + 1) * SUBLANES] for h in range(MEM_HEADS)], axis=1))
    out = x + _dot(jnp.concatenate(outs, axis=0).astype(BF), wo_ref[...])
    o_ref[...] = out.reshape(ns, SUBLANES, D)


def _mem_sample(x8, g, wq, cache_k, cache_v, wo, layer):
    nb = x8.shape[0]
    rows = cache_k.shape[2]
    ns = SAMPLES_PER_STEP
    return pl.pallas_call(
        _mem_sample_kernel,
        grid=(nb // ns,),
        in_specs=[
            pl.BlockSpec((ns, SUBLANES, D), lambda i: (i, 0, 0)), _full((1, D)), _full(wq.shape),
            pl.BlockSpec((1, ns, rows, MEM_HD), lambda i: (layer, i, 0, 0)),
            pl.BlockSpec((1, ns, rows, MEM_HD), lambda i: (layer, i, 0, 0)),
            _full(wo.shape),
        ],
        out_specs=pl.BlockSpec((ns, SUBLANES, D), lambda i: (i, 0, 0)),
        out_shape=jax.ShapeDtypeStruct((nb, SUBLANES, D), F32),
        compiler_params=_cp(("parallel",), vmem_mb=48),
        name="mem_attn_sample",
    )(x8, g, wq, cache_k, cache_v, wo)


def _router_kernel(x_ref, g_ref, wr_ref, xn_ref, route_ref):
    xn = _rms(x_ref[...], g_ref[...])
    xn_ref[...] = xn
    lane = lax.broadcasted_iota(jnp.int32, (xn.shape[0], LANES), 1).astype(F32)
    lg = jnp.where(lane < N_EXP, _dot_f32(xn, wr_ref[...]), NEG)
    m1 = jnp.max(lg, axis=-1, keepdims=True)
    i1 = jnp.min(jnp.where(lg == m1, lane, float(LANES)), axis=-1, keepdims=True)
    lg2 = jnp.where(lane == i1, NEG, lg)
    m2 = jnp.max(lg2, axis=-1, keepdims=True)
    i2 = jnp.min(jnp.where(lg2 == m2, lane, float(LANES)), axis=-1, keepdims=True)
    e = jnp.exp(m2 - m1)
    g1 = 1.0 / (1.0 + e)
    g2 = e * g1
    route_ref[...] = jnp.where(lane == 0, i1, jnp.where(lane == 1, i2, jnp.where(lane == 2, g1,
                               jnp.where(lane == 3, g2, 0.0))))


def _router(x, g, wr_pad):
    t = x.shape[0]
    tm = min(ROW_TILE, t)
    return pl.pallas_call(
        _router_kernel,
        grid=(t // tm,),
        in_specs=[pl.BlockSpec((tm, D), lambda i: (i, 0)), _full((1, D)), _full(wr_pad.shape)],
        out_specs=[pl.BlockSpec((tm, D), lambda i: (i, 0)), pl.BlockSpec((tm, LANES), lambda i: (i, 0))],
        out_shape=[jax.ShapeDtypeStruct((t, D), F32), jax.ShapeDtypeStruct((t, LANES), F32)],
        compiler_params=_cp(("parallel",)),
        name="router",
    )(x, g, wr_pad)


SC_ROWS = 32


def _sc_mesh():
    return plsc.VectorSubcoreMesh(core_axis_name="c", subcore_axis_name="s")


def _sc_workers():
    info = plsc.get_sparse_core_info()
    return info.num_cores, info.num_cores * info.num_subcores


def _sc_gather_rows(table, idx):
    n = idx.shape[0]
    width = table.shape[1]
    ncores, nw = _sc_workers()
    per_w = n // nw
    n_chunks = per_w // SC_ROWS
    assert per_w * nw == n and n_chunks * SC_ROWS == per_w

    @functools.partial(
        pl.kernel, mesh=_sc_mesh(),
        out_type=jax.ShapeDtypeStruct((n, width), table.dtype),
        scratch_types=[pltpu.VMEM((per_w,), jnp.int32), pltpu.VMEM((SC_ROWS, width), table.dtype),
                       pltpu.SemaphoreType.DMA],
        name="sc_gather_rows",
    )
    def body(table_hbm, idx_hbm, out_hbm, idx_v, rows_v, sem):
        wid = lax.axis_index("s") * ncores + lax.axis_index("c")
        base = wid * per_w
        pltpu.sync_copy(idx_hbm.at[pl.ds(base, per_w)], idx_v)

        @pl.loop(0, n_chunks)
        def _(j):
            off = pl.multiple_of(j * SC_ROWS, SC_ROWS)
            pltpu.async_copy(table_hbm.at[idx_v.at[pl.ds(off, SC_ROWS)]], rows_v, sem).wait()
            pltpu.sync_copy(rows_v, out_hbm.at[pl.ds(base + off, SC_ROWS)])

    return body(table, idx)


def _sc_scatter_rows(src, idx, n_out):
    n = idx.shape[0]
    t, width = src.shape
    ncores, nw = _sc_workers()
    per_w = n // nw
    n_chunks = per_w // SC_ROWS
    assert per_w * nw == n and n_chunks * SC_ROWS == per_w and t % per_w == 0
    idx3 = idx.reshape(nw, n_chunks, SC_ROWS)

    @functools.partial(
        pl.kernel, mesh=_sc_mesh(),
        out_type=jax.ShapeDtypeStruct((n_out, width), src.dtype),
        scratch_types=[pltpu.VMEM((n_chunks, SC_ROWS), jnp.int32), pltpu.VMEM((SC_ROWS, width), src.dtype),
                       pltpu.SemaphoreType.DMA],
        name="sc_scatter_rows",
    )
    def body(src_hbm, idx_hbm, out_hbm, idx_v, rows_v, sem):
        wid = lax.axis_index("s") * ncores + lax.axis_index("c")
        base = lax.rem(wid * per_w, t)
        pltpu.sync_copy(idx_hbm.at[wid], idx_v)

        @pl.loop(0, n_chunks)
        def _(j):
            off = pl.multiple_of(j * SC_ROWS, SC_ROWS)
            pltpu.sync_copy(src_hbm.at[pl.ds(base + off, SC_ROWS)], rows_v)
            pltpu.async_copy(rows_v, out_hbm.at[idx_v.at[j]], sem).wait()

    return body(src, idx3)


MOE_TILE = 1024
MOE_TILE_SMALL = 256
MOE_BLOCK = 512
MOE_CHUNK = 256


def _moe_ffn_kernel(te_ref, nused_ref, x_ref, wg_ref, wu_ref, wd_ref, o_ref, xb_ref):
    del te_ref
    i = pl.program_id(0)
    j = pl.program_id(1)

    @pl.when(i < nused_ref[0])
    def _():
        @pl.when(j == 0)
        def _():
            xb_ref[...] = x_ref[...].astype(BF)
            o_ref[...] = jnp.zeros_like(o_ref)

        xb = xb_ref[...]
        part = None
        block = wd_ref.shape[2]
        for lo in range(0, block, MOE_CHUNK):
            sl = slice(lo, min(lo + MOE_CHUNK, block))
            gate = _dot(xb, wg_ref[0, 0, :, sl].astype(BF))
            up = _dot(xb, wu_ref[0, 0, :, sl].astype(BF))
            contrib = _dot((_silu(gate) * up).astype(BF), wd_ref[0, 0, sl, :].astype(BF))
            part = contrib if part is None else part + contrib
        o_ref[...] += part

    @pl.when(i >= nused_ref[0])
    def _():
        o_ref[...] = jnp.zeros_like(o_ref)


def _moe_ffn(tile_expert, nused, xs, wgu, wd, layer, *, tm):
    npad = xs.shape[0]
    block = MOE_BLOCK
    nblk = D_FFE // block

    def blk(i, j, nu):
        return jnp.where(i < nu[0], j, nblk - 1)

    grid_spec = pltpu.PrefetchScalarGridSpec(
        num_scalar_prefetch=2,
        grid=(npad // tm, nblk),
        in_specs=[
            pl.BlockSpec((tm, D), lambda i, j, te, nu: (jnp.minimum(i, nu[0] - 1), 0)),
            pl.BlockSpec((1, 1, D, block), lambda i, j, te, nu: (layer, te[i], 0, blk(i, j, nu))),
            pl.BlockSpec((1, 1, D, block), lambda i, j, te, nu: (layer, te[i], 0, nblk + blk(i, j, nu))),
            pl.BlockSpec((1, 1, block, D), lambda i, j, te, nu: (layer, te[i], blk(i, j, nu), 0)),
        ],
        out_specs=pl.BlockSpec((tm, D), lambda i, j, te, nu: (i, 0)),
        scratch_shapes=[pltpu.VMEM((tm, D), BF)],
    )
    return pl.pallas_call(
        _moe_ffn_kernel,
        grid_spec=grid_spec,
        out_shape=jax.ShapeDtypeStruct((npad, D), F32),
        compiler_params=_cp(("parallel", "arbitrary"), vmem_mb=56),
        name="moe_ffn",
    )(tile_expert, nused, xs, wgu, wgu, wd)


ROUTE_GATE_LANE = 2


def _combine_kernel(h_ref, y0_ref, y1_ref, route_ref, g_ref, o_ref, *, final):
    gl = ROUTE_GATE_LANE
    route = route_ref[...]
    out = h_ref[...] + route[:, gl:gl + 1] * y0_ref[...] + route[:, gl + 1:gl + 2] * y1_ref[...]
    if final:
        out = _rms(out, g_ref[...])
    o_ref[...] = out


def _combine(h, y2, route, g, *, final):
    t = h.shape[0]
    tm = min(ROW_TILE, t)
    nt = t // tm
    return pl.pallas_call(
        functools.partial(_combine_kernel, final=final),
        grid=(nt,),
        in_specs=[
            pl.BlockSpec((tm, D), lambda i: (i, 0)),
            pl.BlockSpec((tm, D), lambda i: (i, 0)),
            pl.BlockSpec((tm, D), lambda i: (nt + i, 0)),
            pl.BlockSpec((tm, LANES), lambda i: (i, 0)),
            _full((1, D)),
        ],
        out_specs=pl.BlockSpec((tm, D), lambda i: (i, 0)),
        out_shape=jax.ShapeDtypeStruct((t, D), F32),
        compiler_params=_cp(("parallel",)),
        name="moe_combine",
    )(h, y2, y2, route, g)


def _moe(h, g, wr_pad, wgu, wd, layer, final_g, *, final):
    t = h.shape[0]
    tm = MOE_TILE if 2 * t >= 2 * N_EXP * MOE_TILE else MOE_TILE_SMALL
    xn, route = _router(h, g, wr_pad)
    eidx = route[:, :ROUTE_GATE_LANE].astype(jnp.int32)
    e_flat = eidx.T.reshape(-1)
    onehot = (e_flat[:, None] == jnp.arange(N_EXP, dtype=jnp.int32)[None, :]).astype(jnp.int32)
    csum = jnp.cumsum(onehot, axis=0)
    counts = csum[-1]
    rank = jnp.sum(onehot * csum, axis=1) - 1
    padded = ((counts + tm - 1) // tm) * tm
    ends = jnp.cumsum(padded)
    starts = ends - padded
    dest = (jnp.sum(onehot * starts[None, :], axis=1) + rank).astype(jnp.int32)
    n_tiles = -(-2 * t // tm) + N_EXP
    tile_start = jnp.arange(n_tiles, dtype=jnp.int32) * tm
    tile_expert = jnp.minimum(jnp.sum((tile_start[:, None] >= ends[None, :]).astype(jnp.int32), axis=1),
                              N_EXP - 1).astype(jnp.int32)
    nused = (ends[-1] // tm).astype(jnp.int32).reshape(1)
    tile_expert = jnp.where(jnp.arange(n_tiles) < nused[0], tile_expert, tile_expert[nused[0] - 1])
    xs = _sc_scatter_rows(xn, dest, n_tiles * tm)
    ys = _moe_ffn(tile_expert, nused, xs, wgu, wd, layer, tm=tm)
    y2 = _sc_gather_rows(ys, dest)
    return _combine(h, y2, route, final_g, final=final)


def _rope_tables(pos):
    half = ROPE // 2
    inv_freq = ROPE_BASE ** (-jnp.arange(half, dtype=F32) / half)
    ang = pos.astype(F32)[:, None] * inv_freq
    cos = jnp.cos(ang)
    sin = jnp.sin(ang)
    cc = jnp.concatenate([cos, cos], axis=-1)
    ss = jnp.concatenate([-sin, sin], axis=-1)
    return jnp.tile(cc, (1, MLA_HEADS)), jnp.tile(ss, (1, MLA_HEADS))


def _even_layer(h, g, prm, cinit, sinit, layer, *, nb, sample, state_layers=1, state_prev=None):
    t = h.shape[0]
    seq = t // nb
    proj, dtp = _even_proj(h, g, prm["w_zxu"], prm["w_dt"])
    proj = proj.reshape(nb, seq, ZXU)
    dtp = dtp.reshape(nb, seq, LANES)
    if sample:
        pad = ((0, 0), (0, SUBLANES - seq), (0, 0))
        proj = jnp.pad(proj, pad)
        dtp = jnp.pad(dtp, pad)
        q, lb, lc = SAMPLE_Q, SUBLANES, seq
    else:
        q, lb, lc = CHUNK, CHUNK, CHUNK
    cinit8 = jnp.pad(cinit, ((0, 0), (SUBLANES - (SSD_CONV - 1), 0), (0, 0)))
    outs = _even_mixer(proj, dtp, cinit8, sinit.reshape(-1, nb, SSD_INNER, SSD_STATE), prm,
                       q=q, lb=lb, lc=lc, want_v=sample, layer=layer,
                       state_layers=state_layers, state_prev=state_prev)
    ymix, cout, sout = outs[:3]
    v = None
    if sample:
        ymix = ymix[:, :seq]
        v = outs[3][:, :seq]
    h = _matmul_res(ymix.reshape(t, SSD_INNER + GMLP_WIDTH), prm["w_out"], h)
    return h, cout, sout, v


def _prep_even(i, w_in, conv_w, conv_b, dt_bias, a_log, d_skip, ssd_gain, ln_g, ln_b, ws, bs, w_out):
    w = w_in[i]
    o1 = SSD_INNER + CONV_DIM
    w_zxu = jnp.concatenate([w[:, :o1], w[:, o1 + SSD_HEADS:]], axis=1).astype(BF)
    w_dt = jnp.pad(w[:, o1:o1 + SSD_HEADS], ((0, 0), (0, LANES - SSD_HEADS)))
    padl = (0, LANES - SSD_HEADS)
    return dict(
        w_zxu=w_zxu, w_dt=w_dt,
        conv_w=jnp.pad(conv_w[i], ((0, SUBLANES - SSD_CONV), (0, 0))),
        conv_b=conv_b[i][None, :],
        dt_bias=jnp.pad(dt_bias[i], padl)[None, :],
        a_log=jnp.pad(a_log[i], padl)[None, :],
        d_skip=jnp.repeat(d_skip[i], SSD_HEAD_DIM)[None, :],
        ssd_gain=ssd_gain[i][None, :],
        ln_g=ln_g[i][None, :], ln_b=ln_b[i][None, :],
        ws=ws[i], bst=bs[i].T,
        expand=(jnp.arange(LANES)[:, None] == jnp.arange(SSD_INNER)[None, :] // SSD_HEAD_DIM).astype(BF),
        w_out=w_out[i].astype(BF),
    )


def _prep_mla(i, w_down, q_gain, kv_gain, w_uq, w_uk, w_uv, w_o):
    wd = w_down[i]
    wk = wd[:, Q_RANK + KV_RANK:]
    half = ROPE // 2
    rot = lambda a: jnp.concatenate([a[..., half:], a[..., :half]], axis=-1)
    uq = w_uq[i]
    uq_pe = uq[:, :, NOPE:]
    return dict(
        wdq=wd[:, :Q_RANK].astype(BF),
        wdkv=wd[:, Q_RANK:Q_RANK + KV_RANK].astype(BF),
        wdk2=jnp.concatenate([wk, rot(wk)], axis=1).astype(BF),
        q_gain=q_gain[i][None, :], kv_gain=kv_gain[i][None, :],
        wqn=uq[:, :, :NOPE].reshape(Q_RANK, MLA_HEADS * NOPE).astype(BF),
        wqp=jnp.concatenate([uq_pe.reshape(Q_RANK, -1), rot(uq_pe).reshape(Q_RANK, -1)], axis=1).astype(BF),
        wuk=jnp.transpose(w_uk[i], (1, 2, 0)).astype(BF),
        wuv=jnp.transpose(w_uv[i], (1, 0, 2)).astype(BF),
        wkn=w_uk[i].reshape(KV_RANK, MLA_HEADS * NOPE).astype(BF),
        wv=w_uv[i].reshape(KV_RANK, MLA_HEADS * MLA_V).T.astype(BF),
        wo=w_o[i].astype(BF),
    )


def kernel(x_prompt, x_sample, state_ssd, state_conv, cache_mla_ckv, cache_mla_kpe, cache_mem_k, cache_mem_v, page_table, mem_prompt, mix_norm, w_in, conv_w, conv_b, dt_bias, a_log, d_skip, ssd_gain, gmlp_ln_g, gmlp_ln_b, gmlp_ws, gmlp_bs, w_out_even, w_mla_down, mla_q_gain, mla_kv_gain, w_mla_uq, w_mla_uk, w_mla_uv, w_mla_o, xattn_norm, mem_norm, w_mem_q, w_mem_k, w_mem_v, w_mem_o, ffn_norm, w_ffn_gu, w_ffn_down, w_router, w_exp_gu, w_exp_down, final_norm):
    nbp, seq, _ = x_prompt.shape
    nbs, dseq, _ = x_sample.shape
    depth = mix_norm.shape[0]
    past = page_table.shape[1] * PAGE
    mt = mem_prompt.shape[1]
    hp = x_prompt.reshape(nbp * seq, D)
    hs = x_sample.reshape(nbs * dseq, D)
    cos_p, sin_p = _rope_tables(jnp.arange(seq, dtype=jnp.int32))
    cos_s, sin_s = _rope_tables(past + jnp.arange(dseq, dtype=jnp.int32))
    cos_s = jnp.tile(cos_s, (nbs, 1))
    sin_s = jnp.tile(sin_s, (nbs, 1))
    cache_k4 = cache_mem_k.reshape(depth, nbs, mt * MEM_HEADS, MEM_HD)
    cache_v4 = cache_mem_v.reshape(depth, nbs, mt * MEM_HEADS, MEM_HD)
    cache_kpe_t = jnp.swapaxes(cache_mla_kpe, 2, 3)
    final_g = final_norm[None, :]

    p_ssd, p_conv, p_ckv, p_kpe, p_mk, p_mv = [], [], [], [], [], []
    s_conv, s_v, s_ckv, s_kpe = [], [], [], []
    n_even = (depth + 1) // 2
    s_state = None
    for l in range(depth):
        i = l // 2
        g_mix = mix_norm[l][None, :]
        if l % 2 == 0:
            prm = _prep_even(i, w_in, conv_w, conv_b, dt_bias, a_log, d_skip, ssd_gain, gmlp_ln_g,
                             gmlp_ln_b, gmlp_ws, gmlp_bs, w_out_even)
            buf0 = jnp.zeros((nbp, SSD_CONV - 1, CONV_DIM), F32)
            h00 = jnp.zeros((1, nbp, SSD_HEADS, SSD_HEAD_DIM, SSD_STATE), F32)
            hp, buf_p, ssd_p, _ = _even_layer(hp, g_mix, prm, buf0, h00, 0, nb=nbp, sample=False)
            hs, buf_s, s_state, v_s = _even_layer(hs, g_mix, prm, state_conv[i], state_ssd, i, nb=nbs,
                                                  sample=True, state_layers=n_even, state_prev=s_state)
            p_ssd.append(ssd_p.reshape(nbp, SSD_HEADS, SSD_HEAD_DIM, SSD_STATE))
            p_conv.append(buf_p)
            s_conv.append(buf_s)
            s_v.append(v_s)
        else:
            prm = _prep_mla(i, w_mla_down, mla_q_gain, mla_kv_gain, w_mla_uq, w_mla_uk, w_mla_uv, w_mla_o)
            ckv, kpe, qh, kh, vh = _mla_proj_prompt(hp, g_mix, prm, cos_p, sin_p, nb=nbp)
            o_p = _flash(qh, kh, vh)
            hp = _matmul_res(o_p.reshape(nbp * seq, MLA_HEADS * MLA_V), prm["wo"], hp)
            p_ckv.append(ckv.reshape(nbp, seq, KV_RANK))
            p_kpe.append(kpe.reshape(nbp, seq, ROPE))

            ckv_s, kpe_s, kcat_s, q_s = _mla_proj(hs, g_mix, prm, cos_s, sin_s, nb=1)
            q_s = q_s[0].reshape(MLA_HEADS, nbs, dseq, QK).transpose(1, 0, 2, 3).reshape(nbs, MLA_HEADS * dseq, QK)
            newk = jnp.pad(kcat_s.reshape(nbs, dseq, QK), ((0, 0), (0, NEW_KEY_ROWS - dseq), (0, 0)))
            o_s = _decode(page_table, q_s, newk, cache_mla_ckv, cache_kpe_t, i, dseq)
            o_s = o_s.reshape(nbs, MLA_HEADS, dseq, KV_RANK).transpose(1, 0, 2, 3)
            o_s = o_s.reshape(1, MLA_HEADS, nbs * dseq, KV_RANK).astype(BF)
            hs = _mla_out(o_s, prm["wuv"], prm["wo"], hs)
            s_ckv.append(ckv_s.reshape(nbs, dseq, KV_RANK))
            s_kpe.append(kpe_s.reshape(nbs, dseq, ROPE))

        wkv = jnp.concatenate([w_mem_k[l], w_mem_v[l]], axis=1).astype(BF)
        kv = _rms_matmul(mem_prompt.reshape(nbp * mt, D), mem_norm[l][None, :], wkv, tn=MEM_INNER)
        mk_p = kv[:, :MEM_INNER].reshape(nbp, mt, MEM_INNER)
        mv_p = kv[:, MEM_INNER:].reshape(nbp, mt, MEM_INNER)
        g_x = xattn_norm[l][None, :]
        wq = w_mem_q[l].astype(BF)
        wo = w_mem_o[l].astype(BF)
        hp = _mem_prompt(hp, g_x, wq, mk_p, mv_p, wo)
        x8 = jnp.pad(hs.reshape(nbs, dseq, D), ((0, 0), (0, SUBLANES - dseq), (0, 0)))
        hs = _mem_sample(x8, g_x, wq, cache_k4, cache_v4, wo, l)[:, :dseq].reshape(nbs * dseq, D)
        p_mk.append(mk_p.reshape(nbp, mt, MEM_HEADS, MEM_HD))
        p_mv.append(mv_p.reshape(nbp, mt, MEM_HEADS, MEM_HD))

        g_f = ffn_norm[l][None, :]
        if l % 2 == 0:
            wgu = w_ffn_gu[i].astype(BF)
            wd = w_ffn_down[i].astype(BF)
            hp = _ffn(hp, g_f, wgu, wd)
            hs = _ffn(hs, g_f, wgu, wd)
        else:
            wr = jnp.pad(w_router[i], ((0, 0), (0, LANES - N_EXP)))
            final = l == depth - 1
            hp = _moe(hp, g_f, wr, w_exp_gu, w_exp_down, i, final_g, final=final)
            hs = _moe(hs, g_f, wr, w_exp_gu, w_exp_down, i, final_g, final=final)
    if depth % 2 == 1:
        raise NotImplementedError("the final norm is fused into the last routed-expert layer")
    y_prompt = hp.reshape(nbp, seq, D)
    y_sample = hs.reshape(nbs, dseq, D)
    return (y_prompt, y_sample,
            jnp.stack(p_ssd), jnp.stack(p_conv), jnp.stack(p_ckv), jnp.stack(p_kpe),
            jnp.stack(p_mk), jnp.stack(p_mv),
            s_state.reshape(n_even, nbs, SSD_HEADS, SSD_HEAD_DIM, SSD_STATE), jnp.stack(s_conv), jnp.stack(s_v), jnp.stack(s_ckv), jnp.stack(s_kpe))
```

```python
import functools

import jax
import jax.numpy as jnp
from jax import lax
from jax.experimental import pallas as pl
from jax.experimental.pallas import tpu as pltpu
from jax.experimental.pallas import tpu_sc as plsc

F32 = jnp.float32
BF = jnp.bfloat16
EPS = 1e-6
NEG = -1e30

D = 1024
SSD_HEADS = 16
SSD_HEAD_DIM = 64
SSD_INNER = SSD_HEADS * SSD_HEAD_DIM
SSD_GROUPS = 2
SSD_STATE = 128
SSD_CONV = 4
CONV_DIM = SSD_INNER + 2 * SSD_GROUPS * SSD_STATE
GMLP_GROUPS = 8
GMLP_WIDTH = 1024
CHUNK = 128
EVEN_SEQS_PER_STEP = 2
SAMPLE_Q = 16
ZXU = SSD_INNER + CONV_DIM + 2 * GMLP_WIDTH
MLA_HEADS = 8
NOPE = 128
ROPE = 64
MLA_V = 128
Q_RANK = 256
KV_RANK = 256
MLA_SCALE = (NOPE + ROPE) ** -0.5
QK = KV_RANK + ROPE
ROPE_BASE = 10000.0
MEM_HEADS = 4
MEM_HD = 128
MEM_INNER = MEM_HEADS * MEM_HD
N_EXP = 8
D_FFE = 3584
PAGE = 128
LANES = 128
SUBLANES = 8
ROW_TILE = 512
DECODE_SLICES = 4
NEW_KEY_ROWS = 16
SAMPLES_PER_STEP = 8


def _cp(sem, vmem_mb=None):
    kw = dict(dimension_semantics=sem)
    if vmem_mb is not None:
        kw["vmem_limit_bytes"] = vmem_mb * 1024 * 1024
    return pltpu.CompilerParams(**kw)


def _rms(x, g):
    return x * lax.rsqrt(jnp.mean(x * x, axis=-1, keepdims=True) + EPS) * g


def _dot(a, b):
    return jnp.dot(a, b, preferred_element_type=F32)


def _dot_nt(a, b):
    return lax.dot_general(a, b, (((1,), (1,)), ((), ())), preferred_element_type=F32)


def _split_bf16(v):
    hi = v.astype(BF)
    return hi, (v - hi.astype(F32)).astype(BF)


def _dot_split(a, b):
    a_hi, a_lo = _split_bf16(a)
    b_hi, b_lo = _split_bf16(b)
    return _dot(a_hi, b_hi) + _dot(a_hi, b_lo) + _dot(a_lo, b_hi)


def _dot_f32(a, b):
    return jnp.dot(a, b, preferred_element_type=F32, precision=lax.Precision.HIGHEST)


def _silu(x):
    return x * jax.nn.sigmoid(x)


def _full(shape):
    n = len(shape)
    return pl.BlockSpec(shape, lambda *_: (0,) * n)


def _largest_tile(n, cap):
    best = LANES
    for t in range(LANES, cap + 1, LANES):
        if n % t == 0:
            best = t
    return best


def _rms_matmul_kernel(x_ref, g_ref, w_ref, o_ref, xn_ref):
    @pl.when(pl.program_id(1) == 0)
    def _():
        xn_ref[...] = _rms(x_ref[...], g_ref[...]).astype(BF)

    o_ref[...] = _dot(xn_ref[...], w_ref[...])


def _rms_matmul(x, g, w, *, tn):
    t, k = x.shape
    n = w.shape[1]
    tm = min(ROW_TILE, t)
    return pl.pallas_call(
        _rms_matmul_kernel,
        grid=(t // tm, n // tn),
        in_specs=[
            pl.BlockSpec((tm, k), lambda i, j: (i, 0)),
            pl.BlockSpec((1, k), lambda i, j: (0, 0)),
            pl.BlockSpec((k, tn), lambda i, j: (0, j)),
        ],
        out_specs=pl.BlockSpec((tm, tn), lambda i, j: (i, j)),
        out_shape=jax.ShapeDtypeStruct((t, n), F32),
        scratch_shapes=[pltpu.VMEM((tm, k), BF)],
        compiler_params=_cp(("parallel", "arbitrary")),
        name="rms_matmul",
    )(x, g, w)


EVEN_PROJ_CHUNK = 512


def _even_proj_kernel(x_ref, g_ref, w_ref, wdt_ref, o_ref, dt_ref):
    xn = _rms(x_ref[...], g_ref[...])
    dt_ref[...] = _dot_split(xn, wdt_ref[...])
    xb = xn.astype(BF)
    for lo in range(0, o_ref.shape[1], EVEN_PROJ_CHUNK):
        o_ref[:, lo:lo + EVEN_PROJ_CHUNK] = _dot(xb, w_ref[:, lo:lo + EVEN_PROJ_CHUNK])


def _even_proj(x, g, w, wdt):
    t, k = x.shape
    n = w.shape[1]
    tm = min(ROW_TILE, t)
    return pl.pallas_call(
        _even_proj_kernel,
        grid=(t // tm,),
        in_specs=[
            pl.BlockSpec((tm, k), lambda i: (i, 0)),
            _full((1, k)),
            pl.BlockSpec((k, n), lambda i: (0, 0), pipeline_mode=pl.Buffered(1)),
            pl.BlockSpec((k, LANES), lambda i: (0, 0), pipeline_mode=pl.Buffered(1)),
        ],
        out_specs=[pl.BlockSpec((tm, n), lambda i: (i, 0)), pl.BlockSpec((tm, LANES), lambda i: (i, 0))],
        out_shape=[jax.ShapeDtypeStruct((t, n), F32), jax.ShapeDtypeStruct((t, LANES), F32)],
        compiler_params=_cp(("parallel",), vmem_mb=48),
        name="even_proj",
    )(x, g, w, wdt)


def _matmul_res_kernel(a_ref, w_ref, r_ref, o_ref):
    o_ref[...] = r_ref[...] + _dot(a_ref[...].astype(BF), w_ref[...])


def _matmul_res(a, w, res):
    t, k = a.shape
    n = w.shape[1]
    tn = _largest_tile(n, 1024)
    tm = min(ROW_TILE, t)
    return pl.pallas_call(
        _matmul_res_kernel,
        grid=(t // tm, n // tn),
        in_specs=[
            pl.BlockSpec((tm, k), lambda i, j: (i, 0)),
            pl.BlockSpec((k, tn), lambda i, j: (0, j)),
            pl.BlockSpec((tm, tn), lambda i, j: (i, j)),
        ],
        out_specs=pl.BlockSpec((tm, tn), lambda i, j: (i, j)),
        out_shape=jax.ShapeDtypeStruct((t, n), F32),
        compiler_params=_cp(("parallel", "arbitrary")),
        name="matmul_res",
    )(a, w, res)


FF_CHUNK = 256


def _ffn_kernel(x_ref, g_ref, wgu_ref, wd_ref, o_ref, hid_ref, *, ff):
    x = x_ref[...]
    xn = _rms(x, g_ref[...]).astype(BF)
    for c in range(ff // FF_CHUNK):
        lo = c * FF_CHUNK
        gate = _dot(xn, wgu_ref[:, lo:lo + FF_CHUNK])
        up = _dot(xn, wgu_ref[:, ff + lo:ff + lo + FF_CHUNK])
        hid_ref[:, lo:lo + FF_CHUNK] = (_silu(gate) * up).astype(BF)
    o_ref[...] = x + _dot(hid_ref[...], wd_ref[...])


def _ffn(x, g, wgu, wd):
    t = x.shape[0]
    ff = wd.shape[0]
    tm = min(ROW_TILE, t)
    return pl.pallas_call(
        functools.partial(_ffn_kernel, ff=ff),
        grid=(t // tm,),
        in_specs=[
            pl.BlockSpec((tm, D), lambda i: (i, 0)),
            _full((1, D)),
            pl.BlockSpec((D, 2 * ff), lambda i: (0, 0), pipeline_mode=pl.Buffered(1)),
            pl.BlockSpec((ff, D), lambda i: (0, 0), pipeline_mode=pl.Buffered(1)),
        ],
        out_specs=pl.BlockSpec((tm, D), lambda i: (i, 0)),
        out_shape=jax.ShapeDtypeStruct((t, D), F32),
        scratch_shapes=[pltpu.VMEM((tm, ff), BF)],
        compiler_params=_cp(("parallel",), vmem_mb=48),
        name="ffn",
    )(x, g, wgu, wd)


def _softplus(x):
    return jnp.maximum(x, 0.0) + jnp.log1p(jnp.exp(-jnp.abs(x)))


def _gelu_tanh(x):
    return 0.5 * x * (1.0 + jnp.tanh(0.7978845608028654 * (x + 0.044715 * (x * x * x))))


def _even_kernel(proj_ref, dt_ref, cinit_ref, sinit_ref, cw_ref, cb_ref, dtb_ref, alog_ref,
                 dsk_ref, sg_ref, lng_ref, lnb_ref, ws_ref, bst_ref, e_ref,
                 ymix_ref, cout_ref, sout_ref, v_ref, ext_ref, ht_ref, *, q, lb, lc, nbb, out_slot):
    c = pl.program_id(1)

    @pl.when(c == 0)
    def _():
        for bb in range(nbb):
            ext_ref[bb, 0:SUBLANES, :] = cinit_ref[bb]
            ht_ref[bb] = sinit_ref[0, bb].T

    for bb in range(nbb):
        _even_block(proj_ref, dt_ref, cw_ref, cb_ref, dtb_ref, alog_ref,
                    dsk_ref, sg_ref, lng_ref, lnb_ref, ws_ref, bst_ref, e_ref,
                    ymix_ref, v_ref, ext_ref, ht_ref, bb, q=q, lb=lb, lc=lc)

    @pl.when(c == pl.num_programs(1) - 1)
    def _():
        for bb in range(nbb):
            cout_ref[bb] = ext_ref[bb, SUBLANES + lc - 3:SUBLANES + lc, :]
            for slot in range(sout_ref.shape[0]):
                if slot == out_slot:
                    sout_ref[slot, bb] = ht_ref[bb].T
                else:
                    sout_ref[slot, bb] = jnp.zeros((SSD_INNER, SSD_STATE), F32)


def _even_block(proj_ref, dt_ref, cw_ref, cb_ref, dtb_ref, alog_ref,
                dsk_ref, sg_ref, lng_ref, lnb_ref, ws_ref, bst_ref, e_ref,
                ymix_ref, v_ref, ext_ref, ht_ref, bb, *, q, lb, lc):

    if lb == q:
        p = proj_ref[bb]
        dtr = dt_ref[bb]
    else:
        p = jnp.concatenate([proj_ref[bb], jnp.zeros((q - lb, ZXU), F32)], axis=0)
        dtr = jnp.concatenate([dt_ref[bb], jnp.zeros((q - lb, LANES), F32)], axis=0)
    z = p[:, :SSD_INNER]
    xbc_raw = p[:, SSD_INNER:SSD_INNER + CONV_DIM]
    uv = p[:, SSD_INNER + CONV_DIM:]

    ext_ref[bb, SUBLANES:SUBLANES + q, :] = xbc_raw
    conv = (cb_ref[...] + cw_ref[0:1, :] * ext_ref[bb, 5:5 + q, :] + cw_ref[1:2, :] * ext_ref[bb, 6:6 + q, :]
            + cw_ref[2:3, :] * ext_ref[bb, 7:7 + q, :] + cw_ref[3:4, :] * xbc_raw)
    ext_ref[bb, 0:SUBLANES, :] = ext_ref[bb, q:q + SUBLANES, :]

    xbc = _silu(conv)
    xs = xbc[:, :SSD_INNER]
    gw = SSD_STATE
    bm = [xbc[:, SSD_INNER + g * gw:SSD_INNER + (g + 1) * gw] for g in range(SSD_GROUPS)]
    cm = [xbc[:, SSD_INNER + (SSD_GROUPS + g) * gw:SSD_INNER + (SSD_GROUPS + g + 1) * gw]
          for g in range(SSD_GROUPS)]

    row = lax.broadcasted_iota(jnp.int32, (q, q), 0)
    col = lax.broadcasted_iota(jnp.int32, (q, q), 1)
    causal = row >= col

    dt = _softplus(dtr + dtb_ref[...])
    if lc < q:
        dt = jnp.where(lax.broadcasted_iota(jnp.int32, (q, LANES), 0) < lc, dt, 0.0)
    a = dt * (-jnp.exp(alog_ref[...]))
    a_cum = _dot_f32(causal.astype(F32), a)
    a_cum_t = a_cum.T
    a_last = a_cum[q - 1:q, :]
    decay_end = jnp.exp(a_last - a_cum)
    ea = jnp.exp(a_cum)
    chunk_decay = jnp.exp(a_last)

    cmb = [m.astype(BF) for m in cm]
    cb = [_dot_nt(cmb[g], bm[g].astype(BF)) for g in range(SSD_GROUPS)]
    bt = [bm[g].T.astype(BF) for g in range(SSD_GROUPS)]
    heads_per_group = SSD_HEADS // SSD_GROUPS
    gi = SSD_INNER // SSD_GROUPS

    def per_head_lanes(v):
        hi = v.astype(BF)
        lo = (v - hi.astype(F32)).astype(BF)
        return _dot(hi, e_ref[...]) + _dot(lo, e_ref[...])

    dt_x = per_head_lanes(dt)
    ea_x = per_head_lanes(ea)
    de_x = per_head_lanes(decay_end)
    cd_x = per_head_lanes(jnp.broadcast_to(chunk_decay, (SUBLANES, LANES)))[0:1]
    xdt = xs * dt_x
    xdt_b = xdt.astype(BF)
    xd_b = (xdt * de_x).astype(BF)
    y_off = []
    for g in range(SSD_GROUPS):
        h_old = ht_ref[bb, :, g * gi:(g + 1) * gi]
        y_off.append(_dot(cmb[g], h_old.astype(BF)))
        ht_ref[bb, :, g * gi:(g + 1) * gi] = (h_old * cd_x[:, g * gi:(g + 1) * gi]
                                              + _dot(bt[g], xd_b[:, g * gi:(g + 1) * gi]))

    def decay_weights(r):
        seg = a_cum[:, r:r + 1] - a_cum_t[r:r + 1, :]
        lmat = jnp.where(causal, jnp.exp(jnp.minimum(seg, 0.0)), 0.0)
        return (cb[r // heads_per_group] * lmat).astype(BF)

    first_half = lax.broadcasted_iota(jnp.int32, (q, LANES), 1) < SSD_HEAD_DIM
    y_diag = []
    for k in range(SSD_HEADS // 2):
        xp = xdt_b[:, k * LANES:(k + 1) * LANES]
        y_diag.append(jnp.where(first_half, _dot(decay_weights(2 * k), xp), _dot(decay_weights(2 * k + 1), xp)))
    y = jnp.concatenate(y_diag, axis=1) + jnp.concatenate(y_off, axis=1) * ea_x + dsk_ref[...] * xs
    y = y * _silu(z)
    gi = SSD_INNER // SSD_GROUPS
    yn = [_rms(y[:, g * gi:(g + 1) * gi], sg_ref[:, g * gi:(g + 1) * gi]) for g in range(SSD_GROUPS)]
    ymix_ref[bb, :, 0:SSD_INNER] = jnp.concatenate(yn, axis=1)[:lb].astype(BF)

    uvg = _gelu_tanh(uv)
    u = uvg[:, :GMLP_WIDTH]
    v = uvg[:, GMLP_WIDTH:]
    mu = jnp.mean(v, axis=-1, keepdims=True)
    vc = v - mu
    vn = vc * lax.rsqrt(jnp.mean(vc * vc, axis=-1, keepdims=True) + EPS) * lng_ref[...] + lnb_ref[...]
    if v_ref is not None:
        v_ref[bb] = vn[:lb]
    gd = GMLP_WIDTH // GMLP_GROUPS
    yb = []
    for g in range(GMLP_GROUPS):
        wt = jnp.where(causal, ws_ref[g, :q, :q], 0.0).astype(BF)
        sp = _dot(wt, vn[:, g * gd:(g + 1) * gd].astype(BF)) + bst_ref[:q, g:g + 1]
        yb.append(u[:, g * gd:(g + 1) * gd] * sp)
    ymix_ref[bb, :, SSD_INNER:SSD_INNER + GMLP_WIDTH] = jnp.concatenate(yb, axis=1)[:lb].astype(BF)


EVEN_INPUTS = 15


def _even_entry(*refs, has_v, has_prev, **kw):
    ins = refs[:EVEN_INPUTS]
    k = EVEN_INPUTS + (1 if has_prev else 0)
    outs = refs[k:k + 3]
    k += 3
    v_ref = refs[k] if has_v else None
    k += 1 if has_v else 0
    _even_kernel(*ins, *outs, v_ref, *refs[k:], **kw)


def _even_mixer(proj, dtp, cinit8, sinit, prm, *, q, lb, lc, want_v, layer, state_layers=1, state_prev=None):
    b, lp, _ = proj.shape
    nchunks = lp // lb
    nbb = EVEN_SEQS_PER_STEP if b % EVEN_SEQS_PER_STEP == 0 else 1
    par = [prm["conv_w"], prm["conv_b"], prm["dt_bias"], prm["a_log"], prm["d_skip"], prm["ssd_gain"],
           prm["ln_g"], prm["ln_b"], prm["ws"], prm["bst"], prm["expand"]]
    in_specs = [
        pl.BlockSpec((nbb, lb, ZXU), lambda i, c: (i, c, 0)),
        pl.BlockSpec((nbb, lb, LANES), lambda i, c: (i, c, 0)),
        pl.BlockSpec((nbb, SUBLANES, CONV_DIM), lambda i, c: (i, 0, 0)),
        pl.BlockSpec((1, nbb, SSD_INNER, SSD_STATE), lambda i, c: (layer, i, 0, 0)),
    ] + [_full(w.shape) for w in par]
    out_specs = [
        pl.BlockSpec((nbb, lb, SSD_INNER + GMLP_WIDTH), lambda i, c: (i, c, 0)),
        pl.BlockSpec((nbb, SSD_CONV - 1, CONV_DIM), lambda i, c: (i, 0, 0)),
        (pl.BlockSpec((1, nbb, SSD_INNER, SSD_STATE), lambda i, c: (layer, i, 0, 0))
         if state_prev is not None else
         pl.BlockSpec((state_layers, nbb, SSD_INNER, SSD_STATE), lambda i, c: (0, i, 0, 0))),
    ]
    out_slot = 0 if state_prev is not None else (layer if state_layers > 1 else 0)
    out_shape = [
        jax.ShapeDtypeStruct((b, lp, SSD_INNER + GMLP_WIDTH), BF),
        jax.ShapeDtypeStruct((b, SSD_CONV - 1, CONV_DIM), F32),
        jax.ShapeDtypeStruct((state_layers, b, SSD_INNER, SSD_STATE), F32),
    ]
    if want_v:
        out_specs.append(pl.BlockSpec((nbb, lb, GMLP_WIDTH), lambda i, c: (i, c, 0)))
        out_shape.append(jax.ShapeDtypeStruct((b, lp, GMLP_WIDTH), F32))
    args = [proj, dtp, cinit8, sinit, *par]
    aliases = {}
    if state_prev is not None:
        in_specs.append(pl.BlockSpec(memory_space=pl.ANY))
        aliases = {len(args): 2}
        args.append(state_prev)
    return pl.pallas_call(
        functools.partial(_even_entry, has_v=want_v, has_prev=state_prev is not None,
                          q=q, lb=lb, lc=lc, nbb=nbb, out_slot=out_slot),
        grid=(b // nbb, nchunks),
        in_specs=in_specs,
        out_specs=out_specs,
        out_shape=out_shape,
        input_output_aliases=aliases,
        scratch_shapes=[pltpu.VMEM((nbb, q + 2 * SUBLANES, CONV_DIM), F32),
                        pltpu.VMEM((nbb, SSD_STATE, SSD_INNER), F32)],
        compiler_params=_cp(("parallel", "arbitrary"), vmem_mb=48),
        name="even_mixer",
    )(*args)


def _mla_proj_kernel(x_ref, g_ref, wdq_ref, wdkv_ref, wdk2_ref, qg_ref, kvg_ref, wqn_ref, wqp_ref,
                     wuk_ref, cos_ref, sin_ref, ckv_ref, kpe_ref, kcat_ref, q_ref):
    xn = _rms(x_ref[...], g_ref[...]).astype(BF)
    cqn = _rms(_dot(xn, wdq_ref[...]), qg_ref[...]).astype(BF)
    ckv = _rms(_dot(xn, wdkv_ref[...]), kvg_ref[...])
    kk = _dot(xn, wdk2_ref[...])
    cos = cos_ref[...]
    sin = sin_ref[...]
    kpe = kk[:, :ROPE] * cos[:, :ROPE] + kk[:, ROPE:] * sin[:, :ROPE]
    ckv_ref[...] = ckv
    kpe_ref[...] = kpe
    kcat_ref[:, :KV_RANK] = ckv.astype(BF)
    kcat_ref[:, KV_RANK:] = kpe.astype(BF)
    qn = _dot(cqn, wqn_ref[...])
    qp = _dot(cqn, wqp_ref[...])
    hr = MLA_HEADS * ROPE
    qpe = qp[:, :hr] * cos + qp[:, hr:] * sin
    for h in range(MLA_HEADS):
        ql = _dot(qn[:, h * NOPE:(h + 1) * NOPE].astype(BF), wuk_ref[h])
        q_ref[0, h, :, :KV_RANK] = (ql * MLA_SCALE).astype(BF)
        q_ref[0, h, :, KV_RANK:] = (qpe[:, h * ROPE:(h + 1) * ROPE] * MLA_SCALE).astype(BF)


def _mla_proj(x, g, prm, cos8, sin8, *, nb):
    t = x.shape[0]
    seq = t // nb
    tm = min(ROW_TILE, seq)
    tpb = seq // tm
    w = [prm["wdq"], prm["wdkv"], prm["wdk2"], prm["q_gain"], prm["kv_gain"], prm["wqn"], prm["wqp"],
         prm["wuk"]]
    return pl.pallas_call(
        _mla_proj_kernel,
        grid=(t // tm,),
        in_specs=[pl.BlockSpec((tm, D), lambda i: (i, 0)), _full((1, D))] + [_full(a.shape) for a in w] + [
            pl.BlockSpec((tm, MLA_HEADS * ROPE), lambda i: (i % tpb, 0)),
            pl.BlockSpec((tm, MLA_HEADS * ROPE), lambda i: (i % tpb, 0)),
        ],
        out_specs=[
            pl.BlockSpec((tm, KV_RANK), lambda i: (i, 0)),
            pl.BlockSpec((tm, ROPE), lambda i: (i, 0)),
            pl.BlockSpec((tm, QK), lambda i: (i, 0)),
            pl.BlockSpec((1, MLA_HEADS, tm, QK), lambda i: (i // tpb, 0, i % tpb, 0)),
        ],
        out_shape=[
            jax.ShapeDtypeStruct((t, KV_RANK), F32),
            jax.ShapeDtypeStruct((t, ROPE), F32),
            jax.ShapeDtypeStruct((t, QK), BF),
            jax.ShapeDtypeStruct((nb, MLA_HEADS, seq, QK), BF),
        ],
        compiler_params=_cp(("parallel",)),
        name="mla_proj",
    )(x, g, *w, cos8, sin8)


HEAD_QK = NOPE + ROPE
LOG2E = 1.4426950408889634


def _mla_proj_prompt_kernel(x_ref, g_ref, wdq_ref, wdkv_ref, wdk2_ref, qg_ref, kvg_ref, wqn_ref, wqp_ref,
                            wkn_ref, wv_ref, cos_ref, sin_ref, ckv_ref, kpe_ref, q_ref, k_ref, v_ref):
    xn = _rms(x_ref[...], g_ref[...]).astype(BF)
    cqn = _rms(_dot(xn, wdq_ref[...]), qg_ref[...]).astype(BF)
    ckv = _rms(_dot(xn, wdkv_ref[...]), kvg_ref[...])
    kk = _dot(xn, wdk2_ref[...])
    cos = cos_ref[...]
    sin = sin_ref[...]
    kpe = kk[:, :ROPE] * cos[:, :ROPE] + kk[:, ROPE:] * sin[:, :ROPE]
    ckv_ref[...] = ckv
    kpe_ref[...] = kpe
    ckv_b = ckv.astype(BF)
    kpe_b = kpe.astype(BF)
    kn = _dot(ckv_b, wkn_ref[...])
    vt = _dot_nt(wv_ref[...], ckv_b)
    qn = _dot(cqn, wqn_ref[...])
    qp = _dot(cqn, wqp_ref[...])
    hr = MLA_HEADS * ROPE
    qpe = qp[:, :hr] * cos + qp[:, hr:] * sin
    qscale = MLA_SCALE * LOG2E
    for h in range(MLA_HEADS):
        q_ref[0, h, :, :NOPE] = (qn[:, h * NOPE:(h + 1) * NOPE] * qscale).astype(BF)
        q_ref[0, h, :, NOPE:] = (qpe[:, h * ROPE:(h + 1) * ROPE] * qscale).astype(BF)
        k_ref[0, h, :, :NOPE] = kn[:, h * NOPE:(h + 1) * NOPE].astype(BF)
        k_ref[0, h, :, NOPE:] = kpe_b
        v_ref[0, h, 0] = vt[h * MLA_V:(h + 1) * MLA_V].astype(BF)


def _mla_proj_prompt(x, g, prm, cos8, sin8, *, nb):
    t = x.shape[0]
    seq = t // nb
    tm = min(ROW_TILE, seq)
    tpb = seq // tm
    w = [prm["wdq"], prm["wdkv"], prm["wdk2"], prm["q_gain"], prm["kv_gain"], prm["wqn"], prm["wqp"],
         prm["wkn"], prm["wv"]]
    head_spec = lambda width: pl.BlockSpec((1, MLA_HEADS, tm, width), lambda i: (i // tpb, 0, i % tpb, 0))
    head_shape = lambda width: jax.ShapeDtypeStruct((nb, MLA_HEADS, seq, width), BF)
    return pl.pallas_call(
        _mla_proj_prompt_kernel,
        grid=(t // tm,),
        in_specs=[pl.BlockSpec((tm, D), lambda i: (i, 0)), _full((1, D))] + [_full(a.shape) for a in w] + [
            pl.BlockSpec((tm, MLA_HEADS * ROPE), lambda i: (i % tpb, 0)),
            pl.BlockSpec((tm, MLA_HEADS * ROPE), lambda i: (i % tpb, 0)),
        ],
        out_specs=[
            pl.BlockSpec((tm, KV_RANK), lambda i: (i, 0)),
            pl.BlockSpec((tm, ROPE), lambda i: (i, 0)),
            head_spec(HEAD_QK), head_spec(HEAD_QK),
            pl.BlockSpec((1, MLA_HEADS, 1, MLA_V, tm), lambda i: (i // tpb, 0, i % tpb, 0, 0)),
        ],
        out_shape=[
            jax.ShapeDtypeStruct((t, KV_RANK), F32),
            jax.ShapeDtypeStruct((t, ROPE), F32),
            head_shape(HEAD_QK), head_shape(HEAD_QK),
            jax.ShapeDtypeStruct((nb, MLA_HEADS, tpb, MLA_V, tm), BF),
        ],
        compiler_params=_cp(("parallel",)),
        name="mla_proj_prompt",
    )(x, g, *w, cos8, sin8)


FLASH_HEADS = 4
FLASH_KBLOCKS = 4
DENOM_ROWS = 16


def _flash_kernel(q_ref, k_ref, vt_ref, o_ref, m_ref, acc_ref, *, t):
    qi = pl.program_id(2)
    for hh in range(FLASH_HEADS):
        m_ref[hh] = jnp.full((1, t), NEG, F32)
        acc_ref[hh] = jnp.zeros((MLA_V + DENOM_ROWS, t), F32)
    ones = jnp.ones((DENOM_ROWS, t), BF)

    def block(ki, nblk, masked):
        start = pl.multiple_of(ki * t, t)
        ones_n = jnp.concatenate([ones] * nblk, axis=1)
        for hh in range(FLASH_HEADS):
            kb = k_ref[0, hh, pl.ds(start, nblk * t), :]
            vt = jnp.concatenate([vt_ref[0, hh, ki + n] for n in range(nblk)], axis=1)
            v1 = jnp.concatenate([vt, ones_n], axis=0)
            st = _dot_nt(kb, q_ref[0, hh])
            if masked:
                key = lax.broadcasted_iota(jnp.int32, (t, t), 0)
                qry = lax.broadcasted_iota(jnp.int32, (t, t), 1)
                st = jnp.where(key <= qry, st, NEG)
            m_old = m_ref[hh]
            m_new = jnp.maximum(m_old, jnp.max(st, axis=0, keepdims=True))
            alpha = jnp.exp2(m_old - m_new)
            pt = jnp.exp2(st - m_new).astype(BF)
            acc_ref[hh] = alpha * acc_ref[hh] + _dot(v1, pt)
            m_ref[hh] = m_new

    def body(kk, carry):
        block(kk * FLASH_KBLOCKS, FLASH_KBLOCKS, False)
        return carry

    lax.fori_loop(0, qi // FLASH_KBLOCKS, body, 0)
    rem = qi % FLASH_KBLOCKS
    base = qi - rem
    size = FLASH_KBLOCKS // 2
    while size >= 1:
        @pl.when(rem & size != 0)
        def _(size=size):
            block(base + (rem & ~(2 * size - 1)), size, False)
        size //= 2
    block(qi, 1, True)
    for hh in range(FLASH_HEADS):
        acc = acc_ref[hh]
        o_t = acc[:MLA_V] / acc[MLA_V:MLA_V + 1]
        o_ref[0, :, hh * MLA_V:(hh + 1) * MLA_V] = o_t.T.astype(BF)


def _flash(q, k, vt):
    nb, _, seq, _ = q.shape
    t = vt.shape[-1]
    nh = FLASH_HEADS
    return pl.pallas_call(
        functools.partial(_flash_kernel, t=t),
        grid=(nb, MLA_HEADS // nh, seq // t),
        in_specs=[
            pl.BlockSpec((1, nh, t, HEAD_QK), lambda b, h, i: (b, h, i, 0)),
            pl.BlockSpec((1, nh, seq, HEAD_QK), lambda b, h, i: (b, h, 0, 0), pipeline_mode=pl.Buffered(1)),
            pl.BlockSpec((1, nh, seq // t, MLA_V, t), lambda b, h, i: (b, h, 0, 0, 0),
                         pipeline_mode=pl.Buffered(1)),
        ],
        out_specs=pl.BlockSpec((1, t, nh * MLA_V), lambda b, h, i: (b, i, h)),
        out_shape=jax.ShapeDtypeStruct((nb, seq, MLA_HEADS * MLA_V), BF),
        scratch_shapes=[pltpu.VMEM((nh, 1, t), F32), pltpu.VMEM((nh, MLA_V + DENOM_ROWS, t), F32)],
        compiler_params=_cp(("parallel", "parallel", "arbitrary"), vmem_mb=48),
        name="mla_flash",
    )(q, k, vt)


def _page_copies(pt_ref, ckv_hbm, kpt_hbm, ckbuf, kpbuf, sem, sample, slot, *, layer, n_pages):
    copies = []
    for p in range(n_pages):
        pg = pt_ref[sample, p]
        copies.append(pltpu.make_async_copy(
            ckv_hbm.at[layer, pg], ckbuf.at[slot, pl.ds(p * PAGE, PAGE), :], sem.at[0, slot]))
        copies.append(pltpu.make_async_copy(
            kpt_hbm.at[layer, pg], kpbuf.at[slot, :, pl.ds(p * PAGE, PAGE)], sem.at[1, slot]))
    return copies


def _wait_grouped(copies):
    for c in copies[0::2]:
        c.wait()
    for c in copies[1::2]:
        c.wait()


def _decode_kernel(pt_ref, q_ref, nk_ref, ckv_hbm, kpt_hbm, o_ref, ckbuf, kpbuf, sem, *,
                   dec_seq, layer, n_pages):
    b = pl.program_id(0)
    last = pl.num_programs(0) - 1
    slot = b % 2
    copies = functools.partial(_page_copies, pt_ref, ckv_hbm, kpt_hbm, ckbuf, kpbuf, sem,
                               layer=layer, n_pages=n_pages)

    @pl.when(b == 0)
    def _():
        for c in copies(0, 0):
            c.start()

    _wait_grouped(copies(b, slot))
    nxt = jnp.minimum(b + 1, last)
    for c in copies(nxt, 1 - slot):
        c.start()

    rows = MLA_HEADS * dec_seq
    qm = q_ref[0]
    ql = qm[:, :KV_RANK]
    qp = qm[:, KV_RANK:]
    n_slices = DECODE_SLICES if n_pages % DECODE_SLICES == 0 else 1
    width = n_pages * PAGE // n_slices
    cks, scores = [], []
    for c in range(n_slices):
        ck = ckbuf[slot, c * width:(c + 1) * width, :].astype(BF)
        kp = kpbuf[slot, :, c * width:(c + 1) * width].astype(BF)
        cks.append(ck)
        scores.append(_dot_nt(ql, ck) + _dot(qp, kp))
    nk = nk_ref[0]
    kt = lax.broadcasted_iota(jnp.int32, (rows, NEW_KEY_ROWS), 1)
    qt = lax.broadcasted_iota(jnp.int32, (rows, NEW_KEY_ROWS), 0) % dec_seq
    s_new = jnp.where(kt <= qt, _dot_nt(qm, nk), NEG)
    m = jnp.max(s_new, axis=-1, keepdims=True)
    for s in scores:
        m = jnp.maximum(m, jnp.max(s, axis=-1, keepdims=True))
    p_new = jnp.exp(s_new - m)
    denom = jnp.sum(p_new, axis=-1, keepdims=True)
    acc = _dot(p_new.astype(BF), nk[:, :KV_RANK])
    for s, ck in zip(scores, cks):
        p = jnp.exp(s - m)
        denom = denom + jnp.sum(p, axis=-1, keepdims=True)
        acc = acc + _dot(p.astype(BF), ck)
    o_ref[0] = acc / denom

    @pl.when(b == last)
    def _():
        _wait_grouped(copies(nxt, 1 - slot))


def _decode(page_table, q, newk, cache_ckv, cache_kpe_t, layer, dec_seq):
    nb, n_pages = page_table.shape
    rows = MLA_HEADS * dec_seq
    keys = n_pages * PAGE
    grid_spec = pltpu.PrefetchScalarGridSpec(
        num_scalar_prefetch=1,
        grid=(nb,),
        in_specs=[
            pl.BlockSpec((1, rows, QK), lambda b, pt: (b, 0, 0)),
            pl.BlockSpec((1, NEW_KEY_ROWS, QK), lambda b, pt: (b, 0, 0)),
            pl.BlockSpec(memory_space=pl.ANY),
            pl.BlockSpec(memory_space=pl.ANY),
        ],
        out_specs=pl.BlockSpec((1, rows, KV_RANK), lambda b, pt: (b, 0, 0)),
        scratch_shapes=[pltpu.VMEM((2, keys, KV_RANK), F32), pltpu.VMEM((2, ROPE, keys), F32),
                        pltpu.SemaphoreType.DMA((2, 2))],
    )
    return pl.pallas_call(
        functools.partial(_decode_kernel, dec_seq=dec_seq, layer=layer, n_pages=n_pages),
        grid_spec=grid_spec,
        out_shape=jax.ShapeDtypeStruct((nb, rows, KV_RANK), F32),
        compiler_params=_cp(("arbitrary",), vmem_mb=48),
        name="mla_decode",
    )(page_table, q, newk, cache_ckv, cache_kpe_t)


def _mla_out_kernel(o_ref, wuv_ref, wo_ref, r_ref, out_ref):
    parts = [_dot(o_ref[0, h], wuv_ref[h]).astype(BF) for h in range(MLA_HEADS)]
    out_ref[...] = r_ref[...] + _dot(jnp.concatenate(parts, axis=1), wo_ref[...])


def _mla_out(o_lat, wuv, wo, res):
    nb, _, seq, _ = o_lat.shape
    tm = min(ROW_TILE, seq)
    tpb = seq // tm
    t = nb * seq
    return pl.pallas_call(
        _mla_out_kernel,
        grid=(t // tm,),
        in_specs=[
            pl.BlockSpec((1, MLA_HEADS, tm, KV_RANK), lambda i: (i // tpb, 0, i % tpb, 0)),
            _full(wuv.shape), _full(wo.shape),
            pl.BlockSpec((tm, D), lambda i: (i, 0)),
        ],
        out_specs=pl.BlockSpec((tm, D), lambda i: (i, 0)),
        out_shape=jax.ShapeDtypeStruct((t, D), F32),
        compiler_params=_cp(("parallel",)),
        name="mla_out",
    )(o_lat, wuv, wo, res)


def _softmax_rows(s):
    m = jnp.max(s, axis=-1, keepdims=True)
    p = jnp.exp(s - m)
    return p / jnp.sum(p, axis=-1, keepdims=True)


def _mem_prompt_kernel(x_ref, g_ref, wq_ref, k_ref, v_ref, wo_ref, o_ref):
    x = x_ref[...]
    xn = _rms(x, g_ref[...]).astype(BF)
    qm = (_dot(xn, wq_ref[...]) * MEM_HD ** -0.5).astype(BF)
    km = k_ref[0].astype(BF)
    vm = v_ref[0].astype(BF)
    parts = []
    for h in range(MEM_HEADS):
        sl = slice(h * MEM_HD, (h + 1) * MEM_HD)
        p = _softmax_rows(_dot_nt(qm[:, sl], km[:, sl]))
        parts.append(_dot(p.astype(BF), vm[:, sl]).astype(BF))
    o_ref[...] = x + _dot(jnp.concatenate(parts, axis=1), wo_ref[...])


def _mem_prompt(x, g, wq, km, vm, wo):
    t = x.shape[0]
    nb, mt, _ = km.shape
    seq = t // nb
    tm = min(ROW_TILE, seq)
    tpb = seq // tm
    return pl.pallas_call(
        _mem_prompt_kernel,
        grid=(t // tm,),
        in_specs=[
            pl.BlockSpec((tm, D), lambda i: (i, 0)), _full((1, D)), _full(wq.shape),
            pl.BlockSpec((1, mt, MEM_INNER), lambda i: (i // tpb, 0, 0)),
            pl.BlockSpec((1, mt, MEM_INNER), lambda i: (i // tpb, 0, 0)),
            _full(wo.shape),
        ],
        out_specs=pl.BlockSpec((tm, D), lambda i: (i, 0)),
        out_shape=jax.ShapeDtypeStruct((t, D), F32),
        compiler_params=_cp(("parallel",)),
        name="mem_attn_prompt",
    )(x, g, wq, km, vm, wo)


MEM_ROWS = MEM_HEADS * SUBLANES


def _mem_sample_kernel(x_ref, g_ref, wq_ref, k_ref, v_ref, wo_ref, o_ref):
    ns = SAMPLES_PER_STEP
    x = x_ref[...].reshape(ns * SUBLANES, D)
    xn = _rms(x, g_ref[...]).astype(BF)
    qall = _dot(xn, wq_ref[...]) * MEM_HD ** -0.5
    cols = k_ref.shape[2]
    head_of_row = lax.broadcasted_iota(jnp.int32, (MEM_ROWS, cols), 0) // SUBLANES
    head_of_col = lax.broadcasted_iota(jnp.int32, (MEM_ROWS, cols), 1) % MEM_HEADS
    own = head_of_row == head_of_col
    outs = []
    for s in range(ns):
        qs = qall[s * SUBLANES:(s + 1) * SUBLANES]
        qst = jnp.concatenate([qs[:, h * MEM_HD:(h + 1) * MEM_HD] for h in range(MEM_HEADS)], axis=0)
        sc = jnp.where(own, _dot_nt(qst.astype(BF), k_ref[0, s].astype(BF)), NEG)
        o = _dot(_softmax_rows(sc).astype(BF), v_ref[0, s].astype(BF))
        outs.append(jnp.concatenate([o[h * SUBLANES:(h + 1) * SUBLANES] for h in range(MEM_HEADS)], axis=1))
    out = x + _dot(jnp.concatenate(outs, axis=0).astype(BF), wo_ref[...])
    o_ref[...] = out.reshape(ns, SUBLANES, D)


def _mem_sample(x8, g, wq, cache_k, cache_v, wo, layer):
    nb = x8.shape[0]
    rows = cache_k.shape[2]
    ns = SAMPLES_PER_STEP
    return pl.pallas_call(
        _mem_sample_kernel,
        grid=(nb // ns,),
        in_specs=[
            pl.BlockSpec((ns, SUBLANES, D), lambda i: (i, 0, 0)), _full((1, D)), _full(wq.shape),
            pl.BlockSpec((1, ns, rows, MEM_HD), lambda i: (layer, i, 0, 0)),
            pl.BlockSpec((1, ns, rows, MEM_HD), lambda i: (layer, i, 0, 0)),
            _full(wo.shape),
        ],
        out_specs=pl.BlockSpec((ns, SUBLANES, D), lambda i: (i, 0, 0)),
        out_shape=jax.ShapeDtypeStruct((nb, SUBLANES, D), F32),
        compiler_params=_cp(("parallel",), vmem_mb=48),
        name="mem_attn_sample",
    )(x8, g, wq, cache_k, cache_v, wo)


def _router_kernel(x_ref, g_ref, wr_ref, xn_ref, route_ref):
    xn = _rms(x_ref[...], g_ref[...])
    xn_ref[...] = xn
    lane = lax.broadcasted_iota(jnp.int32, (xn.shape[0], LANES), 1).astype(F32)
    lg = jnp.where(lane < N_EXP, _dot_split(xn, wr_ref[...]), NEG)
    m1 = jnp.max(lg, axis=-1, keepdims=True)
    i1 = jnp.min(jnp.where(lg == m1, lane, float(LANES)), axis=-1, keepdims=True)
    lg2 = jnp.where(lane == i1, NEG, lg)
    m2 = jnp.max(lg2, axis=-1, keepdims=True)
    i2 = jnp.min(jnp.where(lg2 == m2, lane, float(LANES)), axis=-1, keepdims=True)
    e = jnp.exp(m2 - m1)
    g1 = 1.0 / (1.0 + e)
    g2 = e * g1
    route_ref[...] = jnp.where(lane == 0, i1, jnp.where(lane == 1, i2, jnp.where(lane == 2, g1,
                               jnp.where(lane == 3, g2, 0.0))))


def _router(x, g, wr_pad):
    t = x.shape[0]
    tm = min(ROW_TILE, t)
    return pl.pallas_call(
        _router_kernel,
        grid=(t // tm,),
        in_specs=[pl.BlockSpec((tm, D), lambda i: (i, 0)), _full((1, D)), _full(wr_pad.shape)],
        out_specs=[pl.BlockSpec((tm, D), lambda i: (i, 0)), pl.BlockSpec((tm, LANES), lambda i: (i, 0))],
        out_shape=[jax.ShapeDtypeStruct((t, D), F32), jax.ShapeDtypeStruct((t, LANES), F32)],
        compiler_params=_cp(("parallel",)),
        name="router",
    )(x, g, wr_pad)


SC_ROWS = 32


def _sc_mesh():
    return plsc.VectorSubcoreMesh(core_axis_name="c", subcore_axis_name="s")


def _sc_workers():
    info = plsc.get_sparse_core_info()
    return info.num_cores, info.num_cores * info.num_subcores


def _sc_gather_rows(table, idx):
    n = idx.shape[0]
    width = table.shape[1]
    ncores, nw = _sc_workers()
    per_w = n // nw
    n_chunks = per_w // SC_ROWS
    assert per_w * nw == n and n_chunks * SC_ROWS == per_w

    @functools.partial(
        pl.kernel, mesh=_sc_mesh(),
        out_type=jax.ShapeDtypeStruct((n, width), table.dtype),
        scratch_types=[pltpu.VMEM((per_w,), jnp.int32), pltpu.VMEM((SC_ROWS, width), table.dtype),
                       pltpu.SemaphoreType.DMA],
        name="sc_gather_rows",
    )
    def body(table_hbm, idx_hbm, out_hbm, idx_v, rows_v, sem):
        wid = lax.axis_index("s") * ncores + lax.axis_index("c")
        base = wid * per_w
        pltpu.sync_copy(idx_hbm.at[pl.ds(base, per_w)], idx_v)

        @pl.loop(0, n_chunks)
        def _(j):
            off = pl.multiple_of(j * SC_ROWS, SC_ROWS)
            pltpu.async_copy(table_hbm.at[idx_v.at[pl.ds(off, SC_ROWS)]], rows_v, sem).wait()
            pltpu.sync_copy(rows_v, out_hbm.at[pl.ds(base + off, SC_ROWS)])

    return body(table, idx)


def _sc_scatter_rows(src, idx, n_out):
    n = idx.shape[0]
    t, width = src.shape
    ncores, nw = _sc_workers()
    per_w = n // nw
    n_chunks = per_w // SC_ROWS
    assert per_w * nw == n and n_chunks * SC_ROWS == per_w and t % per_w == 0
    idx3 = idx.reshape(nw, n_chunks, SC_ROWS)

    @functools.partial(
        pl.kernel, mesh=_sc_mesh(),
        out_type=jax.ShapeDtypeStruct((n_out, width), src.dtype),
        scratch_types=[pltpu.VMEM((n_chunks, SC_ROWS), jnp.int32), pltpu.VMEM((SC_ROWS, width), src.dtype),
                       pltpu.SemaphoreType.DMA],
        name="sc_scatter_rows",
    )
    def body(src_hbm, idx_hbm, out_hbm, idx_v, rows_v, sem):
        wid = lax.axis_index("s") * ncores + lax.axis_index("c")
        base = lax.rem(wid * per_w, t)
        pltpu.sync_copy(idx_hbm.at[wid], idx_v)

        @pl.loop(0, n_chunks)
        def _(j):
            off = pl.multiple_of(j * SC_ROWS, SC_ROWS)
            pltpu.sync_copy(src_hbm.at[pl.ds(base + off, SC_ROWS)], rows_v)
            pltpu.async_copy(rows_v, out_hbm.at[idx_v.at[j]], sem).wait()

    return body(src, idx3)


MOE_TILE = 1024
MOE_TILE_SMALL = 256
MOE_BLOCK = 512
MOE_CHUNK = 256


def _moe_ffn_kernel(te_ref, nused_ref, x_ref, wg_ref, wu_ref, wd_ref, o_ref, xb_ref):
    del te_ref
    i = pl.program_id(0)
    j = pl.program_id(1)

    @pl.when(i < nused_ref[0])
    def _():
        @pl.when(j == 0)
        def _():
            xb_ref[...] = x_ref[...].astype(BF)
            o_ref[...] = jnp.zeros_like(o_ref)

        xb = xb_ref[...]
        part = None
        block = wd_ref.shape[2]
        for lo in range(0, block, MOE_CHUNK):
            sl = slice(lo, min(lo + MOE_CHUNK, block))
            gate = _dot(xb, wg_ref[0, 0, :, sl].astype(BF))
            up = _dot(xb, wu_ref[0, 0, :, sl].astype(BF))
            contrib = _dot((_silu(gate) * up).astype(BF), wd_ref[0, 0, sl, :].astype(BF))
            part = contrib if part is None else part + contrib
        o_ref[...] += part

    @pl.when(i >= nused_ref[0])
    def _():
        o_ref[...] = jnp.zeros_like(o_ref)


def _moe_ffn(tile_expert, nused, xs, wgu, wd, layer, *, tm):
    npad = xs.shape[0]
    block = MOE_BLOCK
    nblk = D_FFE // block

    def blk(i, j, nu):
        return jnp.where(i < nu[0], j, nblk - 1)

    grid_spec = pltpu.PrefetchScalarGridSpec(
        num_scalar_prefetch=2,
        grid=(npad // tm, nblk),
        in_specs=[
            pl.BlockSpec((tm, D), lambda i, j, te, nu: (jnp.minimum(i, nu[0] - 1), 0)),
            pl.BlockSpec((1, 1, D, block), lambda i, j, te, nu: (layer, te[i], 0, blk(i, j, nu))),
            pl.BlockSpec((1, 1, D, block), lambda i, j, te, nu: (layer, te[i], 0, nblk + blk(i, j, nu))),
            pl.BlockSpec((1, 1, block, D), lambda i, j, te, nu: (layer, te[i], blk(i, j, nu), 0)),
        ],
        out_specs=pl.BlockSpec((tm, D), lambda i, j, te, nu: (i, 0)),
        scratch_shapes=[pltpu.VMEM((tm, D), BF)],
    )
    return pl.pallas_call(
        _moe_ffn_kernel,
        grid_spec=grid_spec,
        out_shape=jax.ShapeDtypeStruct((npad, D), F32),
        compiler_params=_cp(("parallel", "arbitrary"), vmem_mb=56),
        name="moe_ffn",
    )(tile_expert, nused, xs, wgu, wgu, wd)


ROUTE_GATE_LANE = 2


def _combine_kernel(h_ref, y0_ref, y1_ref, route_ref, g_ref, o_ref, *, final):
    gl = ROUTE_GATE_LANE
    route = route_ref[...]
    out = h_ref[...] + route[:, gl:gl + 1] * y0_ref[...] + route[:, gl + 1:gl + 2] * y1_ref[...]
    if final:
        out = _rms(out, g_ref[...])
    o_ref[...] = out


def _combine(h, y2, route, g, *, final):
    t = h.shape[0]
    tm = min(ROW_TILE, t)
    nt = t // tm
    return pl.pallas_call(
        functools.partial(_combine_kernel, final=final),
        grid=(nt,),
        in_specs=[
            pl.BlockSpec((tm, D), lambda i: (i, 0)),
            pl.BlockSpec((tm, D), lambda i: (i, 0)),
            pl.BlockSpec((tm, D), lambda i: (nt + i, 0)),
            pl.BlockSpec((tm, LANES), lambda i: (i, 0)),
            _full((1, D)),
        ],
        out_specs=pl.BlockSpec((tm, D), lambda i: (i, 0)),
        out_shape=jax.ShapeDtypeStruct((t, D), F32),
        compiler_params=_cp(("parallel",)),
        name="moe_combine",
    )(h, y2, y2, route, g)


def _moe(h, g, wr_pad, wgu, wd, layer, final_g, *, final):
    t = h.shape[0]
    tm = MOE_TILE if 2 * t >= 2 * N_EXP * MOE_TILE else MOE_TILE_SMALL
    xn, route = _router(h, g, wr_pad)
    eidx = route[:, :ROUTE_GATE_LANE].astype(jnp.int32)
    e_flat = eidx.T.reshape(-1)
    onehot = (e_flat[:, None] == jnp.arange(N_EXP, dtype=jnp.int32)[None, :]).astype(jnp.int32)
    csum = jnp.cumsum(onehot, axis=0)
    counts = csum[-1]
    rank = jnp.sum(onehot * csum, axis=1) - 1
    padded = ((counts + tm - 1) // tm) * tm
    ends = jnp.cumsum(padded)
    starts = ends - padded
    dest = (jnp.sum(onehot * starts[None, :], axis=1) + rank).astype(jnp.int32)
    n_tiles = -(-2 * t // tm) + N_EXP
    tile_start = jnp.arange(n_tiles, dtype=jnp.int32) * tm
    tile_expert = jnp.minimum(jnp.sum((tile_start[:, None] >= ends[None, :]).astype(jnp.int32), axis=1),
                              N_EXP - 1).astype(jnp.int32)
    nused = (ends[-1] // tm).astype(jnp.int32).reshape(1)
    tile_expert = jnp.where(jnp.arange(n_tiles) < nused[0], tile_expert, tile_expert[nused[0] - 1])
    xs = _sc_scatter_rows(xn, dest, n_tiles * tm)
    ys = _moe_ffn(tile_expert, nused, xs, wgu, wd, layer, tm=tm)
    y2 = _sc_gather_rows(ys, dest)
    return _combine(h, y2, route, final_g, final=final)


def _rope_tables(pos):
    half = ROPE // 2
    inv_freq = ROPE_BASE ** (-jnp.arange(half, dtype=F32) / half)
    ang = pos.astype(F32)[:, None] * inv_freq
    cos = jnp.cos(ang)
    sin = jnp.sin(ang)
    cc = jnp.concatenate([cos, cos], axis=-1)
    ss = jnp.concatenate([-sin, sin], axis=-1)
    return jnp.tile(cc, (1, MLA_HEADS)), jnp.tile(ss, (1, MLA_HEADS))


def _even_layer(h, g, prm, cinit, sinit, layer, *, nb, sample, state_layers=1, state_prev=None):
    t = h.shape[0]
    seq = t // nb
    proj, dtp = _even_proj(h, g, prm["w_zxu"], prm["w_dt"])
    proj = proj.reshape(nb, seq, ZXU)
    dtp = dtp.reshape(nb, seq, LANES)
    if sample:
        pad = ((0, 0), (0, SUBLANES - seq), (0, 0))
        proj = jnp.pad(proj, pad)
        dtp = jnp.pad(dtp, pad)
        q, lb, lc = SAMPLE_Q, SUBLANES, seq
    else:
        q, lb, lc = CHUNK, CHUNK, CHUNK
    cinit8 = jnp.pad(cinit, ((0, 0), (SUBLANES - (SSD_CONV - 1), 0), (0, 0)))
    outs = _even_mixer(proj, dtp, cinit8, sinit.reshape(-1, nb, SSD_INNER, SSD_STATE), prm,
                       q=q, lb=lb, lc=lc, want_v=sample, layer=layer,
                       state_layers=state_layers, state_prev=state_prev)
    ymix, cout, sout = outs[:3]
    v = None
    if sample:
        ymix = ymix[:, :seq]
        v = outs[3][:, :seq]
    h = _matmul_res(ymix.reshape(t, SSD_INNER + GMLP_WIDTH), prm["w_out"], h)
    return h, cout, sout, v


def _prep_even(i, w_in, conv_w, conv_b, dt_bias, a_log, d_skip, ssd_gain, ln_g, ln_b, ws, bs, w_out):
    w = w_in[i]
    o1 = SSD_INNER + CONV_DIM
    w_zxu = jnp.concatenate([w[:, :o1], w[:, o1 + SSD_HEADS:]], axis=1).astype(BF)
    w_dt = jnp.pad(w[:, o1:o1 + SSD_HEADS], ((0, 0), (0, LANES - SSD_HEADS)))
    padl = (0, LANES - SSD_HEADS)
    return dict(
        w_zxu=w_zxu, w_dt=w_dt,
        conv_w=jnp.pad(conv_w[i], ((0, SUBLANES - SSD_CONV), (0, 0))),
        conv_b=conv_b[i][None, :],
        dt_bias=jnp.pad(dt_bias[i], padl)[None, :],
        a_log=jnp.pad(a_log[i], padl)[None, :],
        d_skip=jnp.repeat(d_skip[i], SSD_HEAD_DIM)[None, :],
        ssd_gain=ssd_gain[i][None, :],
        ln_g=ln_g[i][None, :], ln_b=ln_b[i][None, :],
        ws=ws[i], bst=bs[i].T,
        expand=(jnp.arange(LANES)[:, None] == jnp.arange(SSD_INNER)[None, :] // SSD_HEAD_DIM).astype(BF),
        w_out=w_out[i].astype(BF),
    )


def _prep_mla(i, w_down, q_gain, kv_gain, w_uq, w_uk, w_uv, w_o):
    wd = w_down[i]
    wk = wd[:, Q_RANK + KV_RANK:]
    half = ROPE // 2
    rot = lambda a: jnp.concatenate([a[..., half:], a[..., :half]], axis=-1)
    uq = w_uq[i]
    uq_pe = uq[:, :, NOPE:]
    return dict(
        wdq=wd[:, :Q_RANK].astype(BF),
        wdkv=wd[:, Q_RANK:Q_RANK + KV_RANK].astype(BF),
        wdk2=jnp.concatenate([wk, rot(wk)], axis=1).astype(BF),
        q_gain=q_gain[i][None, :], kv_gain=kv_gain[i][None, :],
        wqn=uq[:, :, :NOPE].reshape(Q_RANK, MLA_HEADS * NOPE).astype(BF),
        wqp=jnp.concatenate([uq_pe.reshape(Q_RANK, -1), rot(uq_pe).reshape(Q_RANK, -1)], axis=1).astype(BF),
        wuk=jnp.transpose(w_uk[i], (1, 2, 0)).astype(BF),
        wuv=jnp.transpose(w_uv[i], (1, 0, 2)).astype(BF),
        wkn=w_uk[i].reshape(KV_RANK, MLA_HEADS * NOPE).astype(BF),
        wv=w_uv[i].reshape(KV_RANK, MLA_HEADS * MLA_V).T.astype(BF),
        wo=w_o[i].astype(BF),
    )


def kernel(x_prompt, x_sample, state_ssd, state_conv, cache_mla_ckv, cache_mla_kpe, cache_mem_k, cache_mem_v, page_table, mem_prompt, mix_norm, w_in, conv_w, conv_b, dt_bias, a_log, d_skip, ssd_gain, gmlp_ln_g, gmlp_ln_b, gmlp_ws, gmlp_bs, w_out_even, w_mla_down, mla_q_gain, mla_kv_gain, w_mla_uq, w_mla_uk, w_mla_uv, w_mla_o, xattn_norm, mem_norm, w_mem_q, w_mem_k, w_mem_v, w_mem_o, ffn_norm, w_ffn_gu, w_ffn_down, w_router, w_exp_gu, w_exp_down, final_norm):
    nbp, seq, _ = x_prompt.shape
    nbs, dseq, _ = x_sample.shape
    depth = mix_norm.shape[0]
    past = page_table.shape[1] * PAGE
    mt = mem_prompt.shape[1]
    hp = x_prompt.reshape(nbp * seq, D)
    hs = x_sample.reshape(nbs * dseq, D)
    cos_p, sin_p = _rope_tables(jnp.arange(seq, dtype=jnp.int32))
    cos_s, sin_s = _rope_tables(past + jnp.arange(dseq, dtype=jnp.int32))
    cos_s = jnp.tile(cos_s, (nbs, 1))
    sin_s = jnp.tile(sin_s, (nbs, 1))
    cache_k4 = cache_mem_k.reshape(depth, nbs, mt * MEM_HEADS, MEM_HD)
    cache_v4 = cache_mem_v.reshape(depth, nbs, mt * MEM_HEADS, MEM_HD)
    cache_kpe_t = jnp.swapaxes(cache_mla_kpe, 2, 3)
    final_g = final_norm[None, :]

    p_ssd, p_conv, p_ckv, p_kpe, p_mk, p_mv = [], [], [], [], [], []
    s_conv, s_v, s_ckv, s_kpe = [], [], [], []
    n_even = (depth + 1) // 2
    s_state = None
    for l in range(depth):
        i = l // 2
        g_mix = mix_norm[l][None, :]
        if l % 2 == 0:
            prm = _prep_even(i, w_in, conv_w, conv_b, dt_bias, a_log, d_skip, ssd_gain, gmlp_ln_g,
                             gmlp_ln_b, gmlp_ws, gmlp_bs, w_out_even)
            buf0 = jnp.zeros((nbp, SSD_CONV - 1, CONV_DIM), F32)
            h00 = jnp.zeros((1, nbp, SSD_HEADS, SSD_HEAD_DIM, SSD_STATE), F32)
            hp, buf_p, ssd_p, _ = _even_layer(hp, g_mix, prm, buf0, h00, 0, nb=nbp, sample=False)
            hs, buf_s, s_state, v_s = _even_layer(hs, g_mix, prm, state_conv[i], state_ssd, i, nb=nbs,
                                                  sample=True, state_layers=n_even, state_prev=s_state)
            p_ssd.append(ssd_p.reshape(nbp, SSD_HEADS, SSD_HEAD_DIM, SSD_STATE))
            p_conv.append(buf_p)
            s_conv.append(buf_s)
            s_v.append(v_s)
        else:
            prm = _prep_mla(i, w_mla_down, mla_q_gain, mla_kv_gain, w_mla_uq, w_mla_uk, w_mla_uv, w_mla_o)
            ckv, kpe, qh, kh, vh = _mla_proj_prompt(hp, g_mix, prm, cos_p, sin_p, nb=nbp)
            o_p = _flash(qh, kh, vh)
            hp = _matmul_res(o_p.reshape(nbp * seq, MLA_HEADS * MLA_V), prm["wo"], hp)
            p_ckv.append(ckv.reshape(nbp, seq, KV_RANK))
            p_kpe.append(kpe.reshape(nbp, seq, ROPE))

            ckv_s, kpe_s, kcat_s, q_s = _mla_proj(hs, g_mix, prm, cos_s, sin_s, nb=1)
            q_s = q_s[0].reshape(MLA_HEADS, nbs, dseq, QK).transpose(1, 0, 2, 3).reshape(nbs, MLA_HEADS * dseq, QK)
            newk = jnp.pad(kcat_s.reshape(nbs, dseq, QK), ((0, 0), (0, NEW_KEY_ROWS - dseq), (0, 0)))
            o_s = _decode(page_table, q_s, newk, cache_mla_ckv, cache_kpe_t, i, dseq)
            o_s = o_s.reshape(nbs, MLA_HEADS, dseq, KV_RANK).transpose(1, 0, 2, 3)
            o_s = o_s.reshape(1, MLA_HEADS, nbs * dseq, KV_RANK).astype(BF)
            hs = _mla_out(o_s, prm["wuv"], prm["wo"], hs)
            s_ckv.append(ckv_s.reshape(nbs, dseq, KV_RANK))
            s_kpe.append(kpe_s.reshape(nbs, dseq, ROPE))

        wkv = jnp.concatenate([w_mem_k[l], w_mem_v[l]], axis=1).astype(BF)
        kv = _rms_matmul(mem_prompt.reshape(nbp * mt, D), mem_norm[l][None, :], wkv, tn=MEM_INNER)
        mk_p = kv[:, :MEM_INNER].reshape(nbp, mt, MEM_INNER)
        mv_p = kv[:, MEM_INNER:].reshape(nbp, mt, MEM_INNER)
        g_x = xattn_norm[l][None, :]
        wq = w_mem_q[l].astype(BF)
        wo = w_mem_o[l].astype(BF)
        hp = _mem_prompt(hp, g_x, wq, mk_p, mv_p, wo)
        x8 = jnp.pad(hs.reshape(nbs, dseq, D), ((0, 0), (0, SUBLANES - dseq), (0, 0)))
        hs = _mem_sample(x8, g_x, wq, cache_k4, cache_v4, wo, l)[:, :dseq].reshape(nbs * dseq, D)
        p_mk.append(mk_p.reshape(nbp, mt, MEM_HEADS, MEM_HD))
        p_mv.append(mv_p.reshape(nbp, mt, MEM_HEADS, MEM_HD))

        g_f = ffn_norm[l][None, :]
        if l % 2 == 0:
            wgu = w_ffn_gu[i].astype(BF)
            wd = w_ffn_down[i].astype(BF)
            hp = _ffn(hp, g_f, wgu, wd)
            hs = _ffn(hs, g_f, wgu, wd)
        else:
            wr = jnp.pad(w_router[i], ((0, 0), (0, LANES - N_EXP)))
            final = l == depth - 1
            hp = _moe(hp, g_f, wr, w_exp_gu, w_exp_down, i, final_g, final=final)
            hs = _moe(hs, g_f, wr, w_exp_gu, w_exp_down, i, final_g, final=final)
    if depth % 2 == 1:
        raise NotImplementedError("the final norm is fused into the last routed-expert layer")
    y_prompt = hp.reshape(nbp, seq, D)
    y_sample = hs.reshape(nbs, dseq, D)
    return (y_prompt, y_sample,
            jnp.stack(p_ssd), jnp.stack(p_conv), jnp.stack(p_ckv), jnp.stack(p_kpe),
            jnp.stack(p_mk), jnp.stack(p_mv),
            s_state.reshape(n_even, nbs, SSD_HEADS, SSD_HEAD_DIM, SSD_STATE), jnp.stack(s_conv), jnp.stack(s_v), jnp.stack(s_ckv), jnp.stack(s_kpe))
```

```python
import functools

import jax
import jax.numpy as jnp
from jax import lax
from jax.experimental import pallas as pl
from jax.experimental.pallas import tpu as pltpu
from jax.experimental.pallas import tpu_sc as plsc

F32 = jnp.float32
BF = jnp.bfloat16
EPS = 1e-6
NEG = -1e30

D = 1024
SSD_HEADS = 16
SSD_HEAD_DIM = 64
SSD_INNER = SSD_HEADS * SSD_HEAD_DIM
SSD_GROUPS = 2
SSD_STATE = 128
SSD_CONV = 4
CONV_DIM = SSD_INNER + 2 * SSD_GROUPS * SSD_STATE
GMLP_GROUPS = 8
GMLP_WIDTH = 1024
CHUNK = 128
EVEN_SEQS_PER_STEP = 2
SAMPLE_Q = 16
ZXU = SSD_INNER + CONV_DIM + 2 * GMLP_WIDTH
MLA_HEADS = 8
NOPE = 128
ROPE = 64
MLA_V = 128
Q_RANK = 256
KV_RANK = 256
MLA_SCALE = (NOPE + ROPE) ** -0.5
QK = KV_RANK + ROPE
ROPE_BASE = 10000.0
MEM_HEADS = 4
MEM_HD = 128
MEM_INNER = MEM_HEADS * MEM_HD
N_EXP = 8
D_FFE = 3584
PAGE = 128
LANES = 128
SUBLANES = 8
ROW_TILE = 512
DECODE_SLICES = 4
NEW_KEY_ROWS = 16
SAMPLES_PER_STEP = 8


def _cp(sem, vmem_mb=None):
    kw = dict(dimension_semantics=sem)
    if vmem_mb is not None:
        kw["vmem_limit_bytes"] = vmem_mb * 1024 * 1024
    return pltpu.CompilerParams(**kw)


def _rms(x, g):
    return x * lax.rsqrt(jnp.mean(x * x, axis=-1, keepdims=True) + EPS) * g


def _dot(a, b):
    return jnp.dot(a, b, preferred_element_type=F32)


def _dot_nt(a, b):
    return lax.dot_general(a, b, (((1,), (1,)), ((), ())), preferred_element_type=F32)


def _split_bf16(v):
    hi = v.astype(BF)
    return hi, (v - hi.astype(F32)).astype(BF)


def _dot_split(a, b):
    a_hi, a_lo = _split_bf16(a)
    b_hi, b_lo = _split_bf16(b)
    return _dot(a_hi, b_hi) + _dot(a_hi, b_lo) + _dot(a_lo, b_hi)


def _dot_f32(a, b):
    return jnp.dot(a, b, preferred_element_type=F32, precision=lax.Precision.HIGHEST)


def _silu(x):
    return x * jax.nn.sigmoid(x)


def _full(shape):
    n = len(shape)
    return pl.BlockSpec(shape, lambda *_: (0,) * n)


def _largest_tile(n, cap):
    best = LANES
    for t in range(LANES, cap + 1, LANES):
        if n % t == 0:
            best = t
    return best


def _rms_matmul_kernel(x_ref, g_ref, w_ref, o_ref, xn_ref):
    @pl.when(pl.program_id(1) == 0)
    def _():
        xn_ref[...] = _rms(x_ref[...], g_ref[...]).astype(BF)

    o_ref[...] = _dot(xn_ref[...], w_ref[...])


def _rms_matmul(x, g, w, *, tn):
    t, k = x.shape
    n = w.shape[1]
    tm = min(ROW_TILE, t)
    return pl.pallas_call(
        _rms_matmul_kernel,
        grid=(t // tm, n // tn),
        in_specs=[
            pl.BlockSpec((tm, k), lambda i, j: (i, 0)),
            pl.BlockSpec((1, k), lambda i, j: (0, 0)),
            pl.BlockSpec((k, tn), lambda i, j: (0, j)),
        ],
        out_specs=pl.BlockSpec((tm, tn), lambda i, j: (i, j)),
        out_shape=jax.ShapeDtypeStruct((t, n), F32),
        scratch_shapes=[pltpu.VMEM((tm, k), BF)],
        compiler_params=_cp(("parallel", "arbitrary")),
        name="rms_matmul",
    )(x, g, w)


EVEN_PROJ_CHUNK = 512


def _even_proj_kernel(x_ref, g_ref, w_ref, wdt_ref, o_ref, dt_ref):
    xn = _rms(x_ref[...], g_ref[...])
    dt_ref[...] = _dot_split(xn, wdt_ref[...])
    xb = xn.astype(BF)
    for lo in range(0, o_ref.shape[1], EVEN_PROJ_CHUNK):
        o_ref[:, lo:lo + EVEN_PROJ_CHUNK] = _dot(xb, w_ref[:, lo:lo + EVEN_PROJ_CHUNK])


def _even_proj(x, g, w, wdt):
    t, k = x.shape
    n = w.shape[1]
    tm = min(ROW_TILE, t)
    return pl.pallas_call(
        _even_proj_kernel,
        grid=(t // tm,),
        in_specs=[
            pl.BlockSpec((tm, k), lambda i: (i, 0)),
            _full((1, k)),
            pl.BlockSpec((k, n), lambda i: (0, 0), pipeline_mode=pl.Buffered(1)),
            pl.BlockSpec((k, LANES), lambda i: (0, 0), pipeline_mode=pl.Buffered(1)),
        ],
        out_specs=[pl.BlockSpec((tm, n), lambda i: (i, 0)), pl.BlockSpec((tm, LANES), lambda i: (i, 0))],
        out_shape=[jax.ShapeDtypeStruct((t, n), F32), jax.ShapeDtypeStruct((t, LANES), F32)],
        compiler_params=_cp(("parallel",), vmem_mb=48),
        name="even_proj",
    )(x, g, w, wdt)


def _matmul_res_kernel(a_ref, w_ref, r_ref, o_ref):
    o_ref[...] = r_ref[...] + _dot(a_ref[...].astype(BF), w_ref[...])


def _matmul_res(a, w, res):
    t, k = a.shape
    n = w.shape[1]
    tn = _largest_tile(n, 1024)
    tm = min(ROW_TILE, t)
    return pl.pallas_call(
        _matmul_res_kernel,
        grid=(t // tm, n // tn),
        in_specs=[
            pl.BlockSpec((tm, k), lambda i, j: (i, 0)),
            pl.BlockSpec((k, tn), lambda i, j: (0, j)),
            pl.BlockSpec((tm, tn), lambda i, j: (i, j)),
        ],
        out_specs=pl.BlockSpec((tm, tn), lambda i, j: (i, j)),
        out_shape=jax.ShapeDtypeStruct((t, n), F32),
        compiler_params=_cp(("parallel", "arbitrary")),
        name="matmul_res",
    )(a, w, res)


FF_CHUNK = 256


def _ffn_kernel(x_ref, g_ref, wgu_ref, wd_ref, o_ref, hid_ref, *, ff):
    x = x_ref[...]
    xn = _rms(x, g_ref[...]).astype(BF)
    for c in range(ff // FF_CHUNK):
        lo = c * FF_CHUNK
        gate = _dot(xn, wgu_ref[:, lo:lo + FF_CHUNK])
        up = _dot(xn, wgu_ref[:, ff + lo:ff + lo + FF_CHUNK])
        hid_ref[:, lo:lo + FF_CHUNK] = (_silu(gate) * up).astype(BF)
    o_ref[...] = x + _dot(hid_ref[...], wd_ref[...])


def _ffn(x, g, wgu, wd):
    t = x.shape[0]
    ff = wd.shape[0]
    tm = min(ROW_TILE, t)
    return pl.pallas_call(
        functools.partial(_ffn_kernel, ff=ff),
        grid=(t // tm,),
        in_specs=[
            pl.BlockSpec((tm, D), lambda i: (i, 0)),
            _full((1, D)),
            pl.BlockSpec((D, 2 * ff), lambda i: (0, 0), pipeline_mode=pl.Buffered(1)),
            pl.BlockSpec((ff, D), lambda i: (0, 0), pipeline_mode=pl.Buffered(1)),
        ],
        out_specs=pl.BlockSpec((tm, D), lambda i: (i, 0)),
        out_shape=jax.ShapeDtypeStruct((t, D), F32),
        scratch_shapes=[pltpu.VMEM((tm, ff), BF)],
        compiler_params=_cp(("parallel",), vmem_mb=48),
        name="ffn",
    )(x, g, wgu, wd)


def _softplus(x):
    return jnp.maximum(x, 0.0) + jnp.log1p(jnp.exp(-jnp.abs(x)))


def _gelu_tanh(x):
    return 0.5 * x * (1.0 + jnp.tanh(0.7978845608028654 * (x + 0.044715 * (x * x * x))))


def _even_kernel(proj_ref, dt_ref, cinit_ref, sinit_ref, cw_ref, cb_ref, dtb_ref, alog_ref,
                 dsk_ref, sg_ref, lng_ref, lnb_ref, ws_ref, bst_ref, e_ref,
                 ymix_ref, cout_ref, sout_ref, v_ref, ext_ref, ht_ref, *, q, lb, lc, nbb, out_slot):
    c = pl.program_id(1)

    @pl.when(c == 0)
    def _():
        for bb in range(nbb):
            ext_ref[bb, 0:SUBLANES, :] = cinit_ref[bb]
            ht_ref[bb] = sinit_ref[0, bb].T

    for bb in range(nbb):
        _even_block(proj_ref, dt_ref, cw_ref, cb_ref, dtb_ref, alog_ref,
                    dsk_ref, sg_ref, lng_ref, lnb_ref, ws_ref, bst_ref, e_ref,
                    ymix_ref, v_ref, ext_ref, ht_ref, bb, q=q, lb=lb, lc=lc)

    @pl.when(c == pl.num_programs(1) - 1)
    def _():
        for bb in range(nbb):
            cout_ref[bb] = ext_ref[bb, SUBLANES + lc - 3:SUBLANES + lc, :]
            for slot in range(sout_ref.shape[0]):
                if slot == out_slot:
                    sout_ref[slot, bb] = ht_ref[bb].T
                else:
                    sout_ref[slot, bb] = jnp.zeros((SSD_INNER, SSD_STATE), F32)


def _even_block(proj_ref, dt_ref, cw_ref, cb_ref, dtb_ref, alog_ref,
                dsk_ref, sg_ref, lng_ref, lnb_ref, ws_ref, bst_ref, e_ref,
                ymix_ref, v_ref, ext_ref, ht_ref, bb, *, q, lb, lc):

    if lb == q:
        p = proj_ref[bb]
        dtr = dt_ref[bb]
    else:
        p = jnp.concatenate([proj_ref[bb], jnp.zeros((q - lb, ZXU), F32)], axis=0)
        dtr = jnp.concatenate([dt_ref[bb], jnp.zeros((q - lb, LANES), F32)], axis=0)
    z = p[:, :SSD_INNER]
    xbc_raw = p[:, SSD_INNER:SSD_INNER + CONV_DIM]
    uv = p[:, SSD_INNER + CONV_DIM:]

    ext_ref[bb, SUBLANES:SUBLANES + q, :] = xbc_raw
    conv = (cb_ref[...] + cw_ref[0:1, :] * ext_ref[bb, 5:5 + q, :] + cw_ref[1:2, :] * ext_ref[bb, 6:6 + q, :]
            + cw_ref[2:3, :] * ext_ref[bb, 7:7 + q, :] + cw_ref[3:4, :] * xbc_raw)
    ext_ref[bb, 0:SUBLANES, :] = ext_ref[bb, q:q + SUBLANES, :]

    xbc = _silu(conv)
    xs = xbc[:, :SSD_INNER]
    gw = SSD_STATE
    bm = [xbc[:, SSD_INNER + g * gw:SSD_INNER + (g + 1) * gw] for g in range(SSD_GROUPS)]
    cm = [xbc[:, SSD_INNER + (SSD_GROUPS + g) * gw:SSD_INNER + (SSD_GROUPS + g + 1) * gw]
          for g in range(SSD_GROUPS)]

    row = lax.broadcasted_iota(jnp.int32, (q, q), 0)
    col = lax.broadcasted_iota(jnp.int32, (q, q), 1)
    causal = row >= col

    dt = _softplus(dtr + dtb_ref[...])
    if lc < q:
        dt = jnp.where(lax.broadcasted_iota(jnp.int32, (q, LANES), 0) < lc, dt, 0.0)
    a = dt * (-jnp.exp(alog_ref[...]))
    a_cum = _dot_f32(causal.astype(F32), a)
    a_cum_t = a_cum.T
    a_last = a_cum[q - 1:q, :]
    decay_end = jnp.exp(a_last - a_cum)
    ea = jnp.exp(a_cum)
    chunk_decay = jnp.exp(a_last)

    cmb = [m.astype(BF) for m in cm]
    cb = [_dot_nt(cmb[g], bm[g].astype(BF)) for g in range(SSD_GROUPS)]
    bt = [bm[g].T.astype(BF) for g in range(SSD_GROUPS)]
    heads_per_group = SSD_HEADS // SSD_GROUPS
    gi = SSD_INNER // SSD_GROUPS

    def per_head_lanes(v):
        hi = v.astype(BF)
        lo = (v - hi.astype(F32)).astype(BF)
        return _dot(hi, e_ref[...]) + _dot(lo, e_ref[...])

    dt_x = per_head_lanes(dt)
    ea_x = per_head_lanes(ea)
    de_x = per_head_lanes(decay_end)
    cd_x = per_head_lanes(jnp.broadcast_to(chunk_decay, (SUBLANES, LANES)))[0:1]
    xdt = xs * dt_x
    xdt_b = xdt.astype(BF)
    xd_b = (xdt * de_x).astype(BF)
    y_off = []
    for g in range(SSD_GROUPS):
        h_old = ht_ref[bb, :, g * gi:(g + 1) * gi]
        y_off.append(_dot(cmb[g], h_old.astype(BF)))
        ht_ref[bb, :, g * gi:(g + 1) * gi] = (h_old * cd_x[:, g * gi:(g + 1) * gi]
                                              + _dot(bt[g], xd_b[:, g * gi:(g + 1) * gi]))

    def decay_weights(r):
        seg = a_cum[:, r:r + 1] - a_cum_t[r:r + 1, :]
        lmat = jnp.where(causal, jnp.exp(jnp.minimum(seg, 0.0)), 0.0)
        return (cb[r // heads_per_group] * lmat).astype(BF)

    first_half = lax.broadcasted_iota(jnp.int32, (q, LANES), 1) < SSD_HEAD_DIM
    y_diag = []
    for k in range(SSD_HEADS // 2):
        xp = xdt_b[:, k * LANES:(k + 1) * LANES]
        y_diag.append(jnp.where(first_half, _dot(decay_weights(2 * k), xp), _dot(decay_weights(2 * k + 1), xp)))
    y = jnp.concatenate(y_diag, axis=1) + jnp.concatenate(y_off, axis=1) * ea_x + dsk_ref[...] * xs
    y = y * _silu(z)
    gi = SSD_INNER // SSD_GROUPS
    yn = [_rms(y[:, g * gi:(g + 1) * gi], sg_ref[:, g * gi:(g + 1) * gi]) for g in range(SSD_GROUPS)]
    ymix_ref[bb, :, 0:SSD_INNER] = jnp.concatenate(yn, axis=1)[:lb].astype(BF)

    uvg = _gelu_tanh(uv)
    u = uvg[:, :GMLP_WIDTH]
    v = uvg[:, GMLP_WIDTH:]
    mu = jnp.mean(v, axis=-1, keepdims=True)
    vc = v - mu
    vn = vc * lax.rsqrt(jnp.mean(vc * vc, axis=-1, keepdims=True) + EPS) * lng_ref[...] + lnb_ref[...]
    if v_ref is not None:
        v_ref[bb] = vn[:lb]
    gd = GMLP_WIDTH // GMLP_GROUPS
    yb = []
    for g in range(GMLP_GROUPS):
        wt = jnp.where(causal, ws_ref[g, :q, :q], 0.0).astype(BF)
        sp = _dot(wt, vn[:, g * gd:(g + 1) * gd].astype(BF)) + bst_ref[:q, g:g + 1]
        yb.append(u[:, g * gd:(g + 1) * gd] * sp)
    ymix_ref[bb, :, SSD_INNER:SSD_INNER + GMLP_WIDTH] = jnp.concatenate(yb, axis=1)[:lb].astype(BF)


EVEN_INPUTS = 15


def _even_entry(*refs, has_v, has_prev, **kw):
    ins = refs[:EVEN_INPUTS]
    k = EVEN_INPUTS + (1 if has_prev else 0)
    outs = refs[k:k + 3]
    k += 3
    v_ref = refs[k] if has_v else None
    k += 1 if has_v else 0
    _even_kernel(*ins, *outs, v_ref, *refs[k:], **kw)


def _even_mixer(proj, dtp, cinit8, sinit, prm, *, q, lb, lc, want_v, layer, state_layers=1, state_prev=None):
    b, lp, _ = proj.shape
    nchunks = lp // lb
    nbb = EVEN_SEQS_PER_STEP if b % EVEN_SEQS_PER_STEP == 0 else 1
    par = [prm["conv_w"], prm["conv_b"], prm["dt_bias"], prm["a_log"], prm["d_skip"], prm["ssd_gain"],
           prm["ln_g"], prm["ln_b"], prm["ws"], prm["bst"], prm["expand"]]
    in_specs = [
        pl.BlockSpec((nbb, lb, ZXU), lambda i, c: (i, c, 0)),
        pl.BlockSpec((nbb, lb, LANES), lambda i, c: (i, c, 0)),
        pl.BlockSpec((nbb, SUBLANES, CONV_DIM), lambda i, c: (i, 0, 0)),
        pl.BlockSpec((1, nbb, SSD_INNER, SSD_STATE), lambda i, c: (layer, i, 0, 0)),
    ] + [_full(w.shape) for w in par]
    out_specs = [
        pl.BlockSpec((nbb, lb, SSD_INNER + GMLP_WIDTH), lambda i, c: (i, c, 0)),
        pl.BlockSpec((nbb, SSD_CONV - 1, CONV_DIM), lambda i, c: (i, 0, 0)),
        (pl.BlockSpec((1, nbb, SSD_INNER, SSD_STATE), lambda i, c: (layer, i, 0, 0))
         if state_prev is not None else
         pl.BlockSpec((state_layers, nbb, SSD_INNER, SSD_STATE), lambda i, c: (0, i, 0, 0))),
    ]
    out_slot = 0 if state_prev is not None else (layer if state_layers > 1 else 0)
    out_shape = [
        jax.ShapeDtypeStruct((b, lp, SSD_INNER + GMLP_WIDTH), BF),
        jax.ShapeDtypeStruct((b, SSD_CONV - 1, CONV_DIM), F32),
        jax.ShapeDtypeStruct((state_layers, b, SSD_INNER, SSD_STATE), F32),
    ]
    if want_v:
        out_specs.append(pl.BlockSpec((nbb, lb, GMLP_WIDTH), lambda i, c: (i, c, 0)))
        out_shape.append(jax.ShapeDtypeStruct((b, lp, GMLP_WIDTH), F32))
    args = [proj, dtp, cinit8, sinit, *par]
    aliases = {}
    if state_prev is not None:
        in_specs.append(pl.BlockSpec(memory_space=pl.ANY))
        aliases = {len(args): 2}
        args.append(state_prev)
    return pl.pallas_call(
        functools.partial(_even_entry, has_v=want_v, has_prev=state_prev is not None,
                          q=q, lb=lb, lc=lc, nbb=nbb, out_slot=out_slot),
        grid=(b // nbb, nchunks),
        in_specs=in_specs,
        out_specs=out_specs,
        out_shape=out_shape,
        input_output_aliases=aliases,
        scratch_shapes=[pltpu.VMEM((nbb, q + 2 * SUBLANES, CONV_DIM), F32),
                        pltpu.VMEM((nbb, SSD_STATE, SSD_INNER), F32)],
        compiler_params=_cp(("parallel", "arbitrary"), vmem_mb=48),
        name="even_mixer",
    )(*args)


def _mla_proj_kernel(x_ref, g_ref, wdq_ref, wdkv_ref, wdk2_ref, qg_ref, kvg_ref, wqn_ref, wqp_ref,
                     wuk_ref, cos_ref, sin_ref, ckv_ref, kpe_ref, kcat_ref, q_ref):
    xn = _rms(x_ref[...], g_ref[...]).astype(BF)
    cqn = _rms(_dot(xn, wdq_ref[...]), qg_ref[...]).astype(BF)
    ckv = _rms(_dot(xn, wdkv_ref[...]), kvg_ref[...])
    kk = _dot(xn, wdk2_ref[...])
    cos = cos_ref[...]
    sin = sin_ref[...]
    kpe = kk[:, :ROPE] * cos[:, :ROPE] + kk[:, ROPE:] * sin[:, :ROPE]
    ckv_ref[...] = ckv
    kpe_ref[...] = kpe
    kcat_ref[:, :KV_RANK] = ckv.astype(BF)
    kcat_ref[:, KV_RANK:] = kpe.astype(BF)
    qn = _dot(cqn, wqn_ref[...])
    qp = _dot(cqn, wqp_ref[...])
    hr = MLA_HEADS * ROPE
    qpe = qp[:, :hr] * cos + qp[:, hr:] * sin
    for h in range(MLA_HEADS):
        ql = _dot(qn[:, h * NOPE:(h + 1) * NOPE].astype(BF), wuk_ref[h])
        q_ref[0, h, :, :KV_RANK] = (ql * MLA_SCALE).astype(BF)
        q_ref[0, h, :, KV_RANK:] = (qpe[:, h * ROPE:(h + 1) * ROPE] * MLA_SCALE).astype(BF)


def _mla_proj(x, g, prm, cos8, sin8, *, nb):
    t = x.shape[0]
    seq = t // nb
    tm = min(ROW_TILE, seq)
    tpb = seq // tm
    w = [prm["wdq"], prm["wdkv"], prm["wdk2"], prm["q_gain"], prm["kv_gain"], prm["wqn"], prm["wqp"],
         prm["wuk"]]
    return pl.pallas_call(
        _mla_proj_kernel,
        grid=(t // tm,),
        in_specs=[pl.BlockSpec((tm, D), lambda i: (i, 0)), _full((1, D))] + [_full(a.shape) for a in w] + [
            pl.BlockSpec((tm, MLA_HEADS * ROPE), lambda i: (i % tpb, 0)),
            pl.BlockSpec((tm, MLA_HEADS * ROPE), lambda i: (i % tpb, 0)),
        ],
        out_specs=[
            pl.BlockSpec((tm, KV_RANK), lambda i: (i, 0)),
            pl.BlockSpec((tm, ROPE), lambda i: (i, 0)),
            pl.BlockSpec((tm, QK), lambda i: (i, 0)),
            pl.BlockSpec((1, MLA_HEADS, tm, QK), lambda i: (i // tpb, 0, i % tpb, 0)),
        ],
        out_shape=[
            jax.ShapeDtypeStruct((t, KV_RANK), F32),
            jax.ShapeDtypeStruct((t, ROPE), F32),
            jax.ShapeDtypeStruct((t, QK), BF),
            jax.ShapeDtypeStruct((nb, MLA_HEADS, seq, QK), BF),
        ],
        compiler_params=_cp(("parallel",)),
        name="mla_proj",
    )(x, g, *w, cos8, sin8)


HEAD_QK = NOPE + ROPE
LOG2E = 1.4426950408889634


def _mla_proj_prompt_kernel(x_ref, g_ref, wdq_ref, wdkv_ref, wdk2_ref, qg_ref, kvg_ref, wqn_ref, wqp_ref,
                            wkn_ref, wv_ref, cos_ref, sin_ref, ckv_ref, kpe_ref, q_ref, k_ref, v_ref):
    xn = _rms(x_ref[...], g_ref[...]).astype(BF)
    cqn = _rms(_dot(xn, wdq_ref[...]), qg_ref[...]).astype(BF)
    ckv = _rms(_dot(xn, wdkv_ref[...]), kvg_ref[...])
    kk = _dot(xn, wdk2_ref[...])
    cos = cos_ref[...]
    sin = sin_ref[...]
    kpe = kk[:, :ROPE] * cos[:, :ROPE] + kk[:, ROPE:] * sin[:, :ROPE]
    ckv_ref[...] = ckv
    kpe_ref[...] = kpe
    ckv_b = ckv.astype(BF)
    kpe_b = kpe.astype(BF)
    kn = _dot(ckv_b, wkn_ref[...])
    vt = _dot_nt(wv_ref[...], ckv_b)
    qn = _dot(cqn, wqn_ref[...])
    qp = _dot(cqn, wqp_ref[...])
    hr = MLA_HEADS * ROPE
    qpe = qp[:, :hr] * cos + qp[:, hr:] * sin
    qscale = MLA_SCALE * LOG2E
    for h in range(MLA_HEADS):
        q_ref[0, h, :, :NOPE] = (qn[:, h * NOPE:(h + 1) * NOPE] * qscale).astype(BF)
        q_ref[0, h, :, NOPE:] = (qpe[:, h * ROPE:(h + 1) * ROPE] * qscale).astype(BF)
        k_ref[0, h, :, :NOPE] = kn[:, h * NOPE:(h + 1) * NOPE].astype(BF)
        k_ref[0, h, :, NOPE:] = kpe_b
        v_ref[0, h, 0] = vt[h * MLA_V:(h + 1) * MLA_V].astype(BF)


def _mla_proj_prompt(x, g, prm, cos8, sin8, *, nb):
    t = x.shape[0]
    seq = t // nb
    tm = min(ROW_TILE, seq)
    tpb = seq // tm
    w = [prm["wdq"], prm["wdkv"], prm["wdk2"], prm["q_gain"], prm["kv_gain"], prm["wqn"], prm["wqp"],
         prm["wkn"], prm["wv"]]
    head_spec = lambda width: pl.BlockSpec((1, MLA_HEADS, tm, width), lambda i: (i // tpb, 0, i % tpb, 0))
    head_shape = lambda width: jax.ShapeDtypeStruct((nb, MLA_HEADS, seq, width), BF)
    return pl.pallas_call(
        _mla_proj_prompt_kernel,
        grid=(t // tm,),
        in_specs=[pl.BlockSpec((tm, D), lambda i: (i, 0)), _full((1, D))] + [_full(a.shape) for a in w] + [
            pl.BlockSpec((tm, MLA_HEADS * ROPE), lambda i: (i % tpb, 0)),
            pl.BlockSpec((tm, MLA_HEADS * ROPE), lambda i: (i % tpb, 0)),
        ],
        out_specs=[
            pl.BlockSpec((tm, KV_RANK), lambda i: (i, 0)),
            pl.BlockSpec((tm, ROPE), lambda i: (i, 0)),
            head_spec(HEAD_QK), head_spec(HEAD_QK),
            pl.BlockSpec((1, MLA_HEADS, 1, MLA_V, tm), lambda i: (i // tpb, 0, i % tpb, 0, 0)),
        ],
        out_shape=[
            jax.ShapeDtypeStruct((t, KV_RANK), F32),
            jax.ShapeDtypeStruct((t, ROPE), F32),
            head_shape(HEAD_QK), head_shape(HEAD_QK),
            jax.ShapeDtypeStruct((nb, MLA_HEADS, tpb, MLA_V, tm), BF),
        ],
        compiler_params=_cp(("parallel",)),
        name="mla_proj_prompt",
    )(x, g, *w, cos8, sin8)


FLASH_HEADS = 4
FLASH_KBLOCKS = 4
DENOM_ROWS = 16


def _flash_kernel(q_ref, k_ref, vt_ref, o_ref, m_ref, acc_ref, *, t):
    qi = pl.program_id(2)
    for hh in range(FLASH_HEADS):
        m_ref[hh] = jnp.full((1, t), NEG, F32)
        acc_ref[hh] = jnp.zeros((MLA_V + DENOM_ROWS, t), F32)
    ones = jnp.ones((DENOM_ROWS, t), BF)

    def block(ki, nblk, masked):
        start = pl.multiple_of(ki * t, t)
        ones_n = jnp.concatenate([ones] * nblk, axis=1)
        for hh in range(FLASH_HEADS):
            kb = k_ref[0, hh, pl.ds(start, nblk * t), :]
            vt = jnp.concatenate([vt_ref[0, hh, ki + n] for n in range(nblk)], axis=1)
            v1 = jnp.concatenate([vt, ones_n], axis=0)
            st = _dot_nt(kb, q_ref[0, hh])
            if masked:
                key = lax.broadcasted_iota(jnp.int32, (t, t), 0)
                qry = lax.broadcasted_iota(jnp.int32, (t, t), 1)
                st = jnp.where(key <= qry, st, NEG)
            m_old = m_ref[hh]
            m_new = jnp.maximum(m_old, jnp.max(st, axis=0, keepdims=True))
            alpha = jnp.exp2(m_old - m_new)
            pt = jnp.exp2(st - m_new).astype(BF)
            acc_ref[hh] = alpha * acc_ref[hh] + _dot(v1, pt)
            m_ref[hh] = m_new

    def body(kk, carry):
        block(kk * FLASH_KBLOCKS, FLASH_KBLOCKS, False)
        return carry

    lax.fori_loop(0, qi // FLASH_KBLOCKS, body, 0)
    rem = qi % FLASH_KBLOCKS
    base = qi - rem
    size = FLASH_KBLOCKS // 2
    while size >= 1:
        @pl.when(rem & size != 0)
        def _(size=size):
            block(base + (rem & ~(2 * size - 1)), size, False)
        size //= 2
    block(qi, 1, True)
    for hh in range(FLASH_HEADS):
        acc = acc_ref[hh]
        o_t = acc[:MLA_V] / acc[MLA_V:MLA_V + 1]
        o_ref[0, :, hh * MLA_V:(hh + 1) * MLA_V] = o_t.T.astype(BF)


def _flash(q, k, vt):
    nb, _, seq, _ = q.shape
    t = vt.shape[-1]
    nh = FLASH_HEADS
    return pl.pallas_call(
        functools.partial(_flash_kernel, t=t),
        grid=(nb, MLA_HEADS // nh, seq // t),
        in_specs=[
            pl.BlockSpec((1, nh, t, HEAD_QK), lambda b, h, i: (b, h, i, 0)),
            pl.BlockSpec((1, nh, seq, HEAD_QK), lambda b, h, i: (b, h, 0, 0), pipeline_mode=pl.Buffered(1)),
            pl.BlockSpec((1, nh, seq // t, MLA_V, t), lambda b, h, i: (b, h, 0, 0, 0),
                         pipeline_mode=pl.Buffered(1)),
        ],
        out_specs=pl.BlockSpec((1, t, nh * MLA_V), lambda b, h, i: (b, i, h)),
        out_shape=jax.ShapeDtypeStruct((nb, seq, MLA_HEADS * MLA_V), BF),
        scratch_shapes=[pltpu.VMEM((nh, 1, t), F32), pltpu.VMEM((nh, MLA_V + DENOM_ROWS, t), F32)],
        compiler_params=_cp(("parallel", "parallel", "arbitrary"), vmem_mb=48),
        name="mla_flash",
    )(q, k, vt)


def _page_copies(pt_ref, ckv_hbm, kpt_hbm, ckbuf, kpbuf, sem, sample, slot, *, layer, n_pages):
    copies = []
    for p in range(n_pages):
        pg = pt_ref[sample, p]
        copies.append(pltpu.make_async_copy(
            ckv_hbm.at[layer, pg], ckbuf.at[slot, pl.ds(p * PAGE, PAGE), :], sem.at[0, slot]))
        copies.append(pltpu.make_async_copy(
            kpt_hbm.at[layer, pg], kpbuf.at[slot, :, pl.ds(p * PAGE, PAGE)], sem.at[1, slot]))
    return copies


def _start_split(copies):
    for n, c in enumerate(copies):
        c.start(priority=(n // 2) % 2)


def _wait_grouped(copies):
    for c in copies[0::2]:
        c.wait()
    for c in copies[1::2]:
        c.wait()


def _decode_kernel(pt_ref, q_ref, nk_ref, ckv_hbm, kpt_hbm, o_ref, ckbuf, kpbuf, sem, *,
                   dec_seq, layer, n_pages):
    b = pl.program_id(0)
    last = pl.num_programs(0) - 1
    slot = b % 2
    copies = functools.partial(_page_copies, pt_ref, ckv_hbm, kpt_hbm, ckbuf, kpbuf, sem,
                               layer=layer, n_pages=n_pages)

    @pl.when(b == 0)
    def _():
        _start_split(copies(0, 0))

    _wait_grouped(copies(b, slot))
    nxt = jnp.minimum(b + 1, last)
    _start_split(copies(nxt, 1 - slot))

    rows = MLA_HEADS * dec_seq
    qm = q_ref[0]
    ql = qm[:, :KV_RANK]
    qp = qm[:, KV_RANK:]
    n_slices = DECODE_SLICES if n_pages % DECODE_SLICES == 0 else 1
    width = n_pages * PAGE // n_slices
    cks, scores = [], []
    for c in range(n_slices):
        ck = ckbuf[slot, c * width:(c + 1) * width, :].astype(BF)
        kp = kpbuf[slot, :, c * width:(c + 1) * width].astype(BF)
        cks.append(ck)
        scores.append(_dot_nt(ql, ck) + _dot(qp, kp))
    nk = nk_ref[0]
    kt = lax.broadcasted_iota(jnp.int32, (rows, NEW_KEY_ROWS), 1)
    qt = lax.broadcasted_iota(jnp.int32, (rows, NEW_KEY_ROWS), 0) % dec_seq
    s_new = jnp.where(kt <= qt, _dot_nt(qm, nk), NEG)
    m = jnp.max(s_new, axis=-1, keepdims=True)
    for s in scores:
        m = jnp.maximum(m, jnp.max(s, axis=-1, keepdims=True))
    p_new = jnp.exp(s_new - m)
    denom = jnp.sum(p_new, axis=-1, keepdims=True)
    acc = _dot(p_new.astype(BF), nk[:, :KV_RANK])
    for s, ck in zip(scores, cks):
        p = jnp.exp(s - m)
        denom = denom + jnp.sum(p, axis=-1, keepdims=True)
        acc = acc + _dot(p.astype(BF), ck)
    o_ref[0] = acc / denom

    @pl.when(b == last)
    def _():
        _wait_grouped(copies(nxt, 1 - slot))


def _decode(page_table, q, newk, cache_ckv, cache_kpe_t, layer, dec_seq):
    nb, n_pages = page_table.shape
    rows = MLA_HEADS * dec_seq
    keys = n_pages * PAGE
    grid_spec = pltpu.PrefetchScalarGridSpec(
        num_scalar_prefetch=1,
        grid=(nb,),
        in_specs=[
            pl.BlockSpec((1, rows, QK), lambda b, pt: (b, 0, 0)),
            pl.BlockSpec((1, NEW_KEY_ROWS, QK), lambda b, pt: (b, 0, 0)),
            pl.BlockSpec(memory_space=pl.ANY),
            pl.BlockSpec(memory_space=pl.ANY),
        ],
        out_specs=pl.BlockSpec((1, rows, KV_RANK), lambda b, pt: (b, 0, 0)),
        scratch_shapes=[pltpu.VMEM((2, keys, KV_RANK), F32), pltpu.VMEM((2, ROPE, keys), F32),
                        pltpu.SemaphoreType.DMA((2, 2))],
    )
    return pl.pallas_call(
        functools.partial(_decode_kernel, dec_seq=dec_seq, layer=layer, n_pages=n_pages),
        grid_spec=grid_spec,
        out_shape=jax.ShapeDtypeStruct((nb, rows, KV_RANK), F32),
        compiler_params=_cp(("arbitrary",), vmem_mb=48),
        name="mla_decode",
    )(page_table, q, newk, cache_ckv, cache_kpe_t)


def _mla_out_kernel(o_ref, wuv_ref, wo_ref, r_ref, out_ref):
    parts = [_dot(o_ref[0, h], wuv_ref[h]).astype(BF) for h in range(MLA_HEADS)]
    out_ref[...] = r_ref[...] + _dot(jnp.concatenate(parts, axis=1), wo_ref[...])


def _mla_out(o_lat, wuv, wo, res):
    nb, _, seq, _ = o_lat.shape
    tm = min(ROW_TILE, seq)
    tpb = seq // tm
    t = nb * seq
    return pl.pallas_call(
        _mla_out_kernel,
        grid=(t // tm,),
        in_specs=[
            pl.BlockSpec((1, MLA_HEADS, tm, KV_RANK), lambda i: (i // tpb, 0, i % tpb, 0)),
            _full(wuv.shape), _full(wo.shape),
            pl.BlockSpec((tm, D), lambda i: (i, 0)),
        ],
        out_specs=pl.BlockSpec((tm, D), lambda i: (i, 0)),
        out_shape=jax.ShapeDtypeStruct((t, D), F32),
        compiler_params=_cp(("parallel",)),
        name="mla_out",
    )(o_lat, wuv, wo, res)


def _softmax_rows(s):
    m = jnp.max(s, axis=-1, keepdims=True)
    p = jnp.exp(s - m)
    return p / jnp.sum(p, axis=-1, keepdims=True)


def _mem_prompt_kernel(x_ref, g_ref, wq_ref, k_ref, v_ref, wo_ref, o_ref):
    x = x_ref[...]
    xn = _rms(x, g_ref[...]).astype(BF)
    qm = (_dot(xn, wq_ref[...]) * MEM_HD ** -0.5).astype(BF)
    km = k_ref[0].astype(BF)
    vm = v_ref[0].astype(BF)
    parts = []
    for h in range(MEM_HEADS):
        sl = slice(h * MEM_HD, (h + 1) * MEM_HD)
        p = _softmax_rows(_dot_nt(qm[:, sl], km[:, sl]))
        parts.append(_dot(p.astype(BF), vm[:, sl]).astype(BF))
    o_ref[...] = x + _dot(jnp.concatenate(parts, axis=1), wo_ref[...])


def _mem_prompt(x, g, wq, km, vm, wo):
    t = x.shape[0]
    nb, mt, _ = km.shape
    seq = t // nb
    tm = min(ROW_TILE, seq)
    tpb = seq // tm
    return pl.pallas_call(
        _mem_prompt_kernel,
        grid=(t // tm,),
        in_specs=[
            pl.BlockSpec((tm, D), lambda i: (i, 0)), _full((1, D)), _full(wq.shape),
            pl.BlockSpec((1, mt, MEM_INNER), lambda i: (i // tpb, 0, 0)),
            pl.BlockSpec((1, mt, MEM_INNER), lambda i: (i // tpb, 0, 0)),
            _full(wo.shape),
        ],
        out_specs=pl.BlockSpec((tm, D), lambda i: (i, 0)),
        out_shape=jax.ShapeDtypeStruct((t, D), F32),
        compiler_params=_cp(("parallel",)),
        name="mem_attn_prompt",
    )(x, g, wq, km, vm, wo)


MEM_ROWS = MEM_HEADS * SUBLANES


def _mem_sample_kernel(x_ref, g_ref, wq_ref, k_ref, v_ref, wo_ref, o_ref):
    ns = SAMPLES_PER_STEP
    x = x_ref[...].reshape(ns * SUBLANES, D)
    xn = _rms(x, g_ref[...]).astype(BF)
    qall = _dot(xn, wq_ref[...]) * MEM_HD ** -0.5
    cols = k_ref.shape[2]
    head_of_row = lax.broadcasted_iota(jnp.int32, (MEM_ROWS, cols), 0) // SUBLANES
    head_of_col = lax.broadcasted_iota(jnp.int32, (MEM_ROWS, cols), 1) % MEM_HEADS
    own = head_of_row == head_of_col
    outs = []
    for s in range(ns):
        qs = qall[s * SUBLANES:(s + 1) * SUBLANES]
        qst = jnp.concatenate([qs[:, h * MEM_HD:(h + 1) * MEM_HD] for h in range(MEM_HEADS)], axis=0)
        sc = jnp.where(own, _dot_nt(qst.astype(BF), k_ref[0, s].astype(BF)), NEG)
        o = _dot(_softmax_rows(sc).astype(BF), v_ref[0, s].astype(BF))
        outs.append(jnp.concatenate([o[h * SUBLANES:(h + 1) * SUBLANES] for h in range(MEM_HEADS)], axis=1))
    out = x + _dot(jnp.concatenate(outs, axis=0).astype(BF), wo_ref[...])
    o_ref[...] = out.reshape(ns, SUBLANES, D)


def _mem_sample(x8, g, wq, cache_k, cache_v, wo, layer):
    nb = x8.shape[0]
    rows = cache_k.shape[2]
    ns = SAMPLES_PER_STEP
    return pl.pallas_call(
        _mem_sample_kernel,
        grid=(nb // ns,),
        in_specs=[
            pl.BlockSpec((ns, SUBLANES, D), lambda i: (i, 0, 0)), _full((1, D)), _full(wq.shape),
            pl.BlockSpec((1, ns, rows, MEM_HD), lambda i: (layer, i, 0, 0)),
            pl.BlockSpec((1, ns, rows, MEM_HD), lambda i: (layer, i, 0, 0)),
            _full(wo.shape),
        ],
        out_specs=pl.BlockSpec((ns, SUBLANES, D), lambda i: (i, 0, 0)),
        out_shape=jax.ShapeDtypeStruct((nb, SUBLANES, D), F32),
        compiler_params=_cp(("parallel",), vmem_mb=48),
        name="mem_attn_sample",
    )(x8, g, wq, cache_k, cache_v, wo)


def _router_kernel(x_ref, g_ref, wr_ref, xn_ref, route_ref):
    xn = _rms(x_ref[...], g_ref[...])
    xn_ref[...] = xn
    lane = lax.broadcasted_iota(jnp.int32, (xn.shape[0], LANES), 1).astype(F32)
    lg = jnp.where(lane < N_EXP, _dot_split(xn, wr_ref[...]), NEG)
    m1 = jnp.max(lg, axis=-1, keepdims=True)
    i1 = jnp.min(jnp.where(lg == m1, lane, float(LANES)), axis=-1, keepdims=True)
    lg2 = jnp.where(lane == i1, NEG, lg)
    m2 = jnp.max(lg2, axis=-1, keepdims=True)
    i2 = jnp.min(jnp.where(lg2 == m2, lane, float(LANES)), axis=-1, keepdims=True)
    e = jnp.exp(m2 - m1)
    g1 = 1.0 / (1.0 + e)
    g2 = e * g1
    route_ref[...] = jnp.where(lane == 0, i1, jnp.where(lane == 1, i2, jnp.where(lane == 2, g1,
                               jnp.where(lane == 3, g2, 0.0))))


def _router(x, g, wr_pad):
    t = x.shape[0]
    tm = min(ROW_TILE, t)
    return pl.pallas_call(
        _router_kernel,
        grid=(t // tm,),
        in_specs=[pl.BlockSpec((tm, D), lambda i: (i, 0)), _full((1, D)), _full(wr_pad.shape)],
        out_specs=[pl.BlockSpec((tm, D), lambda i: (i, 0)), pl.BlockSpec((tm, LANES), lambda i: (i, 0))],
        out_shape=[jax.ShapeDtypeStruct((t, D), F32), jax.ShapeDtypeStruct((t, LANES), F32)],
        compiler_params=_cp(("parallel",)),
        name="router",
    )(x, g, wr_pad)


SC_ROWS = 32


def _sc_mesh():
    return plsc.VectorSubcoreMesh(core_axis_name="c", subcore_axis_name="s")


def _sc_workers():
    info = plsc.get_sparse_core_info()
    return info.num_cores, info.num_cores * info.num_subcores


def _sc_gather_rows(table, idx):
    n = idx.shape[0]
    width = table.shape[1]
    ncores, nw = _sc_workers()
    per_w = n // nw
    n_chunks = per_w // SC_ROWS
    assert per_w * nw == n and n_chunks * SC_ROWS == per_w

    @functools.partial(
        pl.kernel, mesh=_sc_mesh(),
        out_type=jax.ShapeDtypeStruct((n, width), table.dtype),
        scratch_types=[pltpu.VMEM((per_w,), jnp.int32), pltpu.VMEM((SC_ROWS, width), table.dtype),
                       pltpu.SemaphoreType.DMA],
        name="sc_gather_rows",
    )
    def body(table_hbm, idx_hbm, out_hbm, idx_v, rows_v, sem):
        wid = lax.axis_index("s") * ncores + lax.axis_index("c")
        base = wid * per_w
        pltpu.sync_copy(idx_hbm.at[pl.ds(base, per_w)], idx_v)

        @pl.loop(0, n_chunks)
        def _(j):
            off = pl.multiple_of(j * SC_ROWS, SC_ROWS)
            pltpu.async_copy(table_hbm.at[idx_v.at[pl.ds(off, SC_ROWS)]], rows_v, sem).wait()
            pltpu.sync_copy(rows_v, out_hbm.at[pl.ds(base + off, SC_ROWS)])

    return body(table, idx)


def _sc_scatter_rows(src, idx, n_out):
    n = idx.shape[0]
    t, width = src.shape
    ncores, nw = _sc_workers()
    per_w = n // nw
    n_chunks = per_w // SC_ROWS
    assert per_w * nw == n and n_chunks * SC_ROWS == per_w and t % per_w == 0
    idx3 = idx.reshape(nw, n_chunks, SC_ROWS)

    @functools.partial(
        pl.kernel, mesh=_sc_mesh(),
        out_type=jax.ShapeDtypeStruct((n_out, width), src.dtype),
        scratch_types=[pltpu.VMEM((n_chunks, SC_ROWS), jnp.int32), pltpu.VMEM((SC_ROWS, width), src.dtype),
                       pltpu.SemaphoreType.DMA],
        name="sc_scatter_rows",
    )
    def body(src_hbm, idx_hbm, out_hbm, idx_v, rows_v, sem):
        wid = lax.axis_index("s") * ncores + lax.axis_index("c")
        base = lax.rem(wid * per_w, t)
        pltpu.sync_copy(idx_hbm.at[wid], idx_v)

        @pl.loop(0, n_chunks)
        def _(j):
            off = pl.multiple_of(j * SC_ROWS, SC_ROWS)
            pltpu.sync_copy(src_hbm.at[pl.ds(base + off, SC_ROWS)], rows_v)
            pltpu.async_copy(rows_v, out_hbm.at[idx_v.at[j]], sem).wait()

    return body(src, idx3)


MOE_TILE = 1024
MOE_TILE_SMALL = 256
MOE_BLOCK = 512
MOE_CHUNK = 256


def _moe_ffn_kernel(te_ref, nused_ref, x_ref, wg_ref, wu_ref, wd_ref, o_ref, xb_ref):
    del te_ref
    i = pl.program_id(0)
    j = pl.program_id(1)

    @pl.when(i < nused_ref[0])
    def _():
        @pl.when(j == 0)
        def _():
            xb_ref[...] = x_ref[...].astype(BF)
            o_ref[...] = jnp.zeros_like(o_ref)

        xb = xb_ref[...]
        part = None
        block = wd_ref.shape[2]
        for lo in range(0, block, MOE_CHUNK):
            sl = slice(lo, min(lo + MOE_CHUNK, block))
            gate = _dot(xb, wg_ref[0, 0, :, sl].astype(BF))
            up = _dot(xb, wu_ref[0, 0, :, sl].astype(BF))
            contrib = _dot((_silu(gate) * up).astype(BF), wd_ref[0, 0, sl, :].astype(BF))
            part = contrib if part is None else part + contrib
        o_ref[...] += part

    @pl.when(i >= nused_ref[0])
    def _():
        o_ref[...] = jnp.zeros_like(o_ref)


def _moe_ffn(tile_expert, nused, xs, wgu, wd, layer, *, tm):
    npad = xs.shape[0]
    block = MOE_BLOCK
    nblk = D_FFE // block

    def blk(i, j, nu):
        return jnp.where(i < nu[0], j, nblk - 1)

    grid_spec = pltpu.PrefetchScalarGridSpec(
        num_scalar_prefetch=2,
        grid=(npad // tm, nblk),
        in_specs=[
            pl.BlockSpec((tm, D), lambda i, j, te, nu: (jnp.minimum(i, nu[0] - 1), 0)),
            pl.BlockSpec((1, 1, D, block), lambda i, j, te, nu: (layer, te[i], 0, blk(i, j, nu))),
            pl.BlockSpec((1, 1, D, block), lambda i, j, te, nu: (layer, te[i], 0, nblk + blk(i, j, nu))),
            pl.BlockSpec((1, 1, block, D), lambda i, j, te, nu: (layer, te[i], blk(i, j, nu), 0)),
        ],
        out_specs=pl.BlockSpec((tm, D), lambda i, j, te, nu: (i, 0)),
        scratch_shapes=[pltpu.VMEM((tm, D), BF)],
    )
    return pl.pallas_call(
        _moe_ffn_kernel,
        grid_spec=grid_spec,
        out_shape=jax.ShapeDtypeStruct((npad, D), F32),
        compiler_params=_cp(("parallel", "arbitrary"), vmem_mb=56),
        name="moe_ffn",
    )(tile_expert, nused, xs, wgu, wgu, wd)


ROUTE_GATE_LANE = 2


def _combine_kernel(h_ref, y0_ref, y1_ref, route_ref, g_ref, o_ref, *, final):
    gl = ROUTE_GATE_LANE
    route = route_ref[...]
    out = h_ref[...] + route[:, gl:gl + 1] * y0_ref[...] + route[:, gl + 1:gl + 2] * y1_ref[...]
    if final:
        out = _rms(out, g_ref[...])
    o_ref[...] = out


def _combine(h, y2, route, g, *, final):
    t = h.shape[0]
    tm = min(ROW_TILE, t)
    nt = t // tm
    return pl.pallas_call(
        functools.partial(_combine_kernel, final=final),
        grid=(nt,),
        in_specs=[
            pl.BlockSpec((tm, D), lambda i: (i, 0)),
            pl.BlockSpec((tm, D), lambda i: (i, 0)),
            pl.BlockSpec((tm, D), lambda i: (nt + i, 0)),
            pl.BlockSpec((tm, LANES), lambda i: (i, 0)),
            _full((1, D)),
        ],
        out_specs=pl.BlockSpec((tm, D), lambda i: (i, 0)),
        out_shape=jax.ShapeDtypeStruct((t, D), F32),
        compiler_params=_cp(("parallel",)),
        name="moe_combine",
    )(h, y2, y2, route, g)


def _moe(h, g, wr_pad, wgu, wd, layer, final_g, *, final):
    t = h.shape[0]
    tm = MOE_TILE if 2 * t >= 2 * N_EXP * MOE_TILE else MOE_TILE_SMALL
    xn, route = _router(h, g, wr_pad)
    eidx = route[:, :ROUTE_GATE_LANE].astype(jnp.int32)
    e_flat = eidx.T.reshape(-1)
    onehot = (e_flat[:, None] == jnp.arange(N_EXP, dtype=jnp.int32)[None, :]).astype(jnp.int32)
    csum = jnp.cumsum(onehot, axis=0)
    counts = csum[-1]
    rank = jnp.sum(onehot * csum, axis=1) - 1
    padded = ((counts + tm - 1) // tm) * tm
    ends = jnp.cumsum(padded)
    starts = ends - padded
    dest = (jnp.sum(onehot * starts[None, :], axis=1) + rank).astype(jnp.int32)
    n_tiles = -(-2 * t // tm) + N_EXP
    tile_start = jnp.arange(n_tiles, dtype=jnp.int32) * tm
    tile_expert = jnp.minimum(jnp.sum((tile_start[:, None] >= ends[None, :]).astype(jnp.int32), axis=1),
                              N_EXP - 1).astype(jnp.int32)
    nused = (ends[-1] // tm).astype(jnp.int32).reshape(1)
    tile_expert = jnp.where(jnp.arange(n_tiles) < nused[0], tile_expert, tile_expert[nused[0] - 1])
    xs = _sc_scatter_rows(xn, dest, n_tiles * tm)
    ys = _moe_ffn(tile_expert, nused, xs, wgu, wd, layer, tm=tm)
    y2 = _sc_gather_rows(ys, dest)
    return _combine(h, y2, route, final_g, final=final)


def _rope_tables(pos):
    half = ROPE // 2
    inv_freq = ROPE_BASE ** (-jnp.arange(half, dtype=F32) / half)
    ang = pos.astype(F32)[:, None] * inv_freq
    cos = jnp.cos(ang)
    sin = jnp.sin(ang)
    cc = jnp.concatenate([cos, cos], axis=-1)
    ss = jnp.concatenate([-sin, sin], axis=-1)
    return jnp.tile(cc, (1, MLA_HEADS)), jnp.tile(ss, (1, MLA_HEADS))


def _even_layer(h, g, prm, cinit, sinit, layer, *, nb, sample, state_layers=1, state_prev=None):
    t = h.shape[0]
    seq = t // nb
    proj, dtp = _even_proj(h, g, prm["w_zxu"], prm["w_dt"])
    proj = proj.reshape(nb, seq, ZXU)
    dtp = dtp.reshape(nb, seq, LANES)
    if sample:
        pad = ((0, 0), (0, SUBLANES - seq), (0, 0))
        proj = jnp.pad(proj, pad)
        dtp = jnp.pad(dtp, pad)
        q, lb, lc = SAMPLE_Q, SUBLANES, seq
    else:
        q, lb, lc = CHUNK, CHUNK, CHUNK
    cinit8 = jnp.pad(cinit, ((0, 0), (SUBLANES - (SSD_CONV - 1), 0), (0, 0)))
    outs = _even_mixer(proj, dtp, cinit8, sinit.reshape(-1, nb, SSD_INNER, SSD_STATE), prm,
                       q=q, lb=lb, lc=lc, want_v=sample, layer=layer,
                       state_layers=state_layers, state_prev=state_prev)
    ymix, cout, sout = outs[:3]
    v = None
    if sample:
        ymix = ymix[:, :seq]
        v = outs[3][:, :seq]
    h = _matmul_res(ymix.reshape(t, SSD_INNER + GMLP_WIDTH), prm["w_out"], h)
    return h, cout, sout, v


def _prep_even(i, w_in, conv_w, conv_b, dt_bias, a_log, d_skip, ssd_gain, ln_g, ln_b, ws, bs, w_out):
    w = w_in[i]
    o1 = SSD_INNER + CONV_DIM
    w_zxu = jnp.concatenate([w[:, :o1], w[:, o1 + SSD_HEADS:]], axis=1).astype(BF)
    w_dt = jnp.pad(w[:, o1:o1 + SSD_HEADS], ((0, 0), (0, LANES - SSD_HEADS)))
    padl = (0, LANES - SSD_HEADS)
    return dict(
        w_zxu=w_zxu, w_dt=w_dt,
        conv_w=jnp.pad(conv_w[i], ((0, SUBLANES - SSD_CONV), (0, 0))),
        conv_b=conv_b[i][None, :],
        dt_bias=jnp.pad(dt_bias[i], padl)[None, :],
        a_log=jnp.pad(a_log[i], padl)[None, :],
        d_skip=jnp.repeat(d_skip[i], SSD_HEAD_DIM)[None, :],
        ssd_gain=ssd_gain[i][None, :],
        ln_g=ln_g[i][None, :], ln_b=ln_b[i][None, :],
        ws=ws[i], bst=bs[i].T,
        expand=(jnp.arange(LANES)[:, None] == jnp.arange(SSD_INNER)[None, :] // SSD_HEAD_DIM).astype(BF),
        w_out=w_out[i].astype(BF),
    )


def _prep_mla(i, w_down, q_gain, kv_gain, w_uq, w_uk, w_uv, w_o):
    wd = w_down[i]
    wk = wd[:, Q_RANK + KV_RANK:]
    half = ROPE // 2
    rot = lambda a: jnp.concatenate([a[..., half:], a[..., :half]], axis=-1)
    uq = w_uq[i]
    uq_pe = uq[:, :, NOPE:]
    return dict(
        wdq=wd[:, :Q_RANK].astype(BF),
        wdkv=wd[:, Q_RANK:Q_RANK + KV_RANK].astype(BF),
        wdk2=jnp.concatenate([wk, rot(wk)], axis=1).astype(BF),
        q_gain=q_gain[i][None, :], kv_gain=kv_gain[i][None, :],
        wqn=uq[:, :, :NOPE].reshape(Q_RANK, MLA_HEADS * NOPE).astype(BF),
        wqp=jnp.concatenate([uq_pe.reshape(Q_RANK, -1), rot(uq_pe).reshape(Q_RANK, -1)], axis=1).astype(BF),
        wuk=jnp.transpose(w_uk[i], (1, 2, 0)).astype(BF),
        wuv=jnp.transpose(w_uv[i], (1, 0, 2)).astype(BF),
        wkn=w_uk[i].reshape(KV_RANK, MLA_HEADS * NOPE).astype(BF),
        wv=w_uv[i].reshape(KV_RANK, MLA_HEADS * MLA_V).T.astype(BF),
        wo=w_o[i].astype(BF),
    )


def kernel(x_prompt, x_sample, state_ssd, state_conv, cache_mla_ckv, cache_mla_kpe, cache_mem_k, cache_mem_v, page_table, mem_prompt, mix_norm, w_in, conv_w, conv_b, dt_bias, a_log, d_skip, ssd_gain, gmlp_ln_g, gmlp_ln_b, gmlp_ws, gmlp_bs, w_out_even, w_mla_down, mla_q_gain, mla_kv_gain, w_mla_uq, w_mla_uk, w_mla_uv, w_mla_o, xattn_norm, mem_norm, w_mem_q, w_mem_k, w_mem_v, w_mem_o, ffn_norm, w_ffn_gu, w_ffn_down, w_router, w_exp_gu, w_exp_down, final_norm):
    nbp, seq, _ = x_prompt.shape
    nbs, dseq, _ = x_sample.shape
    depth = mix_norm.shape[0]
    past = page_table.shape[1] * PAGE
    mt = mem_prompt.shape[1]
    hp = x_prompt.reshape(nbp * seq, D)
    hs = x_sample.reshape(nbs * dseq, D)
    cos_p, sin_p = _rope_tables(jnp.arange(seq, dtype=jnp.int32))
    cos_s, sin_s = _rope_tables(past + jnp.arange(dseq, dtype=jnp.int32))
    cos_s = jnp.tile(cos_s, (nbs, 1))
    sin_s = jnp.tile(sin_s, (nbs, 1))
    cache_k4 = cache_mem_k.reshape(depth, nbs, mt * MEM_HEADS, MEM_HD)
    cache_v4 = cache_mem_v.reshape(depth, nbs, mt * MEM_HEADS, MEM_HD)
    cache_kpe_t = jnp.swapaxes(cache_mla_kpe, 2, 3)
    final_g = final_norm[None, :]

    p_ssd, p_conv, p_ckv, p_kpe, p_mk, p_mv = [], [], [], [], [], []
    s_conv, s_v, s_ckv, s_kpe = [], [], [], []
    n_even = (depth + 1) // 2
    s_state = None
    for l in range(depth):
        i = l // 2
        g_mix = mix_norm[l][None, :]
        if l % 2 == 0:
            prm = _prep_even(i, w_in, conv_w, conv_b, dt_bias, a_log, d_skip, ssd_gain, gmlp_ln_g,
                             gmlp_ln_b, gmlp_ws, gmlp_bs, w_out_even)
            buf0 = jnp.zeros((nbp, SSD_CONV - 1, CONV_DIM), F32)
            h00 = jnp.zeros((1, nbp, SSD_HEADS, SSD_HEAD_DIM, SSD_STATE), F32)
            hp, buf_p, ssd_p, _ = _even_layer(hp, g_mix, prm, buf0, h00, 0, nb=nbp, sample=False)
            hs, buf_s, s_state, v_s = _even_layer(hs, g_mix, prm, state_conv[i], state_ssd, i, nb=nbs,
                                                  sample=True, state_layers=n_even, state_prev=s_state)
            p_ssd.append(ssd_p.reshape(nbp, SSD_HEADS, SSD_HEAD_DIM, SSD_STATE))
            p_conv.append(buf_p)
            s_conv.append(buf_s)
            s_v.append(v_s)
        else:
            prm = _prep_mla(i, w_mla_down, mla_q_gain, mla_kv_gain, w_mla_uq, w_mla_uk, w_mla_uv, w_mla_o)
            ckv, kpe, qh, kh, vh = _mla_proj_prompt(hp, g_mix, prm, cos_p, sin_p, nb=nbp)
            o_p = _flash(qh, kh, vh)
            hp = _matmul_res(o_p.reshape(nbp * seq, MLA_HEADS * MLA_V), prm["wo"], hp)
            p_ckv.append(ckv.reshape(nbp, seq, KV_RANK))
            p_kpe.append(kpe.reshape(nbp, seq, ROPE))

            ckv_s, kpe_s, kcat_s, q_s = _mla_proj(hs, g_mix, prm, cos_s, sin_s, nb=1)
            q_s = q_s[0].reshape(MLA_HEADS, nbs, dseq, QK).transpose(1, 0, 2, 3).reshape(nbs, MLA_HEADS * dseq, QK)
            newk = jnp.pad(kcat_s.reshape(nbs, dseq, QK), ((0, 0), (0, NEW_KEY_ROWS - dseq), (0, 0)))
            o_s = _decode(page_table, q_s, newk, cache_mla_ckv, cache_kpe_t, i, dseq)
            o_s = o_s.reshape(nbs, MLA_HEADS, dseq, KV_RANK).transpose(1, 0, 2, 3)
            o_s = o_s.reshape(1, MLA_HEADS, nbs * dseq, KV_RANK).astype(BF)
            hs = _mla_out(o_s, prm["wuv"], prm["wo"], hs)
            s_ckv.append(ckv_s.reshape(nbs, dseq, KV_RANK))
            s_kpe.append(kpe_s.reshape(nbs, dseq, ROPE))

        wkv = jnp.concatenate([w_mem_k[l], w_mem_v[l]], axis=1).astype(BF)
        kv = _rms_matmul(mem_prompt.reshape(nbp * mt, D), mem_norm[l][None, :], wkv, tn=MEM_INNER)
        mk_p = kv[:, :MEM_INNER].reshape(nbp, mt, MEM_INNER)
        mv_p = kv[:, MEM_INNER:].reshape(nbp, mt, MEM_INNER)
        g_x = xattn_norm[l][None, :]
        wq = w_mem_q[l].astype(BF)
        wo = w_mem_o[l].astype(BF)
        hp = _mem_prompt(hp, g_x, wq, mk_p, mv_p, wo)
        x8 = jnp.pad(hs.reshape(nbs, dseq, D), ((0, 0), (0, SUBLANES - dseq), (0, 0)))
        hs = _mem_sample(x8, g_x, wq, cache_k4, cache_v4, wo, l)[:, :dseq].reshape(nbs * dseq, D)
        p_mk.append(mk_p.reshape(nbp, mt, MEM_HEADS, MEM_HD))
        p_mv.append(mv_p.reshape(nbp, mt, MEM_HEADS, MEM_HD))

        g_f = ffn_norm[l][None, :]
        if l % 2 == 0:
            wgu = w_ffn_gu[i].astype(BF)
            wd = w_ffn_down[i].astype(BF)
            hp = _ffn(hp, g_f, wgu, wd)
            hs = _ffn(hs, g_f, wgu, wd)
        else:
            wr = jnp.pad(w_router[i], ((0, 0), (0, LANES - N_EXP)))
            final = l == depth - 1
            hp = _moe(hp, g_f, wr, w_exp_gu, w_exp_down, i, final_g, final=final)
            hs = _moe(hs, g_f, wr, w_exp_gu, w_exp_down, i, final_g, final=final)
    if depth % 2 == 1:
        raise NotImplementedError("the final norm is fused into the last routed-expert layer")
    y_prompt = hp.reshape(nbp, seq, D)
    y_sample = hs.reshape(nbs, dseq, D)
    return (y_prompt, y_sample,
            jnp.stack(p_ssd), jnp.stack(p_conv), jnp.stack(p_ckv), jnp.stack(p_kpe),
            jnp.stack(p_mk), jnp.stack(p_mv),
            s_state.reshape(n_even, nbs, SSD_HEADS, SSD_HEAD_DIM, SSD_STATE), jnp.stack(s_conv), jnp.stack(s_v), jnp.stack(s_ckv), jnp.stack(s_kpe))
```
